```python
import math
import jax
import jax.numpy as jnp
from jax import lax
import numpy as np

D_MODEL = 1024
BATCH = 16
SEQ = 256
DEPTH = 2
DEC_BATCH = 8
DEC_SEQ = 4096
PAST_LEN = 256

GRID_W = 64
N_EVEN = (DEPTH + 1) // 2
N_ODD = DEPTH // 2
H_A = 4
DV_A = D_MODEL // 8
DQK_A = DV_A // 2
W_A = H_A * DV_A
H_B = 4
DH_B = D_MODEL // 8
W_B = H_B * DH_B
KCONV = 3
MLSTM_CHUNK = 128
H_C = 8
DH_C = D_MODEL // H_C
W_C = H_C * DH_C
HGRN_CHUNK = 32
N_EXPERTS = 16
EXPERT_FF = D_MODEL
CAPACITY_FACTOR = 2
ROPE_BASE = 10000.0
RMS_EPS = 1e-6
Q_BLOCK = 128
NEG_BIG = -1e30
AB_SIZES = (W_A, W_A, W_A, W_B, W_B, W_B, W_B, 2 * H_B, 2 * H_B)
AB_SPLITS = tuple(sum(AB_SIZES[:i + 1]) for i in range(len(AB_SIZES) - 1))
P_AB = sum(AB_SIZES)
P_C = 5 * W_C

kernel_name = 'hybrid_diffattn_mlstm_hgrn2_ecmoe_step'


def rms_norm(x, g):
    xf = x.astype(jnp.float32)
    y = xf * lax.rsqrt(jnp.mean(xf * xf, axis=-1, keepdims=True) + RMS_EPS)
    return (y * g.astype(jnp.float32)).astype(x.dtype)


def adaln(cond, w, b):
    return jnp.split(jax.nn.silu(cond) @ w + b, 6, axis=-1)


def modulate(h, shift, scale):
    return h * (1.0 + scale[:, None, :]) + shift[:, None, :]


def axial_rope(T):
    rows = T // GRID_W
    row = jnp.repeat(jnp.arange(rows, dtype=jnp.float32), GRID_W)
    col = jnp.tile(jnp.arange(GRID_W, dtype=jnp.float32), rows)
    n_freq = DQK_A // 4
    inv_freq = jnp.power(ROPE_BASE, -jnp.arange(n_freq, dtype=jnp.float32) / n_freq)
    ang = jnp.concatenate([row[:, None] * inv_freq, col[:, None] * inv_freq], axis=-1)
    return jnp.cos(ang), jnp.sin(ang)


def apply_rope(x, cos, sin):
    half = DQK_A // 2
    xf = x.astype(jnp.float32)
    x1, x2 = xf[..., :half], xf[..., half:]
    c = cos[:, None, None, :]
    s = sin[:, None, None, :]
    return jnp.concatenate([x1 * c - x2 * s, x2 * c + x1 * s], axis=-1).astype(x.dtype)


def centred_depthwise_conv(x, w):
    K = w.shape[0]
    pad = K // 2
    T = x.shape[1]
    xp = jnp.pad(x, ((0, 0), (pad, pad), (0, 0)))
    return sum(xp[:, j:j + T] * w[j] for j in range(K))


def _bidir(a_fwd, a_bwd):
    return jnp.concatenate([a_fwd, a_bwd[:, ::-1]], axis=0)


def _merge_dirs(h, B):
    return h[:B] + h[B:, ::-1]


def _dir_major(s):
    s = jnp.swapaxes(s, 0, 1)
    return s.reshape(s.shape[0] * s.shape[1], *s.shape[2:]).astype(jnp.float32)


def _batch_major(s, B):
    return jnp.swapaxes(s.reshape(2, B, *s.shape[1:]), 0, 1)


def _to_chunks(a, L):
    N, T, H = a.shape[:3]
    a = a.reshape(N, T // L, L, H, *a.shape[3:])
    return jnp.moveaxis(a, (1, 3), (0, 2))


def _from_chunks(a):
    a = jnp.moveaxis(a, (0, 2), (1, 3))
    N, nc, L, H = a.shape[:4]
    return a.reshape(N, nc * L, H, *a.shape[4:])


def diff_attention(q, k, v, lam):
    B, Tq = q.shape[:2]
    nb = Tq // Q_BLOCK
    qb = jnp.swapaxes(q.reshape(B, nb, Q_BLOCK, *q.shape[2:]), 0, 1)
    scale = DQK_A ** -0.5

    def one_block(qblk):
        s = jnp.einsum('bqhjd,bkhjd->bhjqk', qblk, k).astype(jnp.float32) * scale
        p = jax.nn.softmax(s, axis=-1)
        a = p[:, :, 0] - lam * p[:, :, 1]
        return jnp.einsum('bhqk,bkhe->bqhe', a.astype(v.dtype), v)

    o = lax.map(one_block, qb)
    return jnp.swapaxes(o, 0, 1).reshape(B, Tq, *o.shape[3:])


def mlstm_scan(q, k, v, log_i, log_f, C0, n0, m0):
    L = MLSTM_CHUNK
    causal = jnp.tril(jnp.ones((L, L), dtype=bool))

    def step(carry, inp):
        C, n, m = carry
        qc, kc, vc, ic, fc = inp
        b = jnp.cumsum(fc, axis=-1)
        a = b + m[..., None]
        dmat = jnp.where(causal, b[..., :, None] - b[..., None, :] + ic[..., None, :], NEG_BIG)
        m_t = jnp.maximum(a, jnp.max(dmat, axis=-1))
        w_inter = jnp.exp(a - m_t)
        s = jnp.einsum('nhtd,nhsd->nhts', qc, kc) * jnp.exp(dmat - m_t[..., None])
        num = w_inter[..., None] * jnp.einsum('nhtd,nhde->nhte', qc, C) + jnp.einsum('nhts,nhse->nhte', s, vc)
        den = w_inter * jnp.einsum('nhtd,nhd->nht', qc, n) + jnp.sum(s, axis=-1)
        h = num / jnp.maximum(jnp.abs(den), jnp.exp(-m_t))[..., None]
        b_end = b[..., -1]
        g = b_end[..., None] - b + ic
        m_new = jnp.maximum(b_end + m, jnp.max(g, axis=-1))
        w_old = jnp.exp(b_end + m - m_new)
        w_s = jnp.exp(g - m_new[..., None])
        C_new = w_old[..., None, None] * C + jnp.einsum('nhs,nhsd,nhse->nhde', w_s, kc, vc)
        n_new = w_old[..., None] * n + jnp.einsum('nhs,nhsd->nhd', w_s, kc)
        return (C_new, n_new, m_new), h

    xs = tuple(_to_chunks(a.astype(jnp.float32), L) for a in (q, k, v, log_i, log_f))
    (C, n, m), h = lax.scan(step, (C0, n0, m0), xs)
    return _from_chunks(h), C, n, m


def hgrn2_scan(q, k, v, log_f, S0):
    L = HGRN_CHUNK
    causal = jnp.tril(jnp.ones((L, L), dtype=bool))[:, :, None]

    def step(S, inp):
        qc, kc, vc, fc = inp
        b = jnp.cumsum(fc, axis=2)
        inter = jnp.einsum('nhtk,nhkv->nhtv', qc * jnp.exp(b), S)
        decay = jnp.exp(jnp.where(causal, b[:, :, :, None, :] - b[:, :, None, :, :], NEG_BIG))
        att = jnp.einsum('nhtk,nhsk,nhtsk->nhts', qc, kc, decay)
        b_end = b[:, :, -1:, :]
        S_new = jnp.exp(b_end[:, :, 0])[..., None] * S + jnp.einsum('nhsk,nhsv->nhkv', kc * jnp.exp(b_end - b), vc)
        return S_new, inter + jnp.einsum('nhts,nhsv->nhtv', att, vc)

    xs = tuple(_to_chunks(a.astype(jnp.float32), L) for a in (q, k, v, log_f))
    S, o = lax.scan(step, S0.astype(jnp.float32), xs)
    return _from_chunks(o), S


def dattn_mlstm_mixer(h, w_in, b_ig, b_fg, lam_p, conv_w, g_attn, g_mlstm, w_out, layer, rope, ctx):
    B, T, _ = h.shape
    aq, ak, av, bq, bk, bv, bo, bi, bf = jnp.split(h @ w_in, AB_SPLITS, axis=-1)
    aq = aq.reshape(B, T, H_A, 2, DQK_A)
    ak = ak.reshape(B, T, H_A, 2, DQK_A)
    av = av.reshape(B, T, H_A, DV_A)
    if rope is not None:
        aq = apply_rope(aq, *rope)
        ak = apply_rope(ak, *rope)
    keys, vals = ak, av
    if ctx is not None:
        keys = jnp.concatenate([ctx[0].reshape(B, -1, H_A, 2, DQK_A).astype(ak.dtype), ak], axis=1)
        vals = jnp.concatenate([ctx[1].astype(av.dtype), av], axis=1)
    lam_init = 0.8 - 0.6 * math.exp(-0.3 * layer)
    lp = lam_p.astype(jnp.float32)
    lam = jnp.exp(jnp.sum(lp[0] * lp[1])) - jnp.exp(jnp.sum(lp[2] * lp[3])) + lam_init
    o_a = rms_norm(diff_attention(aq, keys, vals, lam), g_attn) * (1.0 - lam_init)
    qk = jax.nn.silu(centred_depthwise_conv(jnp.concatenate([bq, bk], axis=-1), conv_w))
    bq, bk = jnp.split(qk, 2, axis=-1)
    bq = bq.reshape(B, T, H_B, DH_B)
    bk = bk.reshape(B, T, H_B, DH_B) * (DH_B ** -0.5)
    bv = bv.reshape(B, T, H_B, DH_B)
    log_i = bi.reshape(B, T, 2, H_B).astype(jnp.float32) + b_ig
    log_f = jax.nn.log_sigmoid(bf.reshape(B, T, 2, H_B).astype(jnp.float32) + b_fg)
    if ctx is None:
        C0 = jnp.zeros((2 * B, H_B, DH_B, DH_B), jnp.float32)
        n0 = jnp.zeros((2 * B, H_B, DH_B), jnp.float32)
        m0 = jnp.zeros((2 * B, H_B), jnp.float32)
    else:
        C0, n0, m0 = _dir_major(ctx[2]), _dir_major(ctx[3]), _dir_major(ctx[4])
    hs, C, n, m = mlstm_scan(_bidir(bq, bq), _bidir(bk, bk), _bidir(bv, bv),
                             _bidir(log_i[:, :, 0], log_i[:, :, 1]), _bidir(log_f[:, :, 0], log_f[:, :, 1]),
                             C0, n0, m0)
    h_b = _merge_dirs(hs, B).astype(h.dtype)
    o_b = jax.nn.sigmoid(bo.reshape(B, T, H_B, DH_B)) * rms_norm(h_b, g_mlstm)
    out = jnp.concatenate([o_a.reshape(B, T, W_A), o_b.reshape(B, T, W_B)], axis=-1) @ w_out
    ctx_new = (ak.reshape(B, T, H_A, 2 * DQK_A), av, _batch_major(C, B), _batch_major(n, B), _batch_major(m, B))
    return out, ctx_new


def hgrn2_mixer(h, w_in, b_f, gamma_lb, g_hgrn, w_out, layer, S0):
    B, T, _ = h.shape
    q, i, g, f_pre = jnp.split(h @ w_in, (W_C, 2 * W_C, 3 * W_C), axis=-1)
    f_pre = f_pre.reshape(B, T, 2, W_C).astype(jnp.float32) + b_f
    lbs = jnp.cumsum(jax.nn.softmax(gamma_lb.astype(jnp.float32), axis=0), axis=0)
    lb = jnp.maximum(lbs[layer] - lbs[0], 0.0)
    log_f = jnp.logaddexp(jnp.log(lb), jnp.log1p(-lb) + jax.nn.log_sigmoid(f_pre))
    k = (1.0 - lb) * jax.nn.sigmoid(-f_pre)
    shp = (B, T, 2, H_C, DH_C)
    log_f = log_f.reshape(shp)
    k = k.reshape(shp)
    q = q.reshape(B, T, H_C, DH_C)
    v = i.reshape(B, T, H_C, DH_C)
    S_init = jnp.zeros((2 * B, H_C, DH_C, DH_C), jnp.float32) if S0 is None else _dir_major(S0)
    o, S = hgrn2_scan(_bidir(q, q), _bidir(k[:, :, 0], k[:, :, 1]), _bidir(v, v),
                      _bidir(log_f[:, :, 0], log_f[:, :, 1]), S_init)
    o = rms_norm(_merge_dirs(o, B).astype(h.dtype), g_hgrn) * jax.nn.silu(g.reshape(B, T, H_C, DH_C))
    return o.reshape(B, T, W_C) @ w_out, _batch_major(S, B)


def expert_choice_moe(h, w_router, w_gate, w_up, w_down):
    B, T, D = h.shape
    cap = CAPACITY_FACTOR * T // N_EXPERTS
    aff = jax.nn.softmax((h @ w_router).astype(jnp.float32), axis=-1)
    gate, idx = lax.top_k(jnp.swapaxes(aff, 1, 2), cap)
    bidx = jnp.arange(B)[:, None, None]
    xin = h[bidx, idx]
    a = jnp.einsum('becd,edf->becf', xin, w_gate)
    u = jnp.einsum('becd,edf->becf', xin, w_up)
    y = jnp.einsum('becf,efd->becd', jax.nn.silu(a) * u, w_down) * gate[..., None].astype(h.dtype)
    return jnp.zeros_like(h).at[bidx, idx].add(y)


def _stream(x, cond, rope, cache, w_ada, b_ada, g_norm, g_final, w_in_ab, b_ig_ab, b_fg_ab, lam_ab, conv_ab,
            g_attn_ab, g_mlstm_ab, w_out_ab, w_in_c, b_f_c, gamma_lb, g_hgrn_c, w_out_c,
            w_router, w_gate_e, w_up_e, w_down_e):
    even_out, odd_out = [], []
    for l in range(DEPTH):
        j = l // 2
        sh1, sc1, gt1, sh2, sc2, gt2 = adaln(cond, w_ada[l], b_ada[l])
        h = modulate(rms_norm(x, g_norm[l, 0]), sh1, sc1)
        if l % 2 == 0:
            ctx = None if cache is None else tuple(a[:, j] for a in cache[:5])
            mix, ctx_new = dattn_mlstm_mixer(h, w_in_ab[j], b_ig_ab[j], b_fg_ab[j], lam_ab[j], conv_ab[j],
                                             g_attn_ab[j], g_mlstm_ab[j], w_out_ab[j], l, rope, ctx)
            even_out.append(ctx_new)
        else:
            S0 = None if cache is None else cache[5][:, j]
            mix, S_new = hgrn2_mixer(h, w_in_c[j], b_f_c[j], gamma_lb, g_hgrn_c[j], w_out_c[j], l, S0)
            odd_out.append(S_new)
        x = x + gt1[:, None, :] * mix
        h = modulate(rms_norm(x, g_norm[l, 1]), sh2, sc2)
        x = x + gt2[:, None, :] * expert_choice_moe(h, w_router[l], w_gate_e[l], w_up_e[l], w_down_e[l])
    return rms_norm(x, g_final), even_out, odd_out


def setup_inputs(seed: int = 0) -> dict:
    key = jax.random.key(seed)
    keys = iter(jax.random.split(key, 32))

    def nrm(shape, scale=1.0):
        return scale * jax.random.normal(next(keys), shape, dtype=jnp.float32)

    D = D_MODEL
    f_bias = jnp.linspace(3.0, 6.0, H_B, dtype=jnp.float32)
    return {
        'x_prompt': nrm((BATCH, SEQ, D)),
        'x_sample': nrm((DEC_BATCH, DEC_SEQ, D)),
        'cache_dattn_k': nrm((DEC_BATCH, N_EVEN, PAST_LEN, H_A, 2 * DQK_A)),
        'cache_dattn_v': nrm((DEC_BATCH, N_EVEN, PAST_LEN, H_A, DV_A)),
        'state_mlstm_C': nrm((DEC_BATCH, N_EVEN, 2, H_B, DH_B, DH_B), 0.1),
        'state_mlstm_n': nrm((DEC_BATCH, N_EVEN, 2, H_B, DH_B), 0.5),
        'state_mlstm_m': nrm((DEC_BATCH, N_EVEN, 2, H_B)),
        'state_hgrn_S': nrm((DEC_BATCH, N_ODD, 2, H_C, DH_C, DH_C), 0.5),
        'c': nrm((DEC_BATCH, D)),
        'c_ctx': nrm((D,)),
        'w_ada': nrm((DEPTH, D, 6 * D), 0.5 * D ** -0.5),
        'b_ada': nrm((DEPTH, 6 * D), 0.02),
        'g_norm': 1.0 + nrm((DEPTH, 2, D), 0.02),
        'g_final': 1.0 + nrm((D,), 0.02),
        'w_in_ab': nrm((N_EVEN, D, P_AB), D ** -0.5),
        'b_ig_ab': nrm((N_EVEN, 2, H_B), 0.1),
        'b_fg_ab': f_bias + nrm((N_EVEN, 2, H_B), 0.1),
        'lam_ab': nrm((N_EVEN, 4, DQK_A), 0.1),
        'conv_ab': nrm((N_EVEN, KCONV, 2 * W_B), KCONV ** -0.5),
        'g_attn_ab': 1.0 + nrm((N_EVEN, H_A, DV_A), 0.02),
        'g_mlstm_ab': 1.0 + nrm((N_EVEN, H_B, DH_B), 0.02),
        'w_out_ab': nrm((N_EVEN, W_A + W_B, D), (W_A + W_B) ** -0.5),
        'w_in_c': nrm((N_ODD, D, P_C), D ** -0.5),
        'b_f_c': nrm((N_ODD, 2, W_C), 0.1),
        'gamma_lb': nrm((DEPTH, W_C)),
        'g_hgrn_c': 1.0 + nrm((N_ODD, H_C, DH_C), 0.02),
        'w_out_c': nrm((N_ODD, W_C, D), W_C ** -0.5),
        'w_router': nrm((DEPTH, D, N_EXPERTS), D ** -0.5),
        'w_gate_e': nrm((DEPTH, N_EXPERTS, D, EXPERT_FF), D ** -0.5),
        'w_up_e': nrm((DEPTH, N_EXPERTS, D, EXPERT_FF), D ** -0.5),
        'w_down_e': nrm((DEPTH, N_EXPERTS, EXPERT_FF, D), EXPERT_FF ** -0.5),
    }


def reference(x_prompt, x_sample, cache_dattn_k, cache_dattn_v, state_mlstm_C, state_mlstm_n, state_mlstm_m,
              state_hgrn_S, c, c_ctx, w_ada, b_ada, g_norm, g_final, w_in_ab, b_ig_ab, b_fg_ab, lam_ab, conv_ab,
              g_attn_ab, g_mlstm_ab, w_out_ab, w_in_c, b_f_c, gamma_lb, g_hgrn_c, w_out_c,
              w_router, w_gate_e, w_up_e, w_down_e):
    cond_ctx = jnp.broadcast_to(c_ctx, (x_prompt.shape[0], D_MODEL))
    y_prompt, ev, od = _stream(x_prompt, cond_ctx, None, None, w_ada, b_ada, g_norm, g_final, w_in_ab, b_ig_ab,
                               b_fg_ab, lam_ab, conv_ab, g_attn_ab, g_mlstm_ab, w_out_ab, w_in_c, b_f_c, gamma_lb,
                               g_hgrn_c, w_out_c, w_router, w_gate_e, w_up_e, w_down_e)
    new_dattn_k = jnp.stack([e[0] for e in ev], axis=1)
    new_dattn_v = jnp.stack([e[1] for e in ev], axis=1)
    new_mlstm_C = jnp.stack([e[2] for e in ev], axis=1)
    new_mlstm_n = jnp.stack([e[3] for e in ev], axis=1)
    new_mlstm_m = jnp.stack([e[4] for e in ev], axis=1)
    new_hgrn_S = jnp.stack(od, axis=1)
    cache = (cache_dattn_k, cache_dattn_v, state_mlstm_C, state_mlstm_n, state_mlstm_m, state_hgrn_S)
    y_sample, _, _ = _stream(x_sample, c, axial_rope(x_sample.shape[1]), cache, w_ada, b_ada, g_norm, g_final,
                             w_in_ab, b_ig_ab, b_fg_ab, lam_ab, conv_ab, g_attn_ab, g_mlstm_ab, w_out_ab, w_in_c,
                             b_f_c, gamma_lb, g_hgrn_c, w_out_c, w_router, w_gate_e, w_up_e, w_down_e)
    return (y_prompt, y_sample, new_dattn_k, new_dattn_v, new_mlstm_C, new_mlstm_n, new_mlstm_m, new_hgrn_S)
```

```python
import functools
import math

import jax
import jax.numpy as jnp
from jax import lax
from jax.experimental import pallas as pl
from jax.experimental.pallas import tpu as pltpu

F32 = jnp.float32
BF16 = jnp.bfloat16
I32 = jnp.int32

D_MODEL = 1024
H_A = 4
DV_A = 128
DQK_A = 64
W_A = H_A * DV_A
H_B = 4
DH_B = 128
W_B = H_B * DH_B
H_C = 8
DH_C = 128
W_C = H_C * DH_C
N_EXPERTS = 16
CAPACITY_FACTOR = 2
GRID_W = 64
ROPE_BASE = 10000.0
RMS_EPS = 1e-6
NEG_BIG = -1e30
LANES = 128
BF16_ROWS = 16
SCAN_CHUNK = 128
MOE_TOKEN_BLOCK = 128
VMEM_LIMIT = 56 * 1024 * 1024


def _cp(*sem):
    return pltpu.CompilerParams(dimension_semantics=sem, vmem_limit_bytes=VMEM_LIMIT)


def _dot(a, b):
    return jnp.dot(a, b, preferred_element_type=F32)


def _dot_nt(a, b):
    return lax.dot_general(a, b, (((1,), (1,)), ((), ())), preferred_element_type=F32)


def _dot_tn(a, b):
    return lax.dot_general(a, b, (((0,), (0,)), ((), ())), preferred_element_type=F32)


def _split3(x):
    hi = x.astype(BF16)
    r = x - hi.astype(F32)
    mid = r.astype(BF16)
    lo = (r - mid.astype(F32)).astype(BF16)
    return hi, mid, lo


def _dot_sel_left(sel, x):
    hi, mid, lo = _split3(x)
    return _dot(sel, hi) + _dot(sel, mid) + _dot(sel, lo)


def _dot_sel_right(x, sel):
    hi, mid, lo = _split3(x)
    return _dot(hi, sel) + _dot(mid, sel) + _dot(lo, sel)


def _dot_f32(a, b):
    ah, am, al = _split3(a)
    bh, bm, bl = _split3(b)
    return (_dot(ah, bh) + (_dot(ah, bm) + _dot(am, bh))
            + (_dot(ah, bl) + _dot(am, bm) + _dot(al, bh)))


def _sigmoid(x):
    return 1.0 / (1.0 + jnp.exp(-x))


def _silu(x):
    return x * _sigmoid(x)


def _log_sigmoid(x):
    return jnp.minimum(x, 0.0) - jnp.log(1.0 + jnp.exp(-jnp.abs(x)))


def _rms_rows(x, g):
    return x * lax.rsqrt(jnp.mean(x * x, axis=-1, keepdims=True) + RMS_EPS) * g


def _iota(shape, dim):
    return lax.broadcasted_iota(I32, shape, dim)


def _ada_kernel(c_ref, w_ref, b_ref, o_ref):
    c = c_ref[...]
    o_ref[0] = _dot_f32(_silu(c), w_ref[0]) + b_ref[0]


def _ada(cond_rows, w_ada, b_ada):
    depth, d, n6 = w_ada.shape
    rows = cond_rows.shape[0]
    tn = 1536
    return pl.pallas_call(
        _ada_kernel,
        grid=(depth, n6 // tn),
        in_specs=[pl.BlockSpec((rows, d), lambda l, j: (0, 0)),
                  pl.BlockSpec((1, d, tn), lambda l, j: (l, 0, j)),
                  pl.BlockSpec((1, 1, tn), lambda l, j: (l, 0, j))],
        out_specs=pl.BlockSpec((1, rows, tn), lambda l, j: (l, 0, j)),
        out_shape=jax.ShapeDtypeStruct((depth, rows, n6), F32),
        compiler_params=_cp("parallel", "parallel"),
        name="adaln",
    )(cond_rows, w_ada, b_ada.reshape(depth, 1, n6))


def _mod_spec(per_batch, tm, t):
    if per_batch:
        return pl.BlockSpec((1, 1, D_MODEL), lambda i: (i * tm // t, 0, 0))
    return pl.BlockSpec((1, 1, D_MODEL), lambda i: (0, 0, 0))


def _row_tile(t):
    return min(512, t)


def _in_ab_kernel(*refs, use_rope):
    if use_rope:
        (x_ref, g_ref, sh_ref, sc_ref, w_ref, wg_ref, cos_ref, sa_ref, sb_ref,
         aq_ref, ak_ref, av_ref, bqk_ref, bv_ref, bo_ref, gt_ref) = refs
    else:
        (x_ref, g_ref, sh_ref, sc_ref, w_ref, wg_ref,
         aq_ref, ak_ref, av_ref, bqk_ref, bv_ref, bo_ref, gt_ref) = refs
    h = _rms_rows(x_ref[...], g_ref[...]) * (1.0 + sc_ref[0]) + sh_ref[0]
    hb = h.astype(BF16)
    y = _dot(hb, w_ref[...])
    gt_ref[...] = _dot(hb, wg_ref[...])
    aq = y[:, 0:W_A] * (DQK_A ** -0.5)
    ak = y[:, W_A:2 * W_A]
    if use_rope:
        cos = jnp.concatenate([cos_ref[...]] * H_A, axis=1)
        sa = jnp.concatenate([sa_ref[...]] * H_A, axis=1)
        sb = jnp.concatenate([sb_ref[...]] * H_A, axis=1)
        half = DQK_A // 2

        def rope(v):
            return (v * cos + pltpu.roll(v, W_A - half, axis=1) * sa
                    + pltpu.roll(v, half, axis=1) * sb)

        aq = rope(aq)
        ak = rope(ak)
    aq_ref[...] = aq
    ak_ref[...] = ak
    av_ref[...] = y[:, 2 * W_A:3 * W_A]
    bqk_ref[...] = y[:, 3 * W_A:3 * W_A + 2 * W_B]
    bv_ref[...] = y[:, 3 * W_A + 2 * W_B:3 * W_A + 3 * W_B]
    bo_ref[...] = y[:, 3 * W_A + 3 * W_B:3 * W_A + 4 * W_B]


def _in_ab(x2, g, shift, scale, w_main, w_gate, rope, t, per_batch):
    n = x2.shape[0]
    tm = _row_tile(t)
    use_rope = rope is not None
    row = lambda c: pl.BlockSpec((tm, c), lambda i: (i, 0))
    full = lambda a: pl.BlockSpec(a.shape, lambda i: (0,) * a.ndim)
    in_specs = [row(D_MODEL), full(g), _mod_spec(per_batch, tm, t), _mod_spec(per_batch, tm, t),
                full(w_main), full(w_gate)]
    args = [x2, g, shift, scale, w_main, w_gate]
    if use_rope:
        nt = t // tm
        tab = pl.BlockSpec((tm, LANES), lambda i: (i % nt, 0))
        in_specs += [tab, tab, tab]
        args += list(rope)
    widths = (W_A, W_A, W_A, 2 * W_B, W_B, W_B, LANES)
    return pl.pallas_call(
        functools.partial(_in_ab_kernel, use_rope=use_rope),
        grid=(n // tm,),
        in_specs=in_specs,
        out_specs=[row(c) for c in widths],
        out_shape=[jax.ShapeDtypeStruct((n, c), F32) for c in widths],
        compiler_params=_cp("parallel"),
        name="in_proj_even",
    )(*args)


def _attn_kernel(*refs, has_cache, tk, lam_init):
    if has_cache:
        q_ref, k_ref, v_ref, kc_ref, vc_ref, lam_ref, o_ref = refs
    else:
        q_ref, k_ref, v_ref, lam_ref, o_ref = refs
    q = q_ref[0]
    tq = q.shape[0]
    first = _iota((1, LANES), 1) < DQK_A
    qa = jnp.where(first, q, 0.0).astype(BF16)
    qb = jnp.where(first, 0.0, q).astype(BF16)

    def block(kblk, vblk, carry):
        kb = kblk.astype(BF16)
        vb = vblk.astype(BF16)
        out = []
        for qm, (m, l, acc) in zip((qa, qb), carry):
            s = _dot_nt(qm, kb)
            mn = jnp.maximum(m, jnp.max(s, axis=-1, keepdims=True))
            p = jnp.exp(s - mn)
            a = jnp.exp(m - mn)
            out.append((mn, a * l + jnp.sum(p, axis=-1, keepdims=True),
                        a * acc + _dot(p.astype(BF16), vb)))
        return tuple(out)

    init = (jnp.full((tq, 1), NEG_BIG, F32), jnp.zeros((tq, 1), F32), jnp.zeros((tq, DV_A), F32))
    carry = (init, init)
    if has_cache:
        carry = block(kc_ref[0], vc_ref[0], carry)
    nkb = k_ref.shape[1] // tk

    def body(i, c):
        r = pl.multiple_of(i * tk, tk)
        return block(k_ref[0, pl.ds(r, tk), :], v_ref[0, pl.ds(r, tk), :], c)

    (m1, l1, a1), (m2, l2, a2) = lax.fori_loop(0, nkb, body, carry)
    lp = lam_ref[...]
    lam = (jnp.exp(jnp.sum(lp[0:1] * lp[1:2], axis=-1, keepdims=True))
           - jnp.exp(jnp.sum(lp[2:3] * lp[3:4], axis=-1, keepdims=True)) + lam_init)
    o_ref[0] = a1 / l1 - lam * (a2 / l2)


def _attention(aq, ak, av, kc, vc, lam_p, lam_init):
    b, t, _ = aq.shape
    tq = min(256, t)
    tk = min(512, t)
    has_cache = kc is not None
    qs = pl.BlockSpec((1, tq, DV_A), lambda bi, h, i: (bi, i, h))
    ks = pl.BlockSpec((1, t, DV_A), lambda bi, h, i: (bi, 0, h))
    in_specs = [qs, ks, ks]
    args = [aq, ak, av]
    if has_cache:
        cs = pl.BlockSpec((1, kc.shape[1], DV_A), lambda bi, h, i: (bi, 0, h))
        in_specs += [cs, cs]
        args += [kc, vc]
    in_specs.append(pl.BlockSpec(lam_p.shape, lambda bi, h, i: (0, 0)))
    args.append(lam_p)
    return pl.pallas_call(
        functools.partial(_attn_kernel, has_cache=has_cache, tk=tk, lam_init=lam_init),
        grid=(b, H_A, t // tq),
        in_specs=in_specs,
        out_specs=qs,
        out_shape=jax.ShapeDtypeStruct((b, t, W_A), F32),
        compiler_params=_cp("parallel", "parallel", "parallel"),
        name="diff_attention",
    )(*args)


def _conv_kernel(x_ref, w_ref, o_ref):
    x = x_ref[0]
    t = x.shape[0]
    w = w_ref[...]
    r = _iota((t, 1), 0)
    prev = jnp.where(r == 0, 0.0, pltpu.roll(x, 1, axis=0))
    nxt = jnp.where(r == t - 1, 0.0, pltpu.roll(x, t - 1, axis=0))
    o_ref[0] = _silu(prev * w[0:1] + x * w[1:2] + nxt * w[2:3])


def _conv_silu(bqk, conv_w):
    b, t, c = bqk.shape
    tc = 256
    spec = pl.BlockSpec((1, t, tc), lambda bi, j: (bi, 0, j))
    return pl.pallas_call(
        _conv_kernel,
        grid=(b, c // tc),
        in_specs=[spec, pl.BlockSpec((conv_w.shape[0], tc), lambda bi, j: (0, j))],
        out_specs=spec,
        out_shape=jax.ShapeDtypeStruct(bqk.shape, F32),
        compiler_params=_cp("parallel", "parallel"),
        name="mlstm_conv",
    )(bqk, conv_w)


def _tri(n, lower):
    r = _iota((n, n), 0)
    c = _iota((n, n), 1)
    return r >= c if lower else r <= c


def _mlstm_kernel(*refs, has_init, out_state):
    qf, kf, vf, gf, qb, kb, vb, gb, bias = refs[:9]
    pos = 9
    if has_init:
        c0, n0, m0 = refs[pos:pos + 3]
        pos += 3
    hf, hb = refs[pos:pos + 2]
    pos += 2
    if out_state:
        c_out, n_out, m_out = refs[pos:pos + 3]
        pos += 3
    c_s, n_s, m_s = refs[pos:pos + 3]
    step = pl.program_id(1)
    L = SCAN_CHUNK

    @pl.when(step == 0)
    def _():
        if has_init:
            c_s[...] = c0[0]
            n_s[...] = n0[0]
            m_s[...] = m0[0]
        else:
            c_s[...] = jnp.zeros_like(c_s)
            n_s[...] = jnp.zeros_like(n_s)
            m_s[...] = jnp.zeros_like(m_s)

    lower = _tri(L, True)
    upper = _tri(L, False)
    lower_b = lower.astype(BF16)
    upper_b = upper.astype(BF16)
    col = _iota((1, LANES), 1)
    for d, (q_ref, k_ref, v_ref, g_ref, h_ref) in enumerate(((qf, kf, vf, gf, hf), (qb, kb, vb, gb, hb))):
        g = g_ref[0] + bias[...]
        g = jnp.where(col < 2 * H_B, g, _log_sigmoid(g))
        gt = g.T
        if d == 0:
            bc_all = _dot_sel_left(lower_b, g)
            br_all = _dot_sel_right(gt, upper_b)
            mask = lower
        else:
            bc_all = _dot_sel_left(upper_b, g)
            br_all = _dot_sel_right(gt, lower_b)
            mask = upper
        for h in range(H_B):
            u = d * H_B + h
            ji = u
            jf = 2 * H_B + u
            sl = slice(h * DH_B, (h + 1) * DH_B)
            q = q_ref[0, :, sl]
            k = k_ref[0, :, sl] * (DH_B ** -0.5)
            v = v_ref[0, :, sl]
            li_col = g[:, ji:ji + 1]
            li_row = gt[ji:ji + 1, :]
            b_col = bc_all[:, jf:jf + 1]
            b_row = br_all[jf:jf + 1, :]
            m_prev = m_s[u][:, 0:1]
            a_col = b_col + m_prev
            dm = jnp.where(mask, b_col - b_row + li_row, NEG_BIG)
            m_t = jnp.maximum(a_col, jnp.max(dm, axis=-1, keepdims=True))
            w_inter = jnp.exp(a_col - m_t)
            qb16 = q.astype(BF16)
            vb16 = v.astype(BF16)
            s = _dot_nt(qb16, k.astype(BF16)) * jnp.exp(dm - m_t)
            c_old = c_s[u]
            n_old = n_s[u]
            num = w_inter * _dot(qb16, c_old.astype(BF16)) + _dot(s.astype(BF16), vb16)
            qn = jnp.sum(qb16.astype(F32) * n_old.astype(BF16).astype(F32), axis=-1, keepdims=True)
            den = w_inter * qn + jnp.sum(s, axis=-1, keepdims=True)
            h_ref[0, :, sl] = num / jnp.maximum(jnp.abs(den), jnp.exp(-m_t))
            b_end = b_col[L - 1:L, :] if d == 0 else b_col[0:1, :]
            g_col = b_end - b_col + li_col
            m_new = jnp.maximum(b_end + m_prev, jnp.max(g_col, axis=0, keepdims=True))
            w_old = jnp.exp(b_end + m_prev - m_new)
            ks = jnp.exp(g_col - m_new) * k
            ksb = ks.astype(BF16)
            c_s[u] = w_old * c_old + _dot_tn(ksb, vb16)
            n_s[u] = w_old * n_old + jnp.sum(ksb.astype(F32), axis=0, keepdims=True)
            m_s[u] = jnp.broadcast_to(m_new, (1, LANES))

    if out_state:
        @pl.when(step == pl.num_programs(1) - 1)
        def _():
            c_out[0] = c_s[...]
            n_out[0] = n_s[...]
            m_out[0] = m_s[...]


def _mlstm(qk, v, gates, bias, init, out_state):
    b, t, _ = v.shape
    L = SCAN_CHUNK
    nc = t // L
    nu = 2 * H_B
    fwd = lambda j: (lambda bi, c: (bi, c, j))
    bwd = lambda j: (lambda bi, c: (bi, nc - 1 - c, j))
    blk = lambda w, im: pl.BlockSpec((1, L, w), im)
    in_specs = [blk(W_B, fwd(0)), blk(W_B, fwd(1)), blk(W_B, fwd(0)), blk(LANES, fwd(0)),
                blk(W_B, bwd(0)), blk(W_B, bwd(1)), blk(W_B, bwd(0)), blk(LANES, bwd(0)),
                pl.BlockSpec((1, LANES), lambda bi, c: (0, 0))]
    args = [qk, qk, v, gates, qk, qk, v, gates, bias]
    st_specs = [pl.BlockSpec((1, nu, DH_B, DH_B), lambda bi, c: (bi, 0, 0, 0)),
                pl.BlockSpec((1, nu, 1, DH_B), lambda bi, c: (bi, 0, 0, 0)),
                pl.BlockSpec((1, nu, 1, LANES), lambda bi, c: (bi, 0, 0, 0))]
    st_shapes = [jax.ShapeDtypeStruct((b, nu, DH_B, DH_B), F32),
                 jax.ShapeDtypeStruct((b, nu, 1, DH_B), F32),
                 jax.ShapeDtypeStruct((b, nu, 1, LANES), F32)]
    has_init = init is not None
    if has_init:
        in_specs += st_specs
        args += list(init)
    out_specs = [blk(W_B, fwd(0)), blk(W_B, bwd(0))]
    out_shape = [jax.ShapeDtypeStruct((b, t, W_B), F32)] * 2
    if out_state:
        out_specs += st_specs
        out_shape += st_shapes
    return pl.pallas_call(
        functools.partial(_mlstm_kernel, has_init=has_init, out_state=out_state),
        grid=(b, nc),
        in_specs=in_specs,
        out_specs=out_specs,
        out_shape=out_shape,
        scratch_shapes=[pltpu.VMEM((nu, DH_B, DH_B), F32), pltpu.VMEM((nu, 1, DH_B), F32),
                        pltpu.VMEM((nu, 1, LANES), F32)],
        compiler_params=_cp("parallel", "arbitrary"),
        name="mlstm_scan",
    )(*args)


def _head_rms(x, g, heads, width):
    parts = []
    for h in range(heads):
        sl = slice(h * width, (h + 1) * width)
        parts.append(_rms_rows(x[:, sl], g[:, sl]))
    return parts


def _out_ab_kernel(at_ref, hf_ref, hb_ref, bo_ref, x_ref, gate_ref, ga_ref, gm_ref, w_ref, o_ref, *, out_scale):
    oa = [p * out_scale for p in _head_rms(at_ref[...], ga_ref[...], H_A, DV_A)]
    hm = _head_rms(hf_ref[...] + hb_ref[...], gm_ref[...], H_B, DH_B)
    bo = bo_ref[...]
    ob = [_sigmoid(bo[:, h * DH_B:(h + 1) * DH_B]) * hm[h] for h in range(H_B)]
    cat = jnp.concatenate(oa + ob, axis=1).astype(BF16)
    o_ref[...] = x_ref[...] + gate_ref[0] * _dot(cat, w_ref[...])


def _out_ab(attn, hf, hb, bo, x2, gate, g_attn, g_mlstm, w_out, t, per_batch, out_scale):
    n = x2.shape[0]
    tm = _row_tile(t)
    row = lambda c: pl.BlockSpec((tm, c), lambda i: (i, 0))
    full = lambda a: pl.BlockSpec(a.shape, lambda i: (0,) * a.ndim)
    return pl.pallas_call(
        functools.partial(_out_ab_kernel, out_scale=out_scale),
        grid=(n // tm,),
        in_specs=[row(W_A), row(W_B), row(W_B), row(W_B), row(D_MODEL), _mod_spec(per_batch, tm, t),
                  full(g_attn), full(g_mlstm), full(w_out)],
        out_specs=row(D_MODEL),
        out_shape=jax.ShapeDtypeStruct((n, D_MODEL), F32),
        compiler_params=_cp("parallel"),
        name="out_proj_even",
    )(attn, hf, hb, bo, x2, gate, g_attn, g_mlstm, w_out)


def _in_c_kernel(x_ref, g_ref, sh_ref, sc_ref, w_ref, bf_ref, gam_ref, q_ref, v_ref, og_ref, lf_ref, kk_ref, *, layer):
    h = _rms_rows(x_ref[...], g_ref[...]) * (1.0 + sc_ref[0]) + sh_ref[0]
    y = _dot(h.astype(BF16), w_ref[...])
    q_ref[...] = y[:, 0:W_C]
    v_ref[...] = y[:, W_C:2 * W_C]
    og_ref[...] = y[:, 2 * W_C:3 * W_C]
    gam = gam_ref[...]
    e = jnp.exp(gam - jnp.max(gam, axis=0, keepdims=True))
    p = e / jnp.sum(e, axis=0, keepdims=True)
    lbs = p[0:1]
    lbs0 = lbs
    for i in range(1, layer + 1):
        lbs = lbs + p[i:i + 1]
    lb = jnp.maximum(lbs - lbs0, 0.0)
    for d in range(2):
        f = y[:, (3 + d) * W_C:(4 + d) * W_C] + bf_ref[d:d + 1]
        u = jnp.exp(-jnp.abs(f))
        neg = f < 0.0
        lf_ref[:, d * W_C:(d + 1) * W_C] = jnp.log(jnp.where(neg, lb + u, 1.0 + lb * u)) - jnp.log(1.0 + u)
        kk_ref[:, d * W_C:(d + 1) * W_C] = (1.0 - lb) * jnp.where(neg, 1.0, u) / (1.0 + u)


def _in_c(x2, g, shift, scale, w_in, b_f, gamma_lb, layer, t, per_batch):
    n = x2.shape[0]
    tm = _row_tile(t)
    row = lambda c: pl.BlockSpec((tm, c), lambda i: (i, 0))
    full = lambda a: pl.BlockSpec(a.shape, lambda i: (0,) * a.ndim)
    widths = (W_C, W_C, W_C, 2 * W_C, 2 * W_C)
    return pl.pallas_call(
        functools.partial(_in_c_kernel, layer=layer),
        grid=(n // tm,),
        in_specs=[row(D_MODEL), full(g), _mod_spec(per_batch, tm, t), _mod_spec(per_batch, tm, t),
                  full(w_in), full(b_f), full(gamma_lb)],
        out_specs=[row(c) for c in widths],
        out_shape=[jax.ShapeDtypeStruct((n, c), F32) for c in widths],
        compiler_params=_cp("parallel"),
        name="in_proj_odd",
    )(x2, g, shift, scale, w_in, b_f, gamma_lb)


HGRN_LEVELS = (8, 16, 32, 64)
HGRN_DIAG = 8


def _hgrn_unit(q, kk, v, lf, st, b_s, rev):
    L = SCAN_CHUNK
    tri = _tri(L, not rev).astype(BF16)
    b = _dot_sel_left(tri, lf)
    b_s[...] = b
    row = _iota((L, 1), 0)
    col = _iota((1, L), 1)
    q16 = q.astype(BF16)
    v16 = v.astype(BF16)
    o = _dot_nt((q * jnp.exp(b)).astype(BF16), st.astype(BF16))
    c = HGRN_DIAG
    sub = _iota((c, 1), 0)
    zs = []
    for blk in range(L // c):
        r0 = blk * c
        qb = q[r0:r0 + c]
        bb = b[r0:r0 + c]
        for s in range(c):
            keep = (sub <= s) if rev else (sub >= s)
            e = jnp.exp(jnp.where(keep, bb - b_s[r0 + s:r0 + s + 1, :], NEG_BIG))
            zs.append(qb * e * kk[r0 + s:r0 + s + 1, :])
    zall = jnp.concatenate(zs, axis=0).astype(BF16)
    rs = _dot(zall, jnp.ones((LANES, LANES), BF16))
    lane = _iota((c, LANES), 1)
    rows = []
    for blk in range(L // c):
        acc = jnp.zeros((c, LANES), F32)
        for s in range(c):
            i = blk * c + s
            acc = jnp.where(lane == i, rs[i * c:(i + 1) * c], acc)
        rows.append(acc)
    att = jnp.concatenate(rows, axis=0)
    for m in HGRN_LEVELS:
        par = 2 * m
        pieces = []
        for p0 in range(0, L, par):
            ref_row = p0 + m if rev else p0 + m - 1
            pieces.append(jnp.broadcast_to(b_s[ref_row:ref_row + 1, :], (par, LANES)))
        bref = jnp.concatenate(pieces, axis=0)
        in_first = (row % par) < m
        reader = in_first if rev else jnp.logical_not(in_first)
        qs = q * jnp.exp(jnp.where(reader, b - bref, NEG_BIG))
        ks = kk * jnp.exp(jnp.where(reader, NEG_BIG, bref - b))
        a = _dot_nt(qs.astype(BF16), ks.astype(BF16))
        att = att + jnp.where((row // par) == (col // par), a, 0.0)
    o = o + _dot(att.astype(BF16), v16)
    b_end = b_s[0:1, :] if rev else b_s[L - 1:L, :]
    kd = (kk * jnp.exp(b_end - b)).astype(BF16)
    st_new = st * jnp.exp(b_end) + _dot_tn(v16, kd)
    return o, st_new


def _hgrn_kernel(*refs, has_init, out_state):
    qf, vf, lff, kkf, qb, vb, lfb, kkb = refs[:8]
    pos = 8
    if has_init:
        s0 = refs[pos]
        pos += 1
    of, ob = refs[pos:pos + 2]
    pos += 2
    if out_state:
        s_out = refs[pos]
        pos += 1
    st_s, b_s = refs[pos:pos + 2]
    step = pl.program_id(2)

    @pl.when(step == 0)
    def _():
        for d in range(2):
            st_s[d] = s0[0, d, 0].T if has_init else jnp.zeros((DH_C, DH_C), F32)

    for d, (q_ref, v_ref, lf_ref, kk_ref, o_ref) in enumerate(((qf, vf, lff, kkf, of), (qb, vb, lfb, kkb, ob))):
        o, st_new = _hgrn_unit(q_ref[0], kk_ref[0], v_ref[0], lf_ref[0], st_s[d], b_s.at[d], rev=(d == 1))
        o_ref[0] = o
        st_s[d] = st_new

    if out_state:
        @pl.when(step == pl.num_programs(2) - 1)
        def _():
            for d in range(2):
                s_out[0, d, 0] = st_s[d].T


def _hgrn(q, v, lf, kk, init, out_state):
    b, t, _ = q.shape
    L = SCAN_CHUNK
    nc = t // L
    blk = lambda im: pl.BlockSpec((1, L, DH_C), im)
    fwd = lambda off: (lambda bi, h, c: (bi, c, h + off))
    bwd = lambda off: (lambda bi, h, c: (bi, nc - 1 - c, h + off))
    in_specs = [blk(fwd(0)), blk(fwd(0)), blk(fwd(0)), blk(fwd(0)),
                blk(bwd(0)), blk(bwd(0)), blk(bwd(H_C)), blk(bwd(H_C))]
    args = [q, v, lf, kk, q, v, lf, kk]
    st_spec = pl.BlockSpec((1, 2, 1, DH_C, DH_C), lambda bi, h, c: (bi, 0, h, 0, 0))
    has_init = init is not None
    if has_init:
        in_specs.append(st_spec)
        args.append(init)
    out_specs = [blk(fwd(0)), blk(bwd(0))]
    out_shape = [jax.ShapeDtypeStruct((b, t, W_C), F32)] * 2
    if out_state:
        out_specs.append(st_spec)
        out_shape.append(jax.ShapeDtypeStruct((b, 2, H_C, DH_C, DH_C), F32))
    return pl.pallas_call(
        functools.partial(_hgrn_kernel, has_init=has_init, out_state=out_state),
        grid=(b, H_C, nc),
        in_specs=in_specs,
        out_specs=out_specs,
        out_shape=out_shape,
        scratch_shapes=[pltpu.VMEM((2, DH_C, DH_C), F32), pltpu.VMEM((2, L, DH_C), F32)],
        compiler_params=_cp("parallel", "parallel", "arbitrary"),
        name="hgrn_scan",
    )(*args)


def _out_c_kernel(of_ref, ob_ref, og_ref, x_ref, gate_ref, gh_ref, w_ref, o_ref):
    hm = _head_rms(of_ref[...] + ob_ref[...], gh_ref[...], H_C, DH_C)
    og = og_ref[...]
    parts = [hm[h] * _silu(og[:, h * DH_C:(h + 1) * DH_C]) for h in range(H_C)]
    cat = jnp.concatenate(parts, axis=1).astype(BF16)
    o_ref[...] = x_ref[...] + gate_ref[0] * _dot(cat, w_ref[...])


def _out_c(of, ob, og, x2, gate, g_hgrn, w_out, t, per_batch):
    n = x2.shape[0]
    tm = _row_tile(t)
    row = lambda c: pl.BlockSpec((tm, c), lambda i: (i, 0))
    full = lambda a: pl.BlockSpec(a.shape, lambda i: (0,) * a.ndim)
    return pl.pallas_call(
        _out_c_kernel,
        grid=(n // tm,),
        in_specs=[row(W_C), row(W_C), row(W_C), row(D_MODEL), _mod_spec(per_batch, tm, t),
                  full(g_hgrn), full(w_out)],
        out_specs=row(D_MODEL),
        out_shape=jax.ShapeDtypeStruct((n, D_MODEL), F32),
        compiler_params=_cp("parallel"),
        name="out_proj_odd",
    )(of, ob, og, x2, gate, g_hgrn, w_out)


def _router_kernel(x_ref, g_ref, sh_ref, sc_ref, wr_ref, h_ref, aff_ref):
    h = _rms_rows(x_ref[...], g_ref[...]) * (1.0 + sc_ref[0]) + sh_ref[0]
    h_ref[...] = h.astype(BF16)
    logits = _dot_f32(h, wr_ref[...])
    valid = _iota((1, LANES), 1) < N_EXPERTS
    logits = jnp.where(valid, logits, NEG_BIG)
    e = jnp.exp(logits - jnp.max(logits, axis=-1, keepdims=True))
    e = jnp.where(valid, e, 0.0)
    aff_ref[...] = e / jnp.sum(e, axis=-1, keepdims=True)


def _router(x2, g, shift, scale, w_router_pad, t, per_batch):
    n = x2.shape[0]
    tm = _row_tile(t)
    row = lambda c: pl.BlockSpec((tm, c), lambda i: (i, 0))
    full = lambda a: pl.BlockSpec(a.shape, lambda i: (0,) * a.ndim)
    return pl.pallas_call(
        _router_kernel,
        grid=(n // tm,),
        in_specs=[row(D_MODEL), full(g), _mod_spec(per_batch, tm, t), _mod_spec(per_batch, tm, t),
                  full(w_router_pad)],
        out_specs=[row(D_MODEL), row(LANES)],
        out_shape=[jax.ShapeDtypeStruct((n, D_MODEL), BF16), jax.ShapeDtypeStruct((n, LANES), F32)],
        compiler_params=_cp("parallel"),
        name="moe_router",
    )(x2, g, shift, scale, w_router_pad)


def _topk_kernel(aff_ref, pos_ref, s0_ref, bits_s, sel_s, *, cap, tb):
    t = aff_ref.shape[1]
    nb = t // tb
    bits_s[...] = pltpu.bitcast(aff_ref[0], I32)

    def count_ge(cand):
        def body(j, acc):
            r = pl.multiple_of(j * tb, tb)
            blk = bits_s[pl.ds(r, tb), :]
            return acc + jnp.sum((blk >= cand).astype(F32), axis=0, keepdims=True)
        return lax.fori_loop(0, nb, body, jnp.zeros((1, LANES), F32))

    def bit_body(i, thr):
        cand = thr | lax.shift_left(jnp.int32(1), 30 - i)
        return jnp.where(count_ge(cand) >= cap, cand, thr)

    thr = lax.fori_loop(0, 31, bit_body, jnp.zeros((1, LANES), I32))

    def count_gt(j, acc):
        r = pl.multiple_of(j * tb, tb)
        return acc + jnp.sum((bits_s[pl.ds(r, tb), :] > thr).astype(F32), axis=0, keepdims=True)

    need = cap - lax.fori_loop(0, nb, count_gt, jnp.zeros((1, LANES), F32))
    strict = (_iota((tb, tb), 0) > _iota((tb, tb), 1)).astype(BF16)

    def select(j, carry):
        r = pl.multiple_of(j * tb, tb)
        blk = bits_s[pl.ds(r, tb), :]
        eq = blk == thr
        before = carry + _dot(strict, eq.astype(BF16))
        sel = jnp.logical_or(blk > thr, jnp.logical_and(eq, before < need))
        sel_s[pl.ds(r, tb), :] = sel.astype(F32)
        return carry + jnp.sum(eq.astype(F32), axis=0, keepdims=True)

    lax.fori_loop(0, nb, select, jnp.zeros((1, LANES), F32))

    def place(j, carry):
        r = pl.multiple_of(j * tb, tb)
        sel = sel_s[pl.ds(r, tb), :]
        slot = carry + _dot(strict, sel.astype(BF16))
        pos_ref[0, pl.ds(r, tb), :] = jnp.where(sel > 0.5, slot, -1.0).astype(I32)
        s0_ref[0, pl.ds(j, 1), :] = carry.astype(I32)
        return carry + jnp.sum(sel, axis=0, keepdims=True)

    lax.fori_loop(0, nb, place, jnp.zeros((1, LANES), F32))


def _topk(aff, cap, tb):
    b, t, _ = aff.shape
    nb = t // tb
    return pl.pallas_call(
        functools.partial(_topk_kernel, cap=cap, tb=tb),
        grid=(b,),
        in_specs=[pl.BlockSpec((1, t, LANES), lambda bi: (bi, 0, 0))],
        out_specs=[pl.BlockSpec((1, t, LANES), lambda bi: (bi, 0, 0)),
                   pl.BlockSpec((1, nb, LANES), lambda bi: (bi, 0, 0))],
        out_shape=[jax.ShapeDtypeStruct((b, t, LANES), I32), jax.ShapeDtypeStruct((b, nb, LANES), I32)],
        scratch_shapes=[pltpu.VMEM((t, LANES), I32), pltpu.VMEM((t, LANES), F32)],
        compiler_params=_cp("parallel"),
        name="moe_topk",
    )(aff)


def _window(cap, tb):
    if cap <= tb + BF16_ROWS:
        return cap, True
    return tb + BF16_ROWS, False


def _window_start(s0_ref, bi, j, e, cap, w, whole):
    if whole:
        return 0
    s0 = s0_ref[bi, j, e]
    st = jnp.minimum((s0 // BF16_ROWS) * BF16_ROWS, cap - w)
    return pl.multiple_of(st, BF16_ROWS)


def _gather_kernel(s0_ref, h_ref, pos_ref, o_ref, *, cap, tb):
    bi = pl.program_id(0)
    j = pl.program_id(1)
    w, whole = _window(cap, tb)

    @pl.when(j == 0)
    def _():
        o_ref[...] = jnp.zeros_like(o_ref)

    h = h_ref[0]
    pos = pos_ref[0]
    lane = _iota((tb, w), 1)
    for e in range(N_EXPERTS):
        st = _window_start(s0_ref, bi, j, e, cap, w, whole)
        onehot = ((pos[:, e:e + 1] - st) == lane).astype(BF16)
        rows = _dot_tn(onehot, h).astype(BF16)
        cur = o_ref[0, e, pl.ds(st, w), :]
        o_ref[0, e, pl.ds(st, w), :] = (cur.astype(F32) + rows.astype(F32)).astype(BF16)


def _gather(h2, pos, s0, cap, tb):
    b, t, d = h2.shape
    nb = t // tb
    grid_spec = pltpu.PrefetchScalarGridSpec(
        num_scalar_prefetch=1,
        grid=(b, nb),
        in_specs=[pl.BlockSpec((1, tb, d), lambda bi, j, s: (bi, j, 0)),
                  pl.BlockSpec((1, tb, LANES), lambda bi, j, s: (bi, j, 0))],
        out_specs=pl.BlockSpec((1, N_EXPERTS, cap, d), lambda bi, j, s: (bi, 0, 0, 0)),
    )
    return pl.pallas_call(
        functools.partial(_gather_kernel, cap=cap, tb=tb),
        grid_spec=grid_spec,
        out_shape=jax.ShapeDtypeStruct((b, N_EXPERTS, cap, d), BF16),
        compiler_params=_cp("parallel", "arbitrary"),
        name="moe_gather",
    )(s0, h2, pos)


def _expert_kernel(x_ref, wg_ref, wu_ref, wd_ref, y_ref, wg_s, wu_s, wd_s):
    @pl.when(pl.program_id(1) == 0)
    def _():
        wg_s[...] = wg_ref[0].astype(BF16)
        wu_s[...] = wu_ref[0].astype(BF16)
        wd_s[...] = wd_ref[0].astype(BF16)

    shp = x_ref.shape
    x = x_ref[...].reshape(shp[0] * shp[2], shp[3])
    a = _dot(x, wg_s[...])
    u = _dot(x, wu_s[...])
    hm = (_silu(a) * u).astype(BF16)
    y_ref[...] = _dot(hm, wd_s[...]).astype(BF16).reshape(shp)


def _experts(xin, w_gate, w_up, w_down):
    b, e, cap, d = xin.shape
    f = w_gate.shape[-1]
    bb = max(1, min(b, 512 // cap))
    xs = pl.BlockSpec((bb, 1, cap, d), lambda ei, bi: (bi, ei, 0, 0))
    ws = lambda a: pl.BlockSpec((1,) + a.shape[1:], lambda ei, bi: (ei, 0, 0))
    return pl.pallas_call(
        _expert_kernel,
        grid=(e, b // bb),
        in_specs=[xs, ws(w_gate), ws(w_up), ws(w_down)],
        out_specs=xs,
        out_shape=jax.ShapeDtypeStruct(xin.shape, BF16),
        scratch_shapes=[pltpu.VMEM((d, f), BF16), pltpu.VMEM((d, f), BF16), pltpu.VMEM((f, d), BF16)],
        compiler_params=_cp("parallel", "arbitrary"),
        name="moe_experts",
    )(xin, w_gate, w_up, w_down)


def _combine_kernel(*refs, cap, tb, final):
    if final:
        s0_ref, y_ref, pos_ref, aff_ref, x_ref, gate_ref, gf_ref, o_ref = refs
    else:
        s0_ref, y_ref, pos_ref, aff_ref, x_ref, gate_ref, o_ref = refs
    bi = pl.program_id(0)
    j = pl.program_id(1)
    w, whole = _window(cap, tb)
    pos = pos_ref[0]
    aff = aff_ref[0]
    lane = _iota((tb, w), 1)
    acc = jnp.zeros((tb, D_MODEL), F32)
    for e in range(N_EXPERTS):
        st = _window_start(s0_ref, bi, j, e, cap, w, whole)
        onehot = ((pos[:, e:e + 1] - st) == lane).astype(BF16)
        acc = acc + aff[:, e:e + 1] * _dot(onehot, y_ref[0, e, pl.ds(st, w), :])
    out = x_ref[0] + gate_ref[0] * acc
    if final:
        out = _rms_rows(out, gf_ref[...])
    o_ref[0] = out


def _combine(y, pos, aff, x3, gate, s0, cap, tb, per_batch, g_final):
    b, t, d = x3.shape
    nb = t // tb
    final = g_final is not None
    tok = lambda c: pl.BlockSpec((1, tb, c), lambda bi, j, s: (bi, j, 0))
    in_specs = [pl.BlockSpec((1, N_EXPERTS, cap, d), lambda bi, j, s: (bi, 0, 0, 0)),
                tok(LANES), tok(LANES), tok(d),
                pl.BlockSpec((1, 1, d), (lambda bi, j, s: (bi, 0, 0)) if per_batch else (lambda bi, j, s: (0, 0, 0)))]
    args = [y, pos, aff, x3, gate]
    if final:
        in_specs.append(pl.BlockSpec(g_final.shape, lambda bi, j, s: (0, 0)))
        args.append(g_final)
    grid_spec = pltpu.PrefetchScalarGridSpec(
        num_scalar_prefetch=1, grid=(b, nb), in_specs=in_specs, out_specs=tok(d))
    return pl.pallas_call(
        functools.partial(_combine_kernel, cap=cap, tb=tb, final=final),
        grid_spec=grid_spec,
        out_shape=jax.ShapeDtypeStruct((b, t, d), F32),
        compiler_params=_cp("parallel", "arbitrary"),
        name="moe_combine",
    )(s0, *args)


def _moe(x2, b, t, g, shift, scale, gate, w_router_pad, w_gate, w_up, w_down, per_batch, g_final):
    cap = CAPACITY_FACTOR * t // N_EXPERTS
    tb = min(MOE_TOKEN_BLOCK, t)
    h2, aff = _router(x2, g, shift, scale, w_router_pad, t, per_batch)
    aff3 = aff.reshape(b, t, LANES)
    pos, s0 = _topk(aff3, cap, tb)
    s0 = s0[:, :, :N_EXPERTS]
    xin = _gather(h2.reshape(b, t, D_MODEL), pos, s0, cap, tb)
    y = _experts(xin, w_gate, w_up, w_down)
    out = _combine(y, pos, aff3, x2.reshape(b, t, D_MODEL), gate, s0, cap, tb, per_batch, g_final)
    return out.reshape(b * t, D_MODEL)


def _rope_tables(t):
    rows = t // GRID_W
    row = jnp.repeat(jnp.arange(rows, dtype=F32), GRID_W)
    col = jnp.tile(jnp.arange(GRID_W, dtype=F32), rows)
    n_freq = DQK_A // 4
    inv_freq = jnp.power(ROPE_BASE, -jnp.arange(n_freq, dtype=F32) / n_freq)
    ang = jnp.concatenate([row[:, None] * inv_freq, col[:, None] * inv_freq], axis=-1)
    cos, sin = jnp.cos(ang), jnp.sin(ang)
    zero = jnp.zeros_like(sin)
    reps = LANES // DQK_A
    cos_t = jnp.tile(jnp.concatenate([cos, cos], axis=-1), (1, reps))
    sin_a = jnp.tile(jnp.concatenate([-sin, zero], axis=-1), (1, reps))
    sin_b = jnp.tile(jnp.concatenate([zero, sin], axis=-1), (1, reps))
    return cos_t, sin_a, sin_b


def _stream(x, mods, per_batch, rope, cache, p):
    b, t, d = x.shape
    x2 = x.reshape(b * t, d)
    depth = p["g_norm"].shape[0]
    even_out, odd_out = [], []
    for l in range(depth):
        j = l // 2
        sh1, sc1, gt1, sh2, sc2, gt2 = mods[l]
        g1 = p["g_norm"][l, 0].reshape(1, d)
        g2 = p["g_norm"][l, 1].reshape(1, d)
        if l % 2 == 0:
            aq, ak, av, bqk, bv, bo, gates = _in_ab(x2, g1, sh1, sc1, p["w_in_ab_main"][j], p["w_in_ab_gate"][j],
                                                     rope, t, per_batch)
            r3 = lambda a: a.reshape(b, t, a.shape[-1])
            if cache is not None:
                kc = cache[0][:, j].reshape(b, -1, W_A)
                vc = cache[1][:, j].reshape(b, -1, W_A)
                nu = 2 * H_B
                init = (cache[2][:, j].reshape(b, nu, DH_B, DH_B),
                        cache[3][:, j].reshape(b, nu, 1, DH_B),
                        jnp.broadcast_to(cache[4][:, j].reshape(b, nu, 1, 1), (b, nu, 1, LANES)))
            else:
                kc = vc = init = None
            lam_init = 0.8 - 0.6 * math.exp(-0.3 * l)
            attn = _attention(r3(aq), r3(ak), r3(av), kc, vc, p["lam_ab"][j], lam_init)
            qk = _conv_silu(r3(bqk), p["conv_ab"][j])
            bias = jnp.pad(jnp.concatenate([p["b_ig_ab"][j].reshape(-1), p["b_fg_ab"][j].reshape(-1)]),
                           (0, LANES - 4 * H_B)).reshape(1, LANES)
            res = _mlstm(qk, r3(bv), r3(gates), bias, init, cache is None)
            hf, hb = res[0], res[1]
            if cache is None:
                even_out.append((ak, av, res[2], res[3], res[4]))
            x2 = _out_ab(attn.reshape(b * t, W_A), hf.reshape(b * t, W_B), hb.reshape(b * t, W_B), bo, x2, gt1,
                         p["g_attn_ab"][j].reshape(1, W_A), p["g_mlstm_ab"][j].reshape(1, W_B),
                         p["w_out_ab_bf16"][j], t, per_batch, 1.0 - lam_init)
        else:
            q, v, og, lf, kk = _in_c(x2, g1, sh1, sc1, p["w_in_c_bf16"][j], p["b_f_c"][j], p["gamma_lb"], l, t,
                                     per_batch)
            r3 = lambda a: a.reshape(b, t, a.shape[-1])
            init = None if cache is None else cache[5][:, j]
            res = _hgrn(r3(q), r3(v), r3(lf), r3(kk), init, cache is None)
            if cache is None:
                odd_out.append(res[2])
            x2 = _out_c(res[0].reshape(b * t, W_C), res[1].reshape(b * t, W_C), og, x2, gt1,
                        p["g_hgrn_c"][j].reshape(1, W_C), p["w_out_c_bf16"][j], t, per_batch)
        g_final = p["g_final"].reshape(1, d) if l == depth - 1 else None
        x2 = _moe(x2, b, t, g2, sh2, sc2, gt2, p["w_router_pad"][l], p["w_gate_e"][l], p["w_up_e"][l],
                  p["w_down_e"][l], per_batch, g_final)
    return x2.reshape(b, t, d), even_out, odd_out


def kernel(x_prompt, x_sample, cache_dattn_k, cache_dattn_v, state_mlstm_C, state_mlstm_n, state_mlstm_m,
           state_hgrn_S, c, c_ctx, w_ada, b_ada, g_norm, g_final, w_in_ab, b_ig_ab, b_fg_ab, lam_ab, conv_ab,
           g_attn_ab, g_mlstm_ab, w_out_ab, w_in_c, b_f_c, gamma_lb, g_hgrn_c, w_out_c,
           w_router, w_gate_e, w_up_e, w_down_e):
    d = D_MODEL
    bs = c.shape[0]
    bp, tp, _ = x_prompt.shape
    depth = w_ada.shape[0]
    n_main = 3 * W_A + 4 * W_B
    p = dict(
        g_norm=g_norm, g_final=g_final, b_ig_ab=b_ig_ab, b_fg_ab=b_fg_ab, lam_ab=lam_ab, conv_ab=conv_ab,
        g_attn_ab=g_attn_ab, g_mlstm_ab=g_mlstm_ab, b_f_c=b_f_c, gamma_lb=gamma_lb, g_hgrn_c=g_hgrn_c,
        w_gate_e=w_gate_e, w_up_e=w_up_e, w_down_e=w_down_e,
        w_in_ab_main=w_in_ab[:, :, :n_main].astype(BF16),
        w_in_ab_gate=jnp.pad(w_in_ab[:, :, n_main:], ((0, 0), (0, 0), (0, LANES - 4 * H_B))).astype(BF16),
        w_out_ab_bf16=w_out_ab.astype(BF16),
        w_in_c_bf16=w_in_c.astype(BF16),
        w_out_c_bf16=w_out_c.astype(BF16),
        w_router_pad=jnp.pad(w_router, ((0, 0), (0, 0), (0, LANES - N_EXPERTS))),
    )
    rows = -(-(bs + 1) // 8) * 8
    cond = jnp.concatenate([c, c_ctx.reshape(1, d), jnp.zeros((rows - bs - 1, d), F32)], axis=0)
    mods = _ada(cond, w_ada, b_ada)
    mods_s = [[mods[l, :bs, i * d:(i + 1) * d].reshape(bs, 1, d) for i in range(6)] for l in range(depth)]
    mods_c = [[mods[l, bs:bs + 1, i * d:(i + 1) * d].reshape(1, 1, d) for i in range(6)] for l in range(depth)]

    y_prompt, ev, od = _stream(x_prompt, mods_c, False, None, None, p)
    new_k = jnp.stack([e[0].reshape(bp, tp, H_A, 2 * DQK_A) for e in ev], axis=1)
    new_v = jnp.stack([e[1].reshape(bp, tp, H_A, DV_A) for e in ev], axis=1)
    new_c = jnp.stack([e[2].reshape(bp, 2, H_B, DH_B, DH_B) for e in ev], axis=1)
    new_n = jnp.stack([e[3].reshape(bp, 2, H_B, DH_B) for e in ev], axis=1)
    new_m = jnp.stack([e[4][..., 0].reshape(bp, 2, H_B) for e in ev], axis=1)
    new_s = jnp.stack(od, axis=1)

    cache = (cache_dattn_k, cache_dattn_v, state_mlstm_C, state_mlstm_n, state_mlstm_m, state_hgrn_S)
    y_sample, _, _ = _stream(x_sample, mods_s, True, _rope_tables(x_sample.shape[1]), cache, p)
    return (y_prompt, y_sample, new_k, new_v, new_c, new_n, new_m, new_s)
```

```python
import functools
import math

import jax
import jax.numpy as jnp
from jax import lax
from jax.experimental import pallas as pl
from jax.experimental.pallas import tpu as pltpu

F32 = jnp.float32
BF16 = jnp.bfloat16
I32 = jnp.int32

D_MODEL = 1024
H_A = 4
DV_A = 128
DQK_A = 64
W_A = H_A * DV_A
H_B = 4
DH_B = 128
W_B = H_B * DH_B
H_C = 8
DH_C = 128
W_C = H_C * DH_C
N_EXPERTS = 16
CAPACITY_FACTOR = 2
GRID_W = 64
ROPE_BASE = 10000.0
RMS_EPS = 1e-6
NEG_BIG = -1e30
LOG2E = 1.4426950408889634
LANES = 128
BF16_ROWS = 16
SCAN_CHUNK = 128
MOE_TOKEN_BLOCK = 128
VMEM_LIMIT = 56 * 1024 * 1024


def _cp(*sem):
    return pltpu.CompilerParams(dimension_semantics=sem, vmem_limit_bytes=VMEM_LIMIT)


def _dot(a, b):
    return jnp.dot(a, b, preferred_element_type=F32)


def _dot_nt(a, b):
    return lax.dot_general(a, b, (((1,), (1,)), ((), ())), preferred_element_type=F32)


def _dot_tn(a, b):
    return lax.dot_general(a, b, (((0,), (0,)), ((), ())), preferred_element_type=F32)


def _split3(x):
    hi = x.astype(BF16)
    r = x - hi.astype(F32)
    mid = r.astype(BF16)
    lo = (r - mid.astype(F32)).astype(BF16)
    return hi, mid, lo


def _dot_sel_left(sel, x):
    hi, mid, lo = _split3(x)
    return _dot(sel, hi) + _dot(sel, mid) + _dot(sel, lo)


def _dot_sel_right(x, sel):
    hi, mid, lo = _split3(x)
    return _dot(hi, sel) + _dot(mid, sel) + _dot(lo, sel)


def _dot_f32(a, b):
    ah, am, al = _split3(a)
    bh, bm, bl = _split3(b)
    return (_dot(ah, bh) + (_dot(ah, bm) + _dot(am, bh))
            + (_dot(ah, bl) + _dot(am, bm) + _dot(al, bh)))


def _sigmoid(x):
    return 1.0 / (1.0 + jnp.exp(-x))


def _silu(x):
    return x * _sigmoid(x)


def _log_sigmoid(x):
    return jnp.minimum(x, 0.0) - jnp.log(1.0 + jnp.exp(-jnp.abs(x)))


def _rms_rows(x, g):
    return x * lax.rsqrt(jnp.mean(x * x, axis=-1, keepdims=True) + RMS_EPS) * g


def _iota(shape, dim):
    return lax.broadcasted_iota(I32, shape, dim)


def _ada_kernel(c_ref, w_ref, b_ref, o_ref):
    c = c_ref[...]
    o_ref[0] = _dot_f32(_silu(c), w_ref[0]) + b_ref[0]


def _ada(cond_rows, w_ada, b_ada):
    depth, d, n6 = w_ada.shape
    rows = cond_rows.shape[0]
    tn = 1536
    return pl.pallas_call(
        _ada_kernel,
        grid=(depth, n6 // tn),
        in_specs=[pl.BlockSpec((rows, d), lambda l, j: (0, 0)),
                  pl.BlockSpec((1, d, tn), lambda l, j: (l, 0, j)),
                  pl.BlockSpec((1, 1, tn), lambda l, j: (l, 0, j))],
        out_specs=pl.BlockSpec((1, rows, tn), lambda l, j: (l, 0, j)),
        out_shape=jax.ShapeDtypeStruct((depth, rows, n6), F32),
        compiler_params=_cp("parallel", "parallel"),
        name="adaln",
    )(cond_rows, w_ada, b_ada.reshape(depth, 1, n6))


def _mod_spec(per_batch, tm, t):
    if per_batch:
        return pl.BlockSpec((1, 1, D_MODEL), lambda i: (i * tm // t, 0, 0))
    return pl.BlockSpec((1, 1, D_MODEL), lambda i: (0, 0, 0))


def _row_tile(t):
    return min(512, t)


def _in_ab_kernel(*refs, use_rope):
    if use_rope:
        (x_ref, g_ref, sh_ref, sc_ref, w_ref, wg_ref, cos_ref, sa_ref, sb_ref,
         aq_ref, ak_ref, av_ref, bqk_ref, bv_ref, bo_ref, gt_ref) = refs
    else:
        (x_ref, g_ref, sh_ref, sc_ref, w_ref, wg_ref,
         aq_ref, ak_ref, av_ref, bqk_ref, bv_ref, bo_ref, gt_ref) = refs
    h = _rms_rows(x_ref[...], g_ref[...]) * (1.0 + sc_ref[0]) + sh_ref[0]
    hb = h.astype(BF16)
    y = _dot(hb, w_ref[...])
    gt_ref[...] = _dot(hb, wg_ref[...])
    aq = y[:, 0:W_A] * (DQK_A ** -0.5)
    ak = y[:, W_A:2 * W_A]
    if use_rope:
        cos = jnp.concatenate([cos_ref[...]] * H_A, axis=1)
        sa = jnp.concatenate([sa_ref[...]] * H_A, axis=1)
        sb = jnp.concatenate([sb_ref[...]] * H_A, axis=1)
        half = DQK_A // 2

        def rope(v):
            return (v * cos + pltpu.roll(v, W_A - half, axis=1) * sa
                    + pltpu.roll(v, half, axis=1) * sb)

        aq = rope(aq)
        ak = rope(ak)
    aq_ref[...] = aq.astype(aq_ref.dtype)
    ak_ref[...] = ak.astype(ak_ref.dtype)
    av_ref[...] = y[:, 2 * W_A:3 * W_A].astype(av_ref.dtype)
    bqk_ref[...] = y[:, 3 * W_A:3 * W_A + 2 * W_B]
    bv_ref[...] = y[:, 3 * W_A + 2 * W_B:3 * W_A + 3 * W_B]
    bo_ref[...] = y[:, 3 * W_A + 3 * W_B:3 * W_A + 4 * W_B]


def _in_ab(x2, g, shift, scale, w_main, w_gate, rope, t, per_batch, qkv_dtype):
    n = x2.shape[0]
    tm = _row_tile(t)
    use_rope = rope is not None
    row = lambda c: pl.BlockSpec((tm, c), lambda i: (i, 0))
    full = lambda a: pl.BlockSpec(a.shape, lambda i: (0,) * a.ndim)
    in_specs = [row(D_MODEL), full(g), _mod_spec(per_batch, tm, t), _mod_spec(per_batch, tm, t),
                full(w_main), full(w_gate)]
    args = [x2, g, shift, scale, w_main, w_gate]
    if use_rope:
        nt = t // tm
        tab = pl.BlockSpec((tm, LANES), lambda i: (i % nt, 0))
        in_specs += [tab, tab, tab]
        args += list(rope)
    widths = (W_A, W_A, W_A, 2 * W_B, W_B, W_B, LANES)
    return pl.pallas_call(
        functools.partial(_in_ab_kernel, use_rope=use_rope),
        grid=(n // tm,),
        in_specs=in_specs,
        out_specs=[row(c) for c in widths],
        out_shape=[jax.ShapeDtypeStruct((n, c), qkv_dtype if i < 3 else F32) for i, c in enumerate(widths)],
        compiler_params=_cp("parallel"),
        name="in_proj_even",
    )(*args)


def _attn_kernel(*refs, has_cache, tk, lam_init):
    if has_cache:
        q_ref, k_ref, v_ref, kc_ref, vc_ref, lam_ref, o_ref = refs[:7]
    else:
        q_ref, k_ref, v_ref, lam_ref, o_ref = refs[:5]
    s_s = refs[-2] if has_cache else refs[-1]
    sc_s = refs[-1] if has_cache else None
    q = q_ref[0]
    tq = q.shape[0]
    first = _iota((1, LANES), 1) < DQK_A
    qs = (jnp.where(first, q, 0).astype(BF16), jnp.where(first, 0, q).astype(BF16))
    nkb = k_ref.shape[1] // tk

    def groups(x):
        return [x[:, g * LANES:(g + 1) * LANES] for g in range(x.shape[1] // LANES)]

    mpart = [jnp.full((tq, LANES), NEG_BIG, F32)] * 2
    if has_cache:
        kcb = kc_ref[0].astype(BF16)
        for i in range(2):
            s = _dot_nt(qs[i], kcb)
            sc_s[i] = s
            mpart[i] = functools.reduce(jnp.maximum, groups(s), mpart[i])

    def pass1(j, mp):
        r = pl.multiple_of(j * tk, tk)
        kb = k_ref[0, pl.ds(r, tk), :].astype(BF16)
        out = []
        for i in range(2):
            s = _dot_nt(qs[i], kb)
            s_s[i, j] = s
            out.append(functools.reduce(jnp.maximum, groups(s), mp[i]))
        return tuple(out)

    mpart = lax.fori_loop(0, nkb, pass1, tuple(mpart), unroll=True)
    m = [jnp.max(mp, axis=-1, keepdims=True) for mp in mpart]

    def accumulate(s, vb, lp, acc, mi):
        p = jnp.exp(s - mi)
        return functools.reduce(jnp.add, groups(p), lp), acc + _dot(p.astype(BF16), vb)

    lpart = [jnp.zeros((tq, LANES), F32)] * 2
    acc = [jnp.zeros((tq, DV_A), F32)] * 2
    if has_cache:
        vcb = vc_ref[0].astype(BF16)
        for i in range(2):
            lpart[i], acc[i] = accumulate(sc_s[i], vcb, lpart[i], acc[i], m[i])

    def pass2(j, carry):
        r = pl.multiple_of(j * tk, tk)
        vb = v_ref[0, pl.ds(r, tk), :].astype(BF16)
        out = []
        for i in range(2):
            out.append(accumulate(s_s[i, j], vb, carry[i][0], carry[i][1], m[i]))
        return tuple(out)

    (l1, a1), (l2, a2) = lax.fori_loop(0, nkb, pass2, tuple(zip(lpart, acc)), unroll=True)
    l1 = jnp.sum(l1, axis=-1, keepdims=True)
    l2 = jnp.sum(l2, axis=-1, keepdims=True)
    lp = lam_ref[...]
    lam = (jnp.exp(jnp.sum(lp[0:1] * lp[1:2], axis=-1, keepdims=True))
           - jnp.exp(jnp.sum(lp[2:3] * lp[3:4], axis=-1, keepdims=True)) + lam_init)
    o_ref[0] = a1 / l1 - lam * (a2 / l2)


def _attention(aq, ak, av, kc, vc, lam_p, lam_init):
    b, t, _ = aq.shape
    tq = min(256, t)
    tk = min(512, t)
    has_cache = kc is not None
    qs = pl.BlockSpec((1, tq, DV_A), lambda bi, h, i: (bi, i, h))
    ks = pl.BlockSpec((1, t, DV_A), lambda bi, h, i: (bi, 0, h))
    in_specs = [qs, ks, ks]
    args = [aq, ak, av]
    if has_cache:
        cs = pl.BlockSpec((1, kc.shape[1], DV_A), lambda bi, h, i: (bi, 0, h))
        in_specs += [cs, cs]
        args += [kc, vc]
    in_specs.append(pl.BlockSpec(lam_p.shape, lambda bi, h, i: (0, 0)))
    args.append(lam_p)
    scratch = [pltpu.VMEM((2, t // tk, tq, tk), F32)]
    if has_cache:
        scratch.append(pltpu.VMEM((2, tq, kc.shape[1]), F32))
    return pl.pallas_call(
        functools.partial(_attn_kernel, has_cache=has_cache, tk=tk, lam_init=lam_init),
        grid=(b, H_A, t // tq),
        in_specs=in_specs,
        out_specs=qs,
        out_shape=jax.ShapeDtypeStruct((b, t, W_A), F32),
        scratch_shapes=scratch,
        compiler_params=_cp("parallel", "parallel", "parallel"),
        name="diff_attention",
    )(*args)


def _conv_kernel(x_ref, w_ref, o_ref):
    x = x_ref[0]
    t = x.shape[0]
    w = w_ref[...]
    r = _iota((t, 1), 0)
    prev = jnp.where(r == 0, 0.0, pltpu.roll(x, 1, axis=0))
    nxt = jnp.where(r == t - 1, 0.0, pltpu.roll(x, t - 1, axis=0))
    o_ref[0] = _silu(prev * w[0:1] + x * w[1:2] + nxt * w[2:3])


def _conv_silu(bqk, conv_w):
    b, t, c = bqk.shape
    tc = 256
    spec = pl.BlockSpec((1, t, tc), lambda bi, j: (bi, 0, j))
    return pl.pallas_call(
        _conv_kernel,
        grid=(b, c // tc),
        in_specs=[spec, pl.BlockSpec((conv_w.shape[0], tc), lambda bi, j: (0, j))],
        out_specs=spec,
        out_shape=jax.ShapeDtypeStruct(bqk.shape, F32),
        compiler_params=_cp("parallel", "parallel"),
        name="mlstm_conv",
    )(bqk, conv_w)


def _tri(n, lower):
    r = _iota((n, n), 0)
    c = _iota((n, n), 1)
    return r >= c if lower else r <= c


def _mlstm_kernel(*refs, has_init, out_state):
    qf, kf, vf, gf, qb, kb, vb, gb, bias = refs[:9]
    pos = 9
    if has_init:
        c0, n0, m0 = refs[pos:pos + 3]
        pos += 3
    hf, hb = refs[pos:pos + 2]
    pos += 2
    if out_state:
        c_out, n_out, m_out = refs[pos:pos + 3]
        pos += 3
    c_s, n_s, m_s = refs[pos:pos + 3]
    step = pl.program_id(1)
    L = SCAN_CHUNK

    @pl.when(step == 0)
    def _():
        if has_init:
            c_s[...] = c0[0]
            n_s[...] = n0[0]
            m_s[...] = m0[0]
        else:
            c_s[...] = jnp.zeros_like(c_s)
            n_s[...] = jnp.zeros_like(n_s)
            m_s[...] = jnp.zeros_like(m_s)

    lower = _tri(L, True)
    upper = _tri(L, False)
    lower_b = lower.astype(BF16)
    upper_b = upper.astype(BF16)
    col = _iota((1, LANES), 1)
    for d, (q_ref, k_ref, v_ref, g_ref, h_ref) in enumerate(((qf, kf, vf, gf, hf), (qb, kb, vb, gb, hb))):
        g = g_ref[0] + bias[...]
        g = jnp.where(col < 2 * H_B, g, _log_sigmoid(g))
        gt = g.T
        if d == 0:
            bc_all = _dot_sel_left(lower_b, g)
            br_all = _dot_sel_right(gt, upper_b)
            mask = lower
        else:
            bc_all = _dot_sel_left(upper_b, g)
            br_all = _dot_sel_right(gt, lower_b)
            mask = upper
        for h in range(H_B):
            u = d * H_B + h
            ji = u
            jf = 2 * H_B + u
            sl = slice(h * DH_B, (h + 1) * DH_B)
            q = q_ref[0, :, sl]
            k = k_ref[0, :, sl] * (DH_B ** -0.5)
            v = v_ref[0, :, sl]
            li_col = g[:, ji:ji + 1]
            li_row = gt[ji:ji + 1, :]
            b_col = bc_all[:, jf:jf + 1]
            b_row = br_all[jf:jf + 1, :]
            m_prev = m_s[u][:, 0:1]
            a_col = b_col + m_prev
            dm = jnp.where(mask, b_col - b_row + li_row, NEG_BIG)
            m_t = jnp.maximum(a_col, jnp.max(dm, axis=-1, keepdims=True))
            w_inter = jnp.exp(a_col - m_t)
            qb16 = q.astype(BF16)
            vb16 = v.astype(BF16)
            s = _dot_nt(qb16, k.astype(BF16)) * jnp.exp(dm - m_t)
            c_old = c_s[u]
            n_old = n_s[u]
            num = w_inter * _dot(qb16, c_old.astype(BF16)) + _dot(s.astype(BF16), vb16)
            qn = jnp.sum(qb16.astype(F32) * n_old.astype(BF16).astype(F32), axis=-1, keepdims=True)
            den = w_inter * qn + jnp.sum(s, axis=-1, keepdims=True)
            h_ref[0, :, sl] = num / jnp.maximum(jnp.abs(den), jnp.exp(-m_t))
            b_end = b_col[L - 1:L, :] if d == 0 else b_col[0:1, :]
            g_col = b_end - b_col + li_col
            m_new = jnp.maximum(b_end + m_prev, jnp.max(g_col, axis=0, keepdims=True))
            w_old = jnp.exp(b_end + m_prev - m_new)
            ks = jnp.exp(g_col - m_new) * k
            ksb = ks.astype(BF16)
            c_s[u] = w_old * c_old + _dot_tn(ksb, vb16)
            n_s[u] = w_old * n_old + jnp.sum(ksb.astype(F32), axis=0, keepdims=True)
            m_s[u] = jnp.broadcast_to(m_new, (1, LANES))

    if out_state:
        @pl.when(step == pl.num_programs(1) - 1)
        def _():
            c_out[0] = c_s[...]
            n_out[0] = n_s[...]
            m_out[0] = m_s[...]


def _mlstm(qk, v, gates, bias, init, out_state):
    b, t, _ = v.shape
    L = SCAN_CHUNK
    nc = t // L
    nu = 2 * H_B
    fwd = lambda j: (lambda bi, c: (bi, c, j))
    bwd = lambda j: (lambda bi, c: (bi, nc - 1 - c, j))
    blk = lambda w, im: pl.BlockSpec((1, L, w), im)
    in_specs = [blk(W_B, fwd(0)), blk(W_B, fwd(1)), blk(W_B, fwd(0)), blk(LANES, fwd(0)),
                blk(W_B, bwd(0)), blk(W_B, bwd(1)), blk(W_B, bwd(0)), blk(LANES, bwd(0)),
                pl.BlockSpec((1, LANES), lambda bi, c: (0, 0))]
    args = [qk, qk, v, gates, qk, qk, v, gates, bias]
    st_specs = [pl.BlockSpec((1, nu, DH_B, DH_B), lambda bi, c: (bi, 0, 0, 0)),
                pl.BlockSpec((1, nu, 1, DH_B), lambda bi, c: (bi, 0, 0, 0)),
                pl.BlockSpec((1, nu, 1, LANES), lambda bi, c: (bi, 0, 0, 0))]
    st_shapes = [jax.ShapeDtypeStruct((b, nu, DH_B, DH_B), F32),
                 jax.ShapeDtypeStruct((b, nu, 1, DH_B), F32),
                 jax.ShapeDtypeStruct((b, nu, 1, LANES), F32)]
    has_init = init is not None
    if has_init:
        in_specs += st_specs
        args += list(init)
    out_specs = [blk(W_B, fwd(0)), blk(W_B, bwd(0))]
    out_shape = [jax.ShapeDtypeStruct((b, t, W_B), F32)] * 2
    if out_state:
        out_specs += st_specs
        out_shape += st_shapes
    return pl.pallas_call(
        functools.partial(_mlstm_kernel, has_init=has_init, out_state=out_state),
        grid=(b, nc),
        in_specs=in_specs,
        out_specs=out_specs,
        out_shape=out_shape,
        scratch_shapes=[pltpu.VMEM((nu, DH_B, DH_B), F32), pltpu.VMEM((nu, 1, DH_B), F32),
                        pltpu.VMEM((nu, 1, LANES), F32)],
        compiler_params=_cp("parallel", "arbitrary"),
        name="mlstm_scan",
    )(*args)


def _head_rms(x, g, heads, width):
    parts = []
    for h in range(heads):
        sl = slice(h * width, (h + 1) * width)
        parts.append(_rms_rows(x[:, sl], g[:, sl]))
    return parts


def _out_ab_kernel(at_ref, hf_ref, hb_ref, bo_ref, x_ref, gate_ref, ga_ref, gm_ref, w_ref, o_ref, *, out_scale):
    oa = [p * out_scale for p in _head_rms(at_ref[...], ga_ref[...], H_A, DV_A)]
    hm = _head_rms(hf_ref[...] + hb_ref[...], gm_ref[...], H_B, DH_B)
    bo = bo_ref[...]
    ob = [_sigmoid(bo[:, h * DH_B:(h + 1) * DH_B]) * hm[h] for h in range(H_B)]
    cat = jnp.concatenate(oa + ob, axis=1).astype(BF16)
    o_ref[...] = x_ref[...] + gate_ref[0] * _dot(cat, w_ref[...])


def _out_ab(attn, hf, hb, bo, x2, gate, g_attn, g_mlstm, w_out, t, per_batch, out_scale):
    n = x2.shape[0]
    tm = _row_tile(t)
    row = lambda c: pl.BlockSpec((tm, c), lambda i: (i, 0))
    full = lambda a: pl.BlockSpec(a.shape, lambda i: (0,) * a.ndim)
    return pl.pallas_call(
        functools.partial(_out_ab_kernel, out_scale=out_scale),
        grid=(n // tm,),
        in_specs=[row(W_A), row(W_B), row(W_B), row(W_B), row(D_MODEL), _mod_spec(per_batch, tm, t),
                  full(g_attn), full(g_mlstm), full(w_out)],
        out_specs=row(D_MODEL),
        out_shape=jax.ShapeDtypeStruct((n, D_MODEL), F32),
        compiler_params=_cp("parallel"),
        name="out_proj_even",
    )(attn, hf, hb, bo, x2, gate, g_attn, g_mlstm, w_out)


def _in_c_kernel(x_ref, g_ref, sh_ref, sc_ref, w_ref, bf_ref, gam_ref, q_ref, v_ref, og_ref, lf_ref, kk_ref, *, layer):
    h = _rms_rows(x_ref[...], g_ref[...]) * (1.0 + sc_ref[0]) + sh_ref[0]
    y = _dot(h.astype(BF16), w_ref[...])
    q_ref[...] = y[:, 0:W_C]
    v_ref[...] = y[:, W_C:2 * W_C]
    og_ref[...] = y[:, 2 * W_C:3 * W_C]
    gam = gam_ref[...]
    e = jnp.exp(gam - jnp.max(gam, axis=0, keepdims=True))
    p = e / jnp.sum(e, axis=0, keepdims=True)
    lbs = p[0:1]
    lbs0 = lbs
    for i in range(1, layer + 1):
        lbs = lbs + p[i:i + 1]
    lb = jnp.maximum(lbs - lbs0, 0.0)
    L = SCAN_CHUNK
    for d in range(2):
        f = y[:, (3 + d) * W_C:(4 + d) * W_C] + bf_ref[d:d + 1]
        u = jnp.exp(-jnp.abs(f))
        neg = f < 0.0
        lf2 = (jnp.log(jnp.where(neg, lb + u, 1.0 + lb * u)) - jnp.log(1.0 + u)) * LOG2E
        kk_ref[:, d * W_C:(d + 1) * W_C] = (1.0 - lb) * jnp.where(neg, 1.0, u) / (1.0 + u)
        tri = _tri(L, d == 0).astype(BF16)
        for c0 in range(0, lf2.shape[0], L):
            lf_ref[c0:c0 + L, d * W_C:(d + 1) * W_C] = _dot_sel_left(tri, lf2[c0:c0 + L])


def _in_c(x2, g, shift, scale, w_in, b_f, gamma_lb, layer, t, per_batch):
    n = x2.shape[0]
    tm = _row_tile(t)
    row = lambda c: pl.BlockSpec((tm, c), lambda i: (i, 0))
    full = lambda a: pl.BlockSpec(a.shape, lambda i: (0,) * a.ndim)
    widths = (W_C, W_C, W_C, 2 * W_C, 2 * W_C)
    return pl.pallas_call(
        functools.partial(_in_c_kernel, layer=layer),
        grid=(n // tm,),
        in_specs=[row(D_MODEL), full(g), _mod_spec(per_batch, tm, t), _mod_spec(per_batch, tm, t),
                  full(w_in), full(b_f), full(gamma_lb)],
        out_specs=[row(c) for c in widths],
        out_shape=[jax.ShapeDtypeStruct((n, c), F32) for c in widths],
        compiler_params=_cp("parallel"),
        name="in_proj_odd",
    )(x2, g, shift, scale, w_in, b_f, gamma_lb)


HGRN_LEVELS = (8, 16, 32, 64)
HGRN_DIAG = 8
HGRN_HEADS_PER_STEP = 4


def _hgrn_unit(q, kk, v, b, st, b_s, rev):
    L = SCAN_CHUNK
    b_s[...] = b
    row = _iota((L, 1), 0)
    col = _iota((1, L), 1)
    k16 = kk.astype(BF16)
    v16 = v.astype(BF16)
    o = _dot_nt((q * jnp.exp2(b)).astype(BF16), st.astype(BF16))
    c = HGRN_DIAG
    sub = _iota((c, 1), 0)
    ys = []
    for blk in range(L // c):
        r0 = blk * c
        qb = q[r0:r0 + c]
        bb = b[r0:r0 + c]
        for s in range(c):
            keep = (sub <= s) if rev else (sub >= s)
            ys.append(qb * jnp.exp2(jnp.where(keep, bb - b_s[r0 + s:r0 + s + 1, :], NEG_BIG)))
    rs = _dot_nt(jnp.concatenate(ys, axis=0).astype(BF16), k16)
    lane = _iota((c, LANES), 1)
    rows = []
    for blk in range(L // c):
        acc = jnp.zeros((c, LANES), F32)
        for s in range(c):
            i = blk * c + s
            acc = jnp.where(lane == i, rs[i * c:(i + 1) * c], acc)
        rows.append(acc)
    att = jnp.concatenate(rows, axis=0)
    for m in HGRN_LEVELS:
        par = 2 * m
        pieces = []
        for p0 in range(0, L, par):
            ref_row = p0 + m if rev else p0 + m - 1
            pieces.append(jnp.broadcast_to(b_s[ref_row:ref_row + 1, :], (par, LANES)))
        bref = jnp.concatenate(pieces, axis=0)
        in_first = (row % par) < m
        reader = in_first if rev else jnp.logical_not(in_first)
        e = jnp.exp2(-jnp.abs(b - bref))
        qs = jnp.where(reader, q * e, 0.0)
        ks = jnp.where(reader, 0.0, kk * e)
        a = _dot_nt(qs.astype(BF16), ks.astype(BF16))
        att = att + jnp.where((row // par) == (col // par), a, 0.0)
    o = o + _dot(att.astype(BF16), v16)
    b_end = b_s[0:1, :] if rev else b_s[L - 1:L, :]
    kd = (kk * jnp.exp2(b_end - b)).astype(BF16)
    st_new = st * jnp.exp2(b_end) + _dot_tn(v16, kd)
    return o, st_new


def _hgrn_kernel(*refs, has_init, out_state, hp):
    qf, vf, lff, kkf, qb, vb, lfb, kkb = refs[:8]
    pos = 8
    if has_init:
        s0 = refs[pos]
        pos += 1
    of, ob = refs[pos:pos + 2]
    pos += 2
    if out_state:
        s_out = refs[pos]
        pos += 1
    st_s, b_s = refs[pos:pos + 2]
    step = pl.program_id(2)

    @pl.when(step == 0)
    def _():
        for d in range(2):
            for h in range(hp):
                st_s[d * hp + h] = s0[0, d, h].T if has_init else jnp.zeros((DH_C, DH_C), F32)

    for d, (q_ref, v_ref, lf_ref, kk_ref, o_ref) in enumerate(((qf, vf, lff, kkf, of), (qb, vb, lfb, kkb, ob))):
        for h in range(hp):
            u = d * hp + h
            sl = slice(h * DH_C, (h + 1) * DH_C)
            o, st_new = _hgrn_unit(q_ref[0, :, sl], kk_ref[0, :, sl], v_ref[0, :, sl], lf_ref[0, :, sl],
                                   st_s[u], b_s.at[u], rev=(d == 1))
            o_ref[0, :, sl] = o
            st_s[u] = st_new

    if out_state:
        @pl.when(step == pl.num_programs(2) - 1)
        def _():
            for d in range(2):
                for h in range(hp):
                    s_out[0, d, h] = st_s[d * hp + h].T


def _hgrn(q, v, lf, kk, init, out_state):
    b, t, _ = q.shape
    L = SCAN_CHUNK
    nc = t // L
    hp = HGRN_HEADS_PER_STEP
    ng = H_C // hp
    blk = lambda im: pl.BlockSpec((1, L, hp * DH_C), im)
    fwd = lambda off: (lambda bi, h, c: (bi, c, h + off))
    bwd = lambda off: (lambda bi, h, c: (bi, nc - 1 - c, h + off))
    in_specs = [blk(fwd(0)), blk(fwd(0)), blk(fwd(0)), blk(fwd(0)),
                blk(bwd(0)), blk(bwd(0)), blk(bwd(ng)), blk(bwd(ng))]
    args = [q, v, lf, kk, q, v, lf, kk]
    st_spec = pl.BlockSpec((1, 2, hp, DH_C, DH_C), lambda bi, h, c: (bi, 0, h, 0, 0))
    has_init = init is not None
    if has_init:
        in_specs.append(st_spec)
        args.append(init)
    out_specs = [blk(fwd(0)), blk(bwd(0))]
    out_shape = [jax.ShapeDtypeStruct((b, t, W_C), F32)] * 2
    if out_state:
        out_specs.append(st_spec)
        out_shape.append(jax.ShapeDtypeStruct((b, 2, H_C, DH_C, DH_C), F32))
    return pl.pallas_call(
        functools.partial(_hgrn_kernel, has_init=has_init, out_state=out_state, hp=hp),
        grid=(b, ng, nc),
        in_specs=in_specs,
        out_specs=out_specs,
        out_shape=out_shape,
        scratch_shapes=[pltpu.VMEM((2 * hp, DH_C, DH_C), F32), pltpu.VMEM((2 * hp, L, DH_C), F32)],
        compiler_params=_cp("parallel", "parallel", "arbitrary"),
        name="hgrn_scan",
    )(*args)


def _out_c_kernel(of_ref, ob_ref, og_ref, x_ref, gate_ref, gh_ref, w_ref, o_ref):
    hm = _head_rms(of_ref[...] + ob_ref[...], gh_ref[...], H_C, DH_C)
    og = og_ref[...]
    parts = [hm[h] * _silu(og[:, h * DH_C:(h + 1) * DH_C]) for h in range(H_C)]
    cat = jnp.concatenate(parts, axis=1).astype(BF16)
    o_ref[...] = x_ref[...] + gate_ref[0] * _dot(cat, w_ref[...])


def _out_c(of, ob, og, x2, gate, g_hgrn, w_out, t, per_batch):
    n = x2.shape[0]
    tm = _row_tile(t)
    row = lambda c: pl.BlockSpec((tm, c), lambda i: (i, 0))
    full = lambda a: pl.BlockSpec(a.shape, lambda i: (0,) * a.ndim)
    return pl.pallas_call(
        _out_c_kernel,
        grid=(n // tm,),
        in_specs=[row(W_C), row(W_C), row(W_C), row(D_MODEL), _mod_spec(per_batch, tm, t),
                  full(g_hgrn), full(w_out)],
        out_specs=row(D_MODEL),
        out_shape=jax.ShapeDtypeStruct((n, D_MODEL), F32),
        compiler_params=_cp("parallel"),
        name="out_proj_odd",
    )(of, ob, og, x2, gate, g_hgrn, w_out)


def _router_kernel(x_ref, g_ref, sh_ref, sc_ref, wr_ref, h_ref, aff_ref):
    h = _rms_rows(x_ref[...], g_ref[...]) * (1.0 + sc_ref[0]) + sh_ref[0]
    h_ref[...] = h.astype(BF16)
    logits = _dot_f32(h, wr_ref[...])
    valid = _iota((1, LANES), 1) < N_EXPERTS
    logits = jnp.where(valid, logits, NEG_BIG)
    e = jnp.exp(logits - jnp.max(logits, axis=-1, keepdims=True))
    e = jnp.where(valid, e, 0.0)
    aff_ref[...] = e / jnp.sum(e, axis=-1, keepdims=True)


def _router(x2, g, shift, scale, w_router_pad, t, per_batch):
    n = x2.shape[0]
    tm = _row_tile(t)
    row = lambda c: pl.BlockSpec((tm, c), lambda i: (i, 0))
    full = lambda a: pl.BlockSpec(a.shape, lambda i: (0,) * a.ndim)
    return pl.pallas_call(
        _router_kernel,
        grid=(n // tm,),
        in_specs=[row(D_MODEL), full(g), _mod_spec(per_batch, tm, t), _mod_spec(per_batch, tm, t),
                  full(w_router_pad)],
        out_specs=[row(D_MODEL), row(LANES)],
        out_shape=[jax.ShapeDtypeStruct((n, D_MODEL), BF16), jax.ShapeDtypeStruct((n, LANES), F32)],
        compiler_params=_cp("parallel"),
        name="moe_router",
    )(x2, g, shift, scale, w_router_pad)


def _topk_kernel(aff_ref, pos_ref, s0_ref, bits_s, sel_s, *, cap, tb):
    t = aff_ref.shape[1]
    nb = t // tb
    bits_s[...] = pltpu.bitcast(aff_ref[0], I32)

    def count_ge(cand):
        def body(j, acc):
            r = pl.multiple_of(j * tb, tb)
            blk = bits_s[pl.ds(r, tb), :]
            return acc + jnp.sum((blk >= cand).astype(F32), axis=0, keepdims=True)
        return lax.fori_loop(0, nb, body, jnp.zeros((1, LANES), F32))

    def bit_body(i, thr):
        cand = thr | lax.shift_left(jnp.int32(1), 30 - i)
        return jnp.where(count_ge(cand) >= cap, cand, thr)

    thr = lax.fori_loop(0, 31, bit_body, jnp.zeros((1, LANES), I32))

    def count_gt(j, acc):
        r = pl.multiple_of(j * tb, tb)
        return acc + jnp.sum((bits_s[pl.ds(r, tb), :] > thr).astype(F32), axis=0, keepdims=True)

    need = cap - lax.fori_loop(0, nb, count_gt, jnp.zeros((1, LANES), F32))
    strict = (_iota((tb, tb), 0) > _iota((tb, tb), 1)).astype(BF16)

    def select(j, carry):
        r = pl.multiple_of(j * tb, tb)
        blk = bits_s[pl.ds(r, tb), :]
        eq = blk == thr
        before = carry + _dot(strict, eq.astype(BF16))
        sel = jnp.logical_or(blk > thr, jnp.logical_and(eq, before < need))
        sel_s[pl.ds(r, tb), :] = sel.astype(F32)
        return carry + jnp.sum(eq.astype(F32), axis=0, keepdims=True)

    lax.fori_loop(0, nb, select, jnp.zeros((1, LANES), F32))

    def place(j, carry):
        r = pl.multiple_of(j * tb, tb)
        sel = sel_s[pl.ds(r, tb), :]
        slot = carry + _dot(strict, sel.astype(BF16))
        pos_ref[0, pl.ds(r, tb), :] = jnp.where(sel > 0.5, slot, -1.0).astype(I32)
        s0_ref[0, pl.ds(j, 1), :] = carry.astype(I32)
        return carry + jnp.sum(sel, axis=0, keepdims=True)

    lax.fori_loop(0, nb, place, jnp.zeros((1, LANES), F32))


def _topk(aff, cap, tb):
    b, t, _ = aff.shape
    nb = t // tb
    return pl.pallas_call(
        functools.partial(_topk_kernel, cap=cap, tb=tb),
        grid=(b,),
        in_specs=[pl.BlockSpec((1, t, LANES), lambda bi: (bi, 0, 0))],
        out_specs=[pl.BlockSpec((1, t, LANES), lambda bi: (bi, 0, 0)),
                   pl.BlockSpec((1, nb, LANES), lambda bi: (bi, 0, 0))],
        out_shape=[jax.ShapeDtypeStruct((b, t, LANES), I32), jax.ShapeDtypeStruct((b, nb, LANES), I32)],
        scratch_shapes=[pltpu.VMEM((t, LANES), I32), pltpu.VMEM((t, LANES), F32)],
        compiler_params=_cp("parallel"),
        name="moe_topk",
    )(aff)


def _window(cap, tb):
    if cap <= tb + BF16_ROWS:
        return cap, True
    return tb + BF16_ROWS, False


def _window_start(s0_ref, bi, j, e, cap, w, whole):
    if whole:
        return 0
    s0 = s0_ref[bi, j, e]
    st = jnp.minimum((s0 // BF16_ROWS) * BF16_ROWS, cap - w)
    return pl.multiple_of(st, BF16_ROWS)


def _gather_kernel(s0_ref, h_ref, pos_ref, o_ref, *, cap, tb):
    bi = pl.program_id(0)
    j = pl.program_id(1)
    w, whole = _window(cap, tb)

    @pl.when(j == 0)
    def _():
        o_ref[...] = jnp.zeros_like(o_ref)

    h = h_ref[0]
    pos = pos_ref[0]
    lane = _iota((tb, w), 1)
    for e in range(N_EXPERTS):
        st = _window_start(s0_ref, bi, j, e, cap, w, whole)
        onehot = ((pos[:, e:e + 1] - st) == lane).astype(BF16)
        rows = _dot_tn(onehot, h).astype(BF16)
        cur = o_ref[0, e, pl.ds(st, w), :]
        o_ref[0, e, pl.ds(st, w), :] = (cur.astype(F32) + rows.astype(F32)).astype(BF16)


def _gather(h2, pos, s0, cap, tb):
    b, t, d = h2.shape
    nb = t // tb
    grid_spec = pltpu.PrefetchScalarGridSpec(
        num_scalar_prefetch=1,
        grid=(b, nb),
        in_specs=[pl.BlockSpec((1, tb, d), lambda bi, j, s: (bi, j, 0)),
                  pl.BlockSpec((1, tb, LANES), lambda bi, j, s: (bi, j, 0))],
        out_specs=pl.BlockSpec((1, N_EXPERTS, cap, d), lambda bi, j, s: (bi, 0, 0, 0)),
    )
    return pl.pallas_call(
        functools.partial(_gather_kernel, cap=cap, tb=tb),
        grid_spec=grid_spec,
        out_shape=jax.ShapeDtypeStruct((b, N_EXPERTS, cap, d), BF16),
        compiler_params=_cp("parallel", "arbitrary"),
        name="moe_gather",
    )(s0, h2, pos)


def _expert_kernel(x_ref, wg_ref, wu_ref, wd_ref, y_ref, wg_s, wu_s, wd_s):
    @pl.when(pl.program_id(1) == 0)
    def _():
        wg_s[...] = wg_ref[0].astype(BF16)
        wu_s[...] = wu_ref[0].astype(BF16)
        wd_s[...] = wd_ref[0].astype(BF16)

    shp = x_ref.shape
    x = x_ref[...].reshape(shp[0] * shp[2], shp[3])
    a = _dot(x, wg_s[...])
    u = _dot(x, wu_s[...])
    hm = (_silu(a) * u).astype(BF16)
    y_ref[...] = _dot(hm, wd_s[...]).astype(BF16).reshape(shp)


def _experts(xin, w_gate, w_up, w_down):
    b, e, cap, d = xin.shape
    f = w_gate.shape[-1]
    bb = max(1, min(b, 512 // cap))
    xs = pl.BlockSpec((bb, 1, cap, d), lambda ei, bi: (bi, ei, 0, 0))
    ws = lambda a: pl.BlockSpec((1,) + a.shape[1:], lambda ei, bi: (ei, 0, 0))
    return pl.pallas_call(
        _expert_kernel,
        grid=(e, b // bb),
        in_specs=[xs, ws(w_gate), ws(w_up), ws(w_down)],
        out_specs=xs,
        out_shape=jax.ShapeDtypeStruct(xin.shape, BF16),
        scratch_shapes=[pltpu.VMEM((d, f), BF16), pltpu.VMEM((d, f), BF16), pltpu.VMEM((f, d), BF16)],
        compiler_params=_cp("parallel", "arbitrary"),
        name="moe_experts",
    )(xin, w_gate, w_up, w_down)


def _combine_kernel(*refs, cap, tb, final):
    if final:
        s0_ref, y_ref, pos_ref, aff_ref, x_ref, gate_ref, gf_ref, o_ref = refs
    else:
        s0_ref, y_ref, pos_ref, aff_ref, x_ref, gate_ref, o_ref = refs
    bi = pl.program_id(0)
    j = pl.program_id(1)
    w, whole = _window(cap, tb)
    pos = pos_ref[0]
    aff = aff_ref[0]
    lane = _iota((tb, w), 1)
    acc = jnp.zeros((tb, D_MODEL), F32)
    for e in range(N_EXPERTS):
        st = _window_start(s0_ref, bi, j, e, cap, w, whole)
        onehot = ((pos[:, e:e + 1] - st) == lane).astype(BF16)
        acc = acc + aff[:, e:e + 1] * _dot(onehot, y_ref[0, e, pl.ds(st, w), :])
    out = x_ref[0] + gate_ref[0] * acc
    if final:
        out = _rms_rows(out, gf_ref[...])
    o_ref[0] = out


def _combine(y, pos, aff, x3, gate, s0, cap, tb, per_batch, g_final):
    b, t, d = x3.shape
    nb = t // tb
    final = g_final is not None
    tok = lambda c: pl.BlockSpec((1, tb, c), lambda bi, j, s: (bi, j, 0))
    in_specs = [pl.BlockSpec((1, N_EXPERTS, cap, d), lambda bi, j, s: (bi, 0, 0, 0)),
                tok(LANES), tok(LANES), tok(d),
                pl.BlockSpec((1, 1, d), (lambda bi, j, s: (bi, 0, 0)) if per_batch else (lambda bi, j, s: (0, 0, 0)))]
    args = [y, pos, aff, x3, gate]
    if final:
        in_specs.append(pl.BlockSpec(g_final.shape, lambda bi, j, s: (0, 0)))
        args.append(g_final)
    grid_spec = pltpu.PrefetchScalarGridSpec(
        num_scalar_prefetch=1, grid=(b, nb), in_specs=in_specs, out_specs=tok(d))
    return pl.pallas_call(
        functools.partial(_combine_kernel, cap=cap, tb=tb, final=final),
        grid_spec=grid_spec,
        out_shape=jax.ShapeDtypeStruct((b, t, d), F32),
        compiler_params=_cp("parallel", "arbitrary"),
        name="moe_combine",
    )(s0, *args)


def _moe(x2, b, t, g, shift, scale, gate, w_router_pad, w_gate, w_up, w_down, per_batch, g_final):
    cap = CAPACITY_FACTOR * t // N_EXPERTS
    tb = min(MOE_TOKEN_BLOCK, t)
    h2, aff = _router(x2, g, shift, scale, w_router_pad, t, per_batch)
    aff3 = aff.reshape(b, t, LANES)
    pos, s0 = _topk(aff3, cap, tb)
    s0 = s0[:, :, :N_EXPERTS]
    xin = _gather(h2.reshape(b, t, D_MODEL), pos, s0, cap, tb)
    y = _experts(xin, w_gate, w_up, w_down)
    out = _combine(y, pos, aff3, x2.reshape(b, t, D_MODEL), gate, s0, cap, tb, per_batch, g_final)
    return out.reshape(b * t, D_MODEL)


def _rope_tables(t):
    rows = t // GRID_W
    row = jnp.repeat(jnp.arange(rows, dtype=F32), GRID_W)
    col = jnp.tile(jnp.arange(GRID_W, dtype=F32), rows)
    n_freq = DQK_A // 4
    inv_freq = jnp.power(ROPE_BASE, -jnp.arange(n_freq, dtype=F32) / n_freq)
    ang = jnp.concatenate([row[:, None] * inv_freq, col[:, None] * inv_freq], axis=-1)
    cos, sin = jnp.cos(ang), jnp.sin(ang)
    zero = jnp.zeros_like(sin)
    reps = LANES // DQK_A
    cos_t = jnp.tile(jnp.concatenate([cos, cos], axis=-1), (1, reps))
    sin_a = jnp.tile(jnp.concatenate([-sin, zero], axis=-1), (1, reps))
    sin_b = jnp.tile(jnp.concatenate([zero, sin], axis=-1), (1, reps))
    return cos_t, sin_a, sin_b


def _stream(x, mods, per_batch, rope, cache, p):
    b, t, d = x.shape
    x2 = x.reshape(b * t, d)
    depth = p["g_norm"].shape[0]
    even_out, odd_out = [], []
    for l in range(depth):
        j = l // 2
        sh1, sc1, gt1, sh2, sc2, gt2 = mods[l]
        g1 = p["g_norm"][l, 0].reshape(1, d)
        g2 = p["g_norm"][l, 1].reshape(1, d)
        if l % 2 == 0:
            aq, ak, av, bqk, bv, bo, gates = _in_ab(x2, g1, sh1, sc1, p["w_in_ab_main"][j], p["w_in_ab_gate"][j],
                                                     rope, t, per_batch, F32 if cache is None else BF16)
            r3 = lambda a: a.reshape(b, t, a.shape[-1])
            if cache is not None:
                kc = cache[0][:, j].reshape(b, -1, W_A)
                vc = cache[1][:, j].reshape(b, -1, W_A)
                nu = 2 * H_B
                init = (cache[2][:, j].reshape(b, nu, DH_B, DH_B),
                        cache[3][:, j].reshape(b, nu, 1, DH_B),
                        jnp.broadcast_to(cache[4][:, j].reshape(b, nu, 1, 1), (b, nu, 1, LANES)))
            else:
                kc = vc = init = None
            lam_init = 0.8 - 0.6 * math.exp(-0.3 * l)
            attn = _attention(r3(aq), r3(ak), r3(av), kc, vc, p["lam_ab"][j], lam_init)
            qk = _conv_silu(r3(bqk), p["conv_ab"][j])
            bias = jnp.pad(jnp.concatenate([p["b_ig_ab"][j].reshape(-1), p["b_fg_ab"][j].reshape(-1)]),
                           (0, LANES - 4 * H_B)).reshape(1, LANES)
            res = _mlstm(qk, r3(bv), r3(gates), bias, init, cache is None)
            hf, hb = res[0], res[1]
            if cache is None:
                even_out.append((ak, av, res[2], res[3], res[4]))
            x2 = _out_ab(attn.reshape(b * t, W_A), hf.reshape(b * t, W_B), hb.reshape(b * t, W_B), bo, x2, gt1,
                         p["g_attn_ab"][j].reshape(1, W_A), p["g_mlstm_ab"][j].reshape(1, W_B),
                         p["w_out_ab_bf16"][j], t, per_batch, 1.0 - lam_init)
        else:
            q, v, og, lf, kk = _in_c(x2, g1, sh1, sc1, p["w_in_c_bf16"][j], p["b_f_c"][j], p["gamma_lb"], l, t,
                                     per_batch)
            r3 = lambda a: a.reshape(b, t, a.shape[-1])
            init = None if cache is None else cache[5][:, j]
            res = _hgrn(r3(q), r3(v), r3(lf), r3(kk), init, cache is None)
            if cache is None:
                odd_out.append(res[2])
            x2 = _out_c(res[0].reshape(b * t, W_C), res[1].reshape(b * t, W_C), og, x2, gt1,
                        p["g_hgrn_c"][j].reshape(1, W_C), p["w_out_c_bf16"][j], t, per_batch)
        g_final = p["g_final"].reshape(1, d) if l == depth - 1 else None
        x2 = _moe(x2, b, t, g2, sh2, sc2, gt2, p["w_router_pad"][l], p["w_gate_e"][l], p["w_up_e"][l],
                  p["w_down_e"][l], per_batch, g_final)
    return x2.reshape(b, t, d), even_out, odd_out


def kernel(x_prompt, x_sample, cache_dattn_k, cache_dattn_v, state_mlstm_C, state_mlstm_n, state_mlstm_m,
           state_hgrn_S, c, c_ctx, w_ada, b_ada, g_norm, g_final, w_in_ab, b_ig_ab, b_fg_ab, lam_ab, conv_ab,
           g_attn_ab, g_mlstm_ab, w_out_ab, w_in_c, b_f_c, gamma_lb, g_hgrn_c, w_out_c,
           w_router, w_gate_e, w_up_e, w_down_e):
    d = D_MODEL
    bs = c.shape[0]
    bp, tp, _ = x_prompt.shape
    depth = w_ada.shape[0]
    n_main = 3 * W_A + 4 * W_B
    p = dict(
        g_norm=g_norm, g_final=g_final, b_ig_ab=b_ig_ab, b_fg_ab=b_fg_ab, lam_ab=lam_ab, conv_ab=conv_ab,
        g_attn_ab=g_attn_ab, g_mlstm_ab=g_mlstm_ab, b_f_c=b_f_c, gamma_lb=gamma_lb, g_hgrn_c=g_hgrn_c,
        w_gate_e=w_gate_e, w_up_e=w_up_e, w_down_e=w_down_e,
        w_in_ab_main=w_in_ab[:, :, :n_main].astype(BF16),
        w_in_ab_gate=jnp.pad(w_in_ab[:, :, n_main:], ((0, 0), (0, 0), (0, LANES - 4 * H_B))).astype(BF16),
        w_out_ab_bf16=w_out_ab.astype(BF16),
        w_in_c_bf16=w_in_c.astype(BF16),
        w_out_c_bf16=w_out_c.astype(BF16),
        w_router_pad=jnp.pad(w_router, ((0, 0), (0, 0), (0, LANES - N_EXPERTS))),
    )
    rows = -(-(bs + 1) // 8) * 8
    cond = jnp.concatenate([c, c_ctx.reshape(1, d), jnp.zeros((rows - bs - 1, d), F32)], axis=0)
    mods = _ada(cond, w_ada, b_ada)
    mods_s = [[mods[l, :bs, i * d:(i + 1) * d].reshape(bs, 1, d) for i in range(6)] for l in range(depth)]
    mods_c = [[mods[l, bs:bs + 1, i * d:(i + 1) * d].reshape(1, 1, d) for i in range(6)] for l in range(depth)]

    y_prompt, ev, od = _stream(x_prompt, mods_c, False, None, None, p)
    new_k = jnp.stack([e[0].reshape(bp, tp, H_A, 2 * DQK_A) for e in ev], axis=1)
    new_v = jnp.stack([e[1].reshape(bp, tp, H_A, DV_A) for e in ev], axis=1)
    new_c = jnp.stack([e[2].reshape(bp, 2, H_B, DH_B, DH_B) for e in ev], axis=1)
    new_n = jnp.stack([e[3].reshape(bp, 2, H_B, DH_B) for e in ev], axis=1)
    new_m = jnp.stack([e[4][..., 0].reshape(bp, 2, H_B) for e in ev], axis=1)
    new_s = jnp.stack(od, axis=1)

    cache = (cache_dattn_k, cache_dattn_v, state_mlstm_C, state_mlstm_n, state_mlstm_m, state_hgrn_S)
    y_sample, _, _ = _stream(x_sample, mods_s, True, _rope_tables(x_sample.shape[1]), cache, p)
    return (y_prompt, y_sample, new_k, new_v, new_c, new_n, new_m, new_s)
```

```python
import functools
import math

import jax
import jax.numpy as jnp
from jax import lax
from jax.experimental import pallas as pl
from jax.experimental.pallas import tpu as pltpu

F32 = jnp.float32
BF16 = jnp.bfloat16
I32 = jnp.int32

D_MODEL = 1024
H_A = 4
DV_A = 128
DQK_A = 64
W_A = H_A * DV_A
H_B = 4
DH_B = 128
W_B = H_B * DH_B
H_C = 8
DH_C = 128
W_C = H_C * DH_C
N_EXPERTS = 16
CAPACITY_FACTOR = 2
GRID_W = 64
ROPE_BASE = 10000.0
RMS_EPS = 1e-6
NEG_BIG = -1e30
LOG2E = 1.4426950408889634
LANES = 128
BF16_ROWS = 16
SCAN_CHUNK = 128
MOE_TOKEN_BLOCK = 128
VMEM_LIMIT = 56 * 1024 * 1024


def _cp(*sem):
    return pltpu.CompilerParams(dimension_semantics=sem, vmem_limit_bytes=VMEM_LIMIT)


def _dot(a, b):
    return jnp.dot(a, b, preferred_element_type=F32)


def _dot_nt(a, b):
    return lax.dot_general(a, b, (((1,), (1,)), ((), ())), preferred_element_type=F32)


def _dot_tn(a, b):
    return lax.dot_general(a, b, (((0,), (0,)), ((), ())), preferred_element_type=F32)


def _split3(x):
    hi = x.astype(BF16)
    r = x - hi.astype(F32)
    mid = r.astype(BF16)
    lo = (r - mid.astype(F32)).astype(BF16)
    return hi, mid, lo


def _dot_sel_left(sel, x):
    hi, mid, lo = _split3(x)
    return _dot(sel, hi) + _dot(sel, mid) + _dot(sel, lo)


def _dot_sel_right(x, sel):
    hi, mid, lo = _split3(x)
    return _dot(hi, sel) + _dot(mid, sel) + _dot(lo, sel)


def _dot_f32(a, b):
    ah, am, al = _split3(a)
    bh, bm, bl = _split3(b)
    return (_dot(ah, bh) + (_dot(ah, bm) + _dot(am, bh))
            + (_dot(ah, bl) + _dot(am, bm) + _dot(al, bh)))


def _sigmoid(x):
    return 1.0 / (1.0 + jnp.exp(-x))


def _silu(x):
    return x * _sigmoid(x)


def _log_sigmoid(x):
    return jnp.minimum(x, 0.0) - jnp.log(1.0 + jnp.exp(-jnp.abs(x)))


def _rms_rows(x, g):
    return x * lax.rsqrt(jnp.mean(x * x, axis=-1, keepdims=True) + RMS_EPS) * g


def _iota(shape, dim):
    return lax.broadcasted_iota(I32, shape, dim)


def _ada_kernel(c_ref, w_ref, b_ref, o_ref):
    c = c_ref[...]
    o_ref[0] = _dot_f32(_silu(c), w_ref[0]) + b_ref[0]


def _ada(cond_rows, w_ada, b_ada):
    depth, d, n6 = w_ada.shape
    rows = cond_rows.shape[0]
    tn = 1536
    return pl.pallas_call(
        _ada_kernel,
        grid=(depth, n6 // tn),
        in_specs=[pl.BlockSpec((rows, d), lambda l, j: (0, 0)),
                  pl.BlockSpec((1, d, tn), lambda l, j: (l, 0, j)),
                  pl.BlockSpec((1, 1, tn), lambda l, j: (l, 0, j))],
        out_specs=pl.BlockSpec((1, rows, tn), lambda l, j: (l, 0, j)),
        out_shape=jax.ShapeDtypeStruct((depth, rows, n6), F32),
        compiler_params=_cp("parallel", "parallel"),
        name="adaln",
    )(cond_rows, w_ada, b_ada.reshape(depth, 1, n6))


def _mod_spec(per_batch, tm, t):
    if per_batch:
        return pl.BlockSpec((1, 1, D_MODEL), lambda i: (i * tm // t, 0, 0))
    return pl.BlockSpec((1, 1, D_MODEL), lambda i: (0, 0, 0))


def _row_tile(t):
    return min(512, t)


def _in_ab_kernel(*refs, use_rope):
    if use_rope:
        (x_ref, g_ref, sh_ref, sc_ref, w_ref, wg_ref, cos_ref, sa_ref, sb_ref,
         aq_ref, ak_ref, av_ref, bqk_ref, bv_ref, bo_ref, gt_ref) = refs
    else:
        (x_ref, g_ref, sh_ref, sc_ref, w_ref, wg_ref,
         aq_ref, ak_ref, av_ref, bqk_ref, bv_ref, bo_ref, gt_ref) = refs
    h = _rms_rows(x_ref[...], g_ref[...]) * (1.0 + sc_ref[0]) + sh_ref[0]
    hb = h.astype(BF16)
    y = _dot(hb, w_ref[...])
    gt_ref[...] = _dot(hb, wg_ref[...])
    aq = y[:, 0:W_A] * (DQK_A ** -0.5)
    ak = y[:, W_A:2 * W_A]
    if use_rope:
        cos = jnp.concatenate([cos_ref[...]] * H_A, axis=1)
        sa = jnp.concatenate([sa_ref[...]] * H_A, axis=1)
        sb = jnp.concatenate([sb_ref[...]] * H_A, axis=1)
        half = DQK_A // 2

        def rope(v):
            return (v * cos + pltpu.roll(v, W_A - half, axis=1) * sa
                    + pltpu.roll(v, half, axis=1) * sb)

        aq = rope(aq)
        ak = rope(ak)
    aq_ref[...] = aq.astype(aq_ref.dtype)
    ak_ref[...] = ak.astype(ak_ref.dtype)
    av_ref[...] = y[:, 2 * W_A:3 * W_A].astype(av_ref.dtype)
    bqk_ref[...] = y[:, 3 * W_A:3 * W_A + 2 * W_B]
    bv_ref[...] = y[:, 3 * W_A + 2 * W_B:3 * W_A + 3 * W_B]
    bo_ref[...] = y[:, 3 * W_A + 3 * W_B:3 * W_A + 4 * W_B]


def _in_ab(x2, g, shift, scale, w_main, w_gate, rope, t, per_batch, qkv_dtype):
    n = x2.shape[0]
    tm = _row_tile(t)
    use_rope = rope is not None
    row = lambda c: pl.BlockSpec((tm, c), lambda i: (i, 0))
    full = lambda a: pl.BlockSpec(a.shape, lambda i: (0,) * a.ndim)
    in_specs = [row(D_MODEL), full(g), _mod_spec(per_batch, tm, t), _mod_spec(per_batch, tm, t),
                full(w_main), full(w_gate)]
    args = [x2, g, shift, scale, w_main, w_gate]
    if use_rope:
        nt = t // tm
        tab = pl.BlockSpec((tm, LANES), lambda i: (i % nt, 0))
        in_specs += [tab, tab, tab]
        args += list(rope)
    widths = (W_A, W_A, W_A, 2 * W_B, W_B, W_B, LANES)
    return pl.pallas_call(
        functools.partial(_in_ab_kernel, use_rope=use_rope),
        grid=(n // tm,),
        in_specs=in_specs,
        out_specs=[row(c) for c in widths],
        out_shape=[jax.ShapeDtypeStruct((n, c), qkv_dtype if i < 3 else F32) for i, c in enumerate(widths)],
        compiler_params=_cp("parallel"),
        name="in_proj_even",
    )(*args)


def _attn_kernel(*refs, has_cache, tk, lam_init):
    if has_cache:
        q_ref, k_ref, v_ref, kc_ref, vc_ref, lam_ref, o_ref = refs[:7]
    else:
        q_ref, k_ref, v_ref, lam_ref, o_ref = refs[:5]
    s_s = refs[-2] if has_cache else refs[-1]
    sc_s = refs[-1] if has_cache else None
    q = q_ref[0]
    tq = q.shape[0]
    first = _iota((1, LANES), 1) < DQK_A
    qs = (jnp.where(first, q, 0).astype(BF16), jnp.where(first, 0, q).astype(BF16))
    nkb = k_ref.shape[1] // tk

    def groups(x):
        return [x[:, g * LANES:(g + 1) * LANES] for g in range(x.shape[1] // LANES)]

    mpart = [jnp.full((tq, LANES), NEG_BIG, F32)] * 2
    if has_cache:
        kcb = kc_ref[0].astype(BF16)
        for i in range(2):
            s = _dot_nt(qs[i], kcb)
            sc_s[i] = s
            mpart[i] = functools.reduce(jnp.maximum, groups(s), mpart[i])

    def pass1(j, mp):
        r = pl.multiple_of(j * tk, tk)
        kb = k_ref[0, pl.ds(r, tk), :].astype(BF16)
        out = []
        for i in range(2):
            s = _dot_nt(qs[i], kb)
            s_s[i, j] = s
            out.append(functools.reduce(jnp.maximum, groups(s), mp[i]))
        return tuple(out)

    mpart = lax.fori_loop(0, nkb, pass1, tuple(mpart), unroll=True)
    m = [jnp.max(mp, axis=-1, keepdims=True) for mp in mpart]

    def accumulate(s, vb, lp, acc, mi):
        p = jnp.exp(s - mi)
        return functools.reduce(jnp.add, groups(p), lp), acc + _dot(p.astype(BF16), vb)

    lpart = [jnp.zeros((tq, LANES), F32)] * 2
    acc = [jnp.zeros((tq, DV_A), F32)] * 2
    if has_cache:
        vcb = vc_ref[0].astype(BF16)
        for i in range(2):
            lpart[i], acc[i] = accumulate(sc_s[i], vcb, lpart[i], acc[i], m[i])

    def pass2(j, carry):
        r = pl.multiple_of(j * tk, tk)
        vb = v_ref[0, pl.ds(r, tk), :].astype(BF16)
        out = []
        for i in range(2):
            out.append(accumulate(s_s[i, j], vb, carry[i][0], carry[i][1], m[i]))
        return tuple(out)

    (l1, a1), (l2, a2) = lax.fori_loop(0, nkb, pass2, tuple(zip(lpart, acc)), unroll=True)
    l1 = jnp.sum(l1, axis=-1, keepdims=True)
    l2 = jnp.sum(l2, axis=-1, keepdims=True)
    lp = lam_ref[...]
    lam = (jnp.exp(jnp.sum(lp[0:1] * lp[1:2], axis=-1, keepdims=True))
           - jnp.exp(jnp.sum(lp[2:3] * lp[3:4], axis=-1, keepdims=True)) + lam_init)
    o_ref[0] = a1 / l1 - lam * (a2 / l2)


def _attention(aq, ak, av, kc, vc, lam_p, lam_init):
    b, t, _ = aq.shape
    tq = min(256, t)
    tk = min(512, t)
    has_cache = kc is not None
    qs = pl.BlockSpec((1, tq, DV_A), lambda bi, h, i: (bi, i, h))
    ks = pl.BlockSpec((1, t, DV_A), lambda bi, h, i: (bi, 0, h))
    in_specs = [qs, ks, ks]
    args = [aq, ak, av]
    if has_cache:
        cs = pl.BlockSpec((1, kc.shape[1], DV_A), lambda bi, h, i: (bi, 0, h))
        in_specs += [cs, cs]
        args += [kc, vc]
    in_specs.append(pl.BlockSpec(lam_p.shape, lambda bi, h, i: (0, 0)))
    args.append(lam_p)
    scratch = [pltpu.VMEM((2, t // tk, tq, tk), F32)]
    if has_cache:
        scratch.append(pltpu.VMEM((2, tq, kc.shape[1]), F32))
    return pl.pallas_call(
        functools.partial(_attn_kernel, has_cache=has_cache, tk=tk, lam_init=lam_init),
        grid=(b, H_A, t // tq),
        in_specs=in_specs,
        out_specs=qs,
        out_shape=jax.ShapeDtypeStruct((b, t, W_A), F32),
        scratch_shapes=scratch,
        compiler_params=_cp("parallel", "parallel", "parallel"),
        name="diff_attention",
    )(*args)


def _conv_kernel(x_ref, w_ref, o_ref):
    x = x_ref[0]
    t = x.shape[0]
    w = w_ref[...]
    r = _iota((t, 1), 0)
    prev = jnp.where(r == 0, 0.0, pltpu.roll(x, 1, axis=0))
    nxt = jnp.where(r == t - 1, 0.0, pltpu.roll(x, t - 1, axis=0))
    o_ref[0] = _silu(prev * w[0:1] + x * w[1:2] + nxt * w[2:3])


def _conv_silu(bqk, conv_w):
    b, t, c = bqk.shape
    tc = 256
    spec = pl.BlockSpec((1, t, tc), lambda bi, j: (bi, 0, j))
    return pl.pallas_call(
        _conv_kernel,
        grid=(b, c // tc),
        in_specs=[spec, pl.BlockSpec((conv_w.shape[0], tc), lambda bi, j: (0, j))],
        out_specs=spec,
        out_shape=jax.ShapeDtypeStruct(bqk.shape, F32),
        compiler_params=_cp("parallel", "parallel"),
        name="mlstm_conv",
    )(bqk, conv_w)


def _tri(n, lower):
    r = _iota((n, n), 0)
    c = _iota((n, n), 1)
    return r >= c if lower else r <= c


def _mlstm_kernel(*refs, has_init, out_state):
    qf, kf, vf, gf, qb, kb, vb, gb, bias = refs[:9]
    pos = 9
    if has_init:
        c0, n0, m0 = refs[pos:pos + 3]
        pos += 3
    hf, hb = refs[pos:pos + 2]
    pos += 2
    if out_state:
        c_out, n_out, m_out = refs[pos:pos + 3]
        pos += 3
    cn_s, m_s = refs[pos:pos + 2]
    step = pl.program_id(1)
    L = SCAN_CHUNK
    nu = 2 * H_B

    @pl.when(step == 0)
    def _():
        if has_init:
            for u in range(nu):
                cn_s[u, :, 0:DH_B] = c0[0, u]
                cn_s[u, :, DH_B:2 * DH_B] = jnp.broadcast_to(n0[0, u], (DH_B, DH_B)).T
            m_s[...] = m0[0]
        else:
            cn_s[...] = jnp.zeros_like(cn_s)
            m_s[...] = jnp.zeros_like(m_s)

    ones_b = jnp.ones((L, DH_B), BF16)
    sel_rows = _iota((LANES, LANES), 0)
    lower = _tri(L, True)
    upper = _tri(L, False)
    lower_b = lower.astype(BF16)
    upper_b = upper.astype(BF16)
    col = _iota((1, LANES), 1)
    units = []
    for d, (q_ref, k_ref, v_ref, g_ref, h_ref) in enumerate(((qf, kf, vf, gf, hf), (qb, kb, vb, gb, hb))):
        g = g_ref[0] + bias[...]
        g = jnp.where(col < 2 * H_B, g, _log_sigmoid(g))
        gt = g.T
        if d == 0:
            bc_all = _dot_sel_left(lower_b, g)
            br_all = _dot_sel_right(gt, upper_b)
            mask = lower
        else:
            bc_all = _dot_sel_left(upper_b, g)
            br_all = _dot_sel_right(gt, lower_b)
            mask = upper
        for h in range(H_B):
            u = d * H_B + h
            sl = slice(h * DH_B, (h + 1) * DH_B)
            units.append(dict(
                u=u, d=d, sl=sl, h_ref=h_ref, mask=mask, g=g, bc_all=bc_all,
                li_row=gt[u:u + 1, :], b_row=br_all[2 * H_B + u:2 * H_B + u + 1, :],
                q=q_ref[0, :, sl].astype(BF16), k=k_ref[0, :, sl] * (DH_B ** -0.5),
                v1=jnp.concatenate([v_ref[0, :, sl].astype(BF16), ones_b], axis=1)))
    for x in units:
        x["li_b"] = _dot_sel_right(x["g"], (sel_rows == x["u"]).astype(BF16))
        x["b_b"] = _dot_sel_right(x["bc_all"], (sel_rows == 2 * H_B + x["u"]).astype(BF16))
        x["qk"] = _dot_nt(x["q"], x["k"].astype(BF16))
        x["cn"] = cn_s[x["u"]]
        x["qc"] = _dot(x["q"], x["cn"].astype(BF16))
    for x in units:
        x["m_prev"] = m_s[x["u"]]
        a_b = x["b_b"] + x["m_prev"]
        dm = jnp.where(x["mask"], x["b_b"] - x["b_row"] + x["li_row"], NEG_BIG)
        x["m_t"] = jnp.maximum(a_b, jnp.max(dm, axis=-1, keepdims=True))
        x["w_inter"] = jnp.exp(a_b - x["m_t"])
        x["s"] = (x["qk"] * jnp.exp(dm - x["m_t"])).astype(BF16)
    for x in units:
        x["sv"] = _dot(x["s"], x["v1"])
        b_end = x["b_b"][L - 1:L, :] if x["d"] == 0 else x["b_b"][0:1, :]
        g_b = b_end - x["b_b"] + x["li_b"]
        x["m_new"] = jnp.maximum(b_end + x["m_prev"], jnp.max(g_b, axis=0, keepdims=True))
        x["w_old"] = jnp.exp(b_end + x["m_prev"] - x["m_new"])
        x["ks"] = (jnp.exp(g_b - x["m_new"]) * x["k"]).astype(BF16)
    for x in units:
        num = x["w_inter"] * x["qc"][:, 0:DH_B] + x["sv"][:, 0:DH_B]
        den = x["w_inter"] * x["qc"][:, DH_B:] + x["sv"][:, DH_B:]
        x["h_ref"][0, :, x["sl"]] = num / jnp.maximum(jnp.abs(den), jnp.exp(-x["m_t"]))
        w2 = jnp.concatenate([x["w_old"], x["w_old"]], axis=1)
        cn_s[x["u"]] = w2 * x["cn"] + _dot_tn(x["ks"], x["v1"])
        m_s[x["u"]] = x["m_new"]

    if out_state:
        @pl.when(step == pl.num_programs(1) - 1)
        def _():
            for u in range(nu):
                c_out[0, u] = cn_s[u, :, 0:DH_B]
                n_out[0, u] = cn_s[u, :, DH_B:2 * DH_B].T[0:1, :]
            m_out[0] = m_s[...]


def _mlstm(qk, v, gates, bias, init, out_state):
    b, t, _ = v.shape
    L = SCAN_CHUNK
    nc = t // L
    nu = 2 * H_B
    fwd = lambda j: (lambda bi, c: (bi, c, j))
    bwd = lambda j: (lambda bi, c: (bi, nc - 1 - c, j))
    blk = lambda w, im: pl.BlockSpec((1, L, w), im)
    in_specs = [blk(W_B, fwd(0)), blk(W_B, fwd(1)), blk(W_B, fwd(0)), blk(LANES, fwd(0)),
                blk(W_B, bwd(0)), blk(W_B, bwd(1)), blk(W_B, bwd(0)), blk(LANES, bwd(0)),
                pl.BlockSpec((1, LANES), lambda bi, c: (0, 0))]
    args = [qk, qk, v, gates, qk, qk, v, gates, bias]
    st_specs = [pl.BlockSpec((1, nu, DH_B, DH_B), lambda bi, c: (bi, 0, 0, 0)),
                pl.BlockSpec((1, nu, 1, DH_B), lambda bi, c: (bi, 0, 0, 0)),
                pl.BlockSpec((1, nu, 1, LANES), lambda bi, c: (bi, 0, 0, 0))]
    st_shapes = [jax.ShapeDtypeStruct((b, nu, DH_B, DH_B), F32),
                 jax.ShapeDtypeStruct((b, nu, 1, DH_B), F32),
                 jax.ShapeDtypeStruct((b, nu, 1, LANES), F32)]
    has_init = init is not None
    if has_init:
        in_specs += st_specs
        args += list(init)
    out_specs = [blk(W_B, fwd(0)), blk(W_B, bwd(0))]
    out_shape = [jax.ShapeDtypeStruct((b, t, W_B), F32)] * 2
    if out_state:
        out_specs += st_specs
        out_shape += st_shapes
    return pl.pallas_call(
        functools.partial(_mlstm_kernel, has_init=has_init, out_state=out_state),
        grid=(b, nc),
        in_specs=in_specs,
        out_specs=out_specs,
        out_shape=out_shape,
        scratch_shapes=[pltpu.VMEM((nu, DH_B, 2 * DH_B), F32), pltpu.VMEM((nu, 1, LANES), F32)],
        compiler_params=_cp("parallel", "arbitrary"),
        name="mlstm_scan",
    )(*args)


def _head_rms(x, g, heads, width):
    parts = []
    for h in range(heads):
        sl = slice(h * width, (h + 1) * width)
        parts.append(_rms_rows(x[:, sl], g[:, sl]))
    return parts


def _out_ab_kernel(at_ref, hf_ref, hb_ref, bo_ref, x_ref, gate_ref, ga_ref, gm_ref, w_ref, o_ref, *, out_scale):
    oa = [p * out_scale for p in _head_rms(at_ref[...], ga_ref[...], H_A, DV_A)]
    hm = _head_rms(hf_ref[...] + hb_ref[...], gm_ref[...], H_B, DH_B)
    bo = bo_ref[...]
    ob = [_sigmoid(bo[:, h * DH_B:(h + 1) * DH_B]) * hm[h] for h in range(H_B)]
    cat = jnp.concatenate(oa + ob, axis=1).astype(BF16)
    o_ref[...] = x_ref[...] + gate_ref[0] * _dot(cat, w_ref[...])


def _out_ab(attn, hf, hb, bo, x2, gate, g_attn, g_mlstm, w_out, t, per_batch, out_scale):
    n = x2.shape[0]
    tm = _row_tile(t)
    row = lambda c: pl.BlockSpec((tm, c), lambda i: (i, 0))
    full = lambda a: pl.BlockSpec(a.shape, lambda i: (0,) * a.ndim)
    return pl.pallas_call(
        functools.partial(_out_ab_kernel, out_scale=out_scale),
        grid=(n // tm,),
        in_specs=[row(W_A), row(W_B), row(W_B), row(W_B), row(D_MODEL), _mod_spec(per_batch, tm, t),
                  full(g_attn), full(g_mlstm), full(w_out)],
        out_specs=row(D_MODEL),
        out_shape=jax.ShapeDtypeStruct((n, D_MODEL), F32),
        compiler_params=_cp("parallel"),
        name="out_proj_even",
    )(attn, hf, hb, bo, x2, gate, g_attn, g_mlstm, w_out)


def _in_c_kernel(x_ref, g_ref, sh_ref, sc_ref, w_ref, bf_ref, gam_ref, q_ref, v_ref, og_ref, lf_ref, kk_ref, *, layer):
    h = _rms_rows(x_ref[...], g_ref[...]) * (1.0 + sc_ref[0]) + sh_ref[0]
    y = _dot(h.astype(BF16), w_ref[...])
    q_ref[...] = y[:, 0:W_C]
    v_ref[...] = y[:, W_C:2 * W_C]
    og_ref[...] = y[:, 2 * W_C:3 * W_C]
    gam = gam_ref[...]
    e = jnp.exp(gam - jnp.max(gam, axis=0, keepdims=True))
    p = e / jnp.sum(e, axis=0, keepdims=True)
    lbs = p[0:1]
    lbs0 = lbs
    for i in range(1, layer + 1):
        lbs = lbs + p[i:i + 1]
    lb = jnp.maximum(lbs - lbs0, 0.0)
    L = SCAN_CHUNK
    for d in range(2):
        f = y[:, (3 + d) * W_C:(4 + d) * W_C] + bf_ref[d:d + 1]
        u = jnp.exp(-jnp.abs(f))
        neg = f < 0.0
        lf2 = (jnp.log(jnp.where(neg, lb + u, 1.0 + lb * u)) - jnp.log(1.0 + u)) * LOG2E
        kk_ref[:, d * W_C:(d + 1) * W_C] = (1.0 - lb) * jnp.where(neg, 1.0, u) / (1.0 + u)
        tri = _tri(L, d == 0).astype(BF16)
        for c0 in range(0, lf2.shape[0], L):
            lf_ref[c0:c0 + L, d * W_C:(d + 1) * W_C] = _dot_sel_left(tri, lf2[c0:c0 + L])


def _in_c(x2, g, shift, scale, w_in, b_f, gamma_lb, layer, t, per_batch):
    n = x2.shape[0]
    tm = _row_tile(t)
    row = lambda c: pl.BlockSpec((tm, c), lambda i: (i, 0))
    full = lambda a: pl.BlockSpec(a.shape, lambda i: (0,) * a.ndim)
    widths = (W_C, W_C, W_C, 2 * W_C, 2 * W_C)
    return pl.pallas_call(
        functools.partial(_in_c_kernel, layer=layer),
        grid=(n // tm,),
        in_specs=[row(D_MODEL), full(g), _mod_spec(per_batch, tm, t), _mod_spec(per_batch, tm, t),
                  full(w_in), full(b_f), full(gamma_lb)],
        out_specs=[row(c) for c in widths],
        out_shape=[jax.ShapeDtypeStruct((n, c), F32) for c in widths],
        compiler_params=_cp("parallel"),
        name="in_proj_odd",
    )(x2, g, shift, scale, w_in, b_f, gamma_lb)


HGRN_LEVELS = (8, 16, 32, 64)
HGRN_DIAG = 8
HGRN_HEADS_PER_STEP = 4


def _hgrn_unit(q, kk, v, b, st_ref, b_s, rev, store_o):
    L = SCAN_CHUNK
    b_s[...] = b
    row = _iota((L, 1), 0)
    col = _iota((1, L), 1)
    k16 = kk.astype(BF16)
    v16 = v.astype(BF16)
    st = st_ref[...]
    o = _dot_nt((q * jnp.exp2(b)).astype(BF16), st.astype(BF16))
    yield
    c = HGRN_DIAG
    sub = _iota((c, 1), 0)
    ys = []
    for blk in range(L // c):
        r0 = blk * c
        qb = q[r0:r0 + c]
        bb = b[r0:r0 + c]
        for s in range(c):
            keep = (sub <= s) if rev else (sub >= s)
            ys.append(qb * jnp.exp2(jnp.where(keep, bb - b_s[r0 + s:r0 + s + 1, :], NEG_BIG)))
    rs = _dot_nt(jnp.concatenate(ys, axis=0).astype(BF16), k16)
    yield
    parts = []
    for m in HGRN_LEVELS:
        par = 2 * m
        pieces = []
        for p0 in range(0, L, par):
            ref_row = p0 + m if rev else p0 + m - 1
            pieces.append(jnp.broadcast_to(b_s[ref_row:ref_row + 1, :], (par, LANES)))
        bref = jnp.concatenate(pieces, axis=0)
        in_first = (row % par) < m
        reader = in_first if rev else jnp.logical_not(in_first)
        e = jnp.exp2(-jnp.abs(b - bref))
        qs = jnp.where(reader, q * e, 0.0)
        ks = jnp.where(reader, 0.0, kk * e)
        parts.append((par, _dot_nt(qs.astype(BF16), ks.astype(BF16))))
        yield
    lane = _iota((c, LANES), 1)
    rows = []
    for blk in range(L // c):
        acc = jnp.zeros((c, LANES), F32)
        for s in range(c):
            i = blk * c + s
            acc = jnp.where(lane == i, rs[i * c:(i + 1) * c], acc)
        rows.append(acc)
    att = jnp.concatenate(rows, axis=0)
    for par, a in parts:
        att = att + jnp.where((row // par) == (col // par), a, 0.0)
    o = o + _dot(att.astype(BF16), v16)
    b_end = b_s[0:1, :] if rev else b_s[L - 1:L, :]
    kd = (kk * jnp.exp2(b_end - b)).astype(BF16)
    st_new = st * jnp.exp2(b_end) + _dot_tn(v16, kd)
    yield
    store_o(o)
    st_ref[...] = st_new
    yield


def _hgrn_kernel(*refs, has_init, out_state, hp):
    qf, vf, lff, kkf, qb, vb, lfb, kkb = refs[:8]
    pos = 8
    if has_init:
        s0 = refs[pos]
        pos += 1
    of, ob = refs[pos:pos + 2]
    pos += 2
    if out_state:
        s_out = refs[pos]
        pos += 1
    st_s, b_s = refs[pos:pos + 2]
    step = pl.program_id(2)

    @pl.when(step == 0)
    def _():
        for d in range(2):
            for h in range(hp):
                st_s[d * hp + h] = s0[0, d, h].T if has_init else jnp.zeros((DH_C, DH_C), F32)

    def storer(o_ref, sl):
        def store(o):
            o_ref[0, :, sl] = o
        return store

    units = []
    for d, (q_ref, v_ref, lf_ref, kk_ref, o_ref) in enumerate(((qf, vf, lff, kkf, of), (qb, vb, lfb, kkb, ob))):
        for h in range(hp):
            u = d * hp + h
            sl = slice(h * DH_C, (h + 1) * DH_C)
            units.append(_hgrn_unit(q_ref[0, :, sl], kk_ref[0, :, sl], v_ref[0, :, sl], lf_ref[0, :, sl],
                                    st_s.at[u], b_s.at[u], d == 1, storer(o_ref, sl)))
    live = units
    while live:
        live = [g for g in live if next(g, "done") != "done"]

    if out_state:
        @pl.when(step == pl.num_programs(2) - 1)
        def _():
            for d in range(2):
                for h in range(hp):
                    s_out[0, d, h] = st_s[d * hp + h].T


def _hgrn(q, v, lf, kk, init, out_state):
    b, t, _ = q.shape
    L = SCAN_CHUNK
    nc = t // L
    hp = HGRN_HEADS_PER_STEP
    ng = H_C // hp
    blk = lambda im: pl.BlockSpec((1, L, hp * DH_C), im)
    fwd = lambda off: (lambda bi, h, c: (bi, c, h + off))
    bwd = lambda off: (lambda bi, h, c: (bi, nc - 1 - c, h + off))
    in_specs = [blk(fwd(0)), blk(fwd(0)), blk(fwd(0)), blk(fwd(0)),
                blk(bwd(0)), blk(bwd(0)), blk(bwd(ng)), blk(bwd(ng))]
    args = [q, v, lf, kk, q, v, lf, kk]
    st_spec = pl.BlockSpec((1, 2, hp, DH_C, DH_C), lambda bi, h, c: (bi, 0, h, 0, 0))
    has_init = init is not None
    if has_init:
        in_specs.append(st_spec)
        args.append(init)
    out_specs = [blk(fwd(0)), blk(bwd(0))]
    out_shape = [jax.ShapeDtypeStruct((b, t, W_C), F32)] * 2
    if out_state:
        out_specs.append(st_spec)
        out_shape.append(jax.ShapeDtypeStruct((b, 2, H_C, DH_C, DH_C), F32))
    return pl.pallas_call(
        functools.partial(_hgrn_kernel, has_init=has_init, out_state=out_state, hp=hp),
        grid=(b, ng, nc),
        in_specs=in_specs,
        out_specs=out_specs,
        out_shape=out_shape,
        scratch_shapes=[pltpu.VMEM((2 * hp, DH_C, DH_C), F32), pltpu.VMEM((2 * hp, L, DH_C), F32)],
        compiler_params=_cp("parallel", "parallel", "arbitrary"),
        name="hgrn_scan",
    )(*args)


def _out_c_kernel(of_ref, ob_ref, og_ref, x_ref, gate_ref, gh_ref, w_ref, o_ref):
    hm = _head_rms(of_ref[...] + ob_ref[...], gh_ref[...], H_C, DH_C)
    og = og_ref[...]
    parts = [hm[h] * _silu(og[:, h * DH_C:(h + 1) * DH_C]) for h in range(H_C)]
    cat = jnp.concatenate(parts, axis=1).astype(BF16)
    o_ref[...] = x_ref[...] + gate_ref[0] * _dot(cat, w_ref[...])


def _out_c(of, ob, og, x2, gate, g_hgrn, w_out, t, per_batch):
    n = x2.shape[0]
    tm = _row_tile(t)
    row = lambda c: pl.BlockSpec((tm, c), lambda i: (i, 0))
    full = lambda a: pl.BlockSpec(a.shape, lambda i: (0,) * a.ndim)
    return pl.pallas_call(
        _out_c_kernel,
        grid=(n // tm,),
        in_specs=[row(W_C), row(W_C), row(W_C), row(D_MODEL), _mod_spec(per_batch, tm, t),
                  full(g_hgrn), full(w_out)],
        out_specs=row(D_MODEL),
        out_shape=jax.ShapeDtypeStruct((n, D_MODEL), F32),
        compiler_params=_cp("parallel"),
        name="out_proj_odd",
    )(of, ob, og, x2, gate, g_hgrn, w_out)


def _router_kernel(x_ref, g_ref, sh_ref, sc_ref, wr_ref, h_ref, aff_ref):
    h = _rms_rows(x_ref[...], g_ref[...]) * (1.0 + sc_ref[0]) + sh_ref[0]
    h_ref[...] = h.astype(BF16)
    logits = _dot_f32(h, wr_ref[...])
    valid = _iota((1, LANES), 1) < N_EXPERTS
    logits = jnp.where(valid, logits, NEG_BIG)
    e = jnp.exp(logits - jnp.max(logits, axis=-1, keepdims=True))
    e = jnp.where(valid, e, 0.0)
    aff_ref[...] = (e / jnp.sum(e, axis=-1, keepdims=True))[:, :N_EXPERTS]


def _router(x2, g, shift, scale, w_router_pad, t, per_batch):
    n = x2.shape[0]
    tm = _row_tile(t)
    row = lambda c: pl.BlockSpec((tm, c), lambda i: (i, 0))
    full = lambda a: pl.BlockSpec(a.shape, lambda i: (0,) * a.ndim)
    return pl.pallas_call(
        _router_kernel,
        grid=(n // tm,),
        in_specs=[row(D_MODEL), full(g), _mod_spec(per_batch, tm, t), _mod_spec(per_batch, tm, t),
                  full(w_router_pad)],
        out_specs=[row(D_MODEL), row(N_EXPERTS)],
        out_shape=[jax.ShapeDtypeStruct((n, D_MODEL), BF16), jax.ShapeDtypeStruct((n, N_EXPERTS), F32)],
        compiler_params=_cp("parallel"),
        name="moe_router",
    )(x2, g, shift, scale, w_router_pad)


def _topk_kernel(aff_ref, pos_ref, s0_ref, bits_s, sel_s, *, cap, tb):
    t = aff_ref.shape[1]
    nb = t // tb
    bits_s[...] = pltpu.bitcast(aff_ref[0], I32)

    def count_ge(cand):
        def body(j, acc):
            r = pl.multiple_of(j * tb, tb)
            blk = bits_s[pl.ds(r, tb), :]
            return acc + jnp.sum((blk >= cand).astype(F32), axis=0, keepdims=True)
        return lax.fori_loop(0, nb, body, jnp.zeros((1, LANES), F32))

    def bit_body(i, thr):
        cand = thr | lax.shift_left(jnp.int32(1), 30 - i)
        return jnp.where(count_ge(cand) >= cap, cand, thr)

    thr = lax.fori_loop(0, 31, bit_body, jnp.zeros((1, LANES), I32))

    def count_gt(j, acc):
        r = pl.multiple_of(j * tb, tb)
        return acc + jnp.sum((bits_s[pl.ds(r, tb), :] > thr).astype(F32), axis=0, keepdims=True)

    need = cap - lax.fori_loop(0, nb, count_gt, jnp.zeros((1, LANES), F32))
    strict = (_iota((tb, tb), 0) > _iota((tb, tb), 1)).astype(BF16)

    def select(j, carry):
        r = pl.multiple_of(j * tb, tb)
        blk = bits_s[pl.ds(r, tb), :]
        eq = blk == thr
        before = carry + _dot(strict, eq.astype(BF16))
        sel = jnp.logical_or(blk > thr, jnp.logical_and(eq, before < need))
        sel_s[pl.ds(r, tb), :] = sel.astype(F32)
        return carry + jnp.sum(eq.astype(F32), axis=0, keepdims=True)

    lax.fori_loop(0, nb, select, jnp.zeros((1, LANES), F32))

    def place(j, carry):
        r = pl.multiple_of(j * tb, tb)
        sel = sel_s[pl.ds(r, tb), :]
        slot = carry + _dot(strict, sel.astype(BF16))
        pos_ref[0, pl.ds(r, tb), :] = jnp.where(sel > 0.5, slot, -1.0).astype(I32)
        s0_ref[0, pl.ds(j, 1), :] = carry.astype(I32)
        return carry + jnp.sum(sel, axis=0, keepdims=True)

    lax.fori_loop(0, nb, place, jnp.zeros((1, LANES), F32))


def _topk(aff, cap, tb):
    b, t, _ = aff.shape
    nb = t // tb
    return pl.pallas_call(
        functools.partial(_topk_kernel, cap=cap, tb=tb),
        grid=(b,),
        in_specs=[pl.BlockSpec((1, t, LANES), lambda bi: (bi, 0, 0))],
        out_specs=[pl.BlockSpec((1, t, LANES), lambda bi: (bi, 0, 0)),
                   pl.BlockSpec((1, nb, LANES), lambda bi: (bi, 0, 0))],
        out_shape=[jax.ShapeDtypeStruct((b, t, LANES), I32), jax.ShapeDtypeStruct((b, nb, LANES), I32)],
        scratch_shapes=[pltpu.VMEM((t, LANES), I32), pltpu.VMEM((t, LANES), F32)],
        compiler_params=_cp("parallel"),
        name="moe_topk",
    )(aff)


MOE_SMALL_COUNT = 32


def _windows(cap, tb):
    if cap <= tb + BF16_ROWS:
        return (cap,)
    return (MOE_SMALL_COUNT + BF16_ROWS, tb + BF16_ROWS)


def _window_starts(s0_ref, bi, j, cap, w):
    if w == cap:
        return [0] * N_EXPERTS
    out = []
    for e in range(N_EXPERTS):
        st = jnp.minimum((s0_ref[bi, j, e] // BF16_ROWS) * BF16_ROWS, cap - w)
        out.append(pl.multiple_of(st, BF16_ROWS))
    return out


def _per_window(windows, narrow_ref, bi, j, run):
    if len(windows) == 1:
        run(windows[0])
    else:
        narrow = narrow_ref[bi, j] > 0
        pl.when(narrow)(lambda: run(windows[0]))
        pl.when(jnp.logical_not(narrow))(lambda: run(windows[1]))


def _gather_kernel(s0_ref, narrow_ref, h_ref, post_ref, o_ref, *, cap, tb):
    bi = pl.program_id(0)
    j = pl.program_id(1)

    @pl.when(j == 0)
    def _():
        o_ref[...] = jnp.zeros_like(o_ref)

    def run(w):
        starts = _window_starts(s0_ref, bi, j, cap, w)
        post = post_ref[0]
        r = _iota((w, tb), 0)
        onehot = jnp.concatenate(
            [jnp.where((post[e:e + 1, :] - starts[e]) == r, 1.0, 0.0).astype(BF16) for e in range(N_EXPERTS)],
            axis=0)
        rows = _dot(onehot, h_ref[0]).astype(BF16)
        for e in range(N_EXPERTS):
            cur = o_ref[0, e, pl.ds(starts[e], w), :]
            o_ref[0, e, pl.ds(starts[e], w), :] = cur + rows[e * w:(e + 1) * w]

    _per_window(_windows(cap, tb), narrow_ref, bi, j, run)


def _gather(h2, post, s0, narrow, cap, tb):
    b, t, d = h2.shape
    nb = t // tb
    grid_spec = pltpu.PrefetchScalarGridSpec(
        num_scalar_prefetch=2,
        grid=(b, nb),
        in_specs=[pl.BlockSpec((1, tb, d), lambda bi, j, s, n: (bi, j, 0)),
                  pl.BlockSpec((1, N_EXPERTS, tb), lambda bi, j, s, n: (bi, 0, j))],
        out_specs=pl.BlockSpec((1, N_EXPERTS, cap, d), lambda bi, j, s, n: (bi, 0, 0, 0)),
    )
    return pl.pallas_call(
        functools.partial(_gather_kernel, cap=cap, tb=tb),
        grid_spec=grid_spec,
        out_shape=jax.ShapeDtypeStruct((b, N_EXPERTS, cap, d), BF16),
        compiler_params=_cp("parallel", "arbitrary"),
        name="moe_gather",
    )(s0, narrow, h2, post)


def _expert_kernel(x_ref, wg_ref, wu_ref, wd_ref, y_ref, wg_s, wu_s, wd_s):
    @pl.when(pl.program_id(1) == 0)
    def _():
        wg_s[...] = wg_ref[0].astype(BF16)
        wu_s[...] = wu_ref[0].astype(BF16)
        wd_s[...] = wd_ref[0].astype(BF16)

    shp = x_ref.shape
    x = x_ref[...].reshape(shp[0] * shp[2], shp[3])
    a = _dot(x, wg_s[...])
    u = _dot(x, wu_s[...])
    hm = (_silu(a) * u).astype(BF16)
    y_ref[...] = _dot(hm, wd_s[...]).astype(BF16).reshape(shp)


def _experts(xin, w_gate, w_up, w_down, layer):
    b, e, cap, d = xin.shape
    f = w_gate.shape[-1]
    bb = max(1, min(b, 512 // cap))
    xs = pl.BlockSpec((bb, 1, cap, d), lambda ei, bi: (bi, ei, 0, 0))
    ws = lambda a: pl.BlockSpec((None, 1) + a.shape[2:], lambda ei, bi: (layer, ei, 0, 0))
    return pl.pallas_call(
        _expert_kernel,
        grid=(e, b // bb),
        in_specs=[xs, ws(w_gate), ws(w_up), ws(w_down)],
        out_specs=xs,
        out_shape=jax.ShapeDtypeStruct(xin.shape, BF16),
        scratch_shapes=[pltpu.VMEM((d, f), BF16), pltpu.VMEM((d, f), BF16), pltpu.VMEM((f, d), BF16)],
        compiler_params=_cp("parallel", "arbitrary"),
        name="moe_experts",
    )(xin, w_gate, w_up, w_down)


def _combine_kernel(*refs, cap, tb, final):
    windows = _windows(cap, tb)
    s0_ref, narrow_ref, y_ref, pos_ref, aff_ref, x_ref, gate_ref = refs[:7]
    pos_n = 7
    if final:
        gf_ref = refs[pos_n]
        pos_n += 1
    spread_refs = refs[pos_n:pos_n + 2 * len(windows)]
    o_ref = refs[-1]
    bi = pl.program_id(0)
    j = pl.program_id(1)
    offset = 1024
    lane = _iota((1, N_EXPERTS), 1)

    def run(w):
        k = windows.index(w)
        spread = spread_refs[2 * k][...]
        row_id = spread_refs[2 * k + 1][...]
        starts = _window_starts(s0_ref, bi, j, cap, w)
        stv = jnp.zeros((1, N_EXPERTS), I32)
        for e in range(N_EXPERTS):
            stv = jnp.where(lane == e, starts[e], stv)
        rel = pos_ref[0] - stv + offset
        hi = lax.shift_right_logical(rel, 5).astype(F32).astype(BF16)
        lo = jnp.bitwise_and(rel, 31).astype(F32).astype(BF16)
        relx = 32.0 * _dot(hi, spread) + _dot(lo, spread) - offset
        gx = _dot_sel_right(aff_ref[0], spread)
        g = jnp.where(relx == row_id, gx, 0.0)
        ghi = g.astype(BF16)
        glo = (g - ghi.astype(F32)).astype(BF16)
        ys = jnp.concatenate([y_ref[0, e, pl.ds(starts[e], w), :] for e in range(N_EXPERTS)], axis=0)
        out = x_ref[0] + gate_ref[0] * (_dot(ghi, ys) + _dot(glo, ys))
        if final:
            out = _rms_rows(out, gf_ref[...])
        o_ref[0] = out

    _per_window(windows, narrow_ref, bi, j, run)


def _combine(y, pos, aff, x3, gate, s0, narrow, cap, tb, per_batch, g_final):
    b, t, d = x3.shape
    nb = t // tb
    final = g_final is not None
    tok = lambda c: pl.BlockSpec((1, tb, c), lambda bi, j, s, n: (bi, j, 0))
    const = lambda a: pl.BlockSpec(a.shape, lambda bi, j, s, n: (0, 0))
    in_specs = [pl.BlockSpec((1, N_EXPERTS, cap, d), lambda bi, j, s, n: (bi, 0, 0, 0)),
                tok(N_EXPERTS), tok(N_EXPERTS), tok(d),
                pl.BlockSpec((1, 1, d), (lambda bi, j, s, n: (bi, 0, 0)) if per_batch
                             else (lambda bi, j, s, n: (0, 0, 0)))]
    args = [y, pos, aff, x3, gate]
    if final:
        in_specs.append(const(g_final))
        args.append(g_final)
    for w in _windows(cap, tb):
        spread = jnp.repeat(jnp.eye(N_EXPERTS, dtype=BF16), w, axis=1)
        row_id = jnp.tile(jnp.arange(w, dtype=F32), N_EXPERTS).reshape(1, -1)
        in_specs += [const(spread), const(row_id)]
        args += [spread, row_id]
    grid_spec = pltpu.PrefetchScalarGridSpec(
        num_scalar_prefetch=2, grid=(b, nb), in_specs=in_specs, out_specs=tok(d))
    return pl.pallas_call(
        functools.partial(_combine_kernel, cap=cap, tb=tb, final=final),
        grid_spec=grid_spec,
        out_shape=jax.ShapeDtypeStruct((b, t, d), F32),
        compiler_params=_cp("parallel", "arbitrary"),
        name="moe_combine",
    )(s0, narrow, *args)


def _moe(x2, b, t, g, shift, scale, gate, w_router_pad, w_gate, w_up, w_down, layer, per_batch, g_final):
    cap = CAPACITY_FACTOR * t // N_EXPERTS
    tb = min(MOE_TOKEN_BLOCK, t)
    h2, aff = _router(x2, g, shift, scale, w_router_pad, t, per_batch)
    aff3 = aff.reshape(b, t, N_EXPERTS)
    per = LANES // N_EXPERTS
    groups = -(-b // per)
    nb = t // tb

    def pack(a):
        a = jnp.pad(a, ((0, groups * per - b), (0, 0), (0, 0)))
        return a.reshape(groups, per, a.shape[1], N_EXPERTS).transpose(0, 2, 1, 3).reshape(groups, a.shape[1], LANES)

    def unpack(a):
        return a.reshape(groups, a.shape[1], per, N_EXPERTS).transpose(0, 2, 1, 3).reshape(
            groups * per, a.shape[1], N_EXPERTS)[:b]

    pos, s0 = _topk(pack(aff3), cap, tb)
    pos, s0 = unpack(pos), unpack(s0)
    counts = jnp.diff(s0, axis=1, append=jnp.full((b, 1, N_EXPERTS), cap, I32))
    narrow = jnp.all(counts <= MOE_SMALL_COUNT, axis=-1).astype(I32)
    xin = _gather(h2.reshape(b, t, D_MODEL), pos.transpose(0, 2, 1), s0, narrow, cap, tb)
    y = _experts(xin, w_gate, w_up, w_down, layer)
    out = _combine(y, pos, aff3, x2.reshape(b, t, D_MODEL), gate, s0, narrow, cap, tb, per_batch, g_final)
    return out.reshape(b * t, D_MODEL)


def _rope_tables(t):
    rows = t // GRID_W
    row = jnp.repeat(jnp.arange(rows, dtype=F32), GRID_W)
    col = jnp.tile(jnp.arange(GRID_W, dtype=F32), rows)
    n_freq = DQK_A // 4
    inv_freq = jnp.power(ROPE_BASE, -jnp.arange(n_freq, dtype=F32) / n_freq)
    ang = jnp.concatenate([row[:, None] * inv_freq, col[:, None] * inv_freq], axis=-1)
    cos, sin = jnp.cos(ang), jnp.sin(ang)
    zero = jnp.zeros_like(sin)
    reps = LANES // DQK_A
    cos_t = jnp.tile(jnp.concatenate([cos, cos], axis=-1), (1, reps))
    sin_a = jnp.tile(jnp.concatenate([-sin, zero], axis=-1), (1, reps))
    sin_b = jnp.tile(jnp.concatenate([zero, sin], axis=-1), (1, reps))
    return cos_t, sin_a, sin_b


def _stream(x, mods, per_batch, rope, cache, p):
    b, t, d = x.shape
    x2 = x.reshape(b * t, d)
    depth = p["g_norm"].shape[0]
    even_out, odd_out = [], []
    for l in range(depth):
        j = l // 2
        sh1, sc1, gt1, sh2, sc2, gt2 = mods[l]
        g1 = p["g_norm"][l, 0].reshape(1, d)
        g2 = p["g_norm"][l, 1].reshape(1, d)
        if l % 2 == 0:
            aq, ak, av, bqk, bv, bo, gates = _in_ab(x2, g1, sh1, sc1, p["w_in_ab_main"][j], p["w_in_ab_gate"][j],
                                                     rope, t, per_batch, F32 if cache is None else BF16)
            r3 = lambda a: a.reshape(b, t, a.shape[-1])
            if cache is not None:
                kc = cache[0][:, j].reshape(b, -1, W_A)
                vc = cache[1][:, j].reshape(b, -1, W_A)
                nu = 2 * H_B
                init = (cache[2][:, j].reshape(b, nu, DH_B, DH_B),
                        cache[3][:, j].reshape(b, nu, 1, DH_B),
                        jnp.broadcast_to(cache[4][:, j].reshape(b, nu, 1, 1), (b, nu, 1, LANES)))
            else:
                kc = vc = init = None
            lam_init = 0.8 - 0.6 * math.exp(-0.3 * l)
            attn = _attention(r3(aq), r3(ak), r3(av), kc, vc, p["lam_ab"][j], lam_init)
            qk = _conv_silu(r3(bqk), p["conv_ab"][j])
            bias = jnp.pad(jnp.concatenate([p["b_ig_ab"][j].reshape(-1), p["b_fg_ab"][j].reshape(-1)]),
                           (0, LANES - 4 * H_B)).reshape(1, LANES)
            res = _mlstm(qk, r3(bv), r3(gates), bias, init, cache is None)
            hf, hb = res[0], res[1]
            if cache is None:
                even_out.append((ak, av, res[2], res[3], res[4]))
            x2 = _out_ab(attn.reshape(b * t, W_A), hf.reshape(b * t, W_B), hb.reshape(b * t, W_B), bo, x2, gt1,
                         p["g_attn_ab"][j].reshape(1, W_A), p["g_mlstm_ab"][j].reshape(1, W_B),
                         p["w_out_ab_bf16"][j], t, per_batch, 1.0 - lam_init)
        else:
            q, v, og, lf, kk = _in_c(x2, g1, sh1, sc1, p["w_in_c_bf16"][j], p["b_f_c"][j], p["gamma_lb"], l, t,
                                     per_batch)
            r3 = lambda a: a.reshape(b, t, a.shape[-1])
            init = None if cache is None else cache[5][:, j]
            res = _hgrn(r3(q), r3(v), r3(lf), r3(kk), init, cache is None)
            if cache is None:
                odd_out.append(res[2])
            x2 = _out_c(res[0].reshape(b * t, W_C), res[1].reshape(b * t, W_C), og, x2, gt1,
                        p["g_hgrn_c"][j].reshape(1, W_C), p["w_out_c_bf16"][j], t, per_batch)
        g_final = p["g_final"].reshape(1, d) if l == depth - 1 else None
        x2 = _moe(x2, b, t, g2, sh2, sc2, gt2, p["w_router_pad"][l], p["w_gate_e"], p["w_up_e"],
                  p["w_down_e"], l, per_batch, g_final)
    return x2.reshape(b, t, d), even_out, odd_out


def kernel(x_prompt, x_sample, cache_dattn_k, cache_dattn_v, state_mlstm_C, state_mlstm_n, state_mlstm_m,
           state_hgrn_S, c, c_ctx, w_ada, b_ada, g_norm, g_final, w_in_ab, b_ig_ab, b_fg_ab, lam_ab, conv_ab,
           g_attn_ab, g_mlstm_ab, w_out_ab, w_in_c, b_f_c, gamma_lb, g_hgrn_c, w_out_c,
           w_router, w_gate_e, w_up_e, w_down_e):
    d = D_MODEL
    bs = c.shape[0]
    bp, tp, _ = x_prompt.shape
    depth = w_ada.shape[0]
    n_main = 3 * W_A + 4 * W_B
    p = dict(
        g_norm=g_norm, g_final=g_final, b_ig_ab=b_ig_ab, b_fg_ab=b_fg_ab, lam_ab=lam_ab, conv_ab=conv_ab,
        g_attn_ab=g_attn_ab, g_mlstm_ab=g_mlstm_ab, b_f_c=b_f_c, gamma_lb=gamma_lb, g_hgrn_c=g_hgrn_c,
        w_gate_e=w_gate_e, w_up_e=w_up_e, w_down_e=w_down_e,
        w_in_ab_main=w_in_ab[:, :, :n_main].astype(BF16),
        w_in_ab_gate=jnp.pad(w_in_ab[:, :, n_main:], ((0, 0), (0, 0), (0, LANES - 4 * H_B))).astype(BF16),
        w_out_ab_bf16=w_out_ab.astype(BF16),
        w_in_c_bf16=w_in_c.astype(BF16),
        w_out_c_bf16=w_out_c.astype(BF16),
        w_router_pad=jnp.pad(w_router, ((0, 0), (0, 0), (0, LANES - N_EXPERTS))),
    )
    rows = -(-(bs + 1) // 8) * 8
    cond = jnp.concatenate([c, c_ctx.reshape(1, d), jnp.zeros((rows - bs - 1, d), F32)], axis=0)
    mods = _ada(cond, w_ada, b_ada)
    mods_s = [[mods[l, :bs, i * d:(i + 1) * d].reshape(bs, 1, d) for i in range(6)] for l in range(depth)]
    mods_c = [[mods[l, bs:bs + 1, i * d:(i + 1) * d].reshape(1, 1, d) for i in range(6)] for l in range(depth)]

    y_prompt, ev, od = _stream(x_prompt, mods_c, False, None, None, p)
    new_k = jnp.stack([e[0].reshape(bp, tp, H_A, 2 * DQK_A) for e in ev], axis=1)
    new_v = jnp.stack([e[1].reshape(bp, tp, H_A, DV_A) for e in ev], axis=1)
    new_c = jnp.stack([e[2].reshape(bp, 2, H_B, DH_B, DH_B) for e in ev], axis=1)
    new_n = jnp.stack([e[3].reshape(bp, 2, H_B, DH_B) for e in ev], axis=1)
    new_m = jnp.stack([e[4][..., 0].reshape(bp, 2, H_B) for e in ev], axis=1)
    new_s = jnp.stack(od, axis=1)

    cache = (cache_dattn_k, cache_dattn_v, state_mlstm_C, state_mlstm_n, state_mlstm_m, state_hgrn_S)
    y_sample, _, _ = _stream(x_sample, mods_s, True, _rope_tables(x_sample.shape[1]), cache, p)
    return (y_prompt, y_sample, new_k, new_v, new_c, new_n, new_m, new_s)
```

```python
import functools
import math

import jax
import jax.numpy as jnp
from jax import lax
from jax.experimental import pallas as pl
from jax.experimental.pallas import tpu as pltpu

F32 = jnp.float32
BF16 = jnp.bfloat16
I32 = jnp.int32

D_MODEL = 1024
H_A = 4
DV_A = 128
DQK_A = 64
W_A = H_A * DV_A
H_B = 4
DH_B = 128
W_B = H_B * DH_B
H_C = 8
DH_C = 128
W_C = H_C * DH_C
N_EXPERTS = 16
CAPACITY_FACTOR = 2
GRID_W = 64
ROPE_BASE = 10000.0
RMS_EPS = 1e-6
NEG_BIG = -1e30
LOG2E = 1.4426950408889634
LANES = 128
BF16_ROWS = 16
SCAN_CHUNK = 128
MOE_TOKEN_BLOCK = 128
VMEM_LIMIT = 56 * 1024 * 1024


def _cp(*sem):
    return pltpu.CompilerParams(dimension_semantics=sem, vmem_limit_bytes=VMEM_LIMIT)


def _dot(a, b):
    return jnp.dot(a, b, preferred_element_type=F32)


def _dot_nt(a, b):
    return lax.dot_general(a, b, (((1,), (1,)), ((), ())), preferred_element_type=F32)


def _dot_tn(a, b):
    return lax.dot_general(a, b, (((0,), (0,)), ((), ())), preferred_element_type=F32)


def _split3(x):
    hi = x.astype(BF16)
    r = x - hi.astype(F32)
    mid = r.astype(BF16)
    lo = (r - mid.astype(F32)).astype(BF16)
    return hi, mid, lo


def _dot_sel_left(sel, x):
    hi, mid, lo = _split3(x)
    return _dot(sel, hi) + _dot(sel, mid) + _dot(sel, lo)


def _dot_sel_right(x, sel):
    hi, mid, lo = _split3(x)
    return _dot(hi, sel) + _dot(mid, sel) + _dot(lo, sel)


def _dot_f32(a, b):
    ah, am, al = _split3(a)
    bh, bm, bl = _split3(b)
    return (_dot(ah, bh) + (_dot(ah, bm) + _dot(am, bh))
            + (_dot(ah, bl) + _dot(am, bm) + _dot(al, bh)))


def _dot_f32x3(a, b):
    ah = a.astype(BF16)
    al = (a - ah.astype(F32)).astype(BF16)
    bh = b.astype(BF16)
    bl = (b - bh.astype(F32)).astype(BF16)
    return _dot(ah, bh) + (_dot(ah, bl) + _dot(al, bh))


def _dot_sel_left2(sel, x):
    hi = x.astype(BF16)
    lo = (x - hi.astype(F32)).astype(BF16)
    return _dot(sel, hi) + _dot(sel, lo)


def _sigmoid(x):
    return 1.0 / (1.0 + jnp.exp(-x))


def _silu(x):
    return x * _sigmoid(x)


def _log_sigmoid(x):
    return jnp.minimum(x, 0.0) - jnp.log(1.0 + jnp.exp(-jnp.abs(x)))


def _rms_rows(x, g):
    return x * lax.rsqrt(jnp.mean(x * x, axis=-1, keepdims=True) + RMS_EPS) * g


def _iota(shape, dim):
    return lax.broadcasted_iota(I32, shape, dim)


def _ada_kernel(c_ref, w_ref, b_ref, o_ref):
    c = c_ref[...]
    o_ref[0] = _dot_f32(_silu(c), w_ref[0]) + b_ref[0]


def _ada(cond_rows, w_ada, b_ada):
    depth, d, n6 = w_ada.shape
    rows = cond_rows.shape[0]
    tn = 1536
    return pl.pallas_call(
        _ada_kernel,
        grid=(depth, n6 // tn),
        in_specs=[pl.BlockSpec((rows, d), lambda l, j: (0, 0)),
                  pl.BlockSpec((1, d, tn), lambda l, j: (l, 0, j)),
                  pl.BlockSpec((1, 1, tn), lambda l, j: (l, 0, j))],
        out_specs=pl.BlockSpec((1, rows, tn), lambda l, j: (l, 0, j)),
        out_shape=jax.ShapeDtypeStruct((depth, rows, n6), F32),
        compiler_params=_cp("parallel", "parallel"),
        name="adaln",
    )(cond_rows, w_ada, b_ada.reshape(depth, 1, n6))


def _mod_spec(per_batch, tm, t):
    if per_batch:
        return pl.BlockSpec((1, 1, D_MODEL), lambda i: (i * tm // t, 0, 0))
    return pl.BlockSpec((1, 1, D_MODEL), lambda i: (0, 0, 0))


def _row_tile(t):
    return min(512, t)


def _in_ab_kernel(*refs, use_rope):
    if use_rope:
        (x_ref, g_ref, sh_ref, sc_ref, w_ref, wg_ref, cos_ref, sa_ref, sb_ref,
         aq_ref, ak_ref, av_ref, bqk_ref, bv_ref, bo_ref, gt_ref) = refs
    else:
        (x_ref, g_ref, sh_ref, sc_ref, w_ref, wg_ref,
         aq_ref, ak_ref, av_ref, bqk_ref, bv_ref, bo_ref, gt_ref) = refs
    h = _rms_rows(x_ref[...], g_ref[...]) * (1.0 + sc_ref[0]) + sh_ref[0]
    hb = h.astype(BF16)
    y = _dot(hb, w_ref[...])
    gt_ref[...] = _dot(hb, wg_ref[...])
    aq = y[:, 0:W_A] * (DQK_A ** -0.5 * LOG2E)
    ak = y[:, W_A:2 * W_A]
    if use_rope:
        cos = jnp.concatenate([cos_ref[...]] * H_A, axis=1)
        sa = jnp.concatenate([sa_ref[...]] * H_A, axis=1)
        sb = jnp.concatenate([sb_ref[...]] * H_A, axis=1)
        half = DQK_A // 2

        def rope(v):
            return (v * cos + pltpu.roll(v, W_A - half, axis=1) * sa
                    + pltpu.roll(v, half, axis=1) * sb)

        aq = rope(aq)
        ak = rope(ak)
    aq_ref[...] = aq.astype(aq_ref.dtype)
    ak_ref[...] = ak.astype(ak_ref.dtype)
    av_ref[...] = y[:, 2 * W_A:3 * W_A].astype(av_ref.dtype)
    bqk_ref[...] = y[:, 3 * W_A:3 * W_A + 2 * W_B]
    bv_ref[...] = y[:, 3 * W_A + 2 * W_B:3 * W_A + 3 * W_B]
    bo_ref[...] = y[:, 3 * W_A + 3 * W_B:3 * W_A + 4 * W_B]


def _in_ab(x2, g, shift, scale, w_main, w_gate, rope, t, per_batch, qkv_dtype):
    n = x2.shape[0]
    tm = _row_tile(t)
    use_rope = rope is not None
    row = lambda c: pl.BlockSpec((tm, c), lambda i: (i, 0))
    full = lambda a: pl.BlockSpec(a.shape, lambda i: (0,) * a.ndim)
    in_specs = [row(D_MODEL), full(g), _mod_spec(per_batch, tm, t), _mod_spec(per_batch, tm, t),
                full(w_main), full(w_gate)]
    args = [x2, g, shift, scale, w_main, w_gate]
    if use_rope:
        nt = t // tm
        tab = pl.BlockSpec((tm, LANES), lambda i: (i % nt, 0))
        in_specs += [tab, tab, tab]
        args += list(rope)
    widths = (W_A, W_A, W_A, 2 * W_B, W_B, W_B, LANES)
    return pl.pallas_call(
        functools.partial(_in_ab_kernel, use_rope=use_rope),
        grid=(n // tm,),
        in_specs=in_specs,
        out_specs=[row(c) for c in widths],
        out_shape=[jax.ShapeDtypeStruct((n, c), qkv_dtype if i < 3 else F32) for i, c in enumerate(widths)],
        compiler_params=_cp("parallel"),
        name="in_proj_even",
    )(*args)


def _attn_pipe_kernel(*refs, has_cache, tk, lam_init, nq):
    if has_cache:
        q_ref, k_ref, v_ref, kc_ref, vc_ref, lam_ref, o_ref, s_s, mp_s, vt_s, sc_s, vct_s = refs
    else:
        q_ref, k_ref, v_ref, lam_ref, o_ref, s_s, mp_s, vt_s = refs
    i = pl.program_id(2)
    q = q_ref[0]
    tq = q.shape[0]
    first = _iota((1, LANES), 1) < DQK_A
    qs = (jnp.where(first, q, 0).astype(BF16), jnp.where(first, 0, q).astype(BF16))
    nkb = k_ref.shape[1] // tk

    @pl.when(i == 0)
    def _():
        for j in range(nkb):
            vt_s[j, 0:DV_A, :] = v_ref[0, j * tk:(j + 1) * tk, :].astype(F32).T.astype(BF16)
            vt_s[j, DV_A:, :] = jnp.ones((BF16_ROWS, tk), BF16)
        if has_cache:
            vct_s[0:DV_A, :] = vc_ref[0].astype(F32).T.astype(BF16)
            vct_s[DV_A:, :] = jnp.ones((BF16_ROWS, vct_s.shape[1]), BF16)

    def groups(x):
        return [x[:, g * LANES:(g + 1) * LANES] for g in range(x.shape[1] // LANES)]

    def body(score, finish, cur):
        prev = 1 - cur
        if finish:
            m = [jnp.max(mp_s[prev, a], axis=-1, keepdims=True) for a in range(2)]
        mpart = [jnp.full((tq, LANES), NEG_BIG, F32)] * 2
        acc = [jnp.zeros((DV_A + BF16_ROWS, tq), F32)] * 2

        def pass1(kb, store):
            for a in range(2):
                s = _dot_nt(qs[a], kb)
                store(a, s)
                mpart[a] = functools.reduce(jnp.maximum, groups(s), mpart[a])

        def pass2(a, s, vt):
            acc[a] = acc[a] + _dot_nt(vt, jnp.exp2(s - m[a]).astype(BF16))

        if has_cache:
            def store_c(a, s):
                sc_s[cur, a] = s
            if score:
                pass1(kc_ref[0].astype(BF16), store_c)
            if finish:
                for a in range(2):
                    pass2(a, sc_s[prev, a], vct_s[...])
        for j in range(nkb):
            def store_j(a, s, j=j):
                s_s[cur, a, j] = s
            if score:
                pass1(k_ref[0, j * tk:(j + 1) * tk, :].astype(BF16), store_j)
            if finish:
                for a in range(2):
                    pass2(a, s_s[prev, a, j], vt_s[j])
        if score:
            for a in range(2):
                mp_s[cur, a] = mpart[a]
        if finish:
            lp = lam_ref[...]
            lam = (jnp.exp(jnp.sum(lp[0:1] * lp[1:2], axis=-1, keepdims=True))
                   - jnp.exp(jnp.sum(lp[2:3] * lp[3:4], axis=-1, keepdims=True)) + lam_init)
            o1 = acc[0][0:DV_A] / acc[0][DV_A:DV_A + 1]
            o2 = acc[1][0:DV_A] / acc[1][DV_A:DV_A + 1]
            o_ref[0] = (o1 - lam * o2).T

    pl.when(i == 0)(lambda: body(True, False, 0))
    inner = jnp.logical_and(i > 0, i < nq)
    pl.when(jnp.logical_and(inner, i % 2 == 0))(lambda: body(True, True, 0))
    pl.when(jnp.logical_and(inner, i % 2 == 1))(lambda: body(True, True, 1))
    pl.when(i == nq)(lambda: body(False, True, nq % 2))


def _attention_pipe(aq, ak, av, kc, vc, lam_p, lam_init):
    b, t, _ = aq.shape
    tq = min(256, t)
    tk = min(512, t)
    nq = t // tq
    has_cache = kc is not None
    qspec = pl.BlockSpec((1, tq, DV_A), lambda bi, h, i: (bi, jnp.minimum(i, nq - 1), h))
    ospec = pl.BlockSpec((1, tq, DV_A), lambda bi, h, i: (bi, jnp.maximum(i - 1, 0), h))
    ks = pl.BlockSpec((1, t, DV_A), lambda bi, h, i: (bi, 0, h))
    in_specs = [qspec, ks, ks]
    args = [aq, ak, av]
    if has_cache:
        cs = pl.BlockSpec((1, kc.shape[1], DV_A), lambda bi, h, i: (bi, 0, h))
        in_specs += [cs, cs]
        args += [kc, vc]
    in_specs.append(pl.BlockSpec(lam_p.shape, lambda bi, h, i: (0, 0)))
    args.append(lam_p)
    scratch = [pltpu.VMEM((2, 2, t // tk, tq, tk), F32), pltpu.VMEM((2, 2, tq, LANES), F32),
               pltpu.VMEM((t // tk, DV_A + BF16_ROWS, tk), BF16)]
    if has_cache:
        scratch += [pltpu.VMEM((2, 2, tq, kc.shape[1]), F32),
                    pltpu.VMEM((DV_A + BF16_ROWS, kc.shape[1]), BF16)]
    return pl.pallas_call(
        functools.partial(_attn_pipe_kernel, has_cache=has_cache, tk=tk, lam_init=lam_init, nq=nq),
        grid=(b, H_A, nq + 1),
        in_specs=in_specs,
        out_specs=ospec,
        out_shape=jax.ShapeDtypeStruct((b, t, W_A), F32),
        scratch_shapes=scratch,
        compiler_params=_cp("parallel", "parallel", "arbitrary"),
        name="diff_attention",
    )(*args)


def _conv_kernel(x_ref, w_ref, o_ref):
    x = x_ref[0]
    t = x.shape[0]
    w = w_ref[...]
    r = _iota((t, 1), 0)
    prev = jnp.where(r == 0, 0.0, pltpu.roll(x, 1, axis=0))
    nxt = jnp.where(r == t - 1, 0.0, pltpu.roll(x, t - 1, axis=0))
    o_ref[0] = _silu(prev * w[0:1] + x * w[1:2] + nxt * w[2:3])


def _conv_silu(bqk, conv_w):
    b, t, c = bqk.shape
    tc = 256
    spec = pl.BlockSpec((1, t, tc), lambda bi, j: (bi, 0, j))
    return pl.pallas_call(
        _conv_kernel,
        grid=(b, c // tc),
        in_specs=[spec, pl.BlockSpec((conv_w.shape[0], tc), lambda bi, j: (0, j))],
        out_specs=spec,
        out_shape=jax.ShapeDtypeStruct(bqk.shape, F32),
        compiler_params=_cp("parallel", "parallel"),
        name="mlstm_conv",
    )(bqk, conv_w)


def _tri(n, lower):
    r = _iota((n, n), 0)
    c = _iota((n, n), 1)
    return r >= c if lower else r <= c


def _mlstm_kernel(*refs, has_init, out_state):
    qf, kf, vf, gf, qb, kb, vb, gb, bias = refs[:9]
    pos = 9
    if has_init:
        c0, n0, m0 = refs[pos:pos + 3]
        pos += 3
    hf, hb = refs[pos:pos + 2]
    pos += 2
    if out_state:
        c_out, n_out, m_out = refs[pos:pos + 3]
        pos += 3
    cn_s, m_s = refs[pos:pos + 2]
    step = pl.program_id(1)
    L = SCAN_CHUNK
    nu = 2 * H_B

    @pl.when(step == 0)
    def _():
        if has_init:
            for u in range(nu):
                cn_s[u, :, 0:DH_B] = c0[0, u]
                cn_s[u, :, DH_B:2 * DH_B] = jnp.broadcast_to(n0[0, u], (DH_B, DH_B)).T
            m_s[...] = m0[0]
        else:
            cn_s[...] = jnp.zeros_like(cn_s)
            m_s[...] = jnp.zeros_like(m_s)

    ones_b = jnp.ones((L, DH_B), BF16)
    sel_rows = _iota((LANES, LANES), 0)
    lower = _tri(L, True)
    upper = _tri(L, False)
    lower_b = lower.astype(BF16)
    upper_b = upper.astype(BF16)
    col = _iota((1, LANES), 1)
    units = []
    for d, (q_ref, k_ref, v_ref, g_ref, h_ref) in enumerate(((qf, kf, vf, gf, hf), (qb, kb, vb, gb, hb))):
        g = g_ref[0] + bias[...]
        g = jnp.where(col < 2 * H_B, g, _log_sigmoid(g))
        gt = g.T
        if d == 0:
            bc_all = _dot_sel_left(lower_b, g)
            br_all = _dot_sel_right(gt, upper_b)
            mask = lower
        else:
            bc_all = _dot_sel_left(upper_b, g)
            br_all = _dot_sel_right(gt, lower_b)
            mask = upper
        for h in range(H_B):
            u = d * H_B + h
            sl = slice(h * DH_B, (h + 1) * DH_B)
            units.append(dict(
                u=u, d=d, sl=sl, h_ref=h_ref, mask=mask, g=g, bc_all=bc_all,
                li_row=gt[u:u + 1, :], b_row=br_all[2 * H_B + u:2 * H_B + u + 1, :],
                q=q_ref[0, :, sl].astype(BF16), k=k_ref[0, :, sl] * (DH_B ** -0.5),
                v1=jnp.concatenate([v_ref[0, :, sl].astype(BF16), ones_b], axis=1)))
    for x in units:
        x["li_b"] = _dot_sel_right(x["g"], (sel_rows == x["u"]).astype(BF16))
        x["b_b"] = _dot_sel_right(x["bc_all"], (sel_rows == 2 * H_B + x["u"]).astype(BF16))
        x["qk"] = _dot_nt(x["q"], x["k"].astype(BF16))
        x["cn"] = cn_s[x["u"]]
        x["qc"] = _dot(x["q"], x["cn"].astype(BF16))
    for x in units:
        x["m_prev"] = m_s[x["u"]]
        a_b = x["b_b"] + x["m_prev"]
        dm = jnp.where(x["mask"], x["b_b"] - x["b_row"] + x["li_row"], NEG_BIG)
        x["m_t"] = jnp.maximum(a_b, jnp.max(dm, axis=-1, keepdims=True))
        x["w_inter"] = jnp.exp(a_b - x["m_t"])
        x["s"] = (x["qk"] * jnp.exp(dm - x["m_t"])).astype(BF16)
    for x in units:
        x["sv"] = _dot(x["s"], x["v1"])
        b_end = x["b_b"][L - 1:L, :] if x["d"] == 0 else x["b_b"][0:1, :]
        g_b = b_end - x["b_b"] + x["li_b"]
        x["m_new"] = jnp.maximum(b_end + x["m_prev"], jnp.max(g_b, axis=0, keepdims=True))
        x["w_old"] = jnp.exp(b_end + x["m_prev"] - x["m_new"])
        x["ks"] = (jnp.exp(g_b - x["m_new"]) * x["k"]).astype(BF16)
    for x in units:
        num = x["w_inter"] * x["qc"][:, 0:DH_B] + x["sv"][:, 0:DH_B]
        den = x["w_inter"] * x["qc"][:, DH_B:] + x["sv"][:, DH_B:]
        x["h_ref"][0, :, x["sl"]] = num / jnp.maximum(jnp.abs(den), jnp.exp(-x["m_t"]))
        w2 = jnp.concatenate([x["w_old"], x["w_old"]], axis=1)
        cn_s[x["u"]] = w2 * x["cn"] + _dot_tn(x["ks"], x["v1"])
        m_s[x["u"]] = x["m_new"]

    if out_state:
        @pl.when(step == pl.num_programs(1) - 1)
        def _():
            for u in range(nu):
                c_out[0, u] = cn_s[u, :, 0:DH_B]
                n_out[0, u] = cn_s[u, :, DH_B:2 * DH_B].T[0:1, :]
            m_out[0] = m_s[...]


def _mlstm(qk, v, gates, bias, init, out_state):
    b, t, _ = v.shape
    L = SCAN_CHUNK
    nc = t // L
    nu = 2 * H_B
    fwd = lambda j: (lambda bi, c: (bi, c, j))
    bwd = lambda j: (lambda bi, c: (bi, nc - 1 - c, j))
    blk = lambda w, im: pl.BlockSpec((1, L, w), im)
    in_specs = [blk(W_B, fwd(0)), blk(W_B, fwd(1)), blk(W_B, fwd(0)), blk(LANES, fwd(0)),
                blk(W_B, bwd(0)), blk(W_B, bwd(1)), blk(W_B, bwd(0)), blk(LANES, bwd(0)),
                pl.BlockSpec((1, LANES), lambda bi, c: (0, 0))]
    args = [qk, qk, v, gates, qk, qk, v, gates, bias]
    st_specs = [pl.BlockSpec((1, nu, DH_B, DH_B), lambda bi, c: (bi, 0, 0, 0)),
                pl.BlockSpec((1, nu, 1, DH_B), lambda bi, c: (bi, 0, 0, 0)),
                pl.BlockSpec((1, nu, 1, LANES), lambda bi, c: (bi, 0, 0, 0))]
    st_shapes = [jax.ShapeDtypeStruct((b, nu, DH_B, DH_B), F32),
                 jax.ShapeDtypeStruct((b, nu, 1, DH_B), F32),
                 jax.ShapeDtypeStruct((b, nu, 1, LANES), F32)]
    has_init = init is not None
    if has_init:
        in_specs += st_specs
        args += list(init)
    out_specs = [blk(W_B, fwd(0)), blk(W_B, bwd(0))]
    out_shape = [jax.ShapeDtypeStruct((b, t, W_B), F32)] * 2
    if out_state:
        out_specs += st_specs
        out_shape += st_shapes
    return pl.pallas_call(
        functools.partial(_mlstm_kernel, has_init=has_init, out_state=out_state),
        grid=(b, nc),
        in_specs=in_specs,
        out_specs=out_specs,
        out_shape=out_shape,
        scratch_shapes=[pltpu.VMEM((nu, DH_B, 2 * DH_B), F32), pltpu.VMEM((nu, 1, LANES), F32)],
        compiler_params=_cp("parallel", "arbitrary"),
        name="mlstm_scan",
    )(*args)


def _head_rms(x, g, heads, width):
    parts = []
    for h in range(heads):
        sl = slice(h * width, (h + 1) * width)
        parts.append(_rms_rows(x[:, sl], g[:, sl]))
    return parts


def _out_ab_kernel(at_ref, hf_ref, hb_ref, bo_ref, x_ref, gate_ref, ga_ref, gm_ref, w_ref, o_ref, *, out_scale):
    oa = [p * out_scale for p in _head_rms(at_ref[...], ga_ref[...], H_A, DV_A)]
    hm = _head_rms(hf_ref[...] + hb_ref[...], gm_ref[...], H_B, DH_B)
    bo = bo_ref[...]
    ob = [_sigmoid(bo[:, h * DH_B:(h + 1) * DH_B]) * hm[h] for h in range(H_B)]
    cat = jnp.concatenate(oa + ob, axis=1).astype(BF16)
    o_ref[...] = x_ref[...] + gate_ref[0] * _dot(cat, w_ref[...])


def _out_ab(attn, hf, hb, bo, x2, gate, g_attn, g_mlstm, w_out, t, per_batch, out_scale):
    n = x2.shape[0]
    tm = _row_tile(t)
    row = lambda c: pl.BlockSpec((tm, c), lambda i: (i, 0))
    full = lambda a: pl.BlockSpec(a.shape, lambda i: (0,) * a.ndim)
    return pl.pallas_call(
        functools.partial(_out_ab_kernel, out_scale=out_scale),
        grid=(n // tm,),
        in_specs=[row(W_A), row(W_B), row(W_B), row(W_B), row(D_MODEL), _mod_spec(per_batch, tm, t),
                  full(g_attn), full(g_mlstm), full(w_out)],
        out_specs=row(D_MODEL),
        out_shape=jax.ShapeDtypeStruct((n, D_MODEL), F32),
        compiler_params=_cp("parallel"),
        name="out_proj_even",
    )(attn, hf, hb, bo, x2, gate, g_attn, g_mlstm, w_out)


def _in_c_kernel(x_ref, g_ref, sh_ref, sc_ref, w_ref, bf_ref, gam_ref, q_ref, v_ref, og_ref, lf_ref, kk_ref, *, layer):
    h = _rms_rows(x_ref[...], g_ref[...]) * (1.0 + sc_ref[0]) + sh_ref[0]
    y = _dot(h.astype(BF16), w_ref[...])
    q_ref[...] = y[:, 0:W_C]
    v_ref[...] = y[:, W_C:2 * W_C]
    og_ref[...] = y[:, 2 * W_C:3 * W_C]
    gam = gam_ref[...]
    e = jnp.exp(gam - jnp.max(gam, axis=0, keepdims=True))
    p = e / jnp.sum(e, axis=0, keepdims=True)
    lbs = p[0:1]
    lbs0 = lbs
    for i in range(1, layer + 1):
        lbs = lbs + p[i:i + 1]
    lb = jnp.maximum(lbs - lbs0, 0.0)
    L = SCAN_CHUNK
    for d in range(2):
        f = y[:, (3 + d) * W_C:(4 + d) * W_C] + bf_ref[d:d + 1]
        u = jnp.exp(-jnp.abs(f))
        neg = f < 0.0
        lf2 = (jnp.log(jnp.where(neg, lb + u, 1.0 + lb * u)) - jnp.log(1.0 + u)) * LOG2E
        kk_ref[:, d * W_C:(d + 1) * W_C] = (1.0 - lb) * jnp.where(neg, 1.0, u) / (1.0 + u)
        tri = _tri(L, d == 0).astype(BF16)
        for c0 in range(0, lf2.shape[0], L):
            lf_ref[c0:c0 + L, d * W_C:(d + 1) * W_C] = _dot_sel_left2(tri, lf2[c0:c0 + L])


def _in_c(x2, g, shift, scale, w_in, b_f, gamma_lb, layer, t, per_batch):
    n = x2.shape[0]
    tm = _row_tile(t)
    row = lambda c: pl.BlockSpec((tm, c), lambda i: (i, 0))
    full = lambda a: pl.BlockSpec(a.shape, lambda i: (0,) * a.ndim)
    widths = (W_C, W_C, W_C, 2 * W_C, 2 * W_C)
    return pl.pallas_call(
        functools.partial(_in_c_kernel, layer=layer),
        grid=(n // tm,),
        in_specs=[row(D_MODEL), full(g), _mod_spec(per_batch, tm, t), _mod_spec(per_batch, tm, t),
                  full(w_in), full(b_f), full(gamma_lb)],
        out_specs=[row(c) for c in widths],
        out_shape=[jax.ShapeDtypeStruct((n, c), F32) for c in widths],
        compiler_params=_cp("parallel"),
        name="in_proj_odd",
    )(x2, g, shift, scale, w_in, b_f, gamma_lb)


HGRN_LEVELS = (8, 16, 32, 64)
HGRN_DIAG = 8
HGRN_HEADS_PER_STEP = 4


def _hgrn_unit(q, kk, v, b, st_ref, b_s, rev, store_o):
    L = SCAN_CHUNK
    b_s[...] = b
    row = _iota((L, 1), 0)
    col = _iota((1, L), 1)
    k16 = kk.astype(BF16)
    v16 = v.astype(BF16)
    st = st_ref[...]
    o = _dot_nt((q * jnp.exp2(b)).astype(BF16), st.astype(BF16))
    yield
    c = HGRN_DIAG
    sub = _iota((c, 1), 0)
    ys = []
    for blk in range(L // c):
        r0 = blk * c
        qb = q[r0:r0 + c]
        bb = b[r0:r0 + c]
        for s in range(c):
            keep = (sub <= s) if rev else (sub >= s)
            ys.append(qb * jnp.exp2(jnp.where(keep, bb - b_s[r0 + s:r0 + s + 1, :], NEG_BIG)))
    rs = _dot_nt(jnp.concatenate(ys, axis=0).astype(BF16), k16)
    yield
    parts = []
    for m in HGRN_LEVELS:
        par = 2 * m
        pieces = []
        for p0 in range(0, L, par):
            ref_row = p0 + m if rev else p0 + m - 1
            pieces.append(jnp.broadcast_to(b_s[ref_row:ref_row + 1, :], (par, LANES)))
        bref = jnp.concatenate(pieces, axis=0)
        in_first = (row % par) < m
        reader = in_first if rev else jnp.logical_not(in_first)
        e = jnp.exp2(-jnp.abs(b - bref))
        qs = jnp.where(reader, q * e, 0.0)
        ks = jnp.where(reader, 0.0, kk * e)
        parts.append((par, _dot_nt(qs.astype(BF16), ks.astype(BF16))))
        yield
    lane = _iota((c, LANES), 1)
    rows = []
    for blk in range(L // c):
        acc = jnp.zeros((c, LANES), F32)
        for s in range(c):
            i = blk * c + s
            acc = jnp.where(lane == i, rs[i * c:(i + 1) * c], acc)
        rows.append(acc)
    att = jnp.concatenate(rows, axis=0)
    for par, a in parts:
        att = att + jnp.where((row // par) == (col // par), a, 0.0)
    o = o + _dot(att.astype(BF16), v16)
    b_end = b_s[0:1, :] if rev else b_s[L - 1:L, :]
    kd = (kk * jnp.exp2(b_end - b)).astype(BF16)
    st_new = st * jnp.exp2(b_end) + _dot_tn(v16, kd)
    yield
    store_o(o)
    st_ref[...] = st_new
    yield


def _hgrn_kernel(*refs, has_init, out_state, hp):
    qf, vf, lff, kkf, qb, vb, lfb, kkb = refs[:8]
    pos = 8
    if has_init:
        s0 = refs[pos]
        pos += 1
    of, ob = refs[pos:pos + 2]
    pos += 2
    if out_state:
        s_out = refs[pos]
        pos += 1
    st_s, b_s = refs[pos:pos + 2]
    step = pl.program_id(2)

    @pl.when(step == 0)
    def _():
        for d in range(2):
            for h in range(hp):
                st_s[d * hp + h] = s0[0, d, h].T if has_init else jnp.zeros((DH_C, DH_C), F32)

    def storer(o_ref, sl):
        def store(o):
            o_ref[0, :, sl] = o
        return store

    units = []
    for d, (q_ref, v_ref, lf_ref, kk_ref, o_ref) in enumerate(((qf, vf, lff, kkf, of), (qb, vb, lfb, kkb, ob))):
        for h in range(hp):
            u = d * hp + h
            sl = slice(h * DH_C, (h + 1) * DH_C)
            units.append(_hgrn_unit(q_ref[0, :, sl], kk_ref[0, :, sl], v_ref[0, :, sl], lf_ref[0, :, sl],
                                    st_s.at[u], b_s.at[u], d == 1, storer(o_ref, sl)))
    live = units
    while live:
        live = [g for g in live if next(g, "done") != "done"]

    if out_state:
        @pl.when(step == pl.num_programs(2) - 1)
        def _():
            for d in range(2):
                for h in range(hp):
                    s_out[0, d, h] = st_s[d * hp + h].T


def _hgrn(q, v, lf, kk, init, out_state):
    b, t, _ = q.shape
    L = SCAN_CHUNK
    nc = t // L
    hp = HGRN_HEADS_PER_STEP
    ng = H_C // hp
    blk = lambda im: pl.BlockSpec((1, L, hp * DH_C), im)
    fwd = lambda off: (lambda bi, h, c: (bi, c, h + off))
    bwd = lambda off: (lambda bi, h, c: (bi, nc - 1 - c, h + off))
    in_specs = [blk(fwd(0)), blk(fwd(0)), blk(fwd(0)), blk(fwd(0)),
                blk(bwd(0)), blk(bwd(0)), blk(bwd(ng)), blk(bwd(ng))]
    args = [q, v, lf, kk, q, v, lf, kk]
    st_spec = pl.BlockSpec((1, 2, hp, DH_C, DH_C), lambda bi, h, c: (bi, 0, h, 0, 0))
    has_init = init is not None
    if has_init:
        in_specs.append(st_spec)
        args.append(init)
    out_specs = [blk(fwd(0)), blk(bwd(0))]
    out_shape = [jax.ShapeDtypeStruct((b, t, W_C), F32)] * 2
    if out_state:
        out_specs.append(st_spec)
        out_shape.append(jax.ShapeDtypeStruct((b, 2, H_C, DH_C, DH_C), F32))
    return pl.pallas_call(
        functools.partial(_hgrn_kernel, has_init=has_init, out_state=out_state, hp=hp),
        grid=(b, ng, nc),
        in_specs=in_specs,
        out_specs=out_specs,
        out_shape=out_shape,
        scratch_shapes=[pltpu.VMEM((2 * hp, DH_C, DH_C), F32), pltpu.VMEM((2 * hp, L, DH_C), F32)],
        compiler_params=_cp("parallel", "parallel", "arbitrary"),
        name="hgrn_scan",
    )(*args)


def _out_c_kernel(of_ref, ob_ref, og_ref, x_ref, gate_ref, gh_ref, w_ref, o_ref):
    hm = _head_rms(of_ref[...] + ob_ref[...], gh_ref[...], H_C, DH_C)
    og = og_ref[...]
    parts = [hm[h] * _silu(og[:, h * DH_C:(h + 1) * DH_C]) for h in range(H_C)]
    cat = jnp.concatenate(parts, axis=1).astype(BF16)
    o_ref[...] = x_ref[...] + gate_ref[0] * _dot(cat, w_ref[...])


def _out_c(of, ob, og, x2, gate, g_hgrn, w_out, t, per_batch):
    n = x2.shape[0]
    tm = _row_tile(t)
    row = lambda c: pl.BlockSpec((tm, c), lambda i: (i, 0))
    full = lambda a: pl.BlockSpec(a.shape, lambda i: (0,) * a.ndim)
    return pl.pallas_call(
        _out_c_kernel,
        grid=(n // tm,),
        in_specs=[row(W_C), row(W_C), row(W_C), row(D_MODEL), _mod_spec(per_batch, tm, t),
                  full(g_hgrn), full(w_out)],
        out_specs=row(D_MODEL),
        out_shape=jax.ShapeDtypeStruct((n, D_MODEL), F32),
        compiler_params=_cp("parallel"),
        name="out_proj_odd",
    )(of, ob, og, x2, gate, g_hgrn, w_out)


def _router_kernel(x_ref, g_ref, sh_ref, sc_ref, wr_ref, h_ref, aff_ref):
    h = _rms_rows(x_ref[...], g_ref[...]) * (1.0 + sc_ref[0]) + sh_ref[0]
    h_ref[...] = h.astype(BF16)
    logits = _dot_f32x3(h, wr_ref[...])
    valid = _iota((1, LANES), 1) < N_EXPERTS
    logits = jnp.where(valid, logits, NEG_BIG)
    e = jnp.exp(logits - jnp.max(logits, axis=-1, keepdims=True))
    e = jnp.where(valid, e, 0.0)
    aff_ref[...] = (e / jnp.sum(e, axis=-1, keepdims=True))[:, :N_EXPERTS]


def _router(x2, g, shift, scale, w_router_pad, t, per_batch):
    n = x2.shape[0]
    tm = _row_tile(t)
    row = lambda c: pl.BlockSpec((tm, c), lambda i: (i, 0))
    full = lambda a: pl.BlockSpec(a.shape, lambda i: (0,) * a.ndim)
    return pl.pallas_call(
        _router_kernel,
        grid=(n // tm,),
        in_specs=[row(D_MODEL), full(g), _mod_spec(per_batch, tm, t), _mod_spec(per_batch, tm, t),
                  full(w_router_pad)],
        out_specs=[row(D_MODEL), row(N_EXPERTS)],
        out_shape=[jax.ShapeDtypeStruct((n, D_MODEL), BF16), jax.ShapeDtypeStruct((n, N_EXPERTS), F32)],
        compiler_params=_cp("parallel"),
        name="moe_router",
    )(x2, g, shift, scale, w_router_pad)


def _topk_kernel(aff_ref, pos_ref, s0_ref, bits_s, sel_s, *, cap, tb):
    t = aff_ref.shape[1]
    nb = t // tb
    bits_s[...] = pltpu.bitcast(aff_ref[0], I32)

    def count_ge(cand):
        def body(j, acc):
            r = pl.multiple_of(j * tb, tb)
            blk = bits_s[pl.ds(r, tb), :]
            return acc + jnp.sum((blk >= cand).astype(F32), axis=0, keepdims=True)
        return lax.fori_loop(0, nb, body, jnp.zeros((1, LANES), F32))

    def bit_body(i, thr):
        cand = thr | lax.shift_left(jnp.int32(1), 30 - i)
        return jnp.where(count_ge(cand) >= cap, cand, thr)

    thr = lax.fori_loop(0, 31, bit_body, jnp.zeros((1, LANES), I32))

    def count_gt(j, acc):
        r = pl.multiple_of(j * tb, tb)
        return acc + jnp.sum((bits_s[pl.ds(r, tb), :] > thr).astype(F32), axis=0, keepdims=True)

    need = cap - lax.fori_loop(0, nb, count_gt, jnp.zeros((1, LANES), F32))
    strict = (_iota((tb, tb), 0) > _iota((tb, tb), 1)).astype(BF16)

    def select(j, carry):
        r = pl.multiple_of(j * tb, tb)
        blk = bits_s[pl.ds(r, tb), :]
        eq = blk == thr
        before = carry + _dot(strict, eq.astype(BF16))
        sel = jnp.logical_or(blk > thr, jnp.logical_and(eq, before < need))
        sel_s[pl.ds(r, tb), :] = sel.astype(F32)
        return carry + jnp.sum(eq.astype(F32), axis=0, keepdims=True)

    lax.fori_loop(0, nb, select, jnp.zeros((1, LANES), F32))

    def place(j, carry):
        r = pl.multiple_of(j * tb, tb)
        sel = sel_s[pl.ds(r, tb), :]
        slot = carry + _dot(strict, sel.astype(BF16))
        pos_ref[0, pl.ds(r, tb), :] = jnp.where(sel > 0.5, slot, -1.0).astype(I32)
        s0_ref[0, pl.ds(j, 1), :] = carry.astype(I32)
        return carry + jnp.sum(sel, axis=0, keepdims=True)

    lax.fori_loop(0, nb, place, jnp.zeros((1, LANES), F32))


def _topk(aff, cap, tb):
    b, t, _ = aff.shape
    nb = t // tb
    return pl.pallas_call(
        functools.partial(_topk_kernel, cap=cap, tb=tb),
        grid=(b,),
        in_specs=[pl.BlockSpec((1, t, LANES), lambda bi: (bi, 0, 0))],
        out_specs=[pl.BlockSpec((1, t, LANES), lambda bi: (bi, 0, 0)),
                   pl.BlockSpec((1, nb, LANES), lambda bi: (bi, 0, 0))],
        out_shape=[jax.ShapeDtypeStruct((b, t, LANES), I32), jax.ShapeDtypeStruct((b, nb, LANES), I32)],
        scratch_shapes=[pltpu.VMEM((t, LANES), I32), pltpu.VMEM((t, LANES), F32)],
        compiler_params=_cp("parallel"),
        name="moe_topk",
    )(aff)


MOE_SMALL_COUNT = 32


def _windows(cap, tb):
    if cap <= tb + BF16_ROWS:
        return (cap,)
    return (MOE_SMALL_COUNT + BF16_ROWS, tb + BF16_ROWS)


def _window_starts(s0_ref, bi, j, cap, w):
    if w == cap:
        return [0] * N_EXPERTS
    out = []
    for e in range(N_EXPERTS):
        st = jnp.minimum((s0_ref[bi, j, e] // BF16_ROWS) * BF16_ROWS, cap - w)
        out.append(pl.multiple_of(st, BF16_ROWS))
    return out


def _per_window(windows, narrow_ref, bi, j, run):
    if len(windows) == 1:
        run(windows[0])
    else:
        narrow = narrow_ref[bi, j] > 0
        pl.when(narrow)(lambda: run(windows[0]))
        pl.when(jnp.logical_not(narrow))(lambda: run(windows[1]))


def _gather_kernel(s0_ref, narrow_ref, h_ref, post_ref, o_ref, *, cap, tb):
    bi = pl.program_id(0)
    j = pl.program_id(1)

    @pl.when(j == 0)
    def _():
        o_ref[...] = jnp.zeros_like(o_ref)

    def run(w):
        starts = _window_starts(s0_ref, bi, j, cap, w)
        post = post_ref[0]
        r = _iota((w, tb), 0)
        onehot = jnp.concatenate(
            [jnp.where((post[e:e + 1, :] - starts[e]) == r, 1.0, 0.0).astype(BF16) for e in range(N_EXPERTS)],
            axis=0)
        rows = _dot(onehot, h_ref[0]).astype(BF16)
        for e in range(N_EXPERTS):
            cur = o_ref[0, e, pl.ds(starts[e], w), :]
            o_ref[0, e, pl.ds(starts[e], w), :] = cur + rows[e * w:(e + 1) * w]

    _per_window(_windows(cap, tb), narrow_ref, bi, j, run)


def _gather(h2, post, s0, narrow, cap, tb):
    b, t, d = h2.shape
    nb = t // tb
    grid_spec = pltpu.PrefetchScalarGridSpec(
        num_scalar_prefetch=2,
        grid=(b, nb),
        in_specs=[pl.BlockSpec((1, tb, d), lambda bi, j, s, n: (bi, j, 0)),
                  pl.BlockSpec((1, N_EXPERTS, tb), lambda bi, j, s, n: (bi, 0, j))],
        out_specs=pl.BlockSpec((1, N_EXPERTS, cap, d), lambda bi, j, s, n: (bi, 0, 0, 0)),
    )
    return pl.pallas_call(
        functools.partial(_gather_kernel, cap=cap, tb=tb),
        grid_spec=grid_spec,
        out_shape=jax.ShapeDtypeStruct((b, N_EXPERTS, cap, d), BF16),
        compiler_params=_cp("parallel", "arbitrary"),
        name="moe_gather",
    )(s0, narrow, h2, post)


def _expert_kernel(x_ref, wg_ref, wu_ref, wd_ref, y_ref, wg_s, wu_s, wd_s):
    @pl.when(pl.program_id(1) == 0)
    def _():
        wg_s[...] = wg_ref[0].astype(BF16)
        wu_s[...] = wu_ref[0].astype(BF16)
        wd_s[...] = wd_ref[0].astype(BF16)

    shp = x_ref.shape
    x = x_ref[...].reshape(shp[0] * shp[2], shp[3])
    a = _dot(x, wg_s[...])
    u = _dot(x, wu_s[...])
    hm = (_silu(a) * u).astype(BF16)
    y_ref[...] = _dot(hm, wd_s[...]).astype(BF16).reshape(shp)


def _experts(xin, w_gate, w_up, w_down, layer):
    b, e, cap, d = xin.shape
    f = w_gate.shape[-1]
    bb = max(1, min(b, 512 // cap))
    xs = pl.BlockSpec((bb, 1, cap, d), lambda ei, bi: (bi, ei, 0, 0))
    ws = lambda a: pl.BlockSpec((None, 1) + a.shape[2:], lambda ei, bi: (layer, ei, 0, 0))
    return pl.pallas_call(
        _expert_kernel,
        grid=(e, b // bb),
        in_specs=[xs, ws(w_gate), ws(w_up), ws(w_down)],
        out_specs=xs,
        out_shape=jax.ShapeDtypeStruct(xin.shape, BF16),
        scratch_shapes=[pltpu.VMEM((d, f), BF16), pltpu.VMEM((d, f), BF16), pltpu.VMEM((f, d), BF16)],
        compiler_params=_cp("parallel", "arbitrary"),
        name="moe_experts",
    )(xin, w_gate, w_up, w_down)


def _combine_kernel(*refs, cap, tb, final):
    windows = _windows(cap, tb)
    s0_ref, narrow_ref, y_ref, pos_ref, aff_ref, x_ref, gate_ref = refs[:7]
    pos_n = 7
    if final:
        gf_ref = refs[pos_n]
        pos_n += 1
    spread_refs = refs[pos_n:pos_n + 2 * len(windows)]
    o_ref = refs[-1]
    bi = pl.program_id(0)
    j = pl.program_id(1)
    offset = 1024
    lane = _iota((1, N_EXPERTS), 1)

    def run(w):
        k = windows.index(w)
        spread = spread_refs[2 * k][...]
        row_id = spread_refs[2 * k + 1][...]
        starts = _window_starts(s0_ref, bi, j, cap, w)
        stv = jnp.zeros((1, N_EXPERTS), I32)
        for e in range(N_EXPERTS):
            stv = jnp.where(lane == e, starts[e], stv)
        rel = pos_ref[0] - stv + offset
        hi = lax.shift_right_logical(rel, 5).astype(F32).astype(BF16)
        lo = jnp.bitwise_and(rel, 31).astype(F32).astype(BF16)
        relx = 32.0 * _dot(hi, spread) + _dot(lo, spread) - offset
        gx = _dot_sel_right(aff_ref[0], spread)
        g = jnp.where(relx == row_id, gx, 0.0)
        ghi = g.astype(BF16)
        glo = (g - ghi.astype(F32)).astype(BF16)
        ys = jnp.concatenate([y_ref[0, e, pl.ds(starts[e], w), :] for e in range(N_EXPERTS)], axis=0)
        out = x_ref[0] + gate_ref[0] * (_dot(ghi, ys) + _dot(glo, ys))
        if final:
            out = _rms_rows(out, gf_ref[...])
        o_ref[0] = out

    _per_window(windows, narrow_ref, bi, j, run)


def _combine(y, pos, aff, x3, gate, s0, narrow, cap, tb, per_batch, g_final):
    b, t, d = x3.shape
    nb = t // tb
    final = g_final is not None
    tok = lambda c: pl.BlockSpec((1, tb, c), lambda bi, j, s, n: (bi, j, 0))
    const = lambda a: pl.BlockSpec(a.shape, lambda bi, j, s, n: (0, 0))
    in_specs = [pl.BlockSpec((1, N_EXPERTS, cap, d), lambda bi, j, s, n: (bi, 0, 0, 0)),
                tok(N_EXPERTS), tok(N_EXPERTS), tok(d),
                pl.BlockSpec((1, 1, d), (lambda bi, j, s, n: (bi, 0, 0)) if per_batch
                             else (lambda bi, j, s, n: (0, 0, 0)))]
    args = [y, pos, aff, x3, gate]
    if final:
        in_specs.append(const(g_final))
        args.append(g_final)
    for w in _windows(cap, tb):
        spread = jnp.repeat(jnp.eye(N_EXPERTS, dtype=BF16), w, axis=1)
        row_id = jnp.tile(jnp.arange(w, dtype=F32), N_EXPERTS).reshape(1, -1)
        in_specs += [const(spread), const(row_id)]
        args += [spread, row_id]
    grid_spec = pltpu.PrefetchScalarGridSpec(
        num_scalar_prefetch=2, grid=(b, nb), in_specs=in_specs, out_specs=tok(d))
    return pl.pallas_call(
        functools.partial(_combine_kernel, cap=cap, tb=tb, final=final),
        grid_spec=grid_spec,
        out_shape=jax.ShapeDtypeStruct((b, t, d), F32),
        compiler_params=_cp("parallel", "arbitrary"),
        name="moe_combine",
    )(s0, narrow, *args)


def _moe(x2, b, t, g, shift, scale, gate, w_router_pad, w_gate, w_up, w_down, layer, per_batch, g_final):
    cap = CAPACITY_FACTOR * t // N_EXPERTS
    tb = min(MOE_TOKEN_BLOCK, t)
    h2, aff = _router(x2, g, shift, scale, w_router_pad, t, per_batch)
    aff3 = aff.reshape(b, t, N_EXPERTS)
    per = LANES // N_EXPERTS
    groups = -(-b // per)
    nb = t // tb

    def pack(a):
        a = jnp.pad(a, ((0, groups * per - b), (0, 0), (0, 0)))
        return a.reshape(groups, per, a.shape[1], N_EXPERTS).transpose(0, 2, 1, 3).reshape(groups, a.shape[1], LANES)

    def unpack(a):
        return a.reshape(groups, a.shape[1], per, N_EXPERTS).transpose(0, 2, 1, 3).reshape(
            groups * per, a.shape[1], N_EXPERTS)[:b]

    pos, s0 = _topk(pack(aff3), cap, tb)
    pos, s0 = unpack(pos), unpack(s0)
    counts = jnp.diff(s0, axis=1, append=jnp.full((b, 1, N_EXPERTS), cap, I32))
    narrow = jnp.all(counts <= MOE_SMALL_COUNT, axis=-1).astype(I32)
    xin = _gather(h2.reshape(b, t, D_MODEL), pos.transpose(0, 2, 1), s0, narrow, cap, tb)
    y = _experts(xin, w_gate, w_up, w_down, layer)
    out = _combine(y, pos, aff3, x2.reshape(b, t, D_MODEL), gate, s0, narrow, cap, tb, per_batch, g_final)
    return out.reshape(b * t, D_MODEL)


def _rope_tables(t):
    rows = t // GRID_W
    row = jnp.repeat(jnp.arange(rows, dtype=F32), GRID_W)
    col = jnp.tile(jnp.arange(GRID_W, dtype=F32), rows)
    n_freq = DQK_A // 4
    inv_freq = jnp.power(ROPE_BASE, -jnp.arange(n_freq, dtype=F32) / n_freq)
    ang = jnp.concatenate([row[:, None] * inv_freq, col[:, None] * inv_freq], axis=-1)
    cos, sin = jnp.cos(ang), jnp.sin(ang)
    zero = jnp.zeros_like(sin)
    reps = LANES // DQK_A
    cos_t = jnp.tile(jnp.concatenate([cos, cos], axis=-1), (1, reps))
    sin_a = jnp.tile(jnp.concatenate([-sin, zero], axis=-1), (1, reps))
    sin_b = jnp.tile(jnp.concatenate([zero, sin], axis=-1), (1, reps))
    return cos_t, sin_a, sin_b


def _stream(x, mods, per_batch, rope, cache, p):
    b, t, d = x.shape
    x2 = x.reshape(b * t, d)
    depth = p["g_norm"].shape[0]
    even_out, odd_out = [], []
    for l in range(depth):
        j = l // 2
        sh1, sc1, gt1, sh2, sc2, gt2 = mods[l]
        g1 = p["g_norm"][l, 0].reshape(1, d)
        g2 = p["g_norm"][l, 1].reshape(1, d)
        if l % 2 == 0:
            aq, ak, av, bqk, bv, bo, gates = _in_ab(x2, g1, sh1, sc1, p["w_in_ab_main"][j], p["w_in_ab_gate"][j],
                                                     rope, t, per_batch, F32 if cache is None else BF16)
            r3 = lambda a: a.reshape(b, t, a.shape[-1])
            if cache is not None:
                kc = cache[0][:, j].reshape(b, -1, W_A)
                vc = cache[1][:, j].reshape(b, -1, W_A)
                nu = 2 * H_B
                init = (cache[2][:, j].reshape(b, nu, DH_B, DH_B),
                        cache[3][:, j].reshape(b, nu, 1, DH_B),
                        jnp.broadcast_to(cache[4][:, j].reshape(b, nu, 1, 1), (b, nu, 1, LANES)))
            else:
                kc = vc = init = None
            lam_init = 0.8 - 0.6 * math.exp(-0.3 * l)
            attn = _attention_pipe(r3(aq), r3(ak), r3(av), kc, vc, p["lam_ab"][j], lam_init)
            qk = _conv_silu(r3(bqk), p["conv_ab"][j])
            bias = jnp.pad(jnp.concatenate([p["b_ig_ab"][j].reshape(-1), p["b_fg_ab"][j].reshape(-1)]),
                           (0, LANES - 4 * H_B)).reshape(1, LANES)
            res = _mlstm(qk, r3(bv), r3(gates), bias, init, cache is None)
            hf, hb = res[0], res[1]
            if cache is None:
                even_out.append((ak, av, res[2], res[3], res[4]))
            x2 = _out_ab(attn.reshape(b * t, W_A), hf.reshape(b * t, W_B), hb.reshape(b * t, W_B), bo, x2, gt1,
                         p["g_attn_ab"][j].reshape(1, W_A), p["g_mlstm_ab"][j].reshape(1, W_B),
                         p["w_out_ab_bf16"][j], t, per_batch, 1.0 - lam_init)
        else:
            q, v, og, lf, kk = _in_c(x2, g1, sh1, sc1, p["w_in_c_bf16"][j], p["b_f_c"][j], p["gamma_lb"], l, t,
                                     per_batch)
            r3 = lambda a: a.reshape(b, t, a.shape[-1])
            init = None if cache is None else cache[5][:, j]
            res = _hgrn(r3(q), r3(v), r3(lf), r3(kk), init, cache is None)
            if cache is None:
                odd_out.append(res[2])
            x2 = _out_c(res[0].reshape(b * t, W_C), res[1].reshape(b * t, W_C), og, x2, gt1,
                        p["g_hgrn_c"][j].reshape(1, W_C), p["w_out_c_bf16"][j], t, per_batch)
        g_final = p["g_final"].reshape(1, d) if l == depth - 1 else None
        x2 = _moe(x2, b, t, g2, sh2, sc2, gt2, p["w_router_pad"][l], p["w_gate_e"], p["w_up_e"],
                  p["w_down_e"], l, per_batch, g_final)
    return x2.reshape(b, t, d), even_out, odd_out


def kernel(x_prompt, x_sample, cache_dattn_k, cache_dattn_v, state_mlstm_C, state_mlstm_n, state_mlstm_m,
           state_hgrn_S, c, c_ctx, w_ada, b_ada, g_norm, g_final, w_in_ab, b_ig_ab, b_fg_ab, lam_ab, conv_ab,
           g_attn_ab, g_mlstm_ab, w_out_ab, w_in_c, b_f_c, gamma_lb, g_hgrn_c, w_out_c,
           w_router, w_gate_e, w_up_e, w_down_e):
    d = D_MODEL
    bs = c.shape[0]
    bp, tp, _ = x_prompt.shape
    depth = w_ada.shape[0]
    n_main = 3 * W_A + 4 * W_B
    p = dict(
        g_norm=g_norm, g_final=g_final, b_ig_ab=b_ig_ab, b_fg_ab=b_fg_ab, lam_ab=lam_ab, conv_ab=conv_ab,
        g_attn_ab=g_attn_ab, g_mlstm_ab=g_mlstm_ab, b_f_c=b_f_c, gamma_lb=gamma_lb, g_hgrn_c=g_hgrn_c,
        w_gate_e=w_gate_e, w_up_e=w_up_e, w_down_e=w_down_e,
        w_in_ab_main=w_in_ab[:, :, :n_main].astype(BF16),
        w_in_ab_gate=jnp.pad(w_in_ab[:, :, n_main:], ((0, 0), (0, 0), (0, LANES - 4 * H_B))).astype(BF16),
        w_out_ab_bf16=w_out_ab.astype(BF16),
        w_in_c_bf16=w_in_c.astype(BF16),
        w_out_c_bf16=w_out_c.astype(BF16),
        w_router_pad=jnp.pad(w_router, ((0, 0), (0, 0), (0, LANES - N_EXPERTS))),
    )
    rows = -(-(bs + 1) // 8) * 8
    cond = jnp.concatenate([c, c_ctx.reshape(1, d), jnp.zeros((rows - bs - 1, d), F32)], axis=0)
    mods = _ada(cond, w_ada, b_ada)
    mods_s = [[mods[l, :bs, i * d:(i + 1) * d].reshape(bs, 1, d) for i in range(6)] for l in range(depth)]
    mods_c = [[mods[l, bs:bs + 1, i * d:(i + 1) * d].reshape(1, 1, d) for i in range(6)] for l in range(depth)]

    y_prompt, ev, od = _stream(x_prompt, mods_c, False, None, None, p)
    new_k = jnp.stack([e[0].reshape(bp, tp, H_A, 2 * DQK_A) for e in ev], axis=1)
    new_v = jnp.stack([e[1].reshape(bp, tp, H_A, DV_A) for e in ev], axis=1)
    new_c = jnp.stack([e[2].reshape(bp, 2, H_B, DH_B, DH_B) for e in ev], axis=1)
    new_n = jnp.stack([e[3].reshape(bp, 2, H_B, DH_B) for e in ev], axis=1)
    new_m = jnp.stack([e[4][..., 0].reshape(bp, 2, H_B) for e in ev], axis=1)
    new_s = jnp.stack(od, axis=1)

    cache = (cache_dattn_k, cache_dattn_v, state_mlstm_C, state_mlstm_n, state_mlstm_m, state_hgrn_S)
    y_sample, _, _ = _stream(x_sample, mods_s, True, _rope_tables(x_sample.shape[1]), cache, p)
    return (y_prompt, y_sample, new_k, new_v, new_c, new_n, new_m, new_s)
```

```python
import functools
import math

import jax
import jax.numpy as jnp
from jax import lax
from jax.experimental import pallas as pl
from jax.experimental.pallas import tpu as pltpu

F32 = jnp.float32
BF16 = jnp.bfloat16
I32 = jnp.int32

D_MODEL = 1024
H_A = 4
DV_A = 128
DQK_A = 64
W_A = H_A * DV_A
H_B = 4
DH_B = 128
W_B = H_B * DH_B
H_C = 8
DH_C = 128
W_C = H_C * DH_C
N_EXPERTS = 16
CAPACITY_FACTOR = 2
GRID_W = 64
ROPE_BASE = 10000.0
RMS_EPS = 1e-6
NEG_BIG = -1e30
LOG2E = 1.4426950408889634
LANES = 128
BF16_ROWS = 16
SCAN_CHUNK = 128
MOE_TOKEN_BLOCK = 128
VMEM_LIMIT = 56 * 1024 * 1024


def _cp(*sem):
    return pltpu.CompilerParams(dimension_semantics=sem, vmem_limit_bytes=VMEM_LIMIT)


def _dot(a, b):
    return jnp.dot(a, b, preferred_element_type=F32)


def _dot_nt(a, b):
    return lax.dot_general(a, b, (((1,), (1,)), ((), ())), preferred_element_type=F32)


def _dot_tn(a, b):
    return lax.dot_general(a, b, (((0,), (0,)), ((), ())), preferred_element_type=F32)


def _split3(x):
    hi = x.astype(BF16)
    r = x - hi.astype(F32)
    mid = r.astype(BF16)
    lo = (r - mid.astype(F32)).astype(BF16)
    return hi, mid, lo


def _dot_sel_left(sel, x):
    hi, mid, lo = _split3(x)
    return _dot(sel, hi) + _dot(sel, mid) + _dot(sel, lo)


def _dot_sel_right(x, sel):
    hi, mid, lo = _split3(x)
    return _dot(hi, sel) + _dot(mid, sel) + _dot(lo, sel)


def _dot_f32(a, b):
    ah, am, al = _split3(a)
    bh, bm, bl = _split3(b)
    return (_dot(ah, bh) + (_dot(ah, bm) + _dot(am, bh))
            + (_dot(ah, bl) + _dot(am, bm) + _dot(al, bh)))


def _dot_f32x3(a, b):
    ah = a.astype(BF16)
    al = (a - ah.astype(F32)).astype(BF16)
    bh = b.astype(BF16)
    bl = (b - bh.astype(F32)).astype(BF16)
    return _dot(ah, bh) + (_dot(ah, bl) + _dot(al, bh))


def _dot_sel_left2(sel, x):
    hi = x.astype(BF16)
    lo = (x - hi.astype(F32)).astype(BF16)
    return _dot(sel, hi) + _dot(sel, lo)


def _sigmoid(x):
    return 1.0 / (1.0 + jnp.exp(-x))


def _silu(x):
    return x * _sigmoid(x)


def _log_sigmoid(x):
    return jnp.minimum(x, 0.0) - jnp.log(1.0 + jnp.exp(-jnp.abs(x)))


def _rms_rows(x, g):
    return x * lax.rsqrt(jnp.mean(x * x, axis=-1, keepdims=True) + RMS_EPS) * g


def _iota(shape, dim):
    return lax.broadcasted_iota(I32, shape, dim)


def _ada_kernel(c_ref, w_ref, b_ref, o_ref):
    c = c_ref[...]
    o_ref[0] = _dot_f32(_silu(c), w_ref[0]) + b_ref[0]


def _ada(cond_rows, w_ada, b_ada):
    depth, d, n6 = w_ada.shape
    rows = cond_rows.shape[0]
    tn = 1536
    return pl.pallas_call(
        _ada_kernel,
        grid=(depth, n6 // tn),
        in_specs=[pl.BlockSpec((rows, d), lambda l, j: (0, 0)),
                  pl.BlockSpec((1, d, tn), lambda l, j: (l, 0, j)),
                  pl.BlockSpec((1, 1, tn), lambda l, j: (l, 0, j))],
        out_specs=pl.BlockSpec((1, rows, tn), lambda l, j: (l, 0, j)),
        out_shape=jax.ShapeDtypeStruct((depth, rows, n6), F32),
        compiler_params=_cp("parallel", "parallel"),
        name="adaln",
    )(cond_rows, w_ada, b_ada.reshape(depth, 1, n6))


def _mod_spec(per_batch, tm, t):
    if per_batch:
        return pl.BlockSpec((1, 1, D_MODEL), lambda i: (i * tm // t, 0, 0))
    return pl.BlockSpec((1, 1, D_MODEL), lambda i: (0, 0, 0))


def _row_tile(t):
    return min(512, t)


def _in_ab_kernel(*refs, use_rope):
    if use_rope:
        (x_ref, g_ref, sh_ref, sc_ref, w_ref, wg_ref, cos_ref, sa_ref, sb_ref,
         aq_ref, ak_ref, av_ref, bqk_ref, bv_ref, bo_ref, gt_ref) = refs
    else:
        (x_ref, g_ref, sh_ref, sc_ref, w_ref, wg_ref,
         aq_ref, ak_ref, av_ref, bqk_ref, bv_ref, bo_ref, gt_ref) = refs
    h = _rms_rows(x_ref[...], g_ref[...]) * (1.0 + sc_ref[0]) + sh_ref[0]
    hb = h.astype(BF16)
    y = _dot(hb, w_ref[...])
    gt_ref[...] = _dot(hb, wg_ref[...])
    aq = y[:, 0:W_A] * (DQK_A ** -0.5 * LOG2E)
    ak = y[:, W_A:2 * W_A]
    if use_rope:
        cos = jnp.concatenate([cos_ref[...]] * H_A, axis=1)
        sa = jnp.concatenate([sa_ref[...]] * H_A, axis=1)
        sb = jnp.concatenate([sb_ref[...]] * H_A, axis=1)
        half = DQK_A // 2

        def rope(v):
            return (v * cos + pltpu.roll(v, W_A - half, axis=1) * sa
                    + pltpu.roll(v, half, axis=1) * sb)

        aq = rope(aq)
        ak = rope(ak)
    aq_ref[...] = aq.astype(aq_ref.dtype)
    ak_ref[...] = ak.astype(ak_ref.dtype)
    av_ref[...] = y[:, 2 * W_A:3 * W_A].astype(av_ref.dtype)
    bqk_ref[...] = y[:, 3 * W_A:3 * W_A + 2 * W_B]
    bv_ref[...] = y[:, 3 * W_A + 2 * W_B:3 * W_A + 3 * W_B]
    bo_ref[...] = y[:, 3 * W_A + 3 * W_B:3 * W_A + 4 * W_B]


def _in_ab(x2, g, shift, scale, w_main, w_gate, rope, t, per_batch, qkv_dtype):
    n = x2.shape[0]
    tm = _row_tile(t)
    use_rope = rope is not None
    row = lambda c: pl.BlockSpec((tm, c), lambda i: (i, 0))
    full = lambda a: pl.BlockSpec(a.shape, lambda i: (0,) * a.ndim)
    in_specs = [row(D_MODEL), full(g), _mod_spec(per_batch, tm, t), _mod_spec(per_batch, tm, t),
                full(w_main), full(w_gate)]
    args = [x2, g, shift, scale, w_main, w_gate]
    if use_rope:
        nt = t // tm
        tab = pl.BlockSpec((tm, LANES), lambda i: (i % nt, 0))
        in_specs += [tab, tab, tab]
        args += list(rope)
    widths = (W_A, W_A, W_A, 2 * W_B, W_B, W_B, LANES)
    return pl.pallas_call(
        functools.partial(_in_ab_kernel, use_rope=use_rope),
        grid=(n // tm,),
        in_specs=in_specs,
        out_specs=[row(c) for c in widths],
        out_shape=[jax.ShapeDtypeStruct((n, c), qkv_dtype if i < 3 else F32) for i, c in enumerate(widths)],
        compiler_params=_cp("parallel"),
        name="in_proj_even",
    )(*args)


def _attn_pipe_kernel(*refs, has_cache, tk, lam_init, nq):
    if has_cache:
        q_ref, k_ref, v_ref, kc_ref, vc_ref, lam_ref, o_ref, s_s, mp_s, vt_s, sc_s, vct_s = refs
    else:
        q_ref, k_ref, v_ref, lam_ref, o_ref, s_s, mp_s, vt_s = refs
    i = pl.program_id(2)
    q = q_ref[0]
    tq = q.shape[0]
    first = _iota((1, LANES), 1) < DQK_A
    qs = (jnp.where(first, q, 0).astype(BF16), jnp.where(first, 0, q).astype(BF16))
    nkb = k_ref.shape[1] // tk

    @pl.when(i == 0)
    def _():
        for j in range(nkb):
            vt_s[j, 0:DV_A, :] = v_ref[0, j * tk:(j + 1) * tk, :].astype(F32).T.astype(BF16)
            vt_s[j, DV_A:, :] = jnp.ones((BF16_ROWS, tk), BF16)
        if has_cache:
            vct_s[0:DV_A, :] = vc_ref[0].astype(F32).T.astype(BF16)
            vct_s[DV_A:, :] = jnp.ones((BF16_ROWS, vct_s.shape[1]), BF16)

    def groups(x):
        return [x[:, g * LANES:(g + 1) * LANES] for g in range(x.shape[1] // LANES)]

    def body(score, finish, cur):
        prev = 1 - cur
        if finish:
            m = [jnp.max(mp_s[prev, a], axis=-1, keepdims=True) for a in range(2)]
        mpart = [jnp.full((tq, LANES), NEG_BIG, F32)] * 2
        acc = [jnp.zeros((DV_A + BF16_ROWS, tq), F32)] * 2

        def pass1(kb, store):
            for a in range(2):
                s = _dot_nt(qs[a], kb)
                store(a, s)
                mpart[a] = functools.reduce(jnp.maximum, groups(s), mpart[a])

        def pass2(a, s, vt):
            acc[a] = acc[a] + _dot_nt(vt, jnp.exp2(s - m[a]).astype(BF16))

        if has_cache:
            def store_c(a, s):
                sc_s[cur, a] = s
            if score:
                pass1(kc_ref[0].astype(BF16), store_c)
            if finish:
                for a in range(2):
                    pass2(a, sc_s[prev, a], vct_s[...])
        for j in range(nkb):
            def store_j(a, s, j=j):
                s_s[cur, a, j] = s
            if score:
                pass1(k_ref[0, j * tk:(j + 1) * tk, :].astype(BF16), store_j)
            if finish:
                for a in range(2):
                    pass2(a, s_s[prev, a, j], vt_s[j])
        if score:
            for a in range(2):
                mp_s[cur, a] = mpart[a]
        if finish:
            lp = lam_ref[...]
            lam = (jnp.exp(jnp.sum(lp[0:1] * lp[1:2], axis=-1, keepdims=True))
                   - jnp.exp(jnp.sum(lp[2:3] * lp[3:4], axis=-1, keepdims=True)) + lam_init)
            o1 = acc[0][0:DV_A] / acc[0][DV_A:DV_A + 1]
            o2 = acc[1][0:DV_A] / acc[1][DV_A:DV_A + 1]
            o_ref[0] = (o1 - lam * o2).T

    pl.when(i == 0)(lambda: body(True, False, 0))
    inner = jnp.logical_and(i > 0, i < nq)
    pl.when(jnp.logical_and(inner, i % 2 == 0))(lambda: body(True, True, 0))
    pl.when(jnp.logical_and(inner, i % 2 == 1))(lambda: body(True, True, 1))
    pl.when(i == nq)(lambda: body(False, True, nq % 2))


def _attention_pipe(aq, ak, av, kc, vc, lam_p, lam_init):
    b, t, _ = aq.shape
    tq = min(256, t)
    tk = min(512, t)
    nq = t // tq
    has_cache = kc is not None
    qspec = pl.BlockSpec((1, tq, DV_A), lambda bi, h, i: (bi, jnp.minimum(i, nq - 1), h))
    ospec = pl.BlockSpec((1, tq, DV_A), lambda bi, h, i: (bi, jnp.maximum(i - 1, 0), h))
    ks = pl.BlockSpec((1, t, DV_A), lambda bi, h, i: (bi, 0, h))
    in_specs = [qspec, ks, ks]
    args = [aq, ak, av]
    if has_cache:
        cs = pl.BlockSpec((1, kc.shape[1], DV_A), lambda bi, h, i: (bi, 0, h))
        in_specs += [cs, cs]
        args += [kc, vc]
    in_specs.append(pl.BlockSpec(lam_p.shape, lambda bi, h, i: (0, 0)))
    args.append(lam_p)
    scratch = [pltpu.VMEM((2, 2, t // tk, tq, tk), F32), pltpu.VMEM((2, 2, tq, LANES), F32),
               pltpu.VMEM((t // tk, DV_A + BF16_ROWS, tk), BF16)]
    if has_cache:
        scratch += [pltpu.VMEM((2, 2, tq, kc.shape[1]), F32),
                    pltpu.VMEM((DV_A + BF16_ROWS, kc.shape[1]), BF16)]
    return pl.pallas_call(
        functools.partial(_attn_pipe_kernel, has_cache=has_cache, tk=tk, lam_init=lam_init, nq=nq),
        grid=(b, H_A, nq + 1),
        in_specs=in_specs,
        out_specs=ospec,
        out_shape=jax.ShapeDtypeStruct((b, t, W_A), F32),
        scratch_shapes=scratch,
        compiler_params=_cp("parallel", "parallel", "arbitrary"),
        name="diff_attention",
    )(*args)


def _conv_kernel(x_ref, w_ref, o_ref):
    x = x_ref[0]
    t = x.shape[0]
    w = w_ref[...]
    r = _iota((t, 1), 0)
    prev = jnp.where(r == 0, 0.0, pltpu.roll(x, 1, axis=0))
    nxt = jnp.where(r == t - 1, 0.0, pltpu.roll(x, t - 1, axis=0))
    o_ref[0] = _silu(prev * w[0:1] + x * w[1:2] + nxt * w[2:3])


def _conv_silu(bqk, conv_w):
    b, t, c = bqk.shape
    tc = 256
    spec = pl.BlockSpec((1, t, tc), lambda bi, j: (bi, 0, j))
    return pl.pallas_call(
        _conv_kernel,
        grid=(b, c // tc),
        in_specs=[spec, pl.BlockSpec((conv_w.shape[0], tc), lambda bi, j: (0, j))],
        out_specs=spec,
        out_shape=jax.ShapeDtypeStruct(bqk.shape, F32),
        compiler_params=_cp("parallel", "parallel"),
        name="mlstm_conv",
    )(bqk, conv_w)


def _tri(n, lower):
    r = _iota((n, n), 0)
    c = _iota((n, n), 1)
    return r >= c if lower else r <= c


def _mlstm_kernel(*refs, has_init, out_state):
    qf, kf, vf, gf, qb, kb, vb, gb, bias = refs[:9]
    pos = 9
    if has_init:
        c0, n0, m0 = refs[pos:pos + 3]
        pos += 3
    hf, hb = refs[pos:pos + 2]
    pos += 2
    if out_state:
        c_out, n_out, m_out = refs[pos:pos + 3]
        pos += 3
    cn_s, m_s = refs[pos:pos + 2]
    step = pl.program_id(1)
    L = SCAN_CHUNK
    nu = 2 * H_B

    @pl.when(step == 0)
    def _():
        if has_init:
            for u in range(nu):
                cn_s[u, :, 0:DH_B] = c0[0, u]
                cn_s[u, :, DH_B:2 * DH_B] = jnp.broadcast_to(n0[0, u], (DH_B, DH_B)).T
            m_s[...] = m0[0]
        else:
            cn_s[...] = jnp.zeros_like(cn_s)
            m_s[...] = jnp.zeros_like(m_s)

    ones_b = jnp.ones((L, DH_B), BF16)
    sel_rows = _iota((LANES, LANES), 0)
    lower = _tri(L, True)
    upper = _tri(L, False)
    lower_b = lower.astype(BF16)
    upper_b = upper.astype(BF16)
    col = _iota((1, LANES), 1)
    units = []
    for d, (q_ref, k_ref, v_ref, g_ref, h_ref) in enumerate(((qf, kf, vf, gf, hf), (qb, kb, vb, gb, hb))):
        g = g_ref[0] + bias[...]
        g = jnp.where(col < 2 * H_B, g, _log_sigmoid(g))
        gt = g.T
        if d == 0:
            bc_all = _dot_sel_left(lower_b, g)
            br_all = _dot_sel_right(gt, upper_b)
            mask = lower
        else:
            bc_all = _dot_sel_left(upper_b, g)
            br_all = _dot_sel_right(gt, lower_b)
            mask = upper
        for h in range(H_B):
            u = d * H_B + h
            sl = slice(h * DH_B, (h + 1) * DH_B)
            units.append(dict(
                u=u, d=d, sl=sl, h_ref=h_ref, mask=mask, g=g, bc_all=bc_all,
                li_row=gt[u:u + 1, :], b_row=br_all[2 * H_B + u:2 * H_B + u + 1, :],
                q=q_ref[0, :, sl].astype(BF16), k=k_ref[0, :, sl] * (DH_B ** -0.5),
                v1=jnp.concatenate([v_ref[0, :, sl].astype(BF16), ones_b], axis=1)))
    for x in units:
        x["li_b"] = _dot_sel_right(x["g"], (sel_rows == x["u"]).astype(BF16))
        x["b_b"] = _dot_sel_right(x["bc_all"], (sel_rows == 2 * H_B + x["u"]).astype(BF16))
        x["qk"] = _dot_nt(x["q"], x["k"].astype(BF16))
        x["cn"] = cn_s[x["u"]]
        x["qc"] = _dot(x["q"], x["cn"].astype(BF16))
    for x in units:
        x["m_prev"] = m_s[x["u"]]
        a_b = x["b_b"] + x["m_prev"]
        dm = jnp.where(x["mask"], x["b_b"] - x["b_row"] + x["li_row"], NEG_BIG)
        x["m_t"] = jnp.maximum(a_b, jnp.max(dm, axis=-1, keepdims=True))
        x["w_inter"] = jnp.exp(a_b - x["m_t"])
        x["s"] = (x["qk"] * jnp.exp(dm - x["m_t"])).astype(BF16)
    for x in units:
        x["sv"] = _dot(x["s"], x["v1"])
        b_end = x["b_b"][L - 1:L, :] if x["d"] == 0 else x["b_b"][0:1, :]
        g_b = b_end - x["b_b"] + x["li_b"]
        x["m_new"] = jnp.maximum(b_end + x["m_prev"], jnp.max(g_b, axis=0, keepdims=True))
        x["w_old"] = jnp.exp(b_end + x["m_prev"] - x["m_new"])
        x["ks"] = (jnp.exp(g_b - x["m_new"]) * x["k"]).astype(BF16)
    for x in units:
        num = x["w_inter"] * x["qc"][:, 0:DH_B] + x["sv"][:, 0:DH_B]
        den = x["w_inter"] * x["qc"][:, DH_B:] + x["sv"][:, DH_B:]
        x["h_ref"][0, :, x["sl"]] = num / jnp.maximum(jnp.abs(den), jnp.exp(-x["m_t"]))
        w2 = jnp.concatenate([x["w_old"], x["w_old"]], axis=1)
        cn_s[x["u"]] = w2 * x["cn"] + _dot_tn(x["ks"], x["v1"])
        m_s[x["u"]] = x["m_new"]

    if out_state:
        @pl.when(step == pl.num_programs(1) - 1)
        def _():
            for u in range(nu):
                c_out[0, u] = cn_s[u, :, 0:DH_B]
                n_out[0, u] = cn_s[u, :, DH_B:2 * DH_B].T[0:1, :]
            m_out[0] = m_s[...]


def _mlstm(qk, v, gates, bias, init, out_state):
    b, t, _ = v.shape
    L = SCAN_CHUNK
    nc = t // L
    nu = 2 * H_B
    fwd = lambda j: (lambda bi, c: (bi, c, j))
    bwd = lambda j: (lambda bi, c: (bi, nc - 1 - c, j))
    blk = lambda w, im: pl.BlockSpec((1, L, w), im)
    in_specs = [blk(W_B, fwd(0)), blk(W_B, fwd(1)), blk(W_B, fwd(0)), blk(LANES, fwd(0)),
                blk(W_B, bwd(0)), blk(W_B, bwd(1)), blk(W_B, bwd(0)), blk(LANES, bwd(0)),
                pl.BlockSpec((1, LANES), lambda bi, c: (0, 0))]
    args = [qk, qk, v, gates, qk, qk, v, gates, bias]
    st_specs = [pl.BlockSpec((1, nu, DH_B, DH_B), lambda bi, c: (bi, 0, 0, 0)),
                pl.BlockSpec((1, nu, 1, DH_B), lambda bi, c: (bi, 0, 0, 0)),
                pl.BlockSpec((1, nu, 1, LANES), lambda bi, c: (bi, 0, 0, 0))]
    st_shapes = [jax.ShapeDtypeStruct((b, nu, DH_B, DH_B), F32),
                 jax.ShapeDtypeStruct((b, nu, 1, DH_B), F32),
                 jax.ShapeDtypeStruct((b, nu, 1, LANES), F32)]
    has_init = init is not None
    if has_init:
        in_specs += st_specs
        args += list(init)
    out_specs = [blk(W_B, fwd(0)), blk(W_B, bwd(0))]
    out_shape = [jax.ShapeDtypeStruct((b, t, W_B), F32)] * 2
    if out_state:
        out_specs += st_specs
        out_shape += st_shapes
    return pl.pallas_call(
        functools.partial(_mlstm_kernel, has_init=has_init, out_state=out_state),
        grid=(b, nc),
        in_specs=in_specs,
        out_specs=out_specs,
        out_shape=out_shape,
        scratch_shapes=[pltpu.VMEM((nu, DH_B, 2 * DH_B), F32), pltpu.VMEM((nu, 1, LANES), F32)],
        compiler_params=_cp("parallel", "arbitrary"),
        name="mlstm_scan",
    )(*args)


def _head_rms(x, g, heads, width):
    parts = []
    for h in range(heads):
        sl = slice(h * width, (h + 1) * width)
        parts.append(_rms_rows(x[:, sl], g[:, sl]))
    return parts


def _route(x, g_ref, sh_ref, sc_ref, wr_ref, h_ref, aff_ref):
    h = _rms_rows(x, g_ref[...]) * (1.0 + sc_ref[0]) + sh_ref[0]
    h_ref[...] = h.astype(BF16)
    logits = _dot_f32x3(h, wr_ref[...])
    valid = _iota((1, LANES), 1) < N_EXPERTS
    logits = jnp.where(valid, logits, NEG_BIG)
    e = jnp.exp(logits - jnp.max(logits, axis=-1, keepdims=True))
    e = jnp.where(valid, e, 0.0)
    aff_ref[...] = (e / jnp.sum(e, axis=-1, keepdims=True))[:, :N_EXPERTS]


def _route_specs(route, tm, t, per_batch):
    g2, sh2, sc2, wr = route
    full = lambda a: pl.BlockSpec(a.shape, lambda i: (0,) * a.ndim)
    row = lambda c: pl.BlockSpec((tm, c), lambda i: (i, 0))
    in_specs = [full(g2), _mod_spec(per_batch, tm, t), _mod_spec(per_batch, tm, t), full(wr)]
    out_specs = [row(D_MODEL), row(D_MODEL), row(N_EXPERTS)]
    return in_specs, out_specs


def _route_shapes(n):
    return [jax.ShapeDtypeStruct((n, D_MODEL), F32), jax.ShapeDtypeStruct((n, D_MODEL), BF16),
            jax.ShapeDtypeStruct((n, N_EXPERTS), F32)]


def _out_ab_kernel(at_ref, hf_ref, hb_ref, bo_ref, x_ref, gate_ref, ga_ref, gm_ref, w_ref,
                   g2_ref, sh2_ref, sc2_ref, wr_ref, o_ref, h_ref, aff_ref, *, out_scale):
    oa = [p * out_scale for p in _head_rms(at_ref[...], ga_ref[...], H_A, DV_A)]
    hm = _head_rms(hf_ref[...] + hb_ref[...], gm_ref[...], H_B, DH_B)
    bo = bo_ref[...]
    ob = [_sigmoid(bo[:, h * DH_B:(h + 1) * DH_B]) * hm[h] for h in range(H_B)]
    cat = jnp.concatenate(oa + ob, axis=1).astype(BF16)
    x = x_ref[...] + gate_ref[0] * _dot(cat, w_ref[...])
    o_ref[...] = x
    _route(x, g2_ref, sh2_ref, sc2_ref, wr_ref, h_ref, aff_ref)


def _out_ab(attn, hf, hb, bo, x2, gate, g_attn, g_mlstm, w_out, route, t, per_batch, out_scale):
    n = x2.shape[0]
    tm = _row_tile(t)
    row = lambda c: pl.BlockSpec((tm, c), lambda i: (i, 0))
    full = lambda a: pl.BlockSpec(a.shape, lambda i: (0,) * a.ndim)
    r_in, r_out = _route_specs(route, tm, t, per_batch)
    return pl.pallas_call(
        functools.partial(_out_ab_kernel, out_scale=out_scale),
        grid=(n // tm,),
        in_specs=[row(W_A), row(W_B), row(W_B), row(W_B), row(D_MODEL), _mod_spec(per_batch, tm, t),
                  full(g_attn), full(g_mlstm), full(w_out)] + r_in,
        out_specs=r_out,
        out_shape=_route_shapes(n),
        compiler_params=_cp("parallel"),
        name="out_proj_even",
    )(attn, hf, hb, bo, x2, gate, g_attn, g_mlstm, w_out, *route)


def _in_c_kernel(x_ref, g_ref, sh_ref, sc_ref, w_ref, bf_ref, gam_ref, q_ref, v_ref, og_ref, lf_ref, kk_ref, *, layer):
    h = _rms_rows(x_ref[...], g_ref[...]) * (1.0 + sc_ref[0]) + sh_ref[0]
    y = _dot(h.astype(BF16), w_ref[...])
    q_ref[...] = y[:, 0:W_C]
    v_ref[...] = y[:, W_C:2 * W_C]
    og_ref[...] = y[:, 2 * W_C:3 * W_C]
    gam = gam_ref[...]
    e = jnp.exp(gam - jnp.max(gam, axis=0, keepdims=True))
    p = e / jnp.sum(e, axis=0, keepdims=True)
    lbs = p[0:1]
    lbs0 = lbs
    for i in range(1, layer + 1):
        lbs = lbs + p[i:i + 1]
    lb = jnp.maximum(lbs - lbs0, 0.0)
    L = SCAN_CHUNK
    for d in range(2):
        f = y[:, (3 + d) * W_C:(4 + d) * W_C] + bf_ref[d:d + 1]
        u = jnp.exp(-jnp.abs(f))
        neg = f < 0.0
        lf2 = (jnp.log(jnp.where(neg, lb + u, 1.0 + lb * u)) - jnp.log(1.0 + u)) * LOG2E
        kk_ref[:, d * W_C:(d + 1) * W_C] = (1.0 - lb) * jnp.where(neg, 1.0, u) / (1.0 + u)
        tri = _tri(L, d == 0).astype(BF16)
        for c0 in range(0, lf2.shape[0], L):
            lf_ref[c0:c0 + L, d * W_C:(d + 1) * W_C] = _dot_sel_left2(tri, lf2[c0:c0 + L])


def _in_c(x2, g, shift, scale, w_in, b_f, gamma_lb, layer, t, per_batch):
    n = x2.shape[0]
    tm = _row_tile(t)
    row = lambda c: pl.BlockSpec((tm, c), lambda i: (i, 0))
    full = lambda a: pl.BlockSpec(a.shape, lambda i: (0,) * a.ndim)
    widths = (W_C, W_C, W_C, 2 * W_C, 2 * W_C)
    return pl.pallas_call(
        functools.partial(_in_c_kernel, layer=layer),
        grid=(n // tm,),
        in_specs=[row(D_MODEL), full(g), _mod_spec(per_batch, tm, t), _mod_spec(per_batch, tm, t),
                  full(w_in), full(b_f), full(gamma_lb)],
        out_specs=[row(c) for c in widths],
        out_shape=[jax.ShapeDtypeStruct((n, c), F32) for c in widths],
        compiler_params=_cp("parallel"),
        name="in_proj_odd",
    )(x2, g, shift, scale, w_in, b_f, gamma_lb)


HGRN_LEVELS = (8, 16, 32, 64)
HGRN_DIAG = 8
HGRN_HEADS_PER_STEP = 4


def _hgrn_unit(q, kk, v, b, st_ref, b_s, rev, store_o):
    L = SCAN_CHUNK
    b_s[...] = b
    row = _iota((L, 1), 0)
    col = _iota((1, L), 1)
    k16 = kk.astype(BF16)
    v16 = v.astype(BF16)
    st = st_ref[...]
    o = _dot_nt((q * jnp.exp2(b)).astype(BF16), st.astype(BF16))
    yield
    c = HGRN_DIAG
    sub = _iota((c, 1), 0)
    ys = []
    for blk in range(L // c):
        r0 = blk * c
        qb = q[r0:r0 + c]
        bb = b[r0:r0 + c]
        for s in range(c):
            keep = (sub <= s) if rev else (sub >= s)
            ys.append(qb * jnp.exp2(jnp.where(keep, bb - b_s[r0 + s:r0 + s + 1, :], NEG_BIG)))
    rs = _dot_nt(jnp.concatenate(ys, axis=0).astype(BF16), k16)
    yield
    parts = []
    for m in HGRN_LEVELS:
        par = 2 * m
        pieces = []
        for p0 in range(0, L, par):
            ref_row = p0 + m if rev else p0 + m - 1
            pieces.append(jnp.broadcast_to(b_s[ref_row:ref_row + 1, :], (par, LANES)))
        bref = jnp.concatenate(pieces, axis=0)
        in_first = (row % par) < m
        reader = in_first if rev else jnp.logical_not(in_first)
        z = (jnp.where(reader, q, kk) * jnp.exp2(-jnp.abs(b - bref))).astype(BF16)
        keep = jnp.logical_and((row // par) == (col // par),
                               ((col % par) >= m) if rev else ((col % par) < m))
        parts.append((jnp.logical_and(keep, reader), _dot_nt(z, z)))
        yield
    lane = _iota((c, LANES), 1)
    rows = []
    for blk in range(L // c):
        acc = jnp.zeros((c, LANES), F32)
        for s in range(c):
            i = blk * c + s
            acc = jnp.where(lane == i, rs[i * c:(i + 1) * c], acc)
        rows.append(acc)
    att = jnp.concatenate(rows, axis=0)
    for keep, a in parts:
        att = att + jnp.where(keep, a, 0.0)
    o = o + _dot(att.astype(BF16), v16)
    b_end = b_s[0:1, :] if rev else b_s[L - 1:L, :]
    kd = (kk * jnp.exp2(b_end - b)).astype(BF16)
    st_new = st * jnp.exp2(b_end) + _dot_tn(v16, kd)
    yield
    store_o(o)
    st_ref[...] = st_new
    yield


def _hgrn_kernel(*refs, has_init, out_state, hp):
    qf, vf, lff, kkf, qb, vb, lfb, kkb = refs[:8]
    pos = 8
    if has_init:
        s0 = refs[pos]
        pos += 1
    of, ob = refs[pos:pos + 2]
    pos += 2
    if out_state:
        s_out = refs[pos]
        pos += 1
    st_s, b_s = refs[pos:pos + 2]
    step = pl.program_id(2)

    @pl.when(step == 0)
    def _():
        for d in range(2):
            for h in range(hp):
                st_s[d * hp + h] = s0[0, d, h].T if has_init else jnp.zeros((DH_C, DH_C), F32)

    def storer(o_ref, sl):
        def store(o):
            o_ref[0, :, sl] = o
        return store

    units = []
    for d, (q_ref, v_ref, lf_ref, kk_ref, o_ref) in enumerate(((qf, vf, lff, kkf, of), (qb, vb, lfb, kkb, ob))):
        for h in range(hp):
            u = d * hp + h
            sl = slice(h * DH_C, (h + 1) * DH_C)
            units.append(_hgrn_unit(q_ref[0, :, sl], kk_ref[0, :, sl], v_ref[0, :, sl], lf_ref[0, :, sl],
                                    st_s.at[u], b_s.at[u], d == 1, storer(o_ref, sl)))
    live = units
    while live:
        live = [g for g in live if next(g, "done") != "done"]

    if out_state:
        @pl.when(step == pl.num_programs(2) - 1)
        def _():
            for d in range(2):
                for h in range(hp):
                    s_out[0, d, h] = st_s[d * hp + h].T


def _hgrn(q, v, lf, kk, init, out_state):
    b, t, _ = q.shape
    L = SCAN_CHUNK
    nc = t // L
    hp = HGRN_HEADS_PER_STEP
    ng = H_C // hp
    blk = lambda im: pl.BlockSpec((1, L, hp * DH_C), im)
    fwd = lambda off: (lambda bi, h, c: (bi, c, h + off))
    bwd = lambda off: (lambda bi, h, c: (bi, nc - 1 - c, h + off))
    in_specs = [blk(fwd(0)), blk(fwd(0)), blk(fwd(0)), blk(fwd(0)),
                blk(bwd(0)), blk(bwd(0)), blk(bwd(ng)), blk(bwd(ng))]
    args = [q, v, lf, kk, q, v, lf, kk]
    st_spec = pl.BlockSpec((1, 2, hp, DH_C, DH_C), lambda bi, h, c: (bi, 0, h, 0, 0))
    has_init = init is not None
    if has_init:
        in_specs.append(st_spec)
        args.append(init)
    out_specs = [blk(fwd(0)), blk(bwd(0))]
    out_shape = [jax.ShapeDtypeStruct((b, t, W_C), F32)] * 2
    if out_state:
        out_specs.append(st_spec)
        out_shape.append(jax.ShapeDtypeStruct((b, 2, H_C, DH_C, DH_C), F32))
    return pl.pallas_call(
        functools.partial(_hgrn_kernel, has_init=has_init, out_state=out_state, hp=hp),
        grid=(b, ng, nc),
        in_specs=in_specs,
        out_specs=out_specs,
        out_shape=out_shape,
        scratch_shapes=[pltpu.VMEM((2 * hp, DH_C, DH_C), F32), pltpu.VMEM((2 * hp, L, DH_C), F32)],
        compiler_params=_cp("parallel", "parallel", "arbitrary"),
        name="hgrn_scan",
    )(*args)


def _out_c_kernel(of_ref, ob_ref, og_ref, x_ref, gate_ref, gh_ref, w_ref,
                  g2_ref, sh2_ref, sc2_ref, wr_ref, o_ref, h_ref, aff_ref):
    hm = _head_rms(of_ref[...] + ob_ref[...], gh_ref[...], H_C, DH_C)
    og = og_ref[...]
    parts = [hm[h] * _silu(og[:, h * DH_C:(h + 1) * DH_C]) for h in range(H_C)]
    cat = jnp.concatenate(parts, axis=1).astype(BF16)
    x = x_ref[...] + gate_ref[0] * _dot(cat, w_ref[...])
    o_ref[...] = x
    _route(x, g2_ref, sh2_ref, sc2_ref, wr_ref, h_ref, aff_ref)


def _out_c(of, ob, og, x2, gate, g_hgrn, w_out, route, t, per_batch):
    n = x2.shape[0]
    tm = _row_tile(t)
    row = lambda c: pl.BlockSpec((tm, c), lambda i: (i, 0))
    full = lambda a: pl.BlockSpec(a.shape, lambda i: (0,) * a.ndim)
    r_in, r_out = _route_specs(route, tm, t, per_batch)
    return pl.pallas_call(
        _out_c_kernel,
        grid=(n // tm,),
        in_specs=[row(W_C), row(W_C), row(W_C), row(D_MODEL), _mod_spec(per_batch, tm, t),
                  full(g_hgrn), full(w_out)] + r_in,
        out_specs=r_out,
        out_shape=_route_shapes(n),
        compiler_params=_cp("parallel"),
        name="out_proj_odd",
    )(of, ob, og, x2, gate, g_hgrn, w_out, *route)


def _topk_kernel(aff_ref, pos_ref, s0_ref, bits_s, sel_s, *, cap, tb):
    t = aff_ref.shape[1]
    nb = t // tb
    bits_s[...] = pltpu.bitcast(aff_ref[0], I32)

    def count_ge(cand):
        def body(j, acc):
            r = pl.multiple_of(j * tb, tb)
            blk = bits_s[pl.ds(r, tb), :]
            return acc + jnp.sum((blk >= cand).astype(F32), axis=0, keepdims=True)
        return lax.fori_loop(0, nb, body, jnp.zeros((1, LANES), F32))

    def bit_body(i, thr):
        cand = thr | lax.shift_left(jnp.int32(1), 30 - i)
        return jnp.where(count_ge(cand) >= cap, cand, thr)

    thr = lax.fori_loop(0, 31, bit_body, jnp.zeros((1, LANES), I32))

    def count_gt(j, acc):
        r = pl.multiple_of(j * tb, tb)
        return acc + jnp.sum((bits_s[pl.ds(r, tb), :] > thr).astype(F32), axis=0, keepdims=True)

    need = cap - lax.fori_loop(0, nb, count_gt, jnp.zeros((1, LANES), F32))
    strict = (_iota((tb, tb), 0) > _iota((tb, tb), 1)).astype(BF16)

    def select(j, carry):
        r = pl.multiple_of(j * tb, tb)
        blk = bits_s[pl.ds(r, tb), :]
        eq = blk == thr
        before = carry + _dot(strict, eq.astype(BF16))
        sel = jnp.logical_or(blk > thr, jnp.logical_and(eq, before < need))
        sel_s[pl.ds(r, tb), :] = sel.astype(F32)
        return carry + jnp.sum(eq.astype(F32), axis=0, keepdims=True)

    lax.fori_loop(0, nb, select, jnp.zeros((1, LANES), F32))

    def place(j, carry):
        r = pl.multiple_of(j * tb, tb)
        sel = sel_s[pl.ds(r, tb), :]
        slot = carry + _dot(strict, sel.astype(BF16))
        pos_ref[0, pl.ds(r, tb), :] = jnp.where(sel > 0.5, slot, -1.0).astype(I32)
        s0_ref[0, pl.ds(j, 1), :] = carry.astype(I32)
        return carry + jnp.sum(sel, axis=0, keepdims=True)

    lax.fori_loop(0, nb, place, jnp.zeros((1, LANES), F32))


def _topk(aff, cap, tb):
    b, t, _ = aff.shape
    nb = t // tb
    return pl.pallas_call(
        functools.partial(_topk_kernel, cap=cap, tb=tb),
        grid=(b,),
        in_specs=[pl.BlockSpec((1, t, LANES), lambda bi: (bi, 0, 0))],
        out_specs=[pl.BlockSpec((1, t, LANES), lambda bi: (bi, 0, 0)),
                   pl.BlockSpec((1, nb, LANES), lambda bi: (bi, 0, 0))],
        out_shape=[jax.ShapeDtypeStruct((b, t, LANES), I32), jax.ShapeDtypeStruct((b, nb, LANES), I32)],
        scratch_shapes=[pltpu.VMEM((t, LANES), I32), pltpu.VMEM((t, LANES), F32)],
        compiler_params=_cp("parallel"),
        name="moe_topk",
    )(aff)


MOE_SMALL_COUNT = 32


def _windows(cap, tb):
    if cap <= tb + BF16_ROWS:
        return (cap,)
    return (MOE_SMALL_COUNT + BF16_ROWS, tb + BF16_ROWS)


def _window_starts(s0_ref, bi, j, cap, w):
    if w == cap:
        return [0] * N_EXPERTS
    out = []
    for e in range(N_EXPERTS):
        st = jnp.minimum((s0_ref[bi, j, e] // BF16_ROWS) * BF16_ROWS, cap - w)
        out.append(pl.multiple_of(st, BF16_ROWS))
    return out


def _per_window(windows, narrow_ref, bi, j, run):
    if len(windows) == 1:
        run(windows[0])
    else:
        narrow = narrow_ref[bi, j] > 0
        pl.when(narrow)(lambda: run(windows[0]))
        pl.when(jnp.logical_not(narrow))(lambda: run(windows[1]))


def _gather_kernel(s0_ref, narrow_ref, h_ref, post_ref, o_ref, *, cap, tb):
    bi = pl.program_id(0)
    j = pl.program_id(1)

    @pl.when(j == 0)
    def _():
        o_ref[...] = jnp.zeros_like(o_ref)

    def run(w):
        starts = _window_starts(s0_ref, bi, j, cap, w)
        post = post_ref[0]
        r = _iota((w, tb), 0)
        onehot = jnp.concatenate(
            [jnp.where((post[e:e + 1, :] - starts[e]) == r, 1.0, 0.0).astype(BF16) for e in range(N_EXPERTS)],
            axis=0)
        rows = _dot(onehot, h_ref[0]).astype(BF16)
        for e in range(N_EXPERTS):
            cur = o_ref[0, e, pl.ds(starts[e], w), :]
            o_ref[0, e, pl.ds(starts[e], w), :] = cur + rows[e * w:(e + 1) * w]

    _per_window(_windows(cap, tb), narrow_ref, bi, j, run)


def _gather(h2, post, s0, narrow, cap, tb):
    b, t, d = h2.shape
    nb = t // tb
    grid_spec = pltpu.PrefetchScalarGridSpec(
        num_scalar_prefetch=2,
        grid=(b, nb),
        in_specs=[pl.BlockSpec((1, tb, d), lambda bi, j, s, n: (bi, j, 0)),
                  pl.BlockSpec((1, N_EXPERTS, tb), lambda bi, j, s, n: (bi, 0, j))],
        out_specs=pl.BlockSpec((1, N_EXPERTS, cap, d), lambda bi, j, s, n: (bi, 0, 0, 0)),
    )
    return pl.pallas_call(
        functools.partial(_gather_kernel, cap=cap, tb=tb),
        grid_spec=grid_spec,
        out_shape=jax.ShapeDtypeStruct((b, N_EXPERTS, cap, d), BF16),
        compiler_params=_cp("parallel", "arbitrary"),
        name="moe_gather",
    )(s0, narrow, h2, post)


def _expert_kernel(x_ref, wg_ref, wu_ref, wd_ref, y_ref, wg_s, wu_s, wd_s):
    @pl.when(pl.program_id(1) == 0)
    def _():
        wg_s[...] = wg_ref[0].astype(BF16)
        wu_s[...] = wu_ref[0].astype(BF16)
        wd_s[...] = wd_ref[0].astype(BF16)

    shp = x_ref.shape
    x = x_ref[...].reshape(shp[0] * shp[2], shp[3])
    a = _dot(x, wg_s[...])
    u = _dot(x, wu_s[...])
    hm = (_silu(a) * u).astype(BF16)
    y_ref[...] = _dot(hm, wd_s[...]).astype(BF16).reshape(shp)


def _experts(xin, w_gate, w_up, w_down, layer):
    b, e, cap, d = xin.shape
    f = w_gate.shape[-1]
    bb = max(1, min(b, 512 // cap))
    xs = pl.BlockSpec((bb, 1, cap, d), lambda ei, bi: (bi, ei, 0, 0))
    ws = lambda a: pl.BlockSpec((None, 1) + a.shape[2:], lambda ei, bi: (layer, ei, 0, 0))
    return pl.pallas_call(
        _expert_kernel,
        grid=(e, b // bb),
        in_specs=[xs, ws(w_gate), ws(w_up), ws(w_down)],
        out_specs=xs,
        out_shape=jax.ShapeDtypeStruct(xin.shape, BF16),
        scratch_shapes=[pltpu.VMEM((d, f), BF16), pltpu.VMEM((d, f), BF16), pltpu.VMEM((f, d), BF16)],
        compiler_params=_cp("parallel", "arbitrary"),
        name="moe_experts",
    )(xin, w_gate, w_up, w_down)


def _combine_kernel(*refs, cap, tb, final):
    windows = _windows(cap, tb)
    s0_ref, narrow_ref, y_ref, pos_ref, aff_ref, x_ref, gate_ref = refs[:7]
    pos_n = 7
    if final:
        gf_ref = refs[pos_n]
        pos_n += 1
    spread_refs = refs[pos_n:pos_n + 2 * len(windows)]
    o_ref = refs[-1]
    bi = pl.program_id(0)
    j = pl.program_id(1)
    offset = 1024
    lane = _iota((1, N_EXPERTS), 1)

    def run(w):
        k = windows.index(w)
        spread = spread_refs[2 * k][...]
        row_id = spread_refs[2 * k + 1][...]
        starts = _window_starts(s0_ref, bi, j, cap, w)
        stv = jnp.zeros((1, N_EXPERTS), I32)
        for e in range(N_EXPERTS):
            stv = jnp.where(lane == e, starts[e], stv)
        rel = pos_ref[0] - stv + offset
        hi = lax.shift_right_logical(rel, 5).astype(F32).astype(BF16)
        lo = jnp.bitwise_and(rel, 31).astype(F32).astype(BF16)
        relx = 32.0 * _dot(hi, spread) + _dot(lo, spread) - offset
        gx = _dot(aff_ref[0].astype(BF16), spread)
        g = jnp.where(relx == row_id, gx, 0.0)
        ys = jnp.concatenate([y_ref[0, e, pl.ds(starts[e], w), :] for e in range(N_EXPERTS)], axis=0)
        out = x_ref[0] + gate_ref[0] * _dot(g.astype(BF16), ys)
        if final:
            out = _rms_rows(out, gf_ref[...])
        o_ref[0] = out

    _per_window(windows, narrow_ref, bi, j, run)


def _combine(y, pos, aff, x3, gate, s0, narrow, cap, tb, per_batch, g_final):
    b, t, d = x3.shape
    nb = t // tb
    final = g_final is not None
    tok = lambda c: pl.BlockSpec((1, tb, c), lambda bi, j, s, n: (bi, j, 0))
    const = lambda a: pl.BlockSpec(a.shape, lambda bi, j, s, n: (0, 0))
    in_specs = [pl.BlockSpec((1, N_EXPERTS, cap, d), lambda bi, j, s, n: (bi, 0, 0, 0)),
                tok(N_EXPERTS), tok(N_EXPERTS), tok(d),
                pl.BlockSpec((1, 1, d), (lambda bi, j, s, n: (bi, 0, 0)) if per_batch
                             else (lambda bi, j, s, n: (0, 0, 0)))]
    args = [y, pos, aff, x3, gate]
    if final:
        in_specs.append(const(g_final))
        args.append(g_final)
    for w in _windows(cap, tb):
        spread = jnp.repeat(jnp.eye(N_EXPERTS, dtype=BF16), w, axis=1)
        row_id = jnp.tile(jnp.arange(w, dtype=F32), N_EXPERTS).reshape(1, -1)
        in_specs += [const(spread), const(row_id)]
        args += [spread, row_id]
    grid_spec = pltpu.PrefetchScalarGridSpec(
        num_scalar_prefetch=2, grid=(b, nb), in_specs=in_specs, out_specs=tok(d))
    return pl.pallas_call(
        functools.partial(_combine_kernel, cap=cap, tb=tb, final=final),
        grid_spec=grid_spec,
        out_shape=jax.ShapeDtypeStruct((b, t, d), F32),
        compiler_params=_cp("parallel", "arbitrary"),
        name="moe_combine",
    )(s0, narrow, *args)


def _moe(x2, h2, aff, b, t, gate, w_gate, w_up, w_down, layer, per_batch, g_final):
    cap = CAPACITY_FACTOR * t // N_EXPERTS
    tb = min(MOE_TOKEN_BLOCK, t)
    aff3 = aff.reshape(b, t, N_EXPERTS)
    per = LANES // N_EXPERTS
    groups = -(-b // per)
    nb = t // tb

    def pack(a):
        a = jnp.pad(a, ((0, groups * per - b), (0, 0), (0, 0)))
        return a.reshape(groups, per, a.shape[1], N_EXPERTS).transpose(0, 2, 1, 3).reshape(groups, a.shape[1], LANES)

    def unpack(a):
        return a.reshape(groups, a.shape[1], per, N_EXPERTS).transpose(0, 2, 1, 3).reshape(
            groups * per, a.shape[1], N_EXPERTS)[:b]

    pos, s0 = _topk(pack(aff3), cap, tb)
    pos, s0 = unpack(pos), unpack(s0)
    counts = jnp.diff(s0, axis=1, append=jnp.full((b, 1, N_EXPERTS), cap, I32))
    narrow = jnp.all(counts <= MOE_SMALL_COUNT, axis=-1).astype(I32)
    xin = _gather(h2.reshape(b, t, D_MODEL), pos.transpose(0, 2, 1), s0, narrow, cap, tb)
    y = _experts(xin, w_gate, w_up, w_down, layer)
    out = _combine(y, pos, aff3, x2.reshape(b, t, D_MODEL), gate, s0, narrow, cap, tb, per_batch, g_final)
    return out.reshape(b * t, D_MODEL)


def _rope_tables(t):
    rows = t // GRID_W
    row = jnp.repeat(jnp.arange(rows, dtype=F32), GRID_W)
    col = jnp.tile(jnp.arange(GRID_W, dtype=F32), rows)
    n_freq = DQK_A // 4
    inv_freq = jnp.power(ROPE_BASE, -jnp.arange(n_freq, dtype=F32) / n_freq)
    ang = jnp.concatenate([row[:, None] * inv_freq, col[:, None] * inv_freq], axis=-1)
    cos, sin = jnp.cos(ang), jnp.sin(ang)
    zero = jnp.zeros_like(sin)
    reps = LANES // DQK_A
    cos_t = jnp.tile(jnp.concatenate([cos, cos], axis=-1), (1, reps))
    sin_a = jnp.tile(jnp.concatenate([-sin, zero], axis=-1), (1, reps))
    sin_b = jnp.tile(jnp.concatenate([zero, sin], axis=-1), (1, reps))
    return cos_t, sin_a, sin_b


def _stream(x, mods, per_batch, rope, cache, p):
    b, t, d = x.shape
    x2 = x.reshape(b * t, d)
    depth = p["g_norm"].shape[0]
    even_out, odd_out = [], []
    for l in range(depth):
        j = l // 2
        sh1, sc1, gt1, sh2, sc2, gt2 = mods[l]
        g1 = p["g_norm"][l, 0].reshape(1, d)
        g2 = p["g_norm"][l, 1].reshape(1, d)
        route = (g2, sh2, sc2, p["w_router_pad"][l])
        if l % 2 == 0:
            aq, ak, av, bqk, bv, bo, gates = _in_ab(x2, g1, sh1, sc1, p["w_in_ab_main"][j], p["w_in_ab_gate"][j],
                                                     rope, t, per_batch, F32 if cache is None else BF16)
            r3 = lambda a: a.reshape(b, t, a.shape[-1])
            if cache is not None:
                kc = cache[0][:, j].reshape(b, -1, W_A)
                vc = cache[1][:, j].reshape(b, -1, W_A)
                nu = 2 * H_B
                init = (cache[2][:, j].reshape(b, nu, DH_B, DH_B),
                        cache[3][:, j].reshape(b, nu, 1, DH_B),
                        jnp.broadcast_to(cache[4][:, j].reshape(b, nu, 1, 1), (b, nu, 1, LANES)))
            else:
                kc = vc = init = None
            lam_init = 0.8 - 0.6 * math.exp(-0.3 * l)
            attn = _attention_pipe(r3(aq), r3(ak), r3(av), kc, vc, p["lam_ab"][j], lam_init)
            qk = _conv_silu(r3(bqk), p["conv_ab"][j])
            bias = jnp.pad(jnp.concatenate([p["b_ig_ab"][j].reshape(-1), p["b_fg_ab"][j].reshape(-1)]),
                           (0, LANES - 4 * H_B)).reshape(1, LANES)
            res = _mlstm(qk, r3(bv), r3(gates), bias, init, cache is None)
            hf, hb = res[0], res[1]
            if cache is None:
                even_out.append((ak, av, res[2], res[3], res[4]))
            x2, h2, aff = _out_ab(attn.reshape(b * t, W_A), hf.reshape(b * t, W_B), hb.reshape(b * t, W_B), bo, x2,
                                  gt1, p["g_attn_ab"][j].reshape(1, W_A), p["g_mlstm_ab"][j].reshape(1, W_B),
                                  p["w_out_ab_bf16"][j], route, t, per_batch, 1.0 - lam_init)
        else:
            q, v, og, lf, kk = _in_c(x2, g1, sh1, sc1, p["w_in_c_bf16"][j], p["b_f_c"][j], p["gamma_lb"], l, t,
                                     per_batch)
            r3 = lambda a: a.reshape(b, t, a.shape[-1])
            init = None if cache is None else cache[5][:, j]
            res = _hgrn(r3(q), r3(v), r3(lf), r3(kk), init, cache is None)
            if cache is None:
                odd_out.append(res[2])
            x2, h2, aff = _out_c(res[0].reshape(b * t, W_C), res[1].reshape(b * t, W_C), og, x2, gt1,
                                 p["g_hgrn_c"][j].reshape(1, W_C), p["w_out_c_bf16"][j], route, t, per_batch)
        g_final = p["g_final"].reshape(1, d) if l == depth - 1 else None
        x2 = _moe(x2, h2, aff, b, t, gt2, p["w_gate_e"], p["w_up_e"], p["w_down_e"], l, per_batch, g_final)
    return x2.reshape(b, t, d), even_out, odd_out


def kernel(x_prompt, x_sample, cache_dattn_k, cache_dattn_v, state_mlstm_C, state_mlstm_n, state_mlstm_m,
           state_hgrn_S, c, c_ctx, w_ada, b_ada, g_norm, g_final, w_in_ab, b_ig_ab, b_fg_ab, lam_ab, conv_ab,
           g_attn_ab, g_mlstm_ab, w_out_ab, w_in_c, b_f_c, gamma_lb, g_hgrn_c, w_out_c,
           w_router, w_gate_e, w_up_e, w_down_e):
    d = D_MODEL
    bs = c.shape[0]
    bp, tp, _ = x_prompt.shape
    depth = w_ada.shape[0]
    n_main = 3 * W_A + 4 * W_B
    p = dict(
        g_norm=g_norm, g_final=g_final, b_ig_ab=b_ig_ab, b_fg_ab=b_fg_ab, lam_ab=lam_ab, conv_ab=conv_ab,
        g_attn_ab=g_attn_ab, g_mlstm_ab=g_mlstm_ab, b_f_c=b_f_c, gamma_lb=gamma_lb, g_hgrn_c=g_hgrn_c,
        w_gate_e=w_gate_e, w_up_e=w_up_e, w_down_e=w_down_e,
        w_in_ab_main=w_in_ab[:, :, :n_main].astype(BF16),
        w_in_ab_gate=jnp.pad(w_in_ab[:, :, n_main:], ((0, 0), (0, 0), (0, LANES - 4 * H_B))).astype(BF16),
        w_out_ab_bf16=w_out_ab.astype(BF16),
        w_in_c_bf16=w_in_c.astype(BF16),
        w_out_c_bf16=w_out_c.astype(BF16),
        w_router_pad=jnp.pad(w_router, ((0, 0), (0, 0), (0, LANES - N_EXPERTS))),
    )
    rows = -(-(bs + 1) // 8) * 8
    cond = jnp.concatenate([c, c_ctx.reshape(1, d), jnp.zeros((rows - bs - 1, d), F32)], axis=0)
    mods = _ada(cond, w_ada, b_ada)
    mods_s = [[mods[l, :bs, i * d:(i + 1) * d].reshape(bs, 1, d) for i in range(6)] for l in range(depth)]
    mods_c = [[mods[l, bs:bs + 1, i * d:(i + 1) * d].reshape(1, 1, d) for i in range(6)] for l in range(depth)]

    y_prompt, ev, od = _stream(x_prompt, mods_c, False, None, None, p)
    new_k = jnp.stack([e[0].reshape(bp, tp, H_A, 2 * DQK_A) for e in ev], axis=1)
    new_v = jnp.stack([e[1].reshape(bp, tp, H_A, DV_A) for e in ev], axis=1)
    new_c = jnp.stack([e[2].reshape(bp, 2, H_B, DH_B, DH_B) for e in ev], axis=1)
    new_n = jnp.stack([e[3].reshape(bp, 2, H_B, DH_B) for e in ev], axis=1)
    new_m = jnp.stack([e[4][..., 0].reshape(bp, 2, H_B) for e in ev], axis=1)
    new_s = jnp.stack(od, axis=1)

    cache = (cache_dattn_k, cache_dattn_v, state_mlstm_C, state_mlstm_n, state_mlstm_m, state_hgrn_S)
    y_sample, _, _ = _stream(x_sample, mods_s, True, _rope_tables(x_sample.shape[1]), cache, p)
    return (y_prompt, y_sample, new_k, new_v, new_c, new_n, new_m, new_s)
```

```python
import functools
import math

import jax
import jax.numpy as jnp
from jax import lax
from jax.experimental import pallas as pl
from jax.experimental.pallas import tpu as pltpu

F32 = jnp.float32
BF16 = jnp.bfloat16
I32 = jnp.int32

D_MODEL = 1024
H_A = 4
DV_A = 128
DQK_A = 64
W_A = H_A * DV_A
H_B = 4
DH_B = 128
W_B = H_B * DH_B
H_C = 8
DH_C = 128
W_C = H_C * DH_C
N_EXPERTS = 16
CAPACITY_FACTOR = 2
GRID_W = 64
ROPE_BASE = 10000.0
RMS_EPS = 1e-6
NEG_BIG = -1e30
LOG2E = 1.4426950408889634
LANES = 128
BF16_ROWS = 16
SCAN_CHUNK = 128
MOE_TOKEN_BLOCK = 256
VMEM_LIMIT = 56 * 1024 * 1024


def _cp(*sem):
    return pltpu.CompilerParams(dimension_semantics=sem, vmem_limit_bytes=VMEM_LIMIT)


def _dot(a, b):
    return jnp.dot(a, b, preferred_element_type=F32)


def _dot_nt(a, b):
    return lax.dot_general(a, b, (((1,), (1,)), ((), ())), preferred_element_type=F32)


def _dot_tn(a, b):
    return lax.dot_general(a, b, (((0,), (0,)), ((), ())), preferred_element_type=F32)


def _split3(x):
    hi = x.astype(BF16)
    r = x - hi.astype(F32)
    mid = r.astype(BF16)
    lo = (r - mid.astype(F32)).astype(BF16)
    return hi, mid, lo


def _dot_sel_left(sel, x):
    hi, mid, lo = _split3(x)
    return _dot(sel, hi) + _dot(sel, mid) + _dot(sel, lo)


def _dot_sel_right(x, sel):
    hi, mid, lo = _split3(x)
    return _dot(hi, sel) + _dot(mid, sel) + _dot(lo, sel)


def _dot_f32(a, b):
    ah, am, al = _split3(a)
    bh, bm, bl = _split3(b)
    return (_dot(ah, bh) + (_dot(ah, bm) + _dot(am, bh))
            + (_dot(ah, bl) + _dot(am, bm) + _dot(al, bh)))


def _dot_f32x3(a, b):
    ah = a.astype(BF16)
    al = (a - ah.astype(F32)).astype(BF16)
    bh = b.astype(BF16)
    bl = (b - bh.astype(F32)).astype(BF16)
    return _dot(ah, bh) + (_dot(ah, bl) + _dot(al, bh))


def _dot_sel_left2(sel, x):
    hi = x.astype(BF16)
    lo = (x - hi.astype(F32)).astype(BF16)
    return _dot(sel, hi) + _dot(sel, lo)


def _sigmoid(x):
    return 1.0 / (1.0 + jnp.exp(-x))


def _silu(x):
    return x * _sigmoid(x)


def _log_sigmoid(x):
    return jnp.minimum(x, 0.0) - jnp.log(1.0 + jnp.exp(-jnp.abs(x)))


def _rms_rows(x, g):
    return x * lax.rsqrt(jnp.mean(x * x, axis=-1, keepdims=True) + RMS_EPS) * g


def _iota(shape, dim):
    return lax.broadcasted_iota(I32, shape, dim)


def _ada_kernel(c_ref, w_ref, b_ref, o_ref):
    c = c_ref[...]
    o_ref[0] = _dot_f32(_silu(c), w_ref[0]) + b_ref[0]


def _ada(cond_rows, w_ada, b_ada):
    depth, d, n6 = w_ada.shape
    rows = cond_rows.shape[0]
    tn = 1536
    return pl.pallas_call(
        _ada_kernel,
        grid=(depth, n6 // tn),
        in_specs=[pl.BlockSpec((rows, d), lambda l, j: (0, 0)),
                  pl.BlockSpec((1, d, tn), lambda l, j: (l, 0, j)),
                  pl.BlockSpec((1, 1, tn), lambda l, j: (l, 0, j))],
        out_specs=pl.BlockSpec((1, rows, tn), lambda l, j: (l, 0, j)),
        out_shape=jax.ShapeDtypeStruct((depth, rows, n6), F32),
        compiler_params=_cp("parallel", "parallel"),
        name="adaln",
    )(cond_rows, w_ada, b_ada.reshape(depth, 1, n6))


def _mod_spec(per_batch, tm, t):
    if per_batch:
        return pl.BlockSpec((1, 1, D_MODEL), lambda i: (i * tm // t, 0, 0))
    return pl.BlockSpec((1, 1, D_MODEL), lambda i: (0, 0, 0))


def _row_tile(t):
    return min(512, t)


def _in_ab_kernel(*refs, use_rope):
    if use_rope:
        (x_ref, g_ref, sh_ref, sc_ref, w_ref, wg_ref, cos_ref, sa_ref, sb_ref,
         aq_ref, ak_ref, av_ref, bqk_ref, bv_ref, bo_ref, gt_ref) = refs
    else:
        (x_ref, g_ref, sh_ref, sc_ref, w_ref, wg_ref,
         aq_ref, ak_ref, av_ref, bqk_ref, bv_ref, bo_ref, gt_ref) = refs
    h = _rms_rows(x_ref[...], g_ref[...]) * (1.0 + sc_ref[0]) + sh_ref[0]
    hb = h.astype(BF16)
    y = _dot(hb, w_ref[...])
    gt_ref[...] = _dot(hb, wg_ref[...])
    aq = y[:, 0:W_A] * (DQK_A ** -0.5 * LOG2E)
    ak = y[:, W_A:2 * W_A]
    if use_rope:
        cos = jnp.concatenate([cos_ref[...]] * H_A, axis=1)
        sa = jnp.concatenate([sa_ref[...]] * H_A, axis=1)
        sb = jnp.concatenate([sb_ref[...]] * H_A, axis=1)
        half = DQK_A // 2

        def rope(v):
            return (v * cos + pltpu.roll(v, W_A - half, axis=1) * sa
                    + pltpu.roll(v, half, axis=1) * sb)

        aq = rope(aq)
        ak = rope(ak)
    aq_ref[...] = aq.astype(aq_ref.dtype)
    ak_ref[...] = ak.astype(ak_ref.dtype)
    av_ref[...] = y[:, 2 * W_A:3 * W_A].astype(av_ref.dtype)
    bqk_ref[...] = y[:, 3 * W_A:3 * W_A + 2 * W_B]
    bv_ref[...] = y[:, 3 * W_A + 2 * W_B:3 * W_A + 3 * W_B]
    bo_ref[...] = y[:, 3 * W_A + 3 * W_B:3 * W_A + 4 * W_B]


def _in_ab(x2, g, shift, scale, w_main, w_gate, rope, t, per_batch, qkv_dtype):
    n = x2.shape[0]
    tm = _row_tile(t)
    use_rope = rope is not None
    row = lambda c: pl.BlockSpec((tm, c), lambda i: (i, 0))
    full = lambda a: pl.BlockSpec(a.shape, lambda i: (0,) * a.ndim)
    in_specs = [row(D_MODEL), full(g), _mod_spec(per_batch, tm, t), _mod_spec(per_batch, tm, t),
                full(w_main), full(w_gate)]
    args = [x2, g, shift, scale, w_main, w_gate]
    if use_rope:
        nt = t // tm
        tab = pl.BlockSpec((tm, LANES), lambda i: (i % nt, 0))
        in_specs += [tab, tab, tab]
        args += list(rope)
    widths = (W_A, W_A, W_A, 2 * W_B, W_B, W_B, LANES)
    return pl.pallas_call(
        functools.partial(_in_ab_kernel, use_rope=use_rope),
        grid=(n // tm,),
        in_specs=in_specs,
        out_specs=[row(c) for c in widths],
        out_shape=[jax.ShapeDtypeStruct((n, c), qkv_dtype if i < 3 else F32) for i, c in enumerate(widths)],
        compiler_params=_cp("parallel"),
        name="in_proj_even",
    )(*args)


def _attn_pipe_kernel(*refs, has_cache, tk, lam_init, nq):
    if has_cache:
        q_ref, k_ref, v_ref, kc_ref, vc_ref, lam_ref, o_ref, s_s, mp_s, vt_s, sc_s, vct_s = refs
    else:
        q_ref, k_ref, v_ref, lam_ref, o_ref, s_s, mp_s, vt_s = refs
    i = pl.program_id(2)
    q = q_ref[0]
    tq = q.shape[0]
    first = _iota((1, LANES), 1) < DQK_A
    qs = (jnp.where(first, q, 0).astype(BF16), jnp.where(first, 0, q).astype(BF16))
    nkb = k_ref.shape[1] // tk

    @pl.when(i == 0)
    def _():
        for j in range(nkb):
            vt_s[j, 0:DV_A, :] = v_ref[0, j * tk:(j + 1) * tk, :].astype(F32).T.astype(BF16)
            vt_s[j, DV_A:, :] = jnp.ones((BF16_ROWS, tk), BF16)
        if has_cache:
            vct_s[0:DV_A, :] = vc_ref[0].astype(F32).T.astype(BF16)
            vct_s[DV_A:, :] = jnp.ones((BF16_ROWS, vct_s.shape[1]), BF16)

    def groups(x):
        return [x[:, g * LANES:(g + 1) * LANES] for g in range(x.shape[1] // LANES)]

    def body(score, finish, cur):
        prev = 1 - cur
        if finish:
            m = [jnp.max(mp_s[prev, a], axis=-1, keepdims=True) for a in range(2)]
        mpart = [jnp.full((tq, LANES), NEG_BIG, F32)] * 2
        acc = [jnp.zeros((DV_A + BF16_ROWS, tq), F32)] * 2

        def pass1(kb, store):
            for a in range(2):
                s = _dot_nt(qs[a], kb)
                store(a, s)
                mpart[a] = functools.reduce(jnp.maximum, groups(s), mpart[a])

        def pass2(a, s, vt):
            acc[a] = acc[a] + _dot_nt(vt, jnp.exp2(s - m[a]).astype(BF16))

        if has_cache:
            def store_c(a, s):
                sc_s[cur, a] = s
            if score:
                pass1(kc_ref[0].astype(BF16), store_c)
            if finish:
                for a in range(2):
                    pass2(a, sc_s[prev, a], vct_s[...])
        for j in range(nkb):
            def store_j(a, s, j=j):
                s_s[cur, a, j] = s
            if score:
                pass1(k_ref[0, j * tk:(j + 1) * tk, :].astype(BF16), store_j)
            if finish:
                for a in range(2):
                    pass2(a, s_s[prev, a, j], vt_s[j])
        if score:
            for a in range(2):
                mp_s[cur, a] = mpart[a]
        if finish:
            lp = lam_ref[...]
            lam = (jnp.exp(jnp.sum(lp[0:1] * lp[1:2], axis=-1, keepdims=True))
                   - jnp.exp(jnp.sum(lp[2:3] * lp[3:4], axis=-1, keepdims=True)) + lam_init)
            o1 = acc[0][0:DV_A] / acc[0][DV_A:DV_A + 1]
            o2 = acc[1][0:DV_A] / acc[1][DV_A:DV_A + 1]
            o_ref[0] = (o1 - lam * o2).T

    pl.when(i == 0)(lambda: body(True, False, 0))
    inner = jnp.logical_and(i > 0, i < nq)
    pl.when(jnp.logical_and(inner, i % 2 == 0))(lambda: body(True, True, 0))
    pl.when(jnp.logical_and(inner, i % 2 == 1))(lambda: body(True, True, 1))
    pl.when(i == nq)(lambda: body(False, True, nq % 2))


def _attention_pipe(aq, ak, av, kc, vc, lam_p, lam_init):
    b, t, _ = aq.shape
    tq = min(256, t)
    tk = min(512, t)
    nq = t // tq
    has_cache = kc is not None
    qspec = pl.BlockSpec((1, tq, DV_A), lambda bi, h, i: (bi, jnp.minimum(i, nq - 1), h))
    ospec = pl.BlockSpec((1, tq, DV_A), lambda bi, h, i: (bi, jnp.maximum(i - 1, 0), h))
    ks = pl.BlockSpec((1, t, DV_A), lambda bi, h, i: (bi, 0, h))
    in_specs = [qspec, ks, ks]
    args = [aq, ak, av]
    if has_cache:
        cs = pl.BlockSpec((1, kc.shape[1], DV_A), lambda bi, h, i: (bi, 0, h))
        in_specs += [cs, cs]
        args += [kc, vc]
    in_specs.append(pl.BlockSpec(lam_p.shape, lambda bi, h, i: (0, 0)))
    args.append(lam_p)
    scratch = [pltpu.VMEM((2, 2, t // tk, tq, tk), F32), pltpu.VMEM((2, 2, tq, LANES), F32),
               pltpu.VMEM((t // tk, DV_A + BF16_ROWS, tk), BF16)]
    if has_cache:
        scratch += [pltpu.VMEM((2, 2, tq, kc.shape[1]), F32),
                    pltpu.VMEM((DV_A + BF16_ROWS, kc.shape[1]), BF16)]
    return pl.pallas_call(
        functools.partial(_attn_pipe_kernel, has_cache=has_cache, tk=tk, lam_init=lam_init, nq=nq),
        grid=(b, H_A, nq + 1),
        in_specs=in_specs,
        out_specs=ospec,
        out_shape=jax.ShapeDtypeStruct((b, t, W_A), F32),
        scratch_shapes=scratch,
        compiler_params=_cp("parallel", "parallel", "arbitrary"),
        name="diff_attention",
    )(*args)


def _conv_kernel(x_ref, w_ref, o_ref):
    x = x_ref[0]
    t = x.shape[0]
    w = w_ref[...]
    r = _iota((t, 1), 0)
    prev = jnp.where(r == 0, 0.0, pltpu.roll(x, 1, axis=0))
    nxt = jnp.where(r == t - 1, 0.0, pltpu.roll(x, t - 1, axis=0))
    o_ref[0] = _silu(prev * w[0:1] + x * w[1:2] + nxt * w[2:3])


def _conv_silu(bqk, conv_w):
    b, t, c = bqk.shape
    tc = 256
    spec = pl.BlockSpec((1, t, tc), lambda bi, j: (bi, 0, j))
    return pl.pallas_call(
        _conv_kernel,
        grid=(b, c // tc),
        in_specs=[spec, pl.BlockSpec((conv_w.shape[0], tc), lambda bi, j: (0, j))],
        out_specs=spec,
        out_shape=jax.ShapeDtypeStruct(bqk.shape, F32),
        compiler_params=_cp("parallel", "parallel"),
        name="mlstm_conv",
    )(bqk, conv_w)


def _tri(n, lower):
    r = _iota((n, n), 0)
    c = _iota((n, n), 1)
    return r >= c if lower else r <= c


def _mlstm_kernel(*refs, has_init, out_state):
    qf, kf, vf, gf, qb, kb, vb, gb, bias = refs[:9]
    pos = 9
    if has_init:
        c0, n0, m0 = refs[pos:pos + 3]
        pos += 3
    hf, hb = refs[pos:pos + 2]
    pos += 2
    if out_state:
        c_out, n_out, m_out = refs[pos:pos + 3]
        pos += 3
    cn_s, m_s = refs[pos:pos + 2]
    step = pl.program_id(1)
    L = SCAN_CHUNK
    nu = 2 * H_B

    @pl.when(step == 0)
    def _():
        if has_init:
            for u in range(nu):
                cn_s[u, :, 0:DH_B] = c0[0, u]
                cn_s[u, :, DH_B:2 * DH_B] = jnp.broadcast_to(n0[0, u], (DH_B, DH_B)).T
            m_s[...] = m0[0]
        else:
            cn_s[...] = jnp.zeros_like(cn_s)
            m_s[...] = jnp.zeros_like(m_s)

    ones_b = jnp.ones((L, DH_B), BF16)
    sel_rows = _iota((LANES, LANES), 0)
    lower = _tri(L, True)
    upper = _tri(L, False)
    lower_b = lower.astype(BF16)
    upper_b = upper.astype(BF16)
    col = _iota((1, LANES), 1)
    units = []
    for d, (q_ref, k_ref, v_ref, g_ref, h_ref) in enumerate(((qf, kf, vf, gf, hf), (qb, kb, vb, gb, hb))):
        g = g_ref[0] + bias[...]
        g = jnp.where(col < 2 * H_B, g, _log_sigmoid(g))
        gt = g.T
        if d == 0:
            bc_all = _dot_sel_left(lower_b, g)
            br_all = _dot_sel_right(gt, upper_b)
            mask = lower
        else:
            bc_all = _dot_sel_left(upper_b, g)
            br_all = _dot_sel_right(gt, lower_b)
            mask = upper
        for h in range(H_B):
            u = d * H_B + h
            sl = slice(h * DH_B, (h + 1) * DH_B)
            units.append(dict(
                u=u, d=d, sl=sl, h_ref=h_ref, mask=mask, g=g, bc_all=bc_all,
                li_row=gt[u:u + 1, :], b_row=br_all[2 * H_B + u:2 * H_B + u + 1, :],
                q=q_ref[0, :, sl].astype(BF16), k=k_ref[0, :, sl] * (DH_B ** -0.5),
                v1=jnp.concatenate([v_ref[0, :, sl].astype(BF16), ones_b], axis=1)))
    for x in units:
        x["li_b"] = _dot_sel_right(x["g"], (sel_rows == x["u"]).astype(BF16))
        x["b_b"] = _dot_sel_right(x["bc_all"], (sel_rows == 2 * H_B + x["u"]).astype(BF16))
        x["qk"] = _dot_nt(x["q"], x["k"].astype(BF16))
        x["cn"] = cn_s[x["u"]]
        x["qc"] = _dot(x["q"], x["cn"].astype(BF16))
    for x in units:
        x["m_prev"] = m_s[x["u"]]
        a_b = x["b_b"] + x["m_prev"]
        dm = jnp.where(x["mask"], x["b_b"] - x["b_row"] + x["li_row"], NEG_BIG)
        x["m_t"] = jnp.maximum(a_b, jnp.max(dm, axis=-1, keepdims=True))
        x["w_inter"] = jnp.exp(a_b - x["m_t"])
        x["s"] = (x["qk"] * jnp.exp(dm - x["m_t"])).astype(BF16)
    for x in units:
        x["sv"] = _dot(x["s"], x["v1"])
        b_end = x["b_b"][L - 1:L, :] if x["d"] == 0 else x["b_b"][0:1, :]
        g_b = b_end - x["b_b"] + x["li_b"]
        x["m_new"] = jnp.maximum(b_end + x["m_prev"], jnp.max(g_b, axis=0, keepdims=True))
        x["w_old"] = jnp.exp(b_end + x["m_prev"] - x["m_new"])
        x["ks"] = (jnp.exp(g_b - x["m_new"]) * x["k"]).astype(BF16)
    for x in units:
        num = x["w_inter"] * x["qc"][:, 0:DH_B] + x["sv"][:, 0:DH_B]
        den = x["w_inter"] * x["qc"][:, DH_B:] + x["sv"][:, DH_B:]
        x["h_ref"][0, :, x["sl"]] = num / jnp.maximum(jnp.abs(den), jnp.exp(-x["m_t"]))
        w2 = jnp.concatenate([x["w_old"], x["w_old"]], axis=1)
        cn_s[x["u"]] = w2 * x["cn"] + _dot_tn(x["ks"], x["v1"])
        m_s[x["u"]] = x["m_new"]

    if out_state:
        @pl.when(step == pl.num_programs(1) - 1)
        def _():
            for u in range(nu):
                c_out[0, u] = cn_s[u, :, 0:DH_B]
                n_out[0, u] = cn_s[u, :, DH_B:2 * DH_B].T[0:1, :]
            m_out[0] = m_s[...]


def _mlstm(qk, v, gates, bias, init, out_state):
    b, t, _ = v.shape
    L = SCAN_CHUNK
    nc = t // L
    nu = 2 * H_B
    fwd = lambda j: (lambda bi, c: (bi, c, j))
    bwd = lambda j: (lambda bi, c: (bi, nc - 1 - c, j))
    blk = lambda w, im: pl.BlockSpec((1, L, w), im)
    in_specs = [blk(W_B, fwd(0)), blk(W_B, fwd(1)), blk(W_B, fwd(0)), blk(LANES, fwd(0)),
                blk(W_B, bwd(0)), blk(W_B, bwd(1)), blk(W_B, bwd(0)), blk(LANES, bwd(0)),
                pl.BlockSpec((1, LANES), lambda bi, c: (0, 0))]
    args = [qk, qk, v, gates, qk, qk, v, gates, bias]
    st_specs = [pl.BlockSpec((1, nu, DH_B, DH_B), lambda bi, c: (bi, 0, 0, 0)),
                pl.BlockSpec((1, nu, 1, DH_B), lambda bi, c: (bi, 0, 0, 0)),
                pl.BlockSpec((1, nu, 1, LANES), lambda bi, c: (bi, 0, 0, 0))]
    st_shapes = [jax.ShapeDtypeStruct((b, nu, DH_B, DH_B), F32),
                 jax.ShapeDtypeStruct((b, nu, 1, DH_B), F32),
                 jax.ShapeDtypeStruct((b, nu, 1, LANES), F32)]
    has_init = init is not None
    if has_init:
        in_specs += st_specs
        args += list(init)
    out_specs = [blk(W_B, fwd(0)), blk(W_B, bwd(0))]
    out_shape = [jax.ShapeDtypeStruct((b, t, W_B), F32)] * 2
    if out_state:
        out_specs += st_specs
        out_shape += st_shapes
    return pl.pallas_call(
        functools.partial(_mlstm_kernel, has_init=has_init, out_state=out_state),
        grid=(b, nc),
        in_specs=in_specs,
        out_specs=out_specs,
        out_shape=out_shape,
        scratch_shapes=[pltpu.VMEM((nu, DH_B, 2 * DH_B), F32), pltpu.VMEM((nu, 1, LANES), F32)],
        compiler_params=_cp("parallel", "arbitrary"),
        name="mlstm_scan",
    )(*args)


def _head_rms(x, g, heads, width):
    parts = []
    for h in range(heads):
        sl = slice(h * width, (h + 1) * width)
        parts.append(_rms_rows(x[:, sl], g[:, sl]))
    return parts


OUT_PROJ_PARTS = 4


def _row_parts(tm):
    step = tm // OUT_PROJ_PARTS
    return [slice(r, r + step) for r in range(0, tm, step)]


def _project_and_route(cats, parts, x_ref, gate_ref, w_ref, g_ref, sh_ref, sc_ref, wr_ref, o_ref, h_ref, aff_ref):
    w = w_ref[...]
    xs = [x_ref[rs, :] + gate_ref[0] * _dot(cat, w) for cat, rs in zip(cats, parts)]
    hs = []
    for x, rs in zip(xs, parts):
        o_ref[rs, :] = x
        h = _rms_rows(x, g_ref[...]) * (1.0 + sc_ref[0]) + sh_ref[0]
        h_ref[rs, :] = h.astype(BF16)
        hs.append(h)
    wr = wr_ref[...]
    logits = [_dot_f32x3(h, wr) for h in hs]
    valid = _iota((1, LANES), 1) < N_EXPERTS
    for lg, rs in zip(logits, parts):
        lg = jnp.where(valid, lg, NEG_BIG)
        e = jnp.exp(lg - jnp.max(lg, axis=-1, keepdims=True))
        e = jnp.where(valid, e, 0.0)
        aff_ref[rs, :] = (e / jnp.sum(e, axis=-1, keepdims=True))[:, :N_EXPERTS]


def _route_specs(route, tm, t, per_batch):
    g2, sh2, sc2, wr = route
    full = lambda a: pl.BlockSpec(a.shape, lambda i: (0,) * a.ndim)
    row = lambda c: pl.BlockSpec((tm, c), lambda i: (i, 0))
    in_specs = [full(g2), _mod_spec(per_batch, tm, t), _mod_spec(per_batch, tm, t), full(wr)]
    out_specs = [row(D_MODEL), row(D_MODEL), row(N_EXPERTS)]
    return in_specs, out_specs


def _route_shapes(n):
    return [jax.ShapeDtypeStruct((n, D_MODEL), F32), jax.ShapeDtypeStruct((n, D_MODEL), BF16),
            jax.ShapeDtypeStruct((n, N_EXPERTS), F32)]


def _out_ab_kernel(at_ref, hf_ref, hb_ref, bo_ref, x_ref, gate_ref, ga_ref, gm_ref, w_ref,
                   g2_ref, sh2_ref, sc2_ref, wr_ref, o_ref, h_ref, aff_ref, *, out_scale):
    parts = _row_parts(x_ref.shape[0])
    cats = []
    for rs in parts:
        oa = [p * out_scale for p in _head_rms(at_ref[rs, :], ga_ref[...], H_A, DV_A)]
        hm = _head_rms(hf_ref[rs, :] + hb_ref[rs, :], gm_ref[...], H_B, DH_B)
        bo = bo_ref[rs, :]
        ob = [_sigmoid(bo[:, h * DH_B:(h + 1) * DH_B]) * hm[h] for h in range(H_B)]
        cats.append(jnp.concatenate(oa + ob, axis=1).astype(BF16))
    _project_and_route(cats, parts, x_ref, gate_ref, w_ref, g2_ref, sh2_ref, sc2_ref, wr_ref, o_ref, h_ref, aff_ref)


def _out_ab(attn, hf, hb, bo, x2, gate, g_attn, g_mlstm, w_out, route, t, per_batch, out_scale):
    n = x2.shape[0]
    tm = _row_tile(t)
    row = lambda c: pl.BlockSpec((tm, c), lambda i: (i, 0))
    full = lambda a: pl.BlockSpec(a.shape, lambda i: (0,) * a.ndim)
    r_in, r_out = _route_specs(route, tm, t, per_batch)
    return pl.pallas_call(
        functools.partial(_out_ab_kernel, out_scale=out_scale),
        grid=(n // tm,),
        in_specs=[row(W_A), row(W_B), row(W_B), row(W_B), row(D_MODEL), _mod_spec(per_batch, tm, t),
                  full(g_attn), full(g_mlstm), full(w_out)] + r_in,
        out_specs=r_out,
        out_shape=_route_shapes(n),
        compiler_params=_cp("parallel"),
        name="out_proj_even",
    )(attn, hf, hb, bo, x2, gate, g_attn, g_mlstm, w_out, *route)


def _in_c_kernel(x_ref, g_ref, sh_ref, sc_ref, w_ref, bf_ref, gam_ref, q_ref, v_ref, og_ref, lf_ref, kk_ref, *, layer):
    h = _rms_rows(x_ref[...], g_ref[...]) * (1.0 + sc_ref[0]) + sh_ref[0]
    y = _dot(h.astype(BF16), w_ref[...])
    q_ref[...] = y[:, 0:W_C]
    v_ref[...] = y[:, W_C:2 * W_C]
    og_ref[...] = y[:, 2 * W_C:3 * W_C]
    gam = gam_ref[...]
    e = jnp.exp(gam - jnp.max(gam, axis=0, keepdims=True))
    p = e / jnp.sum(e, axis=0, keepdims=True)
    lbs = p[0:1]
    lbs0 = lbs
    for i in range(1, layer + 1):
        lbs = lbs + p[i:i + 1]
    lb = jnp.maximum(lbs - lbs0, 0.0)
    L = SCAN_CHUNK
    for d in range(2):
        f = y[:, (3 + d) * W_C:(4 + d) * W_C] + bf_ref[d:d + 1]
        u = jnp.exp(-jnp.abs(f))
        neg = f < 0.0
        lf2 = (jnp.log(jnp.where(neg, lb + u, 1.0 + lb * u)) - jnp.log(1.0 + u)) * LOG2E
        kk_ref[:, d * W_C:(d + 1) * W_C] = (1.0 - lb) * jnp.where(neg, 1.0, u) / (1.0 + u)
        tri = _tri(L, d == 0).astype(BF16)
        for c0 in range(0, lf2.shape[0], L):
            lf_ref[c0:c0 + L, d * W_C:(d + 1) * W_C] = _dot_sel_left2(tri, lf2[c0:c0 + L])


def _in_c(x2, g, shift, scale, w_in, b_f, gamma_lb, layer, t, per_batch):
    n = x2.shape[0]
    tm = _row_tile(t)
    row = lambda c: pl.BlockSpec((tm, c), lambda i: (i, 0))
    full = lambda a: pl.BlockSpec(a.shape, lambda i: (0,) * a.ndim)
    widths = (W_C, W_C, W_C, 2 * W_C, 2 * W_C)
    return pl.pallas_call(
        functools.partial(_in_c_kernel, layer=layer),
        grid=(n // tm,),
        in_specs=[row(D_MODEL), full(g), _mod_spec(per_batch, tm, t), _mod_spec(per_batch, tm, t),
                  full(w_in), full(b_f), full(gamma_lb)],
        out_specs=[row(c) for c in widths],
        out_shape=[jax.ShapeDtypeStruct((n, c), F32) for c in widths],
        compiler_params=_cp("parallel"),
        name="in_proj_odd",
    )(x2, g, shift, scale, w_in, b_f, gamma_lb)


HGRN_LEVELS = (8, 16, 32, 64)
HGRN_DIAG = 8
HGRN_HEADS_PER_STEP = 8


def _hgrn_unit(q, kk, v, b, st_ref, b_s, rev, store_o):
    L = SCAN_CHUNK
    b_s[...] = b
    row = _iota((L, 1), 0)
    col = _iota((1, L), 1)
    k16 = kk.astype(BF16)
    v16 = v.astype(BF16)
    st = st_ref[...]
    o = _dot_nt((q * jnp.exp2(b)).astype(BF16), st.astype(BF16))
    yield
    c = HGRN_DIAG
    sub = _iota((c, 1), 0)
    ys = []
    for blk in range(L // c):
        r0 = blk * c
        qb = q[r0:r0 + c]
        bb = b[r0:r0 + c]
        for s in range(c):
            keep = (sub <= s) if rev else (sub >= s)
            ys.append(qb * jnp.exp2(jnp.where(keep, bb - b_s[r0 + s:r0 + s + 1, :], NEG_BIG)))
    rs = _dot_nt(jnp.concatenate(ys, axis=0).astype(BF16), k16)
    yield
    parts = []
    for m in HGRN_LEVELS:
        par = 2 * m
        pieces = []
        for p0 in range(0, L, par):
            ref_row = p0 + m if rev else p0 + m - 1
            pieces.append(jnp.broadcast_to(b_s[ref_row:ref_row + 1, :], (par, LANES)))
        bref = jnp.concatenate(pieces, axis=0)
        in_first = (row % par) < m
        reader = in_first if rev else jnp.logical_not(in_first)
        z = (jnp.where(reader, q, kk) * jnp.exp2(-jnp.abs(b - bref))).astype(BF16)
        keep = jnp.logical_and((row // par) == (col // par),
                               ((col % par) >= m) if rev else ((col % par) < m))
        parts.append((jnp.logical_and(keep, reader), _dot_nt(z, z)))
        yield
    lane = _iota((c, LANES), 1)
    rows = []
    for blk in range(L // c):
        acc = jnp.zeros((c, LANES), F32)
        for s in range(c):
            i = blk * c + s
            acc = jnp.where(lane == i, rs[i * c:(i + 1) * c], acc)
        rows.append(acc)
    att = jnp.concatenate(rows, axis=0)
    for keep, a in parts:
        att = att + jnp.where(keep, a, 0.0)
    o = o + _dot(att.astype(BF16), v16)
    b_end = b_s[0:1, :] if rev else b_s[L - 1:L, :]
    kd = (kk * jnp.exp2(b_end - b)).astype(BF16)
    st_new = st * jnp.exp2(b_end) + _dot_tn(v16, kd)
    yield
    store_o(o)
    st_ref[...] = st_new
    yield


def _hgrn_kernel(*refs, has_init, out_state, hp):
    qf, vf, lff, kkf, qb, vb, lfb, kkb = refs[:8]
    pos = 8
    if has_init:
        s0 = refs[pos]
        pos += 1
    of, ob = refs[pos:pos + 2]
    pos += 2
    if out_state:
        s_out = refs[pos]
        pos += 1
    st_s, b_s = refs[pos:pos + 2]
    step = pl.program_id(2)

    @pl.when(step == 0)
    def _():
        for d in range(2):
            for h in range(hp):
                st_s[d * hp + h] = s0[0, d, h].T if has_init else jnp.zeros((DH_C, DH_C), F32)

    def storer(o_ref, sl):
        def store(o):
            o_ref[0, :, sl] = o
        return store

    units = []
    for d, (q_ref, v_ref, lf_ref, kk_ref, o_ref) in enumerate(((qf, vf, lff, kkf, of), (qb, vb, lfb, kkb, ob))):
        for h in range(hp):
            u = d * hp + h
            sl = slice(h * DH_C, (h + 1) * DH_C)
            units.append(_hgrn_unit(q_ref[0, :, sl], kk_ref[0, :, sl], v_ref[0, :, sl], lf_ref[0, :, sl],
                                    st_s.at[u], b_s.at[u], d == 1, storer(o_ref, sl)))
    live = units
    while live:
        live = [g for g in live if next(g, "done") != "done"]

    if out_state:
        @pl.when(step == pl.num_programs(2) - 1)
        def _():
            for d in range(2):
                for h in range(hp):
                    s_out[0, d, h] = st_s[d * hp + h].T


def _hgrn(q, v, lf, kk, init, out_state):
    b, t, _ = q.shape
    L = SCAN_CHUNK
    nc = t // L
    hp = HGRN_HEADS_PER_STEP
    ng = H_C // hp
    blk = lambda im: pl.BlockSpec((1, L, hp * DH_C), im)
    fwd = lambda off: (lambda bi, h, c: (bi, c, h + off))
    bwd = lambda off: (lambda bi, h, c: (bi, nc - 1 - c, h + off))
    in_specs = [blk(fwd(0)), blk(fwd(0)), blk(fwd(0)), blk(fwd(0)),
                blk(bwd(0)), blk(bwd(0)), blk(bwd(ng)), blk(bwd(ng))]
    args = [q, v, lf, kk, q, v, lf, kk]
    st_spec = pl.BlockSpec((1, 2, hp, DH_C, DH_C), lambda bi, h, c: (bi, 0, h, 0, 0))
    has_init = init is not None
    if has_init:
        in_specs.append(st_spec)
        args.append(init)
    out_specs = [blk(fwd(0)), blk(bwd(0))]
    out_shape = [jax.ShapeDtypeStruct((b, t, W_C), F32)] * 2
    if out_state:
        out_specs.append(st_spec)
        out_shape.append(jax.ShapeDtypeStruct((b, 2, H_C, DH_C, DH_C), F32))
    return pl.pallas_call(
        functools.partial(_hgrn_kernel, has_init=has_init, out_state=out_state, hp=hp),
        grid=(b, ng, nc),
        in_specs=in_specs,
        out_specs=out_specs,
        out_shape=out_shape,
        scratch_shapes=[pltpu.VMEM((2 * hp, DH_C, DH_C), F32), pltpu.VMEM((2 * hp, L, DH_C), F32)],
        compiler_params=_cp("parallel", "parallel", "arbitrary"),
        name="hgrn_scan",
    )(*args)


def _out_c_kernel(of_ref, ob_ref, og_ref, x_ref, gate_ref, gh_ref, w_ref,
                  g2_ref, sh2_ref, sc2_ref, wr_ref, o_ref, h_ref, aff_ref):
    parts = _row_parts(x_ref.shape[0])
    cats = []
    for rs in parts:
        hm = _head_rms(of_ref[rs, :] + ob_ref[rs, :], gh_ref[...], H_C, DH_C)
        og = og_ref[rs, :]
        heads = [hm[h] * _silu(og[:, h * DH_C:(h + 1) * DH_C]) for h in range(H_C)]
        cats.append(jnp.concatenate(heads, axis=1).astype(BF16))
    _project_and_route(cats, parts, x_ref, gate_ref, w_ref, g2_ref, sh2_ref, sc2_ref, wr_ref, o_ref, h_ref, aff_ref)


def _out_c(of, ob, og, x2, gate, g_hgrn, w_out, route, t, per_batch):
    n = x2.shape[0]
    tm = _row_tile(t)
    row = lambda c: pl.BlockSpec((tm, c), lambda i: (i, 0))
    full = lambda a: pl.BlockSpec(a.shape, lambda i: (0,) * a.ndim)
    r_in, r_out = _route_specs(route, tm, t, per_batch)
    return pl.pallas_call(
        _out_c_kernel,
        grid=(n // tm,),
        in_specs=[row(W_C), row(W_C), row(W_C), row(D_MODEL), _mod_spec(per_batch, tm, t),
                  full(g_hgrn), full(w_out)] + r_in,
        out_specs=r_out,
        out_shape=_route_shapes(n),
        compiler_params=_cp("parallel"),
        name="out_proj_odd",
    )(of, ob, og, x2, gate, g_hgrn, w_out, *route)


def _topk_kernel(aff_ref, pos_ref, s0_ref, bits_s, sel_s, *, cap, tb):
    t = aff_ref.shape[1]
    nb = t // tb
    bits_s[...] = pltpu.bitcast(aff_ref[0], I32)

    def count_ge(cand):
        def body(j, acc):
            r = pl.multiple_of(j * tb, tb)
            blk = bits_s[pl.ds(r, tb), :]
            return acc + jnp.sum((blk >= cand).astype(F32), axis=0, keepdims=True)
        return lax.fori_loop(0, nb, body, jnp.zeros((1, LANES), F32))

    def bit_body(i, thr):
        cand = thr | lax.shift_left(jnp.int32(1), 30 - i)
        return jnp.where(count_ge(cand) >= cap, cand, thr)

    thr = lax.fori_loop(0, 31, bit_body, jnp.zeros((1, LANES), I32))

    def count_gt(j, acc):
        r = pl.multiple_of(j * tb, tb)
        return acc + jnp.sum((bits_s[pl.ds(r, tb), :] > thr).astype(F32), axis=0, keepdims=True)

    need = cap - lax.fori_loop(0, nb, count_gt, jnp.zeros((1, LANES), F32))
    strict = (_iota((tb, tb), 0) > _iota((tb, tb), 1)).astype(BF16)

    def select(j, carry):
        r = pl.multiple_of(j * tb, tb)
        blk = bits_s[pl.ds(r, tb), :]
        eq = blk == thr
        before = carry + _dot(strict, eq.astype(BF16))
        sel = jnp.logical_or(blk > thr, jnp.logical_and(eq, before < need))
        sel_s[pl.ds(r, tb), :] = sel.astype(F32)
        return carry + jnp.sum(eq.astype(F32), axis=0, keepdims=True)

    lax.fori_loop(0, nb, select, jnp.zeros((1, LANES), F32))

    def place(j, carry):
        r = pl.multiple_of(j * tb, tb)
        sel = sel_s[pl.ds(r, tb), :]
        slot = carry + _dot(strict, sel.astype(BF16))
        pos_ref[0, pl.ds(r, tb), :] = jnp.where(sel > 0.5, slot, -1.0).astype(I32)
        s0_ref[0, pl.ds(j, 1), :] = carry.astype(I32)
        return carry + jnp.sum(sel, axis=0, keepdims=True)

    lax.fori_loop(0, nb, place, jnp.zeros((1, LANES), F32))


def _topk(aff, cap, tb):
    b, t, _ = aff.shape
    nb = t // tb
    return pl.pallas_call(
        functools.partial(_topk_kernel, cap=cap, tb=tb),
        grid=(b,),
        in_specs=[pl.BlockSpec((1, t, LANES), lambda bi: (bi, 0, 0))],
        out_specs=[pl.BlockSpec((1, t, LANES), lambda bi: (bi, 0, 0)),
                   pl.BlockSpec((1, nb, LANES), lambda bi: (bi, 0, 0))],
        out_shape=[jax.ShapeDtypeStruct((b, t, LANES), I32), jax.ShapeDtypeStruct((b, nb, LANES), I32)],
        scratch_shapes=[pltpu.VMEM((t, LANES), I32), pltpu.VMEM((t, LANES), F32)],
        compiler_params=_cp("parallel"),
        name="moe_topk",
    )(aff)


MOE_SMALL_COUNT = 64
MOE_STACK_ROWS = 1280


def _windows(cap, tb):
    if cap <= tb + BF16_ROWS:
        return (cap,)
    return (MOE_SMALL_COUNT + BF16_ROWS, tb + BF16_ROWS)


def _window_starts(s0_ref, bi, j, cap, w):
    if w == cap:
        return [0] * N_EXPERTS
    out = []
    for e in range(N_EXPERTS):
        st = jnp.minimum((s0_ref[bi, j, e] // BF16_ROWS) * BF16_ROWS, cap - w)
        out.append(pl.multiple_of(st, BF16_ROWS))
    return out


def _per_window(windows, narrow_ref, bi, j, run):
    if len(windows) == 1:
        run(windows[0])
    else:
        narrow = narrow_ref[bi, j] > 0
        pl.when(narrow)(lambda: run(windows[0]))
        pl.when(jnp.logical_not(narrow))(lambda: run(windows[1]))


def _gather_kernel(s0_ref, narrow_ref, h_ref, post_ref, o_ref, *, cap, tb):
    bi = pl.program_id(0)
    j = pl.program_id(1)

    @pl.when(j == 0)
    def _():
        o_ref[...] = jnp.zeros_like(o_ref)

    def run(w):
        starts = _window_starts(s0_ref, bi, j, cap, w)
        post = post_ref[0]
        r = _iota((w, tb), 0)
        group = max(1, min(N_EXPERTS, MOE_STACK_ROWS // w))
        for e0 in range(0, N_EXPERTS, group):
            es = range(e0, min(e0 + group, N_EXPERTS))
            onehot = jnp.concatenate(
                [jnp.where((post[e:e + 1, :] - starts[e]) == r, 1.0, 0.0).astype(BF16) for e in es],
                axis=0)
            rows = _dot(onehot, h_ref[0]).astype(BF16)
            for k, e in enumerate(es):
                cur = o_ref[0, e, pl.ds(starts[e], w), :]
                o_ref[0, e, pl.ds(starts[e], w), :] = cur + rows[k * w:(k + 1) * w]

    _per_window(_windows(cap, tb), narrow_ref, bi, j, run)


def _gather(h2, post, s0, narrow, cap, tb):
    b, t, d = h2.shape
    nb = t // tb
    grid_spec = pltpu.PrefetchScalarGridSpec(
        num_scalar_prefetch=2,
        grid=(b, nb),
        in_specs=[pl.BlockSpec((1, tb, d), lambda bi, j, s, n: (bi, j, 0)),
                  pl.BlockSpec((1, N_EXPERTS, tb), lambda bi, j, s, n: (bi, 0, j))],
        out_specs=pl.BlockSpec((1, N_EXPERTS, cap, d), lambda bi, j, s, n: (bi, 0, 0, 0)),
    )
    return pl.pallas_call(
        functools.partial(_gather_kernel, cap=cap, tb=tb),
        grid_spec=grid_spec,
        out_shape=jax.ShapeDtypeStruct((b, N_EXPERTS, cap, d), BF16),
        compiler_params=_cp("parallel", "arbitrary"),
        name="moe_gather",
    )(s0, narrow, h2, post)


def _expert_kernel(x_ref, wg_ref, wu_ref, wd_ref, y_ref, wg_s, wu_s, wd_s):
    @pl.when(pl.program_id(1) == 0)
    def _():
        wg_s[...] = wg_ref[0].astype(BF16)
        wu_s[...] = wu_ref[0].astype(BF16)
        wd_s[...] = wd_ref[0].astype(BF16)

    shp = x_ref.shape
    x = x_ref[...].reshape(shp[0] * shp[2], shp[3])
    a = _dot(x, wg_s[...])
    u = _dot(x, wu_s[...])
    hm = (_silu(a) * u).astype(BF16)
    y_ref[...] = _dot(hm, wd_s[...]).astype(BF16).reshape(shp)


def _experts(xin, w_gate, w_up, w_down, layer):
    b, e, cap, d = xin.shape
    f = w_gate.shape[-1]
    bb = max(1, min(b, 512 // cap))
    xs = pl.BlockSpec((bb, 1, cap, d), lambda ei, bi: (bi, ei, 0, 0))
    ws = lambda a: pl.BlockSpec((None, 1) + a.shape[2:], lambda ei, bi: (layer, ei, 0, 0))
    return pl.pallas_call(
        _expert_kernel,
        grid=(e, b // bb),
        in_specs=[xs, ws(w_gate), ws(w_up), ws(w_down)],
        out_specs=xs,
        out_shape=jax.ShapeDtypeStruct(xin.shape, BF16),
        scratch_shapes=[pltpu.VMEM((d, f), BF16), pltpu.VMEM((d, f), BF16), pltpu.VMEM((f, d), BF16)],
        compiler_params=_cp("parallel", "arbitrary"),
        name="moe_experts",
    )(xin, w_gate, w_up, w_down)


def _combine_kernel(*refs, cap, tb, final):
    windows = _windows(cap, tb)
    s0_ref, narrow_ref, y_ref, pos_ref, aff_ref, x_ref, gate_ref = refs[:7]
    pos_n = 7
    if final:
        gf_ref = refs[pos_n]
        pos_n += 1
    spread_ref, row_id_ref = refs[pos_n:pos_n + 2]
    o_ref = refs[-1]
    bi = pl.program_id(0)
    j = pl.program_id(1)
    offset = 1024
    lane = _iota((1, N_EXPERTS), 1)

    def run(w):
        starts = _window_starts(s0_ref, bi, j, cap, w)
        if w == windows[0]:
            stv = jnp.zeros((1, N_EXPERTS), I32)
            for e in range(N_EXPERTS):
                stv = jnp.where(lane == e, starts[e], stv)
            rel = pos_ref[0] - stv + offset
            spread = spread_ref[...]
            hi = lax.shift_right_logical(rel, 5).astype(F32).astype(BF16)
            lo = jnp.bitwise_and(rel, 31).astype(F32).astype(BF16)
            relx = 32.0 * _dot(hi, spread) + _dot(lo, spread) - offset
            gx = _dot(aff_ref[0].astype(BF16), spread)
            g = jnp.where(relx == row_id_ref[...], gx, 0.0)
            ys = jnp.concatenate([y_ref[0, e, pl.ds(starts[e], w), :] for e in range(N_EXPERTS)], axis=0)
            acc = _dot(g.astype(BF16), ys)
        else:
            pos = pos_ref[0]
            aff = aff_ref[0].astype(BF16).astype(F32)
            col = _iota((tb, w), 1)
            acc = jnp.zeros((tb, D_MODEL), F32)
            for e in range(N_EXPERTS):
                g = jnp.where((pos[:, e:e + 1] - starts[e]) == col, aff[:, e:e + 1], 0.0)
                acc = acc + _dot(g.astype(BF16), y_ref[0, e, pl.ds(starts[e], w), :])
        out = x_ref[0] + gate_ref[0] * acc
        if final:
            out = _rms_rows(out, gf_ref[...])
        o_ref[0] = out

    _per_window(windows, narrow_ref, bi, j, run)


def _combine(y, pos, aff, x3, gate, s0, narrow, cap, tb, per_batch, g_final):
    b, t, d = x3.shape
    nb = t // tb
    final = g_final is not None
    tok = lambda c: pl.BlockSpec((1, tb, c), lambda bi, j, s, n: (bi, j, 0))
    const = lambda a: pl.BlockSpec(a.shape, lambda bi, j, s, n: (0, 0))
    in_specs = [pl.BlockSpec((1, N_EXPERTS, cap, d), lambda bi, j, s, n: (bi, 0, 0, 0)),
                tok(N_EXPERTS), tok(N_EXPERTS), tok(d),
                pl.BlockSpec((1, 1, d), (lambda bi, j, s, n: (bi, 0, 0)) if per_batch
                             else (lambda bi, j, s, n: (0, 0, 0)))]
    args = [y, pos, aff, x3, gate]
    if final:
        in_specs.append(const(g_final))
        args.append(g_final)
    w = _windows(cap, tb)[0]
    spread = jnp.repeat(jnp.eye(N_EXPERTS, dtype=BF16), w, axis=1)
    row_id = jnp.tile(jnp.arange(w, dtype=F32), N_EXPERTS).reshape(1, -1)
    in_specs += [const(spread), const(row_id)]
    args += [spread, row_id]
    grid_spec = pltpu.PrefetchScalarGridSpec(
        num_scalar_prefetch=2, grid=(b, nb), in_specs=in_specs, out_specs=tok(d))
    return pl.pallas_call(
        functools.partial(_combine_kernel, cap=cap, tb=tb, final=final),
        grid_spec=grid_spec,
        out_shape=jax.ShapeDtypeStruct((b, t, d), F32),
        compiler_params=_cp("parallel", "arbitrary"),
        name="moe_combine",
    )(s0, narrow, *args)


def _moe(x2, h2, aff, b, t, gate, w_gate, w_up, w_down, layer, per_batch, g_final):
    cap = CAPACITY_FACTOR * t // N_EXPERTS
    tb = min(MOE_TOKEN_BLOCK, t)
    aff3 = aff.reshape(b, t, N_EXPERTS)
    per = LANES // N_EXPERTS
    groups = -(-b // per)
    nb = t // tb

    def pack(a):
        a = jnp.pad(a, ((0, groups * per - b), (0, 0), (0, 0)))
        return a.reshape(groups, per, a.shape[1], N_EXPERTS).transpose(0, 2, 1, 3).reshape(groups, a.shape[1], LANES)

    def unpack(a):
        return a.reshape(groups, a.shape[1], per, N_EXPERTS).transpose(0, 2, 1, 3).reshape(
            groups * per, a.shape[1], N_EXPERTS)[:b]

    pos, s0 = _topk(pack(aff3), cap, tb)
    pos, s0 = unpack(pos), unpack(s0)
    counts = jnp.diff(s0, axis=1, append=jnp.full((b, 1, N_EXPERTS), cap, I32))
    narrow = jnp.all(counts <= MOE_SMALL_COUNT, axis=-1).astype(I32)
    xin = _gather(h2.reshape(b, t, D_MODEL), pos.transpose(0, 2, 1), s0, narrow, cap, tb)
    y = _experts(xin, w_gate, w_up, w_down, layer)
    out = _combine(y, pos, aff3, x2.reshape(b, t, D_MODEL), gate, s0, narrow, cap, tb, per_batch, g_final)
    return out.reshape(b * t, D_MODEL)


def _rope_tables(t):
    rows = t // GRID_W
    row = jnp.repeat(jnp.arange(rows, dtype=F32), GRID_W)
    col = jnp.tile(jnp.arange(GRID_W, dtype=F32), rows)
    n_freq = DQK_A // 4
    inv_freq = jnp.power(ROPE_BASE, -jnp.arange(n_freq, dtype=F32) / n_freq)
    ang = jnp.concatenate([row[:, None] * inv_freq, col[:, None] * inv_freq], axis=-1)
    cos, sin = jnp.cos(ang), jnp.sin(ang)
    zero = jnp.zeros_like(sin)
    reps = LANES // DQK_A
    cos_t = jnp.tile(jnp.concatenate([cos, cos], axis=-1), (1, reps))
    sin_a = jnp.tile(jnp.concatenate([-sin, zero], axis=-1), (1, reps))
    sin_b = jnp.tile(jnp.concatenate([zero, sin], axis=-1), (1, reps))
    return cos_t, sin_a, sin_b


def _stream(x, mods, per_batch, rope, cache, p):
    b, t, d = x.shape
    x2 = x.reshape(b * t, d)
    depth = p["g_norm"].shape[0]
    even_out, odd_out = [], []
    for l in range(depth):
        j = l // 2
        sh1, sc1, gt1, sh2, sc2, gt2 = mods[l]
        g1 = p["g_norm"][l, 0].reshape(1, d)
        g2 = p["g_norm"][l, 1].reshape(1, d)
        route = (g2, sh2, sc2, p["w_router_pad"][l])
        if l % 2 == 0:
            aq, ak, av, bqk, bv, bo, gates = _in_ab(x2, g1, sh1, sc1, p["w_in_ab_main"][j], p["w_in_ab_gate"][j],
                                                     rope, t, per_batch, F32 if cache is None else BF16)
            r3 = lambda a: a.reshape(b, t, a.shape[-1])
            if cache is not None:
                kc = cache[0][:, j].reshape(b, -1, W_A)
                vc = cache[1][:, j].reshape(b, -1, W_A)
                nu = 2 * H_B
                init = (cache[2][:, j].reshape(b, nu, DH_B, DH_B),
                        cache[3][:, j].reshape(b, nu, 1, DH_B),
                        jnp.broadcast_to(cache[4][:, j].reshape(b, nu, 1, 1), (b, nu, 1, LANES)))
            else:
                kc = vc = init = None
            lam_init = 0.8 - 0.6 * math.exp(-0.3 * l)
            attn = _attention_pipe(r3(aq), r3(ak), r3(av), kc, vc, p["lam_ab"][j], lam_init)
            qk = _conv_silu(r3(bqk), p["conv_ab"][j])
            bias = jnp.pad(jnp.concatenate([p["b_ig_ab"][j].reshape(-1), p["b_fg_ab"][j].reshape(-1)]),
                           (0, LANES - 4 * H_B)).reshape(1, LANES)
            res = _mlstm(qk, r3(bv), r3(gates), bias, init, cache is None)
            hf, hb = res[0], res[1]
            if cache is None:
                even_out.append((ak, av, res[2], res[3], res[4]))
            x2, h2, aff = _out_ab(attn.reshape(b * t, W_A), hf.reshape(b * t, W_B), hb.reshape(b * t, W_B), bo, x2,
                                  gt1, p["g_attn_ab"][j].reshape(1, W_A), p["g_mlstm_ab"][j].reshape(1, W_B),
                                  p["w_out_ab_bf16"][j], route, t, per_batch, 1.0 - lam_init)
        else:
            q, v, og, lf, kk = _in_c(x2, g1, sh1, sc1, p["w_in_c_bf16"][j], p["b_f_c"][j], p["gamma_lb"], l, t,
                                     per_batch)
            r3 = lambda a: a.reshape(b, t, a.shape[-1])
            init = None if cache is None else cache[5][:, j]
            res = _hgrn(r3(q), r3(v), r3(lf), r3(kk), init, cache is None)
            if cache is None:
                odd_out.append(res[2])
            x2, h2, aff = _out_c(res[0].reshape(b * t, W_C), res[1].reshape(b * t, W_C), og, x2, gt1,
                                 p["g_hgrn_c"][j].reshape(1, W_C), p["w_out_c_bf16"][j], route, t, per_batch)
        g_final = p["g_final"].reshape(1, d) if l == depth - 1 else None
        x2 = _moe(x2, h2, aff, b, t, gt2, p["w_gate_e"], p["w_up_e"], p["w_down_e"], l, per_batch, g_final)
    return x2.reshape(b, t, d), even_out, odd_out


def kernel(x_prompt, x_sample, cache_dattn_k, cache_dattn_v, state_mlstm_C, state_mlstm_n, state_mlstm_m,
           state_hgrn_S, c, c_ctx, w_ada, b_ada, g_norm, g_final, w_in_ab, b_ig_ab, b_fg_ab, lam_ab, conv_ab,
           g_attn_ab, g_mlstm_ab, w_out_ab, w_in_c, b_f_c, gamma_lb, g_hgrn_c, w_out_c,
           w_router, w_gate_e, w_up_e, w_down_e):
    d = D_MODEL
    bs = c.shape[0]
    bp, tp, _ = x_prompt.shape
    depth = w_ada.shape[0]
    n_main = 3 * W_A + 4 * W_B
    p = dict(
        g_norm=g_norm, g_final=g_final, b_ig_ab=b_ig_ab, b_fg_ab=b_fg_ab, lam_ab=lam_ab, conv_ab=conv_ab,
        g_attn_ab=g_attn_ab, g_mlstm_ab=g_mlstm_ab, b_f_c=b_f_c, gamma_lb=gamma_lb, g_hgrn_c=g_hgrn_c,
        w_gate_e=w_gate_e, w_up_e=w_up_e, w_down_e=w_down_e,
        w_in_ab_main=w_in_ab[:, :, :n_main].astype(BF16),
        w_in_ab_gate=jnp.pad(w_in_ab[:, :, n_main:], ((0, 0), (0, 0), (0, LANES - 4 * H_B))).astype(BF16),
        w_out_ab_bf16=w_out_ab.astype(BF16),
        w_in_c_bf16=w_in_c.astype(BF16),
        w_out_c_bf16=w_out_c.astype(BF16),
        w_router_pad=jnp.pad(w_router, ((0, 0), (0, 0), (0, LANES - N_EXPERTS))),
    )
    rows = -(-(bs + 1) // 8) * 8
    cond = jnp.concatenate([c, c_ctx.reshape(1, d), jnp.zeros((rows - bs - 1, d), F32)], axis=0)
    mods = _ada(cond, w_ada, b_ada)
    mods_s = [[mods[l, :bs, i * d:(i + 1) * d].reshape(bs, 1, d) for i in range(6)] for l in range(depth)]
    mods_c = [[mods[l, bs:bs + 1, i * d:(i + 1) * d].reshape(1, 1, d) for i in range(6)] for l in range(depth)]

    y_prompt, ev, od = _stream(x_prompt, mods_c, False, None, None, p)
    new_k = jnp.stack([e[0].reshape(bp, tp, H_A, 2 * DQK_A) for e in ev], axis=1)
    new_v = jnp.stack([e[1].reshape(bp, tp, H_A, DV_A) for e in ev], axis=1)
    new_c = jnp.stack([e[2].reshape(bp, 2, H_B, DH_B, DH_B) for e in ev], axis=1)
    new_n = jnp.stack([e[3].reshape(bp, 2, H_B, DH_B) for e in ev], axis=1)
    new_m = jnp.stack([e[4][..., 0].reshape(bp, 2, H_B) for e in ev], axis=1)
    new_s = jnp.stack(od, axis=1)

    cache = (cache_dattn_k, cache_dattn_v, state_mlstm_C, state_mlstm_n, state_mlstm_m, state_hgrn_S)
    y_sample, _, _ = _stream(x_sample, mods_s, True, _rope_tables(x_sample.shape[1]), cache, p)
    return (y_prompt, y_sample, new_k, new_v, new_c, new_n, new_m, new_s)
```

```python
import functools
import math

import jax
import jax.numpy as jnp
from jax import lax
from jax.experimental import pallas as pl
from jax.experimental.pallas import tpu as pltpu

F32 = jnp.float32
BF16 = jnp.bfloat16
I32 = jnp.int32

D_MODEL = 1024
H_A = 4
DV_A = 128
DQK_A = 64
W_A = H_A * DV_A
H_B = 4
DH_B = 128
W_B = H_B * DH_B
H_C = 8
DH_C = 128
W_C = H_C * DH_C
N_EXPERTS = 16
CAPACITY_FACTOR = 2
GRID_W = 64
ROPE_BASE = 10000.0
RMS_EPS = 1e-6
NEG_BIG = -1e30
LOG2E = 1.4426950408889634
LANES = 128
BF16_ROWS = 16
SCAN_CHUNK = 128
MOE_TOKEN_BLOCK = 256
VMEM_LIMIT = 56 * 1024 * 1024


def _cp(*sem):
    return pltpu.CompilerParams(dimension_semantics=sem, vmem_limit_bytes=VMEM_LIMIT)


def _dot(a, b):
    return jnp.dot(a, b, preferred_element_type=F32)


def _dot_nt(a, b):
    return lax.dot_general(a, b, (((1,), (1,)), ((), ())), preferred_element_type=F32)


def _dot_tn(a, b):
    return lax.dot_general(a, b, (((0,), (0,)), ((), ())), preferred_element_type=F32)


def _split3(x):
    hi = x.astype(BF16)
    r = x - hi.astype(F32)
    mid = r.astype(BF16)
    lo = (r - mid.astype(F32)).astype(BF16)
    return hi, mid, lo


def _dot_sel_left(sel, x):
    hi, mid, lo = _split3(x)
    return _dot(sel, hi) + _dot(sel, mid) + _dot(sel, lo)


def _dot_sel_right(x, sel):
    hi, mid, lo = _split3(x)
    return _dot(hi, sel) + _dot(mid, sel) + _dot(lo, sel)


def _dot_f32(a, b):
    ah, am, al = _split3(a)
    bh, bm, bl = _split3(b)
    return (_dot(ah, bh) + (_dot(ah, bm) + _dot(am, bh))
            + (_dot(ah, bl) + _dot(am, bm) + _dot(al, bh)))


def _dot_f32x3(a, b):
    ah = a.astype(BF16)
    al = (a - ah.astype(F32)).astype(BF16)
    bh = b.astype(BF16)
    bl = (b - bh.astype(F32)).astype(BF16)
    return _dot(ah, bh) + (_dot(ah, bl) + _dot(al, bh))


def _dot_sel_left2(sel, x):
    hi = x.astype(BF16)
    lo = (x - hi.astype(F32)).astype(BF16)
    return _dot(sel, hi) + _dot(sel, lo)


def _sigmoid(x):
    return 1.0 / (1.0 + jnp.exp(-x))


def _silu(x):
    return x * _sigmoid(x)


def _log_sigmoid(x):
    return jnp.minimum(x, 0.0) - jnp.log(1.0 + jnp.exp(-jnp.abs(x)))


def _rms_rows(x, g):
    return x * lax.rsqrt(jnp.mean(x * x, axis=-1, keepdims=True) + RMS_EPS) * g


def _iota(shape, dim):
    return lax.broadcasted_iota(I32, shape, dim)


def _ada_kernel(c_ref, w_ref, b_ref, o_ref):
    c = c_ref[...]
    o_ref[0] = _dot_f32(_silu(c), w_ref[0]) + b_ref[0]


def _ada(cond_rows, w_ada, b_ada):
    depth, d, n6 = w_ada.shape
    rows = cond_rows.shape[0]
    tn = 1536
    return pl.pallas_call(
        _ada_kernel,
        grid=(depth, n6 // tn),
        in_specs=[pl.BlockSpec((rows, d), lambda l, j: (0, 0)),
                  pl.BlockSpec((1, d, tn), lambda l, j: (l, 0, j)),
                  pl.BlockSpec((1, 1, tn), lambda l, j: (l, 0, j))],
        out_specs=pl.BlockSpec((1, rows, tn), lambda l, j: (l, 0, j)),
        out_shape=jax.ShapeDtypeStruct((depth, rows, n6), F32),
        compiler_params=_cp("parallel", "parallel"),
        name="adaln",
    )(cond_rows, w_ada, b_ada.reshape(depth, 1, n6))


def _mod_spec(per_batch, tm, t):
    if per_batch:
        return pl.BlockSpec((1, 1, D_MODEL), lambda i: (i * tm // t, 0, 0))
    return pl.BlockSpec((1, 1, D_MODEL), lambda i: (0, 0, 0))


def _row_tile(t):
    return min(512, t)


def _in_ab_kernel(*refs, use_rope):
    if use_rope:
        (x_ref, g_ref, sh_ref, sc_ref, w_ref, wg_ref, cos_ref, sa_ref, sb_ref,
         aq_ref, ak_ref, av_ref, bqk_ref, bv_ref, bo_ref, gt_ref) = refs
    else:
        (x_ref, g_ref, sh_ref, sc_ref, w_ref, wg_ref,
         aq_ref, ak_ref, av_ref, bqk_ref, bv_ref, bo_ref, gt_ref) = refs
    h = _rms_rows(x_ref[...], g_ref[...]) * (1.0 + sc_ref[0]) + sh_ref[0]
    hb = h.astype(BF16)
    y = _dot(hb, w_ref[...])
    gt_ref[...] = _dot(hb, wg_ref[...])
    aq = y[:, 0:W_A] * (DQK_A ** -0.5 * LOG2E)
    ak = y[:, W_A:2 * W_A]
    if use_rope:
        cos = jnp.concatenate([cos_ref[...]] * H_A, axis=1)
        sa = jnp.concatenate([sa_ref[...]] * H_A, axis=1)
        sb = jnp.concatenate([sb_ref[...]] * H_A, axis=1)
        half = DQK_A // 2

        def rope(v):
            return (v * cos + pltpu.roll(v, W_A - half, axis=1) * sa
                    + pltpu.roll(v, half, axis=1) * sb)

        aq = rope(aq)
        ak = rope(ak)
    aq_ref[...] = aq.astype(aq_ref.dtype)
    ak_ref[...] = ak.astype(ak_ref.dtype)
    av_ref[...] = y[:, 2 * W_A:3 * W_A].astype(av_ref.dtype)
    bqk_ref[...] = y[:, 3 * W_A:3 * W_A + 2 * W_B]
    bv_ref[...] = y[:, 3 * W_A + 2 * W_B:3 * W_A + 3 * W_B].astype(BF16)
    bo_ref[...] = y[:, 3 * W_A + 3 * W_B:3 * W_A + 4 * W_B].astype(BF16)


def _in_ab(x2, g, shift, scale, w_main, w_gate, rope, t, per_batch, qkv_dtype):
    n = x2.shape[0]
    tm = _row_tile(t)
    use_rope = rope is not None
    row = lambda c: pl.BlockSpec((tm, c), lambda i: (i, 0))
    full = lambda a: pl.BlockSpec(a.shape, lambda i: (0,) * a.ndim)
    in_specs = [row(D_MODEL), full(g), _mod_spec(per_batch, tm, t), _mod_spec(per_batch, tm, t),
                full(w_main), full(w_gate)]
    args = [x2, g, shift, scale, w_main, w_gate]
    if use_rope:
        nt = t // tm
        tab = pl.BlockSpec((tm, LANES), lambda i: (i % nt, 0))
        in_specs += [tab, tab, tab]
        args += list(rope)
    widths = (W_A, W_A, W_A, 2 * W_B, W_B, W_B, LANES)
    return pl.pallas_call(
        functools.partial(_in_ab_kernel, use_rope=use_rope),
        grid=(n // tm,),
        in_specs=in_specs,
        out_specs=[row(c) for c in widths],
        out_shape=[jax.ShapeDtypeStruct((n, c), dt) for c, dt in
                   zip(widths, (qkv_dtype, qkv_dtype, qkv_dtype, F32, BF16, BF16, F32))],
        compiler_params=_cp("parallel"),
        name="in_proj_even",
    )(*args)


def _attn_pipe_kernel(*refs, has_cache, tk, lam_init, nq):
    if has_cache:
        q_ref, k_ref, v_ref, kc_ref, vc_ref, lam_ref, o_ref, s_s, mp_s, vt_s, sc_s, vct_s = refs
    else:
        q_ref, k_ref, v_ref, lam_ref, o_ref, s_s, mp_s, vt_s = refs
    i = pl.program_id(2)
    q = q_ref[0]
    tq = q.shape[0]
    first = _iota((1, LANES), 1) < DQK_A
    qs = (jnp.where(first, q, 0).astype(BF16), jnp.where(first, 0, q).astype(BF16))
    nkb = k_ref.shape[1] // tk

    @pl.when(i == 0)
    def _():
        for j in range(nkb):
            vt_s[j, 0:DV_A, :] = v_ref[0, j * tk:(j + 1) * tk, :].astype(F32).T.astype(BF16)
            vt_s[j, DV_A:, :] = jnp.ones((BF16_ROWS, tk), BF16)
        if has_cache:
            vct_s[0:DV_A, :] = vc_ref[0].astype(F32).T.astype(BF16)
            vct_s[DV_A:, :] = jnp.ones((BF16_ROWS, vct_s.shape[1]), BF16)

    def groups(x):
        return [x[:, g * LANES:(g + 1) * LANES] for g in range(x.shape[1] // LANES)]

    def body(score, finish, cur):
        prev = 1 - cur
        if finish:
            m = [jnp.max(mp_s[prev, a], axis=-1, keepdims=True) for a in range(2)]
        mpart = [jnp.full((tq, LANES), NEG_BIG, F32)] * 2
        acc = [jnp.zeros((DV_A + BF16_ROWS, tq), F32)] * 2

        def pass1(kb, store):
            for a in range(2):
                s = _dot_nt(qs[a], kb)
                store(a, s)
                mpart[a] = functools.reduce(jnp.maximum, groups(s), mpart[a])

        def pass2(a, s, vt):
            acc[a] = acc[a] + _dot_nt(vt, jnp.exp2(s - m[a]).astype(BF16))

        if has_cache:
            def store_c(a, s):
                sc_s[cur, a] = s
            if score:
                pass1(kc_ref[0].astype(BF16), store_c)
            if finish:
                for a in range(2):
                    pass2(a, sc_s[prev, a], vct_s[...])
        for j in range(nkb):
            def store_j(a, s, j=j):
                s_s[cur, a, j] = s
            if score:
                pass1(k_ref[0, j * tk:(j + 1) * tk, :].astype(BF16), store_j)
            if finish:
                for a in range(2):
                    pass2(a, s_s[prev, a, j], vt_s[j])
        if score:
            for a in range(2):
                mp_s[cur, a] = mpart[a]
        if finish:
            lp = lam_ref[...]
            lam = (jnp.exp(jnp.sum(lp[0:1] * lp[1:2], axis=-1, keepdims=True))
                   - jnp.exp(jnp.sum(lp[2:3] * lp[3:4], axis=-1, keepdims=True)) + lam_init)
            o1 = acc[0][0:DV_A] / acc[0][DV_A:DV_A + 1]
            o2 = acc[1][0:DV_A] / acc[1][DV_A:DV_A + 1]
            o_ref[0] = (o1 - lam * o2).T.astype(BF16)

    pl.when(i == 0)(lambda: body(True, False, 0))
    inner = jnp.logical_and(i > 0, i < nq)
    pl.when(jnp.logical_and(inner, i % 2 == 0))(lambda: body(True, True, 0))
    pl.when(jnp.logical_and(inner, i % 2 == 1))(lambda: body(True, True, 1))
    pl.when(i == nq)(lambda: body(False, True, nq % 2))


def _attention_pipe(aq, ak, av, kc, vc, lam_p, lam_init):
    b, t, _ = aq.shape
    tq = min(256, t)
    tk = min(512, t)
    nq = t // tq
    has_cache = kc is not None
    qspec = pl.BlockSpec((1, tq, DV_A), lambda bi, h, i: (bi, jnp.minimum(i, nq - 1), h))
    ospec = pl.BlockSpec((1, tq, DV_A), lambda bi, h, i: (bi, jnp.maximum(i - 1, 0), h))
    ks = pl.BlockSpec((1, t, DV_A), lambda bi, h, i: (bi, 0, h))
    in_specs = [qspec, ks, ks]
    args = [aq, ak, av]
    if has_cache:
        cs = pl.BlockSpec((1, kc.shape[1], DV_A), lambda bi, h, i: (bi, 0, h))
        in_specs += [cs, cs]
        args += [kc, vc]
    in_specs.append(pl.BlockSpec(lam_p.shape, lambda bi, h, i: (0, 0)))
    args.append(lam_p)
    scratch = [pltpu.VMEM((2, 2, t // tk, tq, tk), F32), pltpu.VMEM((2, 2, tq, LANES), F32),
               pltpu.VMEM((t // tk, DV_A + BF16_ROWS, tk), BF16)]
    if has_cache:
        scratch += [pltpu.VMEM((2, 2, tq, kc.shape[1]), F32),
                    pltpu.VMEM((DV_A + BF16_ROWS, kc.shape[1]), BF16)]
    return pl.pallas_call(
        functools.partial(_attn_pipe_kernel, has_cache=has_cache, tk=tk, lam_init=lam_init, nq=nq),
        grid=(b, H_A, nq + 1),
        in_specs=in_specs,
        out_specs=ospec,
        out_shape=jax.ShapeDtypeStruct((b, t, W_A), BF16),
        scratch_shapes=scratch,
        compiler_params=_cp("parallel", "parallel", "arbitrary"),
        name="diff_attention",
    )(*args)


def _conv_kernel(x_ref, w_ref, o_ref):
    x = x_ref[0]
    t = x.shape[0]
    w = w_ref[...]
    r = _iota((t, 1), 0)
    prev = jnp.where(r == 0, 0.0, pltpu.roll(x, 1, axis=0))
    nxt = jnp.where(r == t - 1, 0.0, pltpu.roll(x, t - 1, axis=0))
    scale = jnp.where(pl.program_id(1) * x.shape[1] >= W_B, DH_B ** -0.5, 1.0)
    o_ref[0] = (_silu(prev * w[0:1] + x * w[1:2] + nxt * w[2:3]) * scale).astype(BF16)


def _conv_silu(bqk, conv_w):
    b, t, c = bqk.shape
    tc = 256
    spec = pl.BlockSpec((1, t, tc), lambda bi, j: (bi, 0, j))
    return pl.pallas_call(
        _conv_kernel,
        grid=(b, c // tc),
        in_specs=[spec, pl.BlockSpec((conv_w.shape[0], tc), lambda bi, j: (0, j))],
        out_specs=spec,
        out_shape=jax.ShapeDtypeStruct(bqk.shape, BF16),
        compiler_params=_cp("parallel", "parallel"),
        name="mlstm_conv",
    )(bqk, conv_w)


def _tri(n, lower):
    r = _iota((n, n), 0)
    c = _iota((n, n), 1)
    return r >= c if lower else r <= c


def _mlstm_kernel(*refs, has_init, out_state):
    qf, kf, vf, gf, qb, kb, vb, gb, bias = refs[:9]
    pos = 9
    if has_init:
        c0, n0, m0 = refs[pos:pos + 3]
        pos += 3
    hf, hb = refs[pos:pos + 2]
    pos += 2
    if out_state:
        c_out, n_out, m_out = refs[pos:pos + 3]
        pos += 3
    cn_s, m_s = refs[pos:pos + 2]
    step = pl.program_id(1)
    L = SCAN_CHUNK
    nu = 2 * H_B

    @pl.when(step == 0)
    def _():
        if has_init:
            for u in range(nu):
                cn_s[u, :, 0:DH_B] = c0[0, u]
                cn_s[u, :, DH_B:2 * DH_B] = jnp.broadcast_to(n0[0, u], (DH_B, DH_B)).T
            m_s[...] = m0[0]
        else:
            cn_s[...] = jnp.zeros_like(cn_s)
            m_s[...] = jnp.zeros_like(m_s)

    ones_b = jnp.ones((L, DH_B), BF16)
    sel_rows = _iota((LANES, LANES), 0)
    lower = _tri(L, True)
    upper = _tri(L, False)
    lower_b = lower.astype(BF16)
    upper_b = upper.astype(BF16)
    col = _iota((1, LANES), 1)
    units = []
    for d, (q_ref, k_ref, v_ref, g_ref, h_ref) in enumerate(((qf, kf, vf, gf, hf), (qb, kb, vb, gb, hb))):
        g = g_ref[0] + bias[...]
        g = jnp.where(col < 2 * H_B, g, _log_sigmoid(g))
        gt = g.T
        if d == 0:
            bc_all = _dot_sel_left(lower_b, g)
            br_all = _dot_sel_right(gt, upper_b)
            mask = lower
        else:
            bc_all = _dot_sel_left(upper_b, g)
            br_all = _dot_sel_right(gt, lower_b)
            mask = upper
        for h in range(H_B):
            u = d * H_B + h
            sl = slice(h * DH_B, (h + 1) * DH_B)
            units.append(dict(
                u=u, d=d, sl=sl, h_ref=h_ref, mask=mask, g=g, bc_all=bc_all,
                li_row=gt[u:u + 1, :], b_row=br_all[2 * H_B + u:2 * H_B + u + 1, :],
                q=q_ref[0, :, sl].astype(BF16), k=k_ref[0, :, sl].astype(F32),
                v1=jnp.concatenate([v_ref[0, :, sl].astype(BF16), ones_b], axis=1)))
    for x in units:
        x["li_b"] = _dot_sel_right(x["g"], (sel_rows == x["u"]).astype(BF16))
        x["b_b"] = _dot_sel_right(x["bc_all"], (sel_rows == 2 * H_B + x["u"]).astype(BF16))
        x["qk"] = _dot_nt(x["q"], x["k"].astype(BF16))
        x["cn"] = cn_s[x["u"]]
        x["qc"] = _dot(x["q"], x["cn"].astype(BF16))
    for x in units:
        x["m_prev"] = m_s[x["u"]]
        a_b = x["b_b"] + x["m_prev"]
        dm = jnp.where(x["mask"], x["b_b"] - x["b_row"] + x["li_row"], NEG_BIG)
        x["m_t"] = jnp.maximum(a_b, jnp.max(dm, axis=-1, keepdims=True))
        x["w_inter"] = jnp.exp(a_b - x["m_t"])
        x["s"] = (x["qk"] * jnp.exp(dm - x["m_t"])).astype(BF16)
    for x in units:
        x["sv"] = _dot(x["s"], x["v1"])
        b_end = x["b_b"][L - 1:L, :] if x["d"] == 0 else x["b_b"][0:1, :]
        g_b = b_end - x["b_b"] + x["li_b"]
        x["m_new"] = jnp.maximum(b_end + x["m_prev"], jnp.max(g_b, axis=0, keepdims=True))
        x["w_old"] = jnp.exp(b_end + x["m_prev"] - x["m_new"])
        x["ks"] = (jnp.exp(g_b - x["m_new"]) * x["k"]).astype(BF16)
    for x in units:
        num = x["w_inter"] * x["qc"][:, 0:DH_B] + x["sv"][:, 0:DH_B]
        den = x["w_inter"] * x["qc"][:, DH_B:] + x["sv"][:, DH_B:]
        x["h_ref"][0, :, x["sl"]] = (num / jnp.maximum(jnp.abs(den), jnp.exp(-x["m_t"]))).astype(BF16)
        w2 = jnp.concatenate([x["w_old"], x["w_old"]], axis=1)
        cn_s[x["u"]] = w2 * x["cn"] + _dot_tn(x["ks"], x["v1"])
        m_s[x["u"]] = x["m_new"]

    if out_state:
        @pl.when(step == pl.num_programs(1) - 1)
        def _():
            for u in range(nu):
                c_out[0, u] = cn_s[u, :, 0:DH_B]
                n_out[0, u] = cn_s[u, :, DH_B:2 * DH_B].T[0:1, :]
            m_out[0] = m_s[...]


def _mlstm(qk, v, gates, bias, init, out_state):
    b, t, _ = v.shape
    L = SCAN_CHUNK
    nc = t // L
    nu = 2 * H_B
    fwd = lambda j: (lambda bi, c: (bi, c, j))
    bwd = lambda j: (lambda bi, c: (bi, nc - 1 - c, j))
    blk = lambda w, im: pl.BlockSpec((1, L, w), im)
    in_specs = [blk(W_B, fwd(0)), blk(W_B, fwd(1)), blk(W_B, fwd(0)), blk(LANES, fwd(0)),
                blk(W_B, bwd(0)), blk(W_B, bwd(1)), blk(W_B, bwd(0)), blk(LANES, bwd(0)),
                pl.BlockSpec((1, LANES), lambda bi, c: (0, 0))]
    args = [qk, qk, v, gates, qk, qk, v, gates, bias]
    st_specs = [pl.BlockSpec((1, nu, DH_B, DH_B), lambda bi, c: (bi, 0, 0, 0)),
                pl.BlockSpec((1, nu, 1, DH_B), lambda bi, c: (bi, 0, 0, 0)),
                pl.BlockSpec((1, nu, 1, LANES), lambda bi, c: (bi, 0, 0, 0))]
    st_shapes = [jax.ShapeDtypeStruct((b, nu, DH_B, DH_B), F32),
                 jax.ShapeDtypeStruct((b, nu, 1, DH_B), F32),
                 jax.ShapeDtypeStruct((b, nu, 1, LANES), F32)]
    has_init = init is not None
    if has_init:
        in_specs += st_specs
        args += list(init)
    out_specs = [blk(W_B, fwd(0)), blk(W_B, bwd(0))]
    out_shape = [jax.ShapeDtypeStruct((b, t, W_B), BF16)] * 2
    if out_state:
        out_specs += st_specs
        out_shape += st_shapes
    return pl.pallas_call(
        functools.partial(_mlstm_kernel, has_init=has_init, out_state=out_state),
        grid=(b, nc),
        in_specs=in_specs,
        out_specs=out_specs,
        out_shape=out_shape,
        scratch_shapes=[pltpu.VMEM((nu, DH_B, 2 * DH_B), F32), pltpu.VMEM((nu, 1, LANES), F32)],
        compiler_params=_cp("parallel", "arbitrary"),
        name="mlstm_scan",
    )(*args)


def _head_rms(x, g, heads, width):
    parts = []
    for h in range(heads):
        sl = slice(h * width, (h + 1) * width)
        parts.append(_rms_rows(x[:, sl], g[:, sl]))
    return parts


OUT_PROJ_PARTS = 4


def _row_parts(tm):
    step = tm // OUT_PROJ_PARTS
    return [slice(r, r + step) for r in range(0, tm, step)]


def _project_and_route(cats, parts, x_ref, gate_ref, w_ref, g_ref, sh_ref, sc_ref, wr_ref, o_ref, h_ref, aff_ref):
    w = w_ref[...]
    xs = [x_ref[rs, :] + gate_ref[0] * _dot(cat, w) for cat, rs in zip(cats, parts)]
    hs = []
    for x, rs in zip(xs, parts):
        o_ref[rs, :] = x
        h = _rms_rows(x, g_ref[...]) * (1.0 + sc_ref[0]) + sh_ref[0]
        h_ref[rs, :] = h.astype(BF16)
        hs.append(h)
    wr = wr_ref[...]
    logits = [_dot_f32x3(h, wr) for h in hs]
    valid = _iota((1, LANES), 1) < N_EXPERTS
    for lg, rs in zip(logits, parts):
        lg = jnp.where(valid, lg, NEG_BIG)
        e = jnp.exp(lg - jnp.max(lg, axis=-1, keepdims=True))
        e = jnp.where(valid, e, 0.0)
        aff_ref[rs, :] = (e / jnp.sum(e, axis=-1, keepdims=True))[:, :N_EXPERTS]


def _route_specs(route, tm, t, per_batch):
    g2, sh2, sc2, wr = route
    full = lambda a: pl.BlockSpec(a.shape, lambda i: (0,) * a.ndim)
    row = lambda c: pl.BlockSpec((tm, c), lambda i: (i, 0))
    in_specs = [full(g2), _mod_spec(per_batch, tm, t), _mod_spec(per_batch, tm, t), full(wr)]
    out_specs = [row(D_MODEL), row(D_MODEL), row(N_EXPERTS)]
    return in_specs, out_specs


def _route_shapes(n):
    return [jax.ShapeDtypeStruct((n, D_MODEL), F32), jax.ShapeDtypeStruct((n, D_MODEL), BF16),
            jax.ShapeDtypeStruct((n, N_EXPERTS), F32)]


def _out_ab_kernel(at_ref, hf_ref, hb_ref, bo_ref, x_ref, gate_ref, ga_ref, gm_ref, w_ref,
                   g2_ref, sh2_ref, sc2_ref, wr_ref, o_ref, h_ref, aff_ref, *, out_scale):
    parts = _row_parts(x_ref.shape[0])
    cats = []
    for rs in parts:
        oa = [p * out_scale for p in _head_rms(at_ref[rs, :].astype(F32), ga_ref[...], H_A, DV_A)]
        hm = _head_rms(hf_ref[rs, :].astype(F32) + hb_ref[rs, :].astype(F32), gm_ref[...], H_B, DH_B)
        bo = bo_ref[rs, :].astype(F32)
        ob = [_sigmoid(bo[:, h * DH_B:(h + 1) * DH_B]) * hm[h] for h in range(H_B)]
        cats.append(jnp.concatenate(oa + ob, axis=1).astype(BF16))
    _project_and_route(cats, parts, x_ref, gate_ref, w_ref, g2_ref, sh2_ref, sc2_ref, wr_ref, o_ref, h_ref, aff_ref)


def _out_ab(attn, hf, hb, bo, x2, gate, g_attn, g_mlstm, w_out, route, t, per_batch, out_scale):
    n = x2.shape[0]
    tm = _row_tile(t)
    row = lambda c: pl.BlockSpec((tm, c), lambda i: (i, 0))
    full = lambda a: pl.BlockSpec(a.shape, lambda i: (0,) * a.ndim)
    r_in, r_out = _route_specs(route, tm, t, per_batch)
    return pl.pallas_call(
        functools.partial(_out_ab_kernel, out_scale=out_scale),
        grid=(n // tm,),
        in_specs=[row(W_A), row(W_B), row(W_B), row(W_B), row(D_MODEL), _mod_spec(per_batch, tm, t),
                  full(g_attn), full(g_mlstm), full(w_out)] + r_in,
        out_specs=r_out,
        out_shape=_route_shapes(n),
        compiler_params=_cp("parallel"),
        name="out_proj_even",
    )(attn, hf, hb, bo, x2, gate, g_attn, g_mlstm, w_out, *route)


def _in_c_kernel(x_ref, g_ref, sh_ref, sc_ref, w_ref, bf_ref, gam_ref, q_ref, v_ref, og_ref, lf_ref, kk_ref, *, layer):
    h = _rms_rows(x_ref[...], g_ref[...]) * (1.0 + sc_ref[0]) + sh_ref[0]
    y = _dot(h.astype(BF16), w_ref[...])
    q_ref[...] = y[:, 0:W_C].astype(BF16)
    v_ref[...] = y[:, W_C:2 * W_C].astype(BF16)
    og_ref[...] = y[:, 2 * W_C:3 * W_C].astype(BF16)
    gam = gam_ref[...]
    e = jnp.exp(gam - jnp.max(gam, axis=0, keepdims=True))
    p = e / jnp.sum(e, axis=0, keepdims=True)
    lbs = p[0:1]
    lbs0 = lbs
    for i in range(1, layer + 1):
        lbs = lbs + p[i:i + 1]
    lb = jnp.maximum(lbs - lbs0, 0.0)
    L = SCAN_CHUNK
    for d in range(2):
        f = y[:, (3 + d) * W_C:(4 + d) * W_C] + bf_ref[d:d + 1]
        u = jnp.exp(-jnp.abs(f))
        neg = f < 0.0
        lf2 = (jnp.log(jnp.where(neg, lb + u, 1.0 + lb * u)) - jnp.log(1.0 + u)) * LOG2E
        kk_ref[:, d * W_C:(d + 1) * W_C] = ((1.0 - lb) * jnp.where(neg, 1.0, u) / (1.0 + u)).astype(BF16)
        tri = _tri(L, d == 0).astype(BF16)
        for c0 in range(0, lf2.shape[0], L):
            lf_ref[c0:c0 + L, d * W_C:(d + 1) * W_C] = _dot_sel_left2(tri, lf2[c0:c0 + L])


def _in_c(x2, g, shift, scale, w_in, b_f, gamma_lb, layer, t, per_batch):
    n = x2.shape[0]
    tm = _row_tile(t)
    row = lambda c: pl.BlockSpec((tm, c), lambda i: (i, 0))
    full = lambda a: pl.BlockSpec(a.shape, lambda i: (0,) * a.ndim)
    widths = (W_C, W_C, W_C, 2 * W_C, 2 * W_C)
    return pl.pallas_call(
        functools.partial(_in_c_kernel, layer=layer),
        grid=(n // tm,),
        in_specs=[row(D_MODEL), full(g), _mod_spec(per_batch, tm, t), _mod_spec(per_batch, tm, t),
                  full(w_in), full(b_f), full(gamma_lb)],
        out_specs=[row(c) for c in widths],
        out_shape=[jax.ShapeDtypeStruct((n, c), dt) for c, dt in zip(widths, (BF16, BF16, BF16, F32, BF16))],
        compiler_params=_cp("parallel"),
        name="in_proj_odd",
    )(x2, g, shift, scale, w_in, b_f, gamma_lb)


HGRN_LEVELS = (8, 16, 32, 64)
HGRN_DIAG = 8
HGRN_HEADS_PER_STEP = 8


def _hgrn_unit(q, kk, v, b, st_ref, b_s, rev, store_o):
    L = SCAN_CHUNK
    b_s[...] = b
    row = _iota((L, 1), 0)
    col = _iota((1, L), 1)
    k16 = kk.astype(BF16)
    v16 = v.astype(BF16)
    q = q.astype(F32)
    kk = kk.astype(F32)
    st = st_ref[...]
    o = _dot_nt((q * jnp.exp2(b)).astype(BF16), st.astype(BF16))
    yield
    c = HGRN_DIAG
    sub = _iota((c, 1), 0)
    ys = []
    for blk in range(L // c):
        r0 = blk * c
        qb = q[r0:r0 + c]
        bb = b[r0:r0 + c]
        for s in range(c):
            keep = (sub <= s) if rev else (sub >= s)
            ys.append(qb * jnp.exp2(jnp.where(keep, bb - b_s[r0 + s:r0 + s + 1, :], NEG_BIG)))
    rs = _dot_nt(jnp.concatenate(ys, axis=0).astype(BF16), k16)
    yield
    parts = []
    for m in HGRN_LEVELS:
        par = 2 * m
        pieces = []
        for p0 in range(0, L, par):
            ref_row = p0 + m if rev else p0 + m - 1
            pieces.append(jnp.broadcast_to(b_s[ref_row:ref_row + 1, :], (par, LANES)))
        bref = jnp.concatenate(pieces, axis=0)
        in_first = (row % par) < m
        reader = in_first if rev else jnp.logical_not(in_first)
        z = (jnp.where(reader, q, kk) * jnp.exp2(-jnp.abs(b - bref))).astype(BF16)
        keep = jnp.logical_and((row // par) == (col // par),
                               ((col % par) >= m) if rev else ((col % par) < m))
        parts.append((jnp.logical_and(keep, reader), _dot_nt(z, z)))
        yield
    lane = _iota((c, LANES), 1)
    rows = []
    for blk in range(L // c):
        acc = jnp.zeros((c, LANES), F32)
        for s in range(c):
            i = blk * c + s
            acc = jnp.where(lane == i, rs[i * c:(i + 1) * c], acc)
        rows.append(acc)
    att = jnp.concatenate(rows, axis=0)
    for keep, a in parts:
        att = att + jnp.where(keep, a, 0.0)
    o = o + _dot(att.astype(BF16), v16)
    b_end = b_s[0:1, :] if rev else b_s[L - 1:L, :]
    kd = (kk * jnp.exp2(b_end - b)).astype(BF16)
    st_new = st * jnp.exp2(b_end) + _dot_tn(v16, kd)
    yield
    store_o(o)
    st_ref[...] = st_new
    yield


def _hgrn_kernel(*refs, has_init, out_state, hp):
    qf, vf, lff, kkf, qb, vb, lfb, kkb = refs[:8]
    pos = 8
    if has_init:
        s0 = refs[pos]
        pos += 1
    of, ob = refs[pos:pos + 2]
    pos += 2
    if out_state:
        s_out = refs[pos]
        pos += 1
    st_s, b_s = refs[pos:pos + 2]
    step = pl.program_id(2)

    @pl.when(step == 0)
    def _():
        for d in range(2):
            for h in range(hp):
                st_s[d * hp + h] = s0[0, d, h].T if has_init else jnp.zeros((DH_C, DH_C), F32)

    def storer(o_ref, sl):
        def store(o):
            o_ref[0, :, sl] = o.astype(BF16)
        return store

    units = []
    for d, (q_ref, v_ref, lf_ref, kk_ref, o_ref) in enumerate(((qf, vf, lff, kkf, of), (qb, vb, lfb, kkb, ob))):
        for h in range(hp):
            u = d * hp + h
            sl = slice(h * DH_C, (h + 1) * DH_C)
            units.append(_hgrn_unit(q_ref[0, :, sl], kk_ref[0, :, sl], v_ref[0, :, sl], lf_ref[0, :, sl],
                                    st_s.at[u], b_s.at[u], d == 1, storer(o_ref, sl)))
    live = units
    while live:
        live = [g for g in live if next(g, "done") != "done"]

    if out_state:
        @pl.when(step == pl.num_programs(2) - 1)
        def _():
            for d in range(2):
                for h in range(hp):
                    s_out[0, d, h] = st_s[d * hp + h].T


def _hgrn(q, v, lf, kk, init, out_state):
    b, t, _ = q.shape
    L = SCAN_CHUNK
    nc = t // L
    hp = HGRN_HEADS_PER_STEP
    ng = H_C // hp
    blk = lambda im: pl.BlockSpec((1, L, hp * DH_C), im)
    fwd = lambda off: (lambda bi, h, c: (bi, c, h + off))
    bwd = lambda off: (lambda bi, h, c: (bi, nc - 1 - c, h + off))
    in_specs = [blk(fwd(0)), blk(fwd(0)), blk(fwd(0)), blk(fwd(0)),
                blk(bwd(0)), blk(bwd(0)), blk(bwd(ng)), blk(bwd(ng))]
    args = [q, v, lf, kk, q, v, lf, kk]
    st_spec = pl.BlockSpec((1, 2, hp, DH_C, DH_C), lambda bi, h, c: (bi, 0, h, 0, 0))
    has_init = init is not None
    if has_init:
        in_specs.append(st_spec)
        args.append(init)
    out_specs = [blk(fwd(0)), blk(bwd(0))]
    out_shape = [jax.ShapeDtypeStruct((b, t, W_C), BF16)] * 2
    if out_state:
        out_specs.append(st_spec)
        out_shape.append(jax.ShapeDtypeStruct((b, 2, H_C, DH_C, DH_C), F32))
    return pl.pallas_call(
        functools.partial(_hgrn_kernel, has_init=has_init, out_state=out_state, hp=hp),
        grid=(b, ng, nc),
        in_specs=in_specs,
        out_specs=out_specs,
        out_shape=out_shape,
        scratch_shapes=[pltpu.VMEM((2 * hp, DH_C, DH_C), F32), pltpu.VMEM((2 * hp, L, DH_C), F32)],
        compiler_params=_cp("parallel", "parallel", "arbitrary"),
        name="hgrn_scan",
    )(*args)


def _out_c_kernel(of_ref, ob_ref, og_ref, x_ref, gate_ref, gh_ref, w_ref,
                  g2_ref, sh2_ref, sc2_ref, wr_ref, o_ref, h_ref, aff_ref):
    parts = _row_parts(x_ref.shape[0])
    cats = []
    for rs in parts:
        hm = _head_rms(of_ref[rs, :].astype(F32) + ob_ref[rs, :].astype(F32), gh_ref[...], H_C, DH_C)
        og = og_ref[rs, :].astype(F32)
        heads = [hm[h] * _silu(og[:, h * DH_C:(h + 1) * DH_C]) for h in range(H_C)]
        cats.append(jnp.concatenate(heads, axis=1).astype(BF16))
    _project_and_route(cats, parts, x_ref, gate_ref, w_ref, g2_ref, sh2_ref, sc2_ref, wr_ref, o_ref, h_ref, aff_ref)


def _out_c(of, ob, og, x2, gate, g_hgrn, w_out, route, t, per_batch):
    n = x2.shape[0]
    tm = _row_tile(t)
    row = lambda c: pl.BlockSpec((tm, c), lambda i: (i, 0))
    full = lambda a: pl.BlockSpec(a.shape, lambda i: (0,) * a.ndim)
    r_in, r_out = _route_specs(route, tm, t, per_batch)
    return pl.pallas_call(
        _out_c_kernel,
        grid=(n // tm,),
        in_specs=[row(W_C), row(W_C), row(W_C), row(D_MODEL), _mod_spec(per_batch, tm, t),
                  full(g_hgrn), full(w_out)] + r_in,
        out_specs=r_out,
        out_shape=_route_shapes(n),
        compiler_params=_cp("parallel"),
        name="out_proj_odd",
    )(of, ob, og, x2, gate, g_hgrn, w_out, *route)


def _topk_kernel(aff_ref, pos_ref, s0_ref, bits_s, sel_s, *, cap, tb):
    t = aff_ref.shape[1]
    nb = t // tb
    bits_s[...] = pltpu.bitcast(aff_ref[0], I32)

    def count_ge(cand):
        def body(j, acc):
            r = pl.multiple_of(j * tb, tb)
            blk = bits_s[pl.ds(r, tb), :]
            return acc + jnp.sum((blk >= cand).astype(F32), axis=0, keepdims=True)
        return lax.fori_loop(0, nb, body, jnp.zeros((1, LANES), F32))

    def bit_body(i, thr):
        cand = thr | lax.shift_left(jnp.int32(1), 30 - i)
        return jnp.where(count_ge(cand) >= cap, cand, thr)

    thr = lax.fori_loop(0, 31, bit_body, jnp.zeros((1, LANES), I32))

    def count_gt(j, acc):
        r = pl.multiple_of(j * tb, tb)
        return acc + jnp.sum((bits_s[pl.ds(r, tb), :] > thr).astype(F32), axis=0, keepdims=True)

    need = cap - lax.fori_loop(0, nb, count_gt, jnp.zeros((1, LANES), F32))
    strict = (_iota((tb, tb), 0) > _iota((tb, tb), 1)).astype(BF16)

    def select(j, carry):
        r = pl.multiple_of(j * tb, tb)
        blk = bits_s[pl.ds(r, tb), :]
        eq = blk == thr
        before = carry + _dot(strict, eq.astype(BF16))
        sel = jnp.logical_or(blk > thr, jnp.logical_and(eq, before < need))
        sel_s[pl.ds(r, tb), :] = sel.astype(F32)
        return carry + jnp.sum(eq.astype(F32), axis=0, keepdims=True)

    lax.fori_loop(0, nb, select, jnp.zeros((1, LANES), F32))

    def place(j, carry):
        r = pl.multiple_of(j * tb, tb)
        sel = sel_s[pl.ds(r, tb), :]
        slot = carry + _dot(strict, sel.astype(BF16))
        pos_ref[0, pl.ds(r, tb), :] = jnp.where(sel > 0.5, slot, -1.0).astype(I32)
        s0_ref[0, pl.ds(j, 1), :] = carry.astype(I32)
        return carry + jnp.sum(sel, axis=0, keepdims=True)

    lax.fori_loop(0, nb, place, jnp.zeros((1, LANES), F32))


def _topk(aff, cap, tb):
    b, t, _ = aff.shape
    nb = t // tb
    return pl.pallas_call(
        functools.partial(_topk_kernel, cap=cap, tb=tb),
        grid=(b,),
        in_specs=[pl.BlockSpec((1, t, LANES), lambda bi: (bi, 0, 0))],
        out_specs=[pl.BlockSpec((1, t, LANES), lambda bi: (bi, 0, 0)),
                   pl.BlockSpec((1, nb, LANES), lambda bi: (bi, 0, 0))],
        out_shape=[jax.ShapeDtypeStruct((b, t, LANES), I32), jax.ShapeDtypeStruct((b, nb, LANES), I32)],
        scratch_shapes=[pltpu.VMEM((t, LANES), I32), pltpu.VMEM((t, LANES), F32)],
        compiler_params=_cp("parallel"),
        name="moe_topk",
    )(aff)


MOE_SMALL_COUNT = 64
MOE_STACK_ROWS = 1280


def _windows(cap, tb):
    if cap <= tb + BF16_ROWS:
        return (cap,)
    return (MOE_SMALL_COUNT + BF16_ROWS, tb + BF16_ROWS)


def _window_starts(s0_ref, bi, j, cap, w):
    if w == cap:
        return [0] * N_EXPERTS
    out = []
    for e in range(N_EXPERTS):
        st = jnp.minimum((s0_ref[bi, j, e] // BF16_ROWS) * BF16_ROWS, cap - w)
        out.append(pl.multiple_of(st, BF16_ROWS))
    return out


def _per_window(windows, narrow_ref, bi, j, run):
    if len(windows) == 1:
        run(windows[0])
    else:
        narrow = narrow_ref[bi, j] > 0
        pl.when(narrow)(lambda: run(windows[0]))
        pl.when(jnp.logical_not(narrow))(lambda: run(windows[1]))


def _gather_kernel(s0_ref, narrow_ref, h_ref, post_ref, o_ref, *, cap, tb):
    bi = pl.program_id(0)
    j = pl.program_id(1)

    @pl.when(j == 0)
    def _():
        o_ref[...] = jnp.zeros_like(o_ref)

    def run(w):
        starts = _window_starts(s0_ref, bi, j, cap, w)
        post = post_ref[0]
        r = _iota((w, tb), 0)
        group = max(1, min(N_EXPERTS, MOE_STACK_ROWS // w))
        for e0 in range(0, N_EXPERTS, group):
            es = range(e0, min(e0 + group, N_EXPERTS))
            onehot = jnp.concatenate(
                [jnp.where((post[e:e + 1, :] - starts[e]) == r, 1.0, 0.0).astype(BF16) for e in es],
                axis=0)
            rows = _dot(onehot, h_ref[0]).astype(BF16)
            for k, e in enumerate(es):
                cur = o_ref[0, e, pl.ds(starts[e], w), :]
                o_ref[0, e, pl.ds(starts[e], w), :] = cur + rows[k * w:(k + 1) * w]

    _per_window(_windows(cap, tb), narrow_ref, bi, j, run)


def _gather(h2, post, s0, narrow, cap, tb):
    b, t, d = h2.shape
    nb = t // tb
    grid_spec = pltpu.PrefetchScalarGridSpec(
        num_scalar_prefetch=2,
        grid=(b, nb),
        in_specs=[pl.BlockSpec((1, tb, d), lambda bi, j, s, n: (bi, j, 0)),
                  pl.BlockSpec((1, N_EXPERTS, tb), lambda bi, j, s, n: (bi, 0, j))],
        out_specs=pl.BlockSpec((1, N_EXPERTS, cap, d), lambda bi, j, s, n: (bi, 0, 0, 0)),
    )
    return pl.pallas_call(
        functools.partial(_gather_kernel, cap=cap, tb=tb),
        grid_spec=grid_spec,
        out_shape=jax.ShapeDtypeStruct((b, N_EXPERTS, cap, d), BF16),
        compiler_params=_cp("parallel", "arbitrary"),
        name="moe_gather",
    )(s0, narrow, h2, post)


def _expert_kernel(x_ref, wg_ref, wu_ref, wd_ref, y_ref, wg_s, wu_s, wd_s):
    @pl.when(pl.program_id(1) == 0)
    def _():
        wg_s[...] = wg_ref[0].astype(BF16)
        wu_s[...] = wu_ref[0].astype(BF16)
        wd_s[...] = wd_ref[0].astype(BF16)

    shp = x_ref.shape
    x = x_ref[...].reshape(shp[0] * shp[2], shp[3])
    a = _dot(x, wg_s[...])
    u = _dot(x, wu_s[...])
    hm = (_silu(a) * u).astype(BF16)
    y_ref[...] = _dot(hm, wd_s[...]).astype(BF16).reshape(shp)


def _experts(xin, w_gate, w_up, w_down, layer):
    b, e, cap, d = xin.shape
    f = w_gate.shape[-1]
    bb = max(1, min(b, 512 // cap))
    xs = pl.BlockSpec((bb, 1, cap, d), lambda ei, bi: (bi, ei, 0, 0))
    ws = lambda a: pl.BlockSpec((None, 1) + a.shape[2:], lambda ei, bi: (layer, ei, 0, 0))
    return pl.pallas_call(
        _expert_kernel,
        grid=(e, b // bb),
        in_specs=[xs, ws(w_gate), ws(w_up), ws(w_down)],
        out_specs=xs,
        out_shape=jax.ShapeDtypeStruct(xin.shape, BF16),
        scratch_shapes=[pltpu.VMEM((d, f), BF16), pltpu.VMEM((d, f), BF16), pltpu.VMEM((f, d), BF16)],
        compiler_params=_cp("parallel", "arbitrary"),
        name="moe_experts",
    )(xin, w_gate, w_up, w_down)


def _combine_kernel(*refs, cap, tb, final):
    windows = _windows(cap, tb)
    s0_ref, narrow_ref, y_ref, pos_ref, aff_ref, x_ref, gate_ref = refs[:7]
    pos_n = 7
    if final:
        gf_ref = refs[pos_n]
        pos_n += 1
    spread_ref, row_id_ref = refs[pos_n:pos_n + 2]
    o_ref = refs[-1]
    bi = pl.program_id(0)
    j = pl.program_id(1)
    offset = 1024
    lane = _iota((1, N_EXPERTS), 1)

    def run(w):
        starts = _window_starts(s0_ref, bi, j, cap, w)
        if w == windows[0]:
            stv = jnp.zeros((1, N_EXPERTS), I32)
            for e in range(N_EXPERTS):
                stv = jnp.where(lane == e, starts[e], stv)
            rel = pos_ref[0] - stv + offset
            spread = spread_ref[...]
            hi = lax.shift_right_logical(rel, 5).astype(F32).astype(BF16)
            lo = jnp.bitwise_and(rel, 31).astype(F32).astype(BF16)
            relx = 32.0 * _dot(hi, spread) + _dot(lo, spread) - offset
            gx = _dot(aff_ref[0].astype(BF16), spread)
            g = jnp.where(relx == row_id_ref[...], gx, 0.0)
            ys = jnp.concatenate([y_ref[0, e, pl.ds(starts[e], w), :] for e in range(N_EXPERTS)], axis=0)
            acc = _dot(g.astype(BF16), ys)
        else:
            pos = pos_ref[0]
            aff = aff_ref[0].astype(BF16).astype(F32)
            col = _iota((tb, w), 1)
            acc = jnp.zeros((tb, D_MODEL), F32)
            for e in range(N_EXPERTS):
                g = jnp.where((pos[:, e:e + 1] - starts[e]) == col, aff[:, e:e + 1], 0.0)
                acc = acc + _dot(g.astype(BF16), y_ref[0, e, pl.ds(starts[e], w), :])
        out = x_ref[0] + gate_ref[0] * acc
        if final:
            out = _rms_rows(out, gf_ref[...])
        o_ref[0] = out

    _per_window(windows, narrow_ref, bi, j, run)


def _combine(y, pos, aff, x3, gate, s0, narrow, cap, tb, per_batch, g_final):
    b, t, d = x3.shape
    nb = t // tb
    final = g_final is not None
    tok = lambda c: pl.BlockSpec((1, tb, c), lambda bi, j, s, n: (bi, j, 0))
    const = lambda a: pl.BlockSpec(a.shape, lambda bi, j, s, n: (0, 0))
    in_specs = [pl.BlockSpec((1, N_EXPERTS, cap, d), lambda bi, j, s, n: (bi, 0, 0, 0)),
                tok(N_EXPERTS), tok(N_EXPERTS), tok(d),
                pl.BlockSpec((1, 1, d), (lambda bi, j, s, n: (bi, 0, 0)) if per_batch
                             else (lambda bi, j, s, n: (0, 0, 0)))]
    args = [y, pos, aff, x3, gate]
    if final:
        in_specs.append(const(g_final))
        args.append(g_final)
    w = _windows(cap, tb)[0]
    spread = jnp.repeat(jnp.eye(N_EXPERTS, dtype=BF16), w, axis=1)
    row_id = jnp.tile(jnp.arange(w, dtype=F32), N_EXPERTS).reshape(1, -1)
    in_specs += [const(spread), const(row_id)]
    args += [spread, row_id]
    grid_spec = pltpu.PrefetchScalarGridSpec(
        num_scalar_prefetch=2, grid=(b, nb), in_specs=in_specs, out_specs=tok(d))
    return pl.pallas_call(
        functools.partial(_combine_kernel, cap=cap, tb=tb, final=final),
        grid_spec=grid_spec,
        out_shape=jax.ShapeDtypeStruct((b, t, d), F32),
        compiler_params=_cp("parallel", "arbitrary"),
        name="moe_combine",
    )(s0, narrow, *args)


def _moe(x2, h2, aff, b, t, gate, w_gate, w_up, w_down, layer, per_batch, g_final):
    cap = CAPACITY_FACTOR * t // N_EXPERTS
    tb = min(MOE_TOKEN_BLOCK, t)
    aff3 = aff.reshape(b, t, N_EXPERTS)
    per = LANES // N_EXPERTS
    groups = -(-b // per)
    nb = t // tb

    def pack(a):
        a = jnp.pad(a, ((0, groups * per - b), (0, 0), (0, 0)))
        return a.reshape(groups, per, a.shape[1], N_EXPERTS).transpose(0, 2, 1, 3).reshape(groups, a.shape[1], LANES)

    def unpack(a):
        return a.reshape(groups, a.shape[1], per, N_EXPERTS).transpose(0, 2, 1, 3).reshape(
            groups * per, a.shape[1], N_EXPERTS)[:b]

    pos, s0 = _topk(pack(aff3), cap, tb)
    pos, s0 = unpack(pos), unpack(s0)
    counts = jnp.diff(s0, axis=1, append=jnp.full((b, 1, N_EXPERTS), cap, I32))
    narrow = jnp.all(counts <= MOE_SMALL_COUNT, axis=-1).astype(I32)
    xin = _gather(h2.reshape(b, t, D_MODEL), pos.transpose(0, 2, 1), s0, narrow, cap, tb)
    y = _experts(xin, w_gate, w_up, w_down, layer)
    out = _combine(y, pos, aff3, x2.reshape(b, t, D_MODEL), gate, s0, narrow, cap, tb, per_batch, g_final)
    return out.reshape(b * t, D_MODEL)


def _rope_tables(t):
    rows = t // GRID_W
    row = jnp.repeat(jnp.arange(rows, dtype=F32), GRID_W)
    col = jnp.tile(jnp.arange(GRID_W, dtype=F32), rows)
    n_freq = DQK_A // 4
    inv_freq = jnp.power(ROPE_BASE, -jnp.arange(n_freq, dtype=F32) / n_freq)
    ang = jnp.concatenate([row[:, None] * inv_freq, col[:, None] * inv_freq], axis=-1)
    cos, sin = jnp.cos(ang), jnp.sin(ang)
    zero = jnp.zeros_like(sin)
    reps = LANES // DQK_A
    cos_t = jnp.tile(jnp.concatenate([cos, cos], axis=-1), (1, reps))
    sin_a = jnp.tile(jnp.concatenate([-sin, zero], axis=-1), (1, reps))
    sin_b = jnp.tile(jnp.concatenate([zero, sin], axis=-1), (1, reps))
    return cos_t, sin_a, sin_b


def _stream(x, mods, per_batch, rope, cache, p):
    b, t, d = x.shape
    x2 = x.reshape(b * t, d)
    depth = p["g_norm"].shape[0]
    even_out, odd_out = [], []
    for l in range(depth):
        j = l // 2
        sh1, sc1, gt1, sh2, sc2, gt2 = mods[l]
        g1 = p["g_norm"][l, 0].reshape(1, d)
        g2 = p["g_norm"][l, 1].reshape(1, d)
        route = (g2, sh2, sc2, p["w_router_pad"][l])
        if l % 2 == 0:
            aq, ak, av, bqk, bv, bo, gates = _in_ab(x2, g1, sh1, sc1, p["w_in_ab_main"][j], p["w_in_ab_gate"][j],
                                                     rope, t, per_batch, F32 if cache is None else BF16)
            r3 = lambda a: a.reshape(b, t, a.shape[-1])
            if cache is not None:
                kc = cache[0][:, j].reshape(b, -1, W_A)
                vc = cache[1][:, j].reshape(b, -1, W_A)
                nu = 2 * H_B
                init = (cache[2][:, j].reshape(b, nu, DH_B, DH_B),
                        cache[3][:, j].reshape(b, nu, 1, DH_B),
                        jnp.broadcast_to(cache[4][:, j].reshape(b, nu, 1, 1), (b, nu, 1, LANES)))
            else:
                kc = vc = init = None
            lam_init = 0.8 - 0.6 * math.exp(-0.3 * l)
            attn = _attention_pipe(r3(aq), r3(ak), r3(av), kc, vc, p["lam_ab"][j], lam_init)
            qk = _conv_silu(r3(bqk), p["conv_ab"][j])
            bias = jnp.pad(jnp.concatenate([p["b_ig_ab"][j].reshape(-1), p["b_fg_ab"][j].reshape(-1)]),
                           (0, LANES - 4 * H_B)).reshape(1, LANES)
            res = _mlstm(qk, r3(bv), r3(gates), bias, init, cache is None)
            hf, hb = res[0], res[1]
            if cache is None:
                even_out.append((ak, av, res[2], res[3], res[4]))
            x2, h2, aff = _out_ab(attn.reshape(b * t, W_A), hf.reshape(b * t, W_B), hb.reshape(b * t, W_B), bo, x2,
                                  gt1, p["g_attn_ab"][j].reshape(1, W_A), p["g_mlstm_ab"][j].reshape(1, W_B),
                                  p["w_out_ab_bf16"][j], route, t, per_batch, 1.0 - lam_init)
        else:
            q, v, og, lf, kk = _in_c(x2, g1, sh1, sc1, p["w_in_c_bf16"][j], p["b_f_c"][j], p["gamma_lb"], l, t,
                                     per_batch)
            r3 = lambda a: a.reshape(b, t, a.shape[-1])
            init = None if cache is None else cache[5][:, j]
            res = _hgrn(r3(q), r3(v), r3(lf), r3(kk), init, cache is None)
            if cache is None:
                odd_out.append(res[2])
            x2, h2, aff = _out_c(res[0].reshape(b * t, W_C), res[1].reshape(b * t, W_C), og, x2, gt1,
                                 p["g_hgrn_c"][j].reshape(1, W_C), p["w_out_c_bf16"][j], route, t, per_batch)
        g_final = p["g_final"].reshape(1, d) if l == depth - 1 else None
        x2 = _moe(x2, h2, aff, b, t, gt2, p["w_gate_e"], p["w_up_e"], p["w_down_e"], l, per_batch, g_final)
    return x2.reshape(b, t, d), even_out, odd_out


def kernel(x_prompt, x_sample, cache_dattn_k, cache_dattn_v, state_mlstm_C, state_mlstm_n, state_mlstm_m,
           state_hgrn_S, c, c_ctx, w_ada, b_ada, g_norm, g_final, w_in_ab, b_ig_ab, b_fg_ab, lam_ab, conv_ab,
           g_attn_ab, g_mlstm_ab, w_out_ab, w_in_c, b_f_c, gamma_lb, g_hgrn_c, w_out_c,
           w_router, w_gate_e, w_up_e, w_down_e):
    d = D_MODEL
    bs = c.shape[0]
    bp, tp, _ = x_prompt.shape
    depth = w_ada.shape[0]
    n_main = 3 * W_A + 4 * W_B
    p = dict(
        g_norm=g_norm, g_final=g_final, b_ig_ab=b_ig_ab, b_fg_ab=b_fg_ab, lam_ab=lam_ab, conv_ab=conv_ab,
        g_attn_ab=g_attn_ab, g_mlstm_ab=g_mlstm_ab, b_f_c=b_f_c, gamma_lb=gamma_lb, g_hgrn_c=g_hgrn_c,
        w_gate_e=w_gate_e, w_up_e=w_up_e, w_down_e=w_down_e,
        w_in_ab_main=w_in_ab[:, :, :n_main].astype(BF16),
        w_in_ab_gate=jnp.pad(w_in_ab[:, :, n_main:], ((0, 0), (0, 0), (0, LANES - 4 * H_B))).astype(BF16),
        w_out_ab_bf16=w_out_ab.astype(BF16),
        w_in_c_bf16=w_in_c.astype(BF16),
        w_out_c_bf16=w_out_c.astype(BF16),
        w_router_pad=jnp.pad(w_router, ((0, 0), (0, 0), (0, LANES - N_EXPERTS))),
    )
    rows = -(-(bs + 1) // 8) * 8
    cond = jnp.concatenate([c, c_ctx.reshape(1, d), jnp.zeros((rows - bs - 1, d), F32)], axis=0)
    mods = _ada(cond, w_ada, b_ada)
    mods_s = [[mods[l, :bs, i * d:(i + 1) * d].reshape(bs, 1, d) for i in range(6)] for l in range(depth)]
    mods_c = [[mods[l, bs:bs + 1, i * d:(i + 1) * d].reshape(1, 1, d) for i in range(6)] for l in range(depth)]

    y_prompt, ev, od = _stream(x_prompt, mods_c, False, None, None, p)
    new_k = jnp.stack([e[0].reshape(bp, tp, H_A, 2 * DQK_A) for e in ev], axis=1)
    new_v = jnp.stack([e[1].reshape(bp, tp, H_A, DV_A) for e in ev], axis=1)
    new_c = jnp.stack([e[2].reshape(bp, 2, H_B, DH_B, DH_B) for e in ev], axis=1)
    new_n = jnp.stack([e[3].reshape(bp, 2, H_B, DH_B) for e in ev], axis=1)
    new_m = jnp.stack([e[4][..., 0].reshape(bp, 2, H_B) for e in ev], axis=1)
    new_s = jnp.stack(od, axis=1)

    cache = (cache_dattn_k, cache_dattn_v, state_mlstm_C, state_mlstm_n, state_mlstm_m, state_hgrn_S)
    y_sample, _, _ = _stream(x_sample, mods_s, True, _rope_tables(x_sample.shape[1]), cache, p)
    return (y_prompt, y_sample, new_k, new_v, new_c, new_n, new_m, new_s)
```

```python
import functools
import math

import jax
import jax.numpy as jnp
from jax import lax
from jax.experimental import pallas as pl
from jax.experimental.pallas import tpu as pltpu

F32 = jnp.float32
BF16 = jnp.bfloat16
I32 = jnp.int32

D_MODEL = 1024
H_A = 4
DV_A = 128
DQK_A = 64
W_A = H_A * DV_A
H_B = 4
DH_B = 128
W_B = H_B * DH_B
H_C = 8
DH_C = 128
W_C = H_C * DH_C
N_EXPERTS = 16
CAPACITY_FACTOR = 2
GRID_W = 64
ROPE_BASE = 10000.0
RMS_EPS = 1e-6
NEG_BIG = -1e30
LOG2E = 1.4426950408889634
LANES = 128
BF16_ROWS = 16
SCAN_CHUNK = 128
MOE_TOKEN_BLOCK = 256
VMEM_LIMIT = 56 * 1024 * 1024


def _cp(*sem):
    return pltpu.CompilerParams(dimension_semantics=sem, vmem_limit_bytes=VMEM_LIMIT)


def _dot(a, b):
    return jnp.dot(a, b, preferred_element_type=F32)


def _dot_nt(a, b):
    return lax.dot_general(a, b, (((1,), (1,)), ((), ())), preferred_element_type=F32)


def _dot_tn(a, b):
    return lax.dot_general(a, b, (((0,), (0,)), ((), ())), preferred_element_type=F32)


def _split3(x):
    hi = x.astype(BF16)
    r = x - hi.astype(F32)
    mid = r.astype(BF16)
    lo = (r - mid.astype(F32)).astype(BF16)
    return hi, mid, lo


def _dot_sel_left(sel, x):
    hi, mid, lo = _split3(x)
    return _dot(sel, hi) + _dot(sel, mid) + _dot(sel, lo)


def _dot_sel_right(x, sel):
    hi, mid, lo = _split3(x)
    return _dot(hi, sel) + _dot(mid, sel) + _dot(lo, sel)


def _dot_f32(a, b):
    ah, am, al = _split3(a)
    bh, bm, bl = _split3(b)
    return (_dot(ah, bh) + (_dot(ah, bm) + _dot(am, bh))
            + (_dot(ah, bl) + _dot(am, bm) + _dot(al, bh)))


def _dot_f32x3(a, b):
    ah = a.astype(BF16)
    al = (a - ah.astype(F32)).astype(BF16)
    bh = b.astype(BF16)
    bl = (b - bh.astype(F32)).astype(BF16)
    return _dot(ah, bh) + (_dot(ah, bl) + _dot(al, bh))


def _dot_sel_left2(sel, x):
    hi = x.astype(BF16)
    lo = (x - hi.astype(F32)).astype(BF16)
    return _dot(sel, hi) + _dot(sel, lo)


def _sigmoid(x):
    return 1.0 / (1.0 + jnp.exp(-x))


def _silu(x):
    return x * _sigmoid(x)


def _log_sigmoid(x):
    return jnp.minimum(x, 0.0) - jnp.log(1.0 + jnp.exp(-jnp.abs(x)))


def _rms_rows(x, g):
    return x * lax.rsqrt(jnp.mean(x * x, axis=-1, keepdims=True) + RMS_EPS) * g


def _iota(shape, dim):
    return lax.broadcasted_iota(I32, shape, dim)


def _ada_kernel(c_ref, w_ref, b_ref, o_ref):
    c = c_ref[...]
    o_ref[0] = _dot_f32(_silu(c), w_ref[0]) + b_ref[0]


def _ada(cond_rows, w_ada, b_ada):
    depth, d, n6 = w_ada.shape
    rows = cond_rows.shape[0]
    tn = 1536
    return pl.pallas_call(
        _ada_kernel,
        grid=(depth, n6 // tn),
        in_specs=[pl.BlockSpec((rows, d), lambda l, j: (0, 0)),
                  pl.BlockSpec((1, d, tn), lambda l, j: (l, 0, j)),
                  pl.BlockSpec((1, 1, tn), lambda l, j: (l, 0, j))],
        out_specs=pl.BlockSpec((1, rows, tn), lambda l, j: (l, 0, j)),
        out_shape=jax.ShapeDtypeStruct((depth, rows, n6), F32),
        compiler_params=_cp("parallel", "parallel"),
        name="adaln",
    )(cond_rows, w_ada, b_ada.reshape(depth, 1, n6))


def _mod_spec(per_batch, tm, t):
    if per_batch:
        return pl.BlockSpec((1, 1, D_MODEL), lambda i: (i * tm // t, 0, 0))
    return pl.BlockSpec((1, 1, D_MODEL), lambda i: (0, 0, 0))


def _row_tile(t):
    return min(512, t)


def _in_ab_kernel(*refs, use_rope):
    if use_rope:
        (x_ref, g_ref, sh_ref, sc_ref, w_ref, wg_ref, cos_ref, sa_ref, sb_ref,
         aq_ref, ak_ref, av_ref, bqk_ref, bv_ref, bo_ref, gt_ref) = refs
    else:
        (x_ref, g_ref, sh_ref, sc_ref, w_ref, wg_ref,
         aq_ref, ak_ref, av_ref, bqk_ref, bv_ref, bo_ref, gt_ref) = refs
    h = _rms_rows(x_ref[...], g_ref[...]) * (1.0 + sc_ref[0]) + sh_ref[0]
    hb = h.astype(BF16)
    y = _dot(hb, w_ref[...])
    gt_ref[...] = _dot(hb, wg_ref[...])
    aq = y[:, 0:W_A] * (DQK_A ** -0.5 * LOG2E)
    ak = y[:, W_A:2 * W_A]
    if use_rope:
        cos = jnp.concatenate([cos_ref[...]] * H_A, axis=1)
        sa = jnp.concatenate([sa_ref[...]] * H_A, axis=1)
        sb = jnp.concatenate([sb_ref[...]] * H_A, axis=1)
        half = DQK_A // 2

        def rope(v):
            return (v * cos + pltpu.roll(v, W_A - half, axis=1) * sa
                    + pltpu.roll(v, half, axis=1) * sb)

        aq = rope(aq)
        ak = rope(ak)
    aq_ref[...] = aq.astype(aq_ref.dtype)
    ak_ref[...] = ak.astype(ak_ref.dtype)
    av_ref[...] = y[:, 2 * W_A:3 * W_A].astype(av_ref.dtype)
    bqk_ref[...] = y[:, 3 * W_A:3 * W_A + 2 * W_B]
    bv_ref[...] = y[:, 3 * W_A + 2 * W_B:3 * W_A + 3 * W_B].astype(BF16)
    bo_ref[...] = y[:, 3 * W_A + 3 * W_B:3 * W_A + 4 * W_B].astype(BF16)


def _in_ab(x2, g, shift, scale, w_main, w_gate, rope, t, per_batch, qkv_dtype):
    n = x2.shape[0]
    tm = _row_tile(t)
    use_rope = rope is not None
    row = lambda c: pl.BlockSpec((tm, c), lambda i: (i, 0))
    full = lambda a: pl.BlockSpec(a.shape, lambda i: (0,) * a.ndim)
    in_specs = [row(D_MODEL), full(g), _mod_spec(per_batch, tm, t), _mod_spec(per_batch, tm, t),
                full(w_main), full(w_gate)]
    args = [x2, g, shift, scale, w_main, w_gate]
    if use_rope:
        nt = t // tm
        tab = pl.BlockSpec((tm, LANES), lambda i: (i % nt, 0))
        in_specs += [tab, tab, tab]
        args += list(rope)
    widths = (W_A, W_A, W_A, 2 * W_B, W_B, W_B, LANES)
    return pl.pallas_call(
        functools.partial(_in_ab_kernel, use_rope=use_rope),
        grid=(n // tm,),
        in_specs=in_specs,
        out_specs=[row(c) for c in widths],
        out_shape=[jax.ShapeDtypeStruct((n, c), dt) for c, dt in
                   zip(widths, (qkv_dtype, qkv_dtype, qkv_dtype, F32, BF16, BF16, F32))],
        compiler_params=_cp("parallel"),
        name="in_proj_even",
    )(*args)


def _attn_pipe_kernel(*refs, has_cache, tk, lam_init, nq):
    if has_cache:
        q_ref, k_ref, v_ref, kc_ref, vc_ref, lam_ref, o_ref, s_a, s_b, mp_s, vt_s, sc_s, vct_s = refs
    else:
        q_ref, k_ref, v_ref, lam_ref, o_ref, s_a, s_b, mp_s, vt_s = refs
    s_s = (s_a, s_b)
    i = pl.program_id(2)
    q = q_ref[0]
    tq = q.shape[0]
    first = _iota((1, LANES), 1) < DQK_A
    qs = (jnp.where(first, q, 0).astype(BF16), jnp.where(first, 0, q).astype(BF16))
    nkb = k_ref.shape[1] // tk

    @pl.when(i == 0)
    def _():
        for j in range(nkb):
            vt_s[j, 0:DV_A, :] = v_ref[0, j * tk:(j + 1) * tk, :].astype(F32).T.astype(BF16)
            vt_s[j, DV_A:, :] = jnp.ones((BF16_ROWS, tk), BF16)
        if has_cache:
            vct_s[0:DV_A, :] = vc_ref[0].astype(F32).T.astype(BF16)
            vct_s[DV_A:, :] = jnp.ones((BF16_ROWS, vct_s.shape[1]), BF16)

    def groups(x):
        return [x[:, g * LANES:(g + 1) * LANES] for g in range(x.shape[1] // LANES)]

    def body(score, finish, cur):
        prev = 1 - cur
        if finish:
            m = [jnp.max(mp_s[prev, a], axis=-1, keepdims=True) for a in range(2)]
        mpart = [jnp.full((tq, LANES), NEG_BIG, F32)] * 2
        acc = [jnp.zeros((DV_A + BF16_ROWS, tq), F32)] * 2

        def pass1(kb, store):
            for a in range(2):
                s = _dot_nt(qs[a], kb)
                store(a, s)
                mpart[a] = functools.reduce(jnp.maximum, groups(s), mpart[a])

        def pass2(a, s, vt):
            acc[a] = acc[a] + _dot_nt(vt, jnp.exp2(s - m[a]).astype(BF16))

        if has_cache:
            def store_c(a, s):
                sc_s[cur, a] = s
            if score:
                pass1(kc_ref[0].astype(BF16), store_c)
            if finish:
                for a in range(2):
                    pass2(a, sc_s[prev, a], vct_s[...])
        for j in range(nkb):
            def store_j(a, s, j=j):
                s_s[cur][a, j, :, 0:tk] = s
            if score:
                pass1(k_ref[0, j * tk:(j + 1) * tk, :].astype(BF16), store_j)
            if finish:
                for a in range(2):
                    pass2(a, s_s[prev][a, j, :, 0:tk], vt_s[j])
        if score:
            for a in range(2):
                mp_s[cur, a] = mpart[a]
        if finish:
            lp = lam_ref[...]
            lam = (jnp.exp(jnp.sum(lp[0:1] * lp[1:2], axis=-1, keepdims=True))
                   - jnp.exp(jnp.sum(lp[2:3] * lp[3:4], axis=-1, keepdims=True)) + lam_init)
            o1 = acc[0][0:DV_A] / acc[0][DV_A:DV_A + 1]
            o2 = acc[1][0:DV_A] / acc[1][DV_A:DV_A + 1]
            o_ref[0] = (o1 - lam * o2).T.astype(BF16)

    pl.when(i == 0)(lambda: body(True, False, 0))
    inner = jnp.logical_and(i > 0, i < nq)
    pl.when(jnp.logical_and(inner, i % 2 == 0))(lambda: body(True, True, 0))
    pl.when(jnp.logical_and(inner, i % 2 == 1))(lambda: body(True, True, 1))
    pl.when(i == nq)(lambda: body(False, True, nq % 2))


def _attention_pipe(aq, ak, av, kc, vc, lam_p, lam_init):
    b, t, _ = aq.shape
    tq = min(256, t)
    tk = min(512, t)
    nq = t // tq
    has_cache = kc is not None
    qspec = pl.BlockSpec((1, tq, DV_A), lambda bi, h, i: (bi, jnp.minimum(i, nq - 1), h))
    ospec = pl.BlockSpec((1, tq, DV_A), lambda bi, h, i: (bi, jnp.maximum(i - 1, 0), h))
    ks = pl.BlockSpec((1, t, DV_A), lambda bi, h, i: (bi, 0, h))
    in_specs = [qspec, ks, ks]
    args = [aq, ak, av]
    if has_cache:
        cs = pl.BlockSpec((1, kc.shape[1], DV_A), lambda bi, h, i: (bi, 0, h))
        in_specs += [cs, cs]
        args += [kc, vc]
    in_specs.append(pl.BlockSpec(lam_p.shape, lambda bi, h, i: (0, 0)))
    args.append(lam_p)
    scratch = [pltpu.VMEM((2, t // tk, tq, tk), F32), pltpu.VMEM((2, t // tk, tq, tk + LANES), F32),
               pltpu.VMEM((2, 2, tq, LANES), F32),
               pltpu.VMEM((t // tk, DV_A + BF16_ROWS, tk), BF16)]
    if has_cache:
        scratch += [pltpu.VMEM((2, 2, tq, kc.shape[1]), F32),
                    pltpu.VMEM((DV_A + BF16_ROWS, kc.shape[1]), BF16)]
    return pl.pallas_call(
        functools.partial(_attn_pipe_kernel, has_cache=has_cache, tk=tk, lam_init=lam_init, nq=nq),
        grid=(b, H_A, nq + 1),
        in_specs=in_specs,
        out_specs=ospec,
        out_shape=jax.ShapeDtypeStruct((b, t, W_A), BF16),
        scratch_shapes=scratch,
        compiler_params=_cp("parallel", "parallel", "arbitrary"),
        name="diff_attention",
    )(*args)


def _conv_kernel(x_ref, w_ref, o_ref):
    x = x_ref[0]
    t = x.shape[0]
    w = w_ref[...]
    r = _iota((t, 1), 0)
    prev = jnp.where(r == 0, 0.0, pltpu.roll(x, 1, axis=0))
    nxt = jnp.where(r == t - 1, 0.0, pltpu.roll(x, t - 1, axis=0))
    scale = jnp.where(pl.program_id(1) * x.shape[1] >= W_B, DH_B ** -0.5, 1.0)
    o_ref[0] = (_silu(prev * w[0:1] + x * w[1:2] + nxt * w[2:3]) * scale).astype(BF16)


def _conv_silu(bqk, conv_w):
    b, t, c = bqk.shape
    tc = 256
    spec = pl.BlockSpec((1, t, tc), lambda bi, j: (bi, 0, j))
    return pl.pallas_call(
        _conv_kernel,
        grid=(b, c // tc),
        in_specs=[spec, pl.BlockSpec((conv_w.shape[0], tc), lambda bi, j: (0, j))],
        out_specs=spec,
        out_shape=jax.ShapeDtypeStruct(bqk.shape, BF16),
        compiler_params=_cp("parallel", "parallel"),
        name="mlstm_conv",
    )(bqk, conv_w)


def _tri(n, lower):
    r = _iota((n, n), 0)
    c = _iota((n, n), 1)
    return r >= c if lower else r <= c


def _mlstm_kernel(*refs, has_init, out_state):
    qf, kf, vf, gf, qb, kb, vb, gb, bias = refs[:9]
    pos = 9
    if has_init:
        c0, n0, m0 = refs[pos:pos + 3]
        pos += 3
    hf, hb = refs[pos:pos + 2]
    pos += 2
    if out_state:
        c_out, n_out, m_out = refs[pos:pos + 3]
        pos += 3
    cn_s, m_s = refs[pos:pos + 2]
    step = pl.program_id(1)
    L = SCAN_CHUNK
    nu = 2 * H_B

    @pl.when(step == 0)
    def _():
        if has_init:
            for u in range(nu):
                cn_s[u, :, 0:DH_B] = c0[0, u]
                cn_s[u, :, DH_B:2 * DH_B] = jnp.broadcast_to(n0[0, u], (DH_B, DH_B)).T
            m_s[...] = m0[0]
        else:
            cn_s[...] = jnp.zeros_like(cn_s)
            m_s[...] = jnp.zeros_like(m_s)

    ones_b = jnp.ones((L, DH_B), BF16)
    sel_rows = _iota((LANES, LANES), 0)
    lower = _tri(L, True)
    upper = _tri(L, False)
    lower_b = lower.astype(BF16)
    upper_b = upper.astype(BF16)
    col = _iota((1, LANES), 1)
    units = []
    for d, (q_ref, k_ref, v_ref, g_ref, h_ref) in enumerate(((qf, kf, vf, gf, hf), (qb, kb, vb, gb, hb))):
        g = g_ref[0] + bias[...]
        g = jnp.where(col < 2 * H_B, g, _log_sigmoid(g))
        gt = g.T
        if d == 0:
            bc_all = _dot_sel_left(lower_b, g)
            br_all = _dot_sel_right(gt, upper_b)
            mask = lower
        else:
            bc_all = _dot_sel_left(upper_b, g)
            br_all = _dot_sel_right(gt, lower_b)
            mask = upper
        for h in range(H_B):
            u = d * H_B + h
            sl = slice(h * DH_B, (h + 1) * DH_B)
            units.append(dict(
                u=u, d=d, sl=sl, h_ref=h_ref, mask=mask, g=g, bc_all=bc_all,
                li_row=gt[u:u + 1, :], b_row=br_all[2 * H_B + u:2 * H_B + u + 1, :],
                q=q_ref[0, :, sl].astype(BF16), k=k_ref[0, :, sl].astype(F32),
                v1=jnp.concatenate([v_ref[0, :, sl].astype(BF16), ones_b], axis=1)))
    for x in units:
        x["li_b"] = _dot_sel_right(x["g"], (sel_rows == x["u"]).astype(BF16))
        x["b_b"] = _dot_sel_right(x["bc_all"], (sel_rows == 2 * H_B + x["u"]).astype(BF16))
        x["qk"] = _dot_nt(x["q"], x["k"].astype(BF16))
        x["cn"] = cn_s[x["u"]]
        x["qc"] = _dot(x["q"], x["cn"].astype(BF16))
    for x in units:
        x["m_prev"] = m_s[x["u"]]
        a_b = x["b_b"] + x["m_prev"]
        dm = jnp.where(x["mask"], x["b_b"] - x["b_row"] + x["li_row"], NEG_BIG)
        x["m_t"] = jnp.maximum(a_b, jnp.max(dm, axis=-1, keepdims=True))
        x["w_inter"] = jnp.exp(a_b - x["m_t"])
        x["s"] = (x["qk"] * jnp.exp(dm - x["m_t"])).astype(BF16)
    for x in units:
        x["sv"] = _dot(x["s"], x["v1"])
        b_end = x["b_b"][L - 1:L, :] if x["d"] == 0 else x["b_b"][0:1, :]
        g_b = b_end - x["b_b"] + x["li_b"]
        x["m_new"] = jnp.maximum(b_end + x["m_prev"], jnp.max(g_b, axis=0, keepdims=True))
        x["w_old"] = jnp.exp(b_end + x["m_prev"] - x["m_new"])
        x["ks"] = (jnp.exp(g_b - x["m_new"]) * x["k"]).astype(BF16)
    for x in units:
        num = x["w_inter"] * x["qc"][:, 0:DH_B] + x["sv"][:, 0:DH_B]
        den = x["w_inter"] * x["qc"][:, DH_B:] + x["sv"][:, DH_B:]
        x["h_ref"][0, :, x["sl"]] = (num / jnp.maximum(jnp.abs(den), jnp.exp(-x["m_t"]))).astype(BF16)
        w2 = jnp.concatenate([x["w_old"], x["w_old"]], axis=1)
        cn_s[x["u"]] = w2 * x["cn"] + _dot_tn(x["ks"], x["v1"])
        m_s[x["u"]] = x["m_new"]

    if out_state:
        @pl.when(step == pl.num_programs(1) - 1)
        def _():
            for u in range(nu):
                c_out[0, u] = cn_s[u, :, 0:DH_B]
                n_out[0, u] = cn_s[u, :, DH_B:2 * DH_B].T[0:1, :]
            m_out[0] = m_s[...]


def _mlstm(qk, v, gates, bias, init, out_state):
    b, t, _ = v.shape
    L = SCAN_CHUNK
    nc = t // L
    nu = 2 * H_B
    fwd = lambda j: (lambda bi, c: (bi, c, j))
    bwd = lambda j: (lambda bi, c: (bi, nc - 1 - c, j))
    blk = lambda w, im: pl.BlockSpec((1, L, w), im)
    in_specs = [blk(W_B, fwd(0)), blk(W_B, fwd(1)), blk(W_B, fwd(0)), blk(LANES, fwd(0)),
                blk(W_B, bwd(0)), blk(W_B, bwd(1)), blk(W_B, bwd(0)), blk(LANES, bwd(0)),
                pl.BlockSpec((1, LANES), lambda bi, c: (0, 0))]
    args = [qk, qk, v, gates, qk, qk, v, gates, bias]
    st_specs = [pl.BlockSpec((1, nu, DH_B, DH_B), lambda bi, c: (bi, 0, 0, 0)),
                pl.BlockSpec((1, nu, 1, DH_B), lambda bi, c: (bi, 0, 0, 0)),
                pl.BlockSpec((1, nu, 1, LANES), lambda bi, c: (bi, 0, 0, 0))]
    st_shapes = [jax.ShapeDtypeStruct((b, nu, DH_B, DH_B), F32),
                 jax.ShapeDtypeStruct((b, nu, 1, DH_B), F32),
                 jax.ShapeDtypeStruct((b, nu, 1, LANES), F32)]
    has_init = init is not None
    if has_init:
        in_specs += st_specs
        args += list(init)
    out_specs = [blk(W_B, fwd(0)), blk(W_B, bwd(0))]
    out_shape = [jax.ShapeDtypeStruct((b, t, W_B), BF16)] * 2
    if out_state:
        out_specs += st_specs
        out_shape += st_shapes
    return pl.pallas_call(
        functools.partial(_mlstm_kernel, has_init=has_init, out_state=out_state),
        grid=(b, nc),
        in_specs=in_specs,
        out_specs=out_specs,
        out_shape=out_shape,
        scratch_shapes=[pltpu.VMEM((nu, DH_B, 2 * DH_B), F32), pltpu.VMEM((nu, 1, LANES), F32)],
        compiler_params=_cp("parallel", "arbitrary"),
        name="mlstm_scan",
    )(*args)


def _head_rms(x, g, heads, width):
    parts = []
    for h in range(heads):
        sl = slice(h * width, (h + 1) * width)
        parts.append(_rms_rows(x[:, sl], g[:, sl]))
    return parts


OUT_PROJ_PARTS = 4


def _row_parts(tm):
    step = tm // OUT_PROJ_PARTS
    return [slice(r, r + step) for r in range(0, tm, step)]


def _project_and_route(cats, parts, x_ref, gate_ref, w_ref, g_ref, sh_ref, sc_ref, wr_ref, o_ref, h_ref, aff_ref):
    w = w_ref[...]
    xs = [x_ref[rs, :] + gate_ref[0] * _dot(cat, w) for cat, rs in zip(cats, parts)]
    hs = []
    for x, rs in zip(xs, parts):
        o_ref[rs, :] = x
        h = _rms_rows(x, g_ref[...]) * (1.0 + sc_ref[0]) + sh_ref[0]
        h_ref[rs, :] = h.astype(BF16)
        hs.append(h)
    wr = wr_ref[...]
    logits = [_dot_f32x3(h, wr) for h in hs]
    valid = _iota((1, LANES), 1) < N_EXPERTS
    for lg, rs in zip(logits, parts):
        lg = jnp.where(valid, lg, NEG_BIG)
        e = jnp.exp(lg - jnp.max(lg, axis=-1, keepdims=True))
        e = jnp.where(valid, e, 0.0)
        aff_ref[rs, :] = (e / jnp.sum(e, axis=-1, keepdims=True))[:, :N_EXPERTS]


def _route_specs(route, tm, t, per_batch):
    g2, sh2, sc2, wr = route
    full = lambda a: pl.BlockSpec(a.shape, lambda i: (0,) * a.ndim)
    row = lambda c: pl.BlockSpec((tm, c), lambda i: (i, 0))
    in_specs = [full(g2), _mod_spec(per_batch, tm, t), _mod_spec(per_batch, tm, t), full(wr)]
    out_specs = [row(D_MODEL), row(D_MODEL), row(N_EXPERTS)]
    return in_specs, out_specs


def _route_shapes(n):
    return [jax.ShapeDtypeStruct((n, D_MODEL), F32), jax.ShapeDtypeStruct((n, D_MODEL), BF16),
            jax.ShapeDtypeStruct((n, N_EXPERTS), F32)]


def _out_ab_kernel(at_ref, hf_ref, hb_ref, bo_ref, x_ref, gate_ref, ga_ref, gm_ref, w_ref,
                   g2_ref, sh2_ref, sc2_ref, wr_ref, o_ref, h_ref, aff_ref, *, out_scale):
    parts = _row_parts(x_ref.shape[0])
    cats = []
    for rs in parts:
        oa = [p * out_scale for p in _head_rms(at_ref[rs, :].astype(F32), ga_ref[...], H_A, DV_A)]
        hm = _head_rms(hf_ref[rs, :].astype(F32) + hb_ref[rs, :].astype(F32), gm_ref[...], H_B, DH_B)
        bo = bo_ref[rs, :].astype(F32)
        ob = [_sigmoid(bo[:, h * DH_B:(h + 1) * DH_B]) * hm[h] for h in range(H_B)]
        cats.append(jnp.concatenate(oa + ob, axis=1).astype(BF16))
    _project_and_route(cats, parts, x_ref, gate_ref, w_ref, g2_ref, sh2_ref, sc2_ref, wr_ref, o_ref, h_ref, aff_ref)


def _out_ab(attn, hf, hb, bo, x2, gate, g_attn, g_mlstm, w_out, route, t, per_batch, out_scale):
    n = x2.shape[0]
    tm = _row_tile(t)
    row = lambda c: pl.BlockSpec((tm, c), lambda i: (i, 0))
    full = lambda a: pl.BlockSpec(a.shape, lambda i: (0,) * a.ndim)
    r_in, r_out = _route_specs(route, tm, t, per_batch)
    return pl.pallas_call(
        functools.partial(_out_ab_kernel, out_scale=out_scale),
        grid=(n // tm,),
        in_specs=[row(W_A), row(W_B), row(W_B), row(W_B), row(D_MODEL), _mod_spec(per_batch, tm, t),
                  full(g_attn), full(g_mlstm), full(w_out)] + r_in,
        out_specs=r_out,
        out_shape=_route_shapes(n),
        compiler_params=_cp("parallel"),
        name="out_proj_even",
    )(attn, hf, hb, bo, x2, gate, g_attn, g_mlstm, w_out, *route)


def _in_c_kernel(x_ref, g_ref, sh_ref, sc_ref, w_ref, bf_ref, gam_ref, q_ref, v_ref, og_ref, lf_ref, kk_ref, *, layer):
    gam = gam_ref[...]
    e = jnp.exp(gam - jnp.max(gam, axis=0, keepdims=True))
    p = e / jnp.sum(e, axis=0, keepdims=True)
    lbs = p[0:1]
    lbs0 = lbs
    for i in range(1, layer + 1):
        lbs = lbs + p[i:i + 1]
    lb = jnp.maximum(lbs - lbs0, 0.0)
    L = SCAN_CHUNK
    w = w_ref[...]
    chunks = [slice(c0, c0 + L) for c0 in range(0, x_ref.shape[0], L)]
    ys = []
    for rs in chunks:
        h = _rms_rows(x_ref[rs, :], g_ref[...]) * (1.0 + sc_ref[0]) + sh_ref[0]
        ys.append(_dot(h.astype(BF16), w))
    for rs, y in zip(chunks, ys):
        q_ref[rs, :] = y[:, 0:W_C].astype(BF16)
        v_ref[rs, :] = y[:, W_C:2 * W_C].astype(BF16)
        og_ref[rs, :] = y[:, 2 * W_C:3 * W_C].astype(BF16)
        for d in range(2):
            f = y[:, (3 + d) * W_C:(4 + d) * W_C] + bf_ref[d:d + 1]
            u = jnp.exp(-jnp.abs(f))
            neg = f < 0.0
            lf2 = (jnp.log(jnp.where(neg, lb + u, 1.0 + lb * u)) - jnp.log(1.0 + u)) * LOG2E
            kk_ref[rs, d * W_C:(d + 1) * W_C] = ((1.0 - lb) * jnp.where(neg, 1.0, u) / (1.0 + u)).astype(BF16)
            lf_ref[rs, d * W_C:(d + 1) * W_C] = _dot_sel_left2(_tri(L, d == 0).astype(BF16), lf2)


def _in_c(x2, g, shift, scale, w_in, b_f, gamma_lb, layer, t, per_batch):
    n = x2.shape[0]
    tm = _row_tile(t)
    row = lambda c: pl.BlockSpec((tm, c), lambda i: (i, 0))
    full = lambda a: pl.BlockSpec(a.shape, lambda i: (0,) * a.ndim)
    widths = (W_C, W_C, W_C, 2 * W_C, 2 * W_C)
    return pl.pallas_call(
        functools.partial(_in_c_kernel, layer=layer),
        grid=(n // tm,),
        in_specs=[row(D_MODEL), full(g), _mod_spec(per_batch, tm, t), _mod_spec(per_batch, tm, t),
                  full(w_in), full(b_f), full(gamma_lb)],
        out_specs=[row(c) for c in widths],
        out_shape=[jax.ShapeDtypeStruct((n, c), dt) for c, dt in zip(widths, (BF16, BF16, BF16, F32, BF16))],
        compiler_params=_cp("parallel"),
        name="in_proj_odd",
    )(x2, g, shift, scale, w_in, b_f, gamma_lb)


HGRN_LEVELS = (8, 16, 32, 64)
HGRN_DIAG = 8
HGRN_HEADS_PER_STEP = 8


def _hgrn_unit(q, kk, v, b, st_ref, b_s, rev, store_o):
    L = SCAN_CHUNK
    b_s[...] = b
    row = _iota((L, 1), 0)
    col = _iota((1, L), 1)
    k16 = kk.astype(BF16)
    v16 = v.astype(BF16)
    q = q.astype(F32)
    kk = kk.astype(F32)
    st = st_ref[...]
    o = _dot_nt((q * jnp.exp2(b)).astype(BF16), st.astype(BF16))
    yield
    c = HGRN_DIAG
    sub = _iota((c, 1), 0)
    ys = []
    for blk in range(L // c):
        r0 = blk * c
        qb = q[r0:r0 + c]
        bb = b[r0:r0 + c]
        for s in range(c):
            keep = (sub <= s) if rev else (sub >= s)
            ys.append(qb * jnp.exp2(jnp.where(keep, bb - b_s[r0 + s:r0 + s + 1, :], NEG_BIG)))
    rs = _dot_nt(jnp.concatenate(ys, axis=0).astype(BF16), k16)
    yield
    parts = []
    for m in HGRN_LEVELS:
        par = 2 * m
        pieces = []
        for p0 in range(0, L, par):
            ref_row = p0 + m if rev else p0 + m - 1
            pieces.append(jnp.broadcast_to(b_s[ref_row:ref_row + 1, :], (par, LANES)))
        bref = jnp.concatenate(pieces, axis=0)
        in_first = (row % par) < m
        reader = in_first if rev else jnp.logical_not(in_first)
        z = (jnp.where(reader, q, kk) * jnp.exp2(-jnp.abs(b - bref))).astype(BF16)
        keep = jnp.logical_and((row // par) == (col // par),
                               ((col % par) >= m) if rev else ((col % par) < m))
        parts.append((jnp.logical_and(keep, reader), _dot_nt(z, z)))
        yield
    lane = _iota((c, LANES), 1)
    rows = []
    for blk in range(L // c):
        acc = jnp.zeros((c, LANES), F32)
        for s in range(c):
            i = blk * c + s
            acc = jnp.where(lane == i, rs[i * c:(i + 1) * c], acc)
        rows.append(acc)
    att = jnp.concatenate(rows, axis=0)
    for keep, a in parts:
        att = att + jnp.where(keep, a, 0.0)
    o = o + _dot(att.astype(BF16), v16)
    b_end = b_s[0:1, :] if rev else b_s[L - 1:L, :]
    kd = (kk * jnp.exp2(b_end - b)).astype(BF16)
    st_new = st * jnp.exp2(b_end) + _dot_tn(v16, kd)
    yield
    store_o(o)
    st_ref[...] = st_new
    yield


def _hgrn_kernel(*refs, has_init, out_state, hp):
    qf, vf, lff, kkf, qb, vb, lfb, kkb = refs[:8]
    pos = 8
    if has_init:
        s0 = refs[pos]
        pos += 1
    of, ob = refs[pos:pos + 2]
    pos += 2
    if out_state:
        s_out = refs[pos]
        pos += 1
    st_s, b_s = refs[pos:pos + 2]
    step = pl.program_id(2)

    @pl.when(step == 0)
    def _():
        for d in range(2):
            for h in range(hp):
                st_s[d * hp + h] = s0[0, d, h].T if has_init else jnp.zeros((DH_C, DH_C), F32)

    def storer(o_ref, sl):
        def store(o):
            o_ref[0, :, sl] = o.astype(BF16)
        return store

    units = []
    for d, (q_ref, v_ref, lf_ref, kk_ref, o_ref) in enumerate(((qf, vf, lff, kkf, of), (qb, vb, lfb, kkb, ob))):
        for h in range(hp):
            u = d * hp + h
            sl = slice(h * DH_C, (h + 1) * DH_C)
            units.append(_hgrn_unit(q_ref[0, :, sl], kk_ref[0, :, sl], v_ref[0, :, sl], lf_ref[0, :, sl],
                                    st_s.at[u], b_s.at[u], d == 1, storer(o_ref, sl)))
    live = units
    while live:
        live = [g for g in live if next(g, "done") != "done"]

    if out_state:
        @pl.when(step == pl.num_programs(2) - 1)
        def _():
            for d in range(2):
                for h in range(hp):
                    s_out[0, d, h] = st_s[d * hp + h].T


def _hgrn(q, v, lf, kk, init, out_state):
    b, t, _ = q.shape
    L = SCAN_CHUNK
    nc = t // L
    hp = HGRN_HEADS_PER_STEP
    ng = H_C // hp
    blk = lambda im: pl.BlockSpec((1, L, hp * DH_C), im)
    fwd = lambda off: (lambda bi, h, c: (bi, c, h + off))
    bwd = lambda off: (lambda bi, h, c: (bi, nc - 1 - c, h + off))
    in_specs = [blk(fwd(0)), blk(fwd(0)), blk(fwd(0)), blk(fwd(0)),
                blk(bwd(0)), blk(bwd(0)), blk(bwd(ng)), blk(bwd(ng))]
    args = [q, v, lf, kk, q, v, lf, kk]
    st_spec = pl.BlockSpec((1, 2, hp, DH_C, DH_C), lambda bi, h, c: (bi, 0, h, 0, 0))
    has_init = init is not None
    if has_init:
        in_specs.append(st_spec)
        args.append(init)
    out_specs = [blk(fwd(0)), blk(bwd(0))]
    out_shape = [jax.ShapeDtypeStruct((b, t, W_C), BF16)] * 2
    if out_state:
        out_specs.append(st_spec)
        out_shape.append(jax.ShapeDtypeStruct((b, 2, H_C, DH_C, DH_C), F32))
    return pl.pallas_call(
        functools.partial(_hgrn_kernel, has_init=has_init, out_state=out_state, hp=hp),
        grid=(b, ng, nc),
        in_specs=in_specs,
        out_specs=out_specs,
        out_shape=out_shape,
        scratch_shapes=[pltpu.VMEM((2 * hp, DH_C, DH_C), F32), pltpu.VMEM((2 * hp, L, DH_C), F32)],
        compiler_params=_cp("parallel", "parallel", "arbitrary"),
        name="hgrn_scan",
    )(*args)


def _out_c_kernel(of_ref, ob_ref, og_ref, x_ref, gate_ref, gh_ref, w_ref,
                  g2_ref, sh2_ref, sc2_ref, wr_ref, o_ref, h_ref, aff_ref):
    parts = _row_parts(x_ref.shape[0])
    cats = []
    for rs in parts:
        hm = _head_rms(of_ref[rs, :].astype(F32) + ob_ref[rs, :].astype(F32), gh_ref[...], H_C, DH_C)
        og = og_ref[rs, :].astype(F32)
        heads = [hm[h] * _silu(og[:, h * DH_C:(h + 1) * DH_C]) for h in range(H_C)]
        cats.append(jnp.concatenate(heads, axis=1).astype(BF16))
    _project_and_route(cats, parts, x_ref, gate_ref, w_ref, g2_ref, sh2_ref, sc2_ref, wr_ref, o_ref, h_ref, aff_ref)


def _out_c(of, ob, og, x2, gate, g_hgrn, w_out, route, t, per_batch):
    n = x2.shape[0]
    tm = _row_tile(t)
    row = lambda c: pl.BlockSpec((tm, c), lambda i: (i, 0))
    full = lambda a: pl.BlockSpec(a.shape, lambda i: (0,) * a.ndim)
    r_in, r_out = _route_specs(route, tm, t, per_batch)
    return pl.pallas_call(
        _out_c_kernel,
        grid=(n // tm,),
        in_specs=[row(W_C), row(W_C), row(W_C), row(D_MODEL), _mod_spec(per_batch, tm, t),
                  full(g_hgrn), full(w_out)] + r_in,
        out_specs=r_out,
        out_shape=_route_shapes(n),
        compiler_params=_cp("parallel"),
        name="out_proj_odd",
    )(of, ob, og, x2, gate, g_hgrn, w_out, *route)


def _topk_kernel(aff_ref, pos_ref, s0_ref, bits_s, sel_s, *, cap, tb):
    t = aff_ref.shape[1]
    nb = t // tb
    bits_s[...] = pltpu.bitcast(aff_ref[0], I32)

    def count_ge(cand):
        def body(j, acc):
            r = pl.multiple_of(j * tb, tb)
            blk = bits_s[pl.ds(r, tb), :]
            return acc + jnp.sum((blk >= cand).astype(F32), axis=0, keepdims=True)
        return lax.fori_loop(0, nb, body, jnp.zeros((1, LANES), F32))

    def bit_body(i, thr):
        cand = thr | lax.shift_left(jnp.int32(1), 30 - i)
        return jnp.where(count_ge(cand) >= cap, cand, thr)

    thr = lax.fori_loop(0, 31, bit_body, jnp.zeros((1, LANES), I32))

    def count_gt(j, acc):
        r = pl.multiple_of(j * tb, tb)
        return acc + jnp.sum((bits_s[pl.ds(r, tb), :] > thr).astype(F32), axis=0, keepdims=True)

    need = cap - lax.fori_loop(0, nb, count_gt, jnp.zeros((1, LANES), F32))
    strict = (_iota((tb, tb), 0) > _iota((tb, tb), 1)).astype(BF16)

    def select(j, carry):
        r = pl.multiple_of(j * tb, tb)
        blk = bits_s[pl.ds(r, tb), :]
        eq = blk == thr
        before = carry + _dot(strict, eq.astype(BF16))
        sel = jnp.logical_or(blk > thr, jnp.logical_and(eq, before < need))
        sel_s[pl.ds(r, tb), :] = sel.astype(F32)
        return carry + jnp.sum(eq.astype(F32), axis=0, keepdims=True)

    lax.fori_loop(0, nb, select, jnp.zeros((1, LANES), F32))

    def place(j, carry):
        r = pl.multiple_of(j * tb, tb)
        sel = sel_s[pl.ds(r, tb), :]
        slot = carry + _dot(strict, sel.astype(BF16))
        pos_ref[0, pl.ds(r, tb), :] = jnp.where(sel > 0.5, slot, -1.0).astype(I32)
        s0_ref[0, pl.ds(j, 1), :] = carry.astype(I32)
        return carry + jnp.sum(sel, axis=0, keepdims=True)

    lax.fori_loop(0, nb, place, jnp.zeros((1, LANES), F32))


def _topk(aff, cap, tb):
    b, t, _ = aff.shape
    nb = t // tb
    return pl.pallas_call(
        functools.partial(_topk_kernel, cap=cap, tb=tb),
        grid=(b,),
        in_specs=[pl.BlockSpec((1, t, LANES), lambda bi: (bi, 0, 0))],
        out_specs=[pl.BlockSpec((1, t, LANES), lambda bi: (bi, 0, 0)),
                   pl.BlockSpec((1, nb, LANES), lambda bi: (bi, 0, 0))],
        out_shape=[jax.ShapeDtypeStruct((b, t, LANES), I32), jax.ShapeDtypeStruct((b, nb, LANES), I32)],
        scratch_shapes=[pltpu.VMEM((t, LANES), I32), pltpu.VMEM((t, LANES), F32)],
        compiler_params=_cp("parallel"),
        name="moe_topk",
    )(aff)


MOE_SMALL_COUNT = 64
MOE_STACK_ROWS = 1280


def _windows(cap, tb):
    if cap <= tb + BF16_ROWS:
        return (cap,)
    return (MOE_SMALL_COUNT + BF16_ROWS, tb + BF16_ROWS)


def _window_starts(s0_ref, bi, j, cap, w):
    if w == cap:
        return [0] * N_EXPERTS
    out = []
    for e in range(N_EXPERTS):
        st = jnp.minimum((s0_ref[bi, j, e] // BF16_ROWS) * BF16_ROWS, cap - w)
        out.append(pl.multiple_of(st, BF16_ROWS))
    return out


def _per_window(windows, narrow_ref, bi, j, run):
    if len(windows) == 1:
        run(windows[0])
    else:
        narrow = narrow_ref[bi, j] > 0
        pl.when(narrow)(lambda: run(windows[0]))
        pl.when(jnp.logical_not(narrow))(lambda: run(windows[1]))


def _gather_kernel(s0_ref, narrow_ref, h_ref, post_ref, o_ref, *, cap, tb):
    bi = pl.program_id(0)
    j = pl.program_id(1)

    @pl.when(j == 0)
    def _():
        o_ref[...] = jnp.zeros_like(o_ref)

    def run(w):
        starts = _window_starts(s0_ref, bi, j, cap, w)
        post = post_ref[0]
        r = _iota((w, tb), 0)
        group = max(1, min(N_EXPERTS, MOE_STACK_ROWS // w))
        for e0 in range(0, N_EXPERTS, group):
            es = range(e0, min(e0 + group, N_EXPERTS))
            onehot = jnp.concatenate(
                [jnp.where((post[e:e + 1, :] - starts[e]) == r, 1.0, 0.0).astype(BF16) for e in es],
                axis=0)
            rows = _dot(onehot, h_ref[0]).astype(BF16)
            for k, e in enumerate(es):
                cur = o_ref[0, e, pl.ds(starts[e], w), :]
                o_ref[0, e, pl.ds(starts[e], w), :] = cur + rows[k * w:(k + 1) * w]

    _per_window(_windows(cap, tb), narrow_ref, bi, j, run)


def _gather(h2, post, s0, narrow, cap, tb):
    b, t, d = h2.shape
    nb = t // tb
    grid_spec = pltpu.PrefetchScalarGridSpec(
        num_scalar_prefetch=2,
        grid=(b, nb),
        in_specs=[pl.BlockSpec((1, tb, d), lambda bi, j, s, n: (bi, j, 0)),
                  pl.BlockSpec((1, N_EXPERTS, tb), lambda bi, j, s, n: (bi, 0, j))],
        out_specs=pl.BlockSpec((1, N_EXPERTS, cap, d), lambda bi, j, s, n: (bi, 0, 0, 0)),
    )
    return pl.pallas_call(
        functools.partial(_gather_kernel, cap=cap, tb=tb),
        grid_spec=grid_spec,
        out_shape=jax.ShapeDtypeStruct((b, N_EXPERTS, cap, d), BF16),
        compiler_params=_cp("parallel", "arbitrary"),
        name="moe_gather",
    )(s0, narrow, h2, post)


def _expert_kernel(x_ref, wg_ref, wu_ref, wd_ref, y_ref, wg_s, wu_s, wd_s):
    @pl.when(pl.program_id(1) == 0)
    def _():
        wg_s[...] = wg_ref[0].astype(BF16)
        wu_s[...] = wu_ref[0].astype(BF16)
        wd_s[...] = wd_ref[0].astype(BF16)

    shp = x_ref.shape
    rows = shp[0] * shp[2]
    x = x_ref[...].reshape(rows, shp[3])
    step = rows // max(1, rows // 128)
    xs = [x[r:r + step] for r in range(0, rows, step)]
    hs = [(_silu(_dot(xp, wg_s[...])) * _dot(xp, wu_s[...])).astype(BF16) for xp in xs]
    ys = [_dot(hm, wd_s[...]).astype(BF16) for hm in hs]
    y_ref[...] = jnp.concatenate(ys, axis=0).reshape(shp)


def _experts(xin, w_gate, w_up, w_down, layer):
    b, e, cap, d = xin.shape
    f = w_gate.shape[-1]
    bb = max(1, min(b, 512 // cap))
    xs = pl.BlockSpec((bb, 1, cap, d), lambda ei, bi: (bi, ei, 0, 0))
    ws = lambda a: pl.BlockSpec((None, 1) + a.shape[2:], lambda ei, bi: (layer, ei, 0, 0))
    return pl.pallas_call(
        _expert_kernel,
        grid=(e, b // bb),
        in_specs=[xs, ws(w_gate), ws(w_up), ws(w_down)],
        out_specs=xs,
        out_shape=jax.ShapeDtypeStruct(xin.shape, BF16),
        scratch_shapes=[pltpu.VMEM((d, f), BF16), pltpu.VMEM((d, f), BF16), pltpu.VMEM((f, d), BF16)],
        compiler_params=_cp("parallel", "arbitrary"),
        name="moe_experts",
    )(xin, w_gate, w_up, w_down)


def _combine_kernel(*refs, cap, tb, final):
    windows = _windows(cap, tb)
    s0_ref, narrow_ref, y_ref, pos_ref, aff_ref, x_ref, gate_ref = refs[:7]
    pos_n = 7
    if final:
        gf_ref = refs[pos_n]
        pos_n += 1
    spread_ref, row_id_ref = refs[pos_n:pos_n + 2]
    o_ref = refs[-1]
    bi = pl.program_id(0)
    j = pl.program_id(1)
    offset = 1024
    lane = _iota((1, N_EXPERTS), 1)

    def run(w):
        starts = _window_starts(s0_ref, bi, j, cap, w)
        if w == windows[0]:
            stv = jnp.zeros((1, N_EXPERTS), I32)
            for e in range(N_EXPERTS):
                stv = jnp.where(lane == e, starts[e], stv)
            rel = pos_ref[0] - stv + offset
            spread = spread_ref[...]
            hi = lax.shift_right_logical(rel, 5).astype(F32).astype(BF16)
            lo = jnp.bitwise_and(rel, 31).astype(F32).astype(BF16)
            relx = 32.0 * _dot(hi, spread) + _dot(lo, spread) - offset
            gx = _dot(aff_ref[0].astype(BF16), spread)
            g = jnp.where(relx == row_id_ref[...], gx, 0.0)
            ys = jnp.concatenate([y_ref[0, e, pl.ds(starts[e], w), :] for e in range(N_EXPERTS)], axis=0)
            acc = _dot(g.astype(BF16), ys)
        else:
            pos = pos_ref[0]
            aff = aff_ref[0].astype(BF16).astype(F32)
            col = _iota((tb, w), 1)
            acc = jnp.zeros((tb, D_MODEL), F32)
            for e in range(N_EXPERTS):
                g = jnp.where((pos[:, e:e + 1] - starts[e]) == col, aff[:, e:e + 1], 0.0)
                acc = acc + _dot(g.astype(BF16), y_ref[0, e, pl.ds(starts[e], w), :])
        out = x_ref[0] + gate_ref[0] * acc
        if final:
            out = _rms_rows(out, gf_ref[...])
        o_ref[0] = out

    _per_window(windows, narrow_ref, bi, j, run)


def _combine(y, pos, aff, x3, gate, s0, narrow, cap, tb, per_batch, g_final):
    b, t, d = x3.shape
    nb = t // tb
    final = g_final is not None
    tok = lambda c: pl.BlockSpec((1, tb, c), lambda bi, j, s, n: (bi, j, 0))
    const = lambda a: pl.BlockSpec(a.shape, lambda bi, j, s, n: (0, 0))
    in_specs = [pl.BlockSpec((1, N_EXPERTS, cap, d), lambda bi, j, s, n: (bi, 0, 0, 0)),
                tok(N_EXPERTS), tok(N_EXPERTS), tok(d),
                pl.BlockSpec((1, 1, d), (lambda bi, j, s, n: (bi, 0, 0)) if per_batch
                             else (lambda bi, j, s, n: (0, 0, 0)))]
    args = [y, pos, aff, x3, gate]
    if final:
        in_specs.append(const(g_final))
        args.append(g_final)
    w = _windows(cap, tb)[0]
    spread = jnp.repeat(jnp.eye(N_EXPERTS, dtype=BF16), w, axis=1)
    row_id = jnp.tile(jnp.arange(w, dtype=F32), N_EXPERTS).reshape(1, -1)
    in_specs += [const(spread), const(row_id)]
    args += [spread, row_id]
    grid_spec = pltpu.PrefetchScalarGridSpec(
        num_scalar_prefetch=2, grid=(b, nb), in_specs=in_specs, out_specs=tok(d))
    return pl.pallas_call(
        functools.partial(_combine_kernel, cap=cap, tb=tb, final=final),
        grid_spec=grid_spec,
        out_shape=jax.ShapeDtypeStruct((b, t, d), F32),
        compiler_params=_cp("parallel", "arbitrary"),
        name="moe_combine",
    )(s0, narrow, *args)


def _moe(x2, h2, aff, b, t, gate, w_gate, w_up, w_down, layer, per_batch, g_final):
    cap = CAPACITY_FACTOR * t // N_EXPERTS
    tb = min(MOE_TOKEN_BLOCK, t)
    aff3 = aff.reshape(b, t, N_EXPERTS)
    per = LANES // N_EXPERTS
    groups = -(-b // per)
    nb = t // tb

    def pack(a):
        a = jnp.pad(a, ((0, groups * per - b), (0, 0), (0, 0)))
        return a.reshape(groups, per, a.shape[1], N_EXPERTS).transpose(0, 2, 1, 3).reshape(groups, a.shape[1], LANES)

    def unpack(a):
        return a.reshape(groups, a.shape[1], per, N_EXPERTS).transpose(0, 2, 1, 3).reshape(
            groups * per, a.shape[1], N_EXPERTS)[:b]

    pos, s0 = _topk(pack(aff3), cap, tb)
    pos, s0 = unpack(pos), unpack(s0)
    counts = jnp.diff(s0, axis=1, append=jnp.full((b, 1, N_EXPERTS), cap, I32))
    narrow = jnp.all(counts <= MOE_SMALL_COUNT, axis=-1).astype(I32)
    xin = _gather(h2.reshape(b, t, D_MODEL), pos.transpose(0, 2, 1), s0, narrow, cap, tb)
    y = _experts(xin, w_gate, w_up, w_down, layer)
    out = _combine(y, pos, aff3, x2.reshape(b, t, D_MODEL), gate, s0, narrow, cap, tb, per_batch, g_final)
    return out.reshape(b * t, D_MODEL)


def _rope_tables(t):
    rows = t // GRID_W
    row = jnp.repeat(jnp.arange(rows, dtype=F32), GRID_W)
    col = jnp.tile(jnp.arange(GRID_W, dtype=F32), rows)
    n_freq = DQK_A // 4
    inv_freq = jnp.power(ROPE_BASE, -jnp.arange(n_freq, dtype=F32) / n_freq)
    ang = jnp.concatenate([row[:, None] * inv_freq, col[:, None] * inv_freq], axis=-1)
    cos, sin = jnp.cos(ang), jnp.sin(ang)
    zero = jnp.zeros_like(sin)
    reps = LANES // DQK_A
    cos_t = jnp.tile(jnp.concatenate([cos, cos], axis=-1), (1, reps))
    sin_a = jnp.tile(jnp.concatenate([-sin, zero], axis=-1), (1, reps))
    sin_b = jnp.tile(jnp.concatenate([zero, sin], axis=-1), (1, reps))
    return cos_t, sin_a, sin_b


def _stream(x, mods, per_batch, rope, cache, p):
    b, t, d = x.shape
    x2 = x.reshape(b * t, d)
    depth = p["g_norm"].shape[0]
    even_out, odd_out = [], []
    for l in range(depth):
        j = l // 2
        sh1, sc1, gt1, sh2, sc2, gt2 = mods[l]
        g1 = p["g_norm"][l, 0].reshape(1, d)
        g2 = p["g_norm"][l, 1].reshape(1, d)
        route = (g2, sh2, sc2, p["w_router_pad"][l])
        if l % 2 == 0:
            aq, ak, av, bqk, bv, bo, gates = _in_ab(x2, g1, sh1, sc1, p["w_in_ab_main"][j], p["w_in_ab_gate"][j],
                                                     rope, t, per_batch, F32 if cache is None else BF16)
            r3 = lambda a: a.reshape(b, t, a.shape[-1])
            if cache is not None:
                kc = cache[0][:, j].reshape(b, -1, W_A)
                vc = cache[1][:, j].reshape(b, -1, W_A)
                nu = 2 * H_B
                init = (cache[2][:, j].reshape(b, nu, DH_B, DH_B),
                        cache[3][:, j].reshape(b, nu, 1, DH_B),
                        jnp.broadcast_to(cache[4][:, j].reshape(b, nu, 1, 1), (b, nu, 1, LANES)))
            else:
                kc = vc = init = None
            lam_init = 0.8 - 0.6 * math.exp(-0.3 * l)
            attn = _attention_pipe(r3(aq), r3(ak), r3(av), kc, vc, p["lam_ab"][j], lam_init)
            qk = _conv_silu(r3(bqk), p["conv_ab"][j])
            bias = jnp.pad(jnp.concatenate([p["b_ig_ab"][j].reshape(-1), p["b_fg_ab"][j].reshape(-1)]),
                           (0, LANES - 4 * H_B)).reshape(1, LANES)
            res = _mlstm(qk, r3(bv), r3(gates), bias, init, cache is None)
            hf, hb = res[0], res[1]
            if cache is None:
                even_out.append((ak, av, res[2], res[3], res[4]))
            x2, h2, aff = _out_ab(attn.reshape(b * t, W_A), hf.reshape(b * t, W_B), hb.reshape(b * t, W_B), bo, x2,
                                  gt1, p["g_attn_ab"][j].reshape(1, W_A), p["g_mlstm_ab"][j].reshape(1, W_B),
                                  p["w_out_ab_bf16"][j], route, t, per_batch, 1.0 - lam_init)
        else:
            q, v, og, lf, kk = _in_c(x2, g1, sh1, sc1, p["w_in_c_bf16"][j], p["b_f_c"][j], p["gamma_lb"], l, t,
                                     per_batch)
            r3 = lambda a: a.reshape(b, t, a.shape[-1])
            init = None if cache is None else cache[5][:, j]
            res = _hgrn(r3(q), r3(v), r3(lf), r3(kk), init, cache is None)
            if cache is None:
                odd_out.append(res[2])
            x2, h2, aff = _out_c(res[0].reshape(b * t, W_C), res[1].reshape(b * t, W_C), og, x2, gt1,
                                 p["g_hgrn_c"][j].reshape(1, W_C), p["w_out_c_bf16"][j], route, t, per_batch)
        g_final = p["g_final"].reshape(1, d) if l == depth - 1 else None
        x2 = _moe(x2, h2, aff, b, t, gt2, p["w_gate_e"], p["w_up_e"], p["w_down_e"], l, per_batch, g_final)
    return x2.reshape(b, t, d), even_out, odd_out


def kernel(x_prompt, x_sample, cache_dattn_k, cache_dattn_v, state_mlstm_C, state_mlstm_n, state_mlstm_m,
           state_hgrn_S, c, c_ctx, w_ada, b_ada, g_norm, g_final, w_in_ab, b_ig_ab, b_fg_ab, lam_ab, conv_ab,
           g_attn_ab, g_mlstm_ab, w_out_ab, w_in_c, b_f_c, gamma_lb, g_hgrn_c, w_out_c,
           w_router, w_gate_e, w_up_e, w_down_e):
    d = D_MODEL
    bs = c.shape[0]
    bp, tp, _ = x_prompt.shape
    depth = w_ada.shape[0]
    n_main = 3 * W_A + 4 * W_B
    p = dict(
        g_norm=g_norm, g_final=g_final, b_ig_ab=b_ig_ab, b_fg_ab=b_fg_ab, lam_ab=lam_ab, conv_ab=conv_ab,
        g_attn_ab=g_attn_ab, g_mlstm_ab=g_mlstm_ab, b_f_c=b_f_c, gamma_lb=gamma_lb, g_hgrn_c=g_hgrn_c,
        w_gate_e=w_gate_e, w_up_e=w_up_e, w_down_e=w_down_e,
        w_in_ab_main=w_in_ab[:, :, :n_main].astype(BF16),
        w_in_ab_gate=jnp.pad(w_in_ab[:, :, n_main:], ((0, 0), (0, 0), (0, LANES - 4 * H_B))).astype(BF16),
        w_out_ab_bf16=w_out_ab.astype(BF16),
        w_in_c_bf16=w_in_c.astype(BF16),
        w_out_c_bf16=w_out_c.astype(BF16),
        w_router_pad=jnp.pad(w_router, ((0, 0), (0, 0), (0, LANES - N_EXPERTS))),
    )
    rows = -(-(bs + 1) // 8) * 8
    cond = jnp.concatenate([c, c_ctx.reshape(1, d), jnp.zeros((rows - bs - 1, d), F32)], axis=0)
    mods = _ada(cond, w_ada, b_ada)
    mods_s = [[mods[l, :bs, i * d:(i + 1) * d].reshape(bs, 1, d) for i in range(6)] for l in range(depth)]
    mods_c = [[mods[l, bs:bs + 1, i * d:(i + 1) * d].reshape(1, 1, d) for i in range(6)] for l in range(depth)]

    y_prompt, ev, od = _stream(x_prompt, mods_c, False, None, None, p)
    new_k = jnp.stack([e[0].reshape(bp, tp, H_A, 2 * DQK_A) for e in ev], axis=1)
    new_v = jnp.stack([e[1].reshape(bp, tp, H_A, DV_A) for e in ev], axis=1)
    new_c = jnp.stack([e[2].reshape(bp, 2, H_B, DH_B, DH_B) for e in ev], axis=1)
    new_n = jnp.stack([e[3].reshape(bp, 2, H_B, DH_B) for e in ev], axis=1)
    new_m = jnp.stack([e[4][..., 0].reshape(bp, 2, H_B) for e in ev], axis=1)
    new_s = jnp.stack(od, axis=1)

    cache = (cache_dattn_k, cache_dattn_v, state_mlstm_C, state_mlstm_n, state_mlstm_m, state_hgrn_S)
    y_sample, _, _ = _stream(x_sample, mods_s, True, _rope_tables(x_sample.shape[1]), cache, p)
    return (y_prompt, y_sample, new_k, new_v, new_c, new_n, new_m, new_s)
```

```python
import functools
import math

import jax
import jax.numpy as jnp
from jax import lax
from jax.experimental import pallas as pl
from jax.experimental.pallas import tpu as pltpu

F32 = jnp.float32
BF16 = jnp.bfloat16
I32 = jnp.int32

D_MODEL = 1024
H_A = 4
DV_A = 128
DQK_A = 64
W_A = H_A * DV_A
H_B = 4
DH_B = 128
W_B = H_B * DH_B
H_C = 8
DH_C = 128
W_C = H_C * DH_C
N_EXPERTS = 16
CAPACITY_FACTOR = 2
GRID_W = 64
ROPE_BASE = 10000.0
RMS_EPS = 1e-6
NEG_BIG = -1e30
LOG2E = 1.4426950408889634
LANES = 128
BF16_ROWS = 16
SCAN_CHUNK = 128
MOE_TOKEN_BLOCK = 256
VMEM_LIMIT = 56 * 1024 * 1024


def _cp(*sem):
    return pltpu.CompilerParams(dimension_semantics=sem, vmem_limit_bytes=VMEM_LIMIT)


def _dot(a, b):
    return jnp.dot(a, b, preferred_element_type=F32)


def _dot_nt(a, b):
    return lax.dot_general(a, b, (((1,), (1,)), ((), ())), preferred_element_type=F32)


def _dot_tn(a, b):
    return lax.dot_general(a, b, (((0,), (0,)), ((), ())), preferred_element_type=F32)


def _split3(x):
    hi = x.astype(BF16)
    r = x - hi.astype(F32)
    mid = r.astype(BF16)
    lo = (r - mid.astype(F32)).astype(BF16)
    return hi, mid, lo


def _dot_sel_left(sel, x):
    hi, mid, lo = _split3(x)
    return _dot(sel, hi) + _dot(sel, mid) + _dot(sel, lo)


def _dot_sel_right(x, sel):
    hi, mid, lo = _split3(x)
    return _dot(hi, sel) + _dot(mid, sel) + _dot(lo, sel)


def _dot_f32(a, b):
    ah, am, al = _split3(a)
    bh, bm, bl = _split3(b)
    return (_dot(ah, bh) + (_dot(ah, bm) + _dot(am, bh))
            + (_dot(ah, bl) + _dot(am, bm) + _dot(al, bh)))


def _dot_f32x3(a, b):
    ah = a.astype(BF16)
    al = (a - ah.astype(F32)).astype(BF16)
    bh = b.astype(BF16)
    bl = (b - bh.astype(F32)).astype(BF16)
    return _dot(ah, bh) + (_dot(ah, bl) + _dot(al, bh))


def _dot_sel_left2(sel, x):
    hi = x.astype(BF16)
    lo = (x - hi.astype(F32)).astype(BF16)
    return _dot(sel, hi) + _dot(sel, lo)


def _sigmoid(x):
    return 1.0 / (1.0 + jnp.exp(-x))


def _silu(x):
    return x * _sigmoid(x)


def _log_sigmoid(x):
    return jnp.minimum(x, 0.0) - jnp.log(1.0 + jnp.exp(-jnp.abs(x)))


def _rms_rows(x, g):
    return x * lax.rsqrt(jnp.mean(x * x, axis=-1, keepdims=True) + RMS_EPS) * g


def _iota(shape, dim):
    return lax.broadcasted_iota(I32, shape, dim)


def _ada_kernel(c_ref, w_ref, b_ref, o_ref):
    c = c_ref[...]
    o_ref[0] = _dot_f32(_silu(c), w_ref[0]) + b_ref[0]


def _ada(cond_rows, w_ada, b_ada):
    depth, d, n6 = w_ada.shape
    rows = cond_rows.shape[0]
    tn = 1536
    return pl.pallas_call(
        _ada_kernel,
        grid=(depth, n6 // tn),
        in_specs=[pl.BlockSpec((rows, d), lambda l, j: (0, 0)),
                  pl.BlockSpec((1, d, tn), lambda l, j: (l, 0, j)),
                  pl.BlockSpec((1, 1, tn), lambda l, j: (l, 0, j))],
        out_specs=pl.BlockSpec((1, rows, tn), lambda l, j: (l, 0, j)),
        out_shape=jax.ShapeDtypeStruct((depth, rows, n6), F32),
        compiler_params=_cp("parallel", "parallel"),
        name="adaln",
    )(cond_rows, w_ada, b_ada.reshape(depth, 1, n6))


def _mod_spec(per_batch, tm, t):
    if per_batch:
        return pl.BlockSpec((1, 1, D_MODEL), lambda i: (i * tm // t, 0, 0))
    return pl.BlockSpec((1, 1, D_MODEL), lambda i: (0, 0, 0))


def _row_tile(t):
    return min(512, t)


def _in_ab_kernel(*refs, use_rope):
    if use_rope:
        (x_ref, g_ref, sh_ref, sc_ref, w_ref, wg_ref, cos_ref, sa_ref, sb_ref,
         aq_ref, ak_ref, av_ref, bqk_ref, bv_ref, bo_ref, gt_ref) = refs
    else:
        (x_ref, g_ref, sh_ref, sc_ref, w_ref, wg_ref,
         aq_ref, ak_ref, av_ref, bqk_ref, bv_ref, bo_ref, gt_ref) = refs
    h = _rms_rows(x_ref[...], g_ref[...]) * (1.0 + sc_ref[0]) + sh_ref[0]
    hb = h.astype(BF16)
    y = _dot(hb, w_ref[...])
    gt_ref[...] = _dot(hb, wg_ref[...])
    aq = y[:, 0:W_A] * (DQK_A ** -0.5 * LOG2E)
    ak = y[:, W_A:2 * W_A]
    if use_rope:
        cos = jnp.concatenate([cos_ref[...]] * H_A, axis=1)
        sa = jnp.concatenate([sa_ref[...]] * H_A, axis=1)
        sb = jnp.concatenate([sb_ref[...]] * H_A, axis=1)
        half = DQK_A // 2

        def rope(v):
            return (v * cos + pltpu.roll(v, W_A - half, axis=1) * sa
                    + pltpu.roll(v, half, axis=1) * sb)

        aq = rope(aq)
        ak = rope(ak)
    aq_ref[...] = aq.astype(aq_ref.dtype)
    ak_ref[...] = ak.astype(ak_ref.dtype)
    av_ref[...] = y[:, 2 * W_A:3 * W_A].astype(av_ref.dtype)
    bqk_ref[...] = y[:, 3 * W_A:3 * W_A + 2 * W_B]
    bv_ref[...] = y[:, 3 * W_A + 2 * W_B:3 * W_A + 3 * W_B].astype(BF16)
    bo_ref[...] = y[:, 3 * W_A + 3 * W_B:3 * W_A + 4 * W_B].astype(BF16)


def _in_ab(x2, g, shift, scale, w_main, w_gate, rope, t, per_batch, qkv_dtype):
    n = x2.shape[0]
    tm = _row_tile(t)
    use_rope = rope is not None
    row = lambda c: pl.BlockSpec((tm, c), lambda i: (i, 0))
    full = lambda a: pl.BlockSpec(a.shape, lambda i: (0,) * a.ndim)
    in_specs = [row(D_MODEL), full(g), _mod_spec(per_batch, tm, t), _mod_spec(per_batch, tm, t),
                full(w_main), full(w_gate)]
    args = [x2, g, shift, scale, w_main, w_gate]
    if use_rope:
        nt = t // tm
        tab = pl.BlockSpec((tm, LANES), lambda i: (i % nt, 0))
        in_specs += [tab, tab, tab]
        args += list(rope)
    widths = (W_A, W_A, W_A, 2 * W_B, W_B, W_B, LANES)
    return pl.pallas_call(
        functools.partial(_in_ab_kernel, use_rope=use_rope),
        grid=(n // tm,),
        in_specs=in_specs,
        out_specs=[row(c) for c in widths],
        out_shape=[jax.ShapeDtypeStruct((n, c), dt) for c, dt in
                   zip(widths, (qkv_dtype, qkv_dtype, qkv_dtype, F32, BF16, BF16, F32))],
        compiler_params=_cp("parallel"),
        name="in_proj_even",
    )(*args)


def _attn_pipe_kernel(*refs, has_cache, tk, lam_init, nq):
    if has_cache:
        q_ref, k_ref, v_ref, kc_ref, vc_ref, lam_ref, o_ref, s_a, s_b, mp_s, vt_s, sc_s, vct_s = refs
    else:
        q_ref, k_ref, v_ref, lam_ref, o_ref, s_a, s_b, mp_s, vt_s = refs
    s_s = (s_a, s_b)
    i = pl.program_id(2)
    q = q_ref[0]
    tq = q.shape[0]
    first = _iota((1, LANES), 1) < DQK_A
    qs = (jnp.where(first, q, 0).astype(BF16), jnp.where(first, 0, q).astype(BF16))
    nkb = k_ref.shape[1] // tk

    @pl.when(i == 0)
    def _():
        for j in range(nkb):
            vt_s[j, 0:DV_A, :] = v_ref[0, j * tk:(j + 1) * tk, :].astype(F32).T.astype(BF16)
            vt_s[j, DV_A:, :] = jnp.ones((BF16_ROWS, tk), BF16)
        if has_cache:
            vct_s[0:DV_A, :] = vc_ref[0].astype(F32).T.astype(BF16)
            vct_s[DV_A:, :] = jnp.ones((BF16_ROWS, vct_s.shape[1]), BF16)

    def groups(x):
        return [x[:, g * LANES:(g + 1) * LANES] for g in range(x.shape[1] // LANES)]

    def body(score, finish, cur):
        prev = 1 - cur
        if finish:
            m = [jnp.max(mp_s[prev, a], axis=-1, keepdims=True) for a in range(2)]
        mpart = [jnp.full((tq, LANES), NEG_BIG, F32)] * 2
        acc = [jnp.zeros((DV_A + BF16_ROWS, tq), F32)] * 2

        def pass1(kb, store):
            for a in range(2):
                s = _dot_nt(qs[a], kb)
                store(a, s)
                mpart[a] = functools.reduce(jnp.maximum, groups(s), mpart[a])

        def pass2(a, s, vt):
            acc[a] = acc[a] + _dot_nt(vt, jnp.exp2(s - m[a]).astype(BF16))

        if has_cache:
            def store_c(a, s):
                sc_s[cur, a] = s
            if score:
                pass1(kc_ref[0].astype(BF16), store_c)
            if finish:
                for a in range(2):
                    pass2(a, sc_s[prev, a], vct_s[...])
        for j in range(nkb):
            def store_j(a, s, j=j):
                s_s[cur][a, j, :, 0:tk] = s
            if score:
                pass1(k_ref[0, j * tk:(j + 1) * tk, :].astype(BF16), store_j)
            if finish:
                for a in range(2):
                    pass2(a, s_s[prev][a, j, :, 0:tk], vt_s[j])
        if score:
            for a in range(2):
                mp_s[cur, a] = mpart[a]
        if finish:
            lp = lam_ref[...]
            lam = (jnp.exp(jnp.sum(lp[0:1] * lp[1:2], axis=-1, keepdims=True))
                   - jnp.exp(jnp.sum(lp[2:3] * lp[3:4], axis=-1, keepdims=True)) + lam_init)
            o1 = acc[0][0:DV_A] / acc[0][DV_A:DV_A + 1]
            o2 = acc[1][0:DV_A] / acc[1][DV_A:DV_A + 1]
            o_ref[0] = (o1 - lam * o2).T.astype(BF16)

    pl.when(i == 0)(lambda: body(True, False, 0))
    inner = jnp.logical_and(i > 0, i < nq)
    pl.when(jnp.logical_and(inner, i % 2 == 0))(lambda: body(True, True, 0))
    pl.when(jnp.logical_and(inner, i % 2 == 1))(lambda: body(True, True, 1))
    pl.when(i == nq)(lambda: body(False, True, nq % 2))


def _attention_pipe(aq, ak, av, kc, vc, lam_p, lam_init):
    b, t, _ = aq.shape
    tq = min(256, t)
    tk = min(512, t)
    nq = t // tq
    has_cache = kc is not None
    qspec = pl.BlockSpec((1, tq, DV_A), lambda bi, h, i: (bi, jnp.minimum(i, nq - 1), h))
    ospec = pl.BlockSpec((1, tq, DV_A), lambda bi, h, i: (bi, jnp.maximum(i - 1, 0), h))
    ks = pl.BlockSpec((1, t, DV_A), lambda bi, h, i: (bi, 0, h))
    in_specs = [qspec, ks, ks]
    args = [aq, ak, av]
    if has_cache:
        cs = pl.BlockSpec((1, kc.shape[1], DV_A), lambda bi, h, i: (bi, 0, h))
        in_specs += [cs, cs]
        args += [kc, vc]
    in_specs.append(pl.BlockSpec(lam_p.shape, lambda bi, h, i: (0, 0)))
    args.append(lam_p)
    scratch = [pltpu.VMEM((2, t // tk, tq, tk), F32), pltpu.VMEM((2, t // tk, tq, tk + LANES), F32),
               pltpu.VMEM((2, 2, tq, LANES), F32),
               pltpu.VMEM((t // tk, DV_A + BF16_ROWS, tk), BF16)]
    if has_cache:
        scratch += [pltpu.VMEM((2, 2, tq, kc.shape[1]), F32),
                    pltpu.VMEM((DV_A + BF16_ROWS, kc.shape[1]), BF16)]
    return pl.pallas_call(
        functools.partial(_attn_pipe_kernel, has_cache=has_cache, tk=tk, lam_init=lam_init, nq=nq),
        grid=(b, H_A, nq + 1),
        in_specs=in_specs,
        out_specs=ospec,
        out_shape=jax.ShapeDtypeStruct((b, t, W_A), BF16),
        scratch_shapes=scratch,
        compiler_params=_cp("parallel", "parallel", "arbitrary"),
        name="diff_attention",
    )(*args)


def _conv_kernel(x_ref, w_ref, o_ref):
    x = x_ref[0]
    t = x.shape[0]
    w = w_ref[...]
    r = _iota((t, 1), 0)
    prev = jnp.where(r == 0, 0.0, pltpu.roll(x, 1, axis=0))
    nxt = jnp.where(r == t - 1, 0.0, pltpu.roll(x, t - 1, axis=0))
    scale = jnp.where(pl.program_id(1) * x.shape[1] >= W_B, DH_B ** -0.5, 1.0)
    o_ref[0] = (_silu(prev * w[0:1] + x * w[1:2] + nxt * w[2:3]) * scale).astype(BF16)


def _conv_silu(bqk, conv_w):
    b, t, c = bqk.shape
    tc = 256
    spec = pl.BlockSpec((1, t, tc), lambda bi, j: (bi, 0, j))
    return pl.pallas_call(
        _conv_kernel,
        grid=(b, c // tc),
        in_specs=[spec, pl.BlockSpec((conv_w.shape[0], tc), lambda bi, j: (0, j))],
        out_specs=spec,
        out_shape=jax.ShapeDtypeStruct(bqk.shape, BF16),
        compiler_params=_cp("parallel", "parallel"),
        name="mlstm_conv",
    )(bqk, conv_w)


def _tri(n, lower):
    r = _iota((n, n), 0)
    c = _iota((n, n), 1)
    return r >= c if lower else r <= c


def _mlstm_kernel(*refs, has_init, out_state):
    qf, kf, vf, gf, qb, kb, vb, gb, bias = refs[:9]
    pos = 9
    if has_init:
        c0, n0, m0 = refs[pos:pos + 3]
        pos += 3
    hf, hb = refs[pos:pos + 2]
    pos += 2
    if out_state:
        c_out, n_out, m_out = refs[pos:pos + 3]
        pos += 3
    cn_s, m_s = refs[pos:pos + 2]
    step = pl.program_id(1)
    L = SCAN_CHUNK
    nu = 2 * H_B

    @pl.when(step == 0)
    def _():
        if has_init:
            for u in range(nu):
                cn_s[u, :, 0:DH_B] = c0[0, u]
                cn_s[u, :, DH_B:2 * DH_B] = jnp.broadcast_to(n0[0, u], (DH_B, DH_B)).T
            m_s[...] = m0[0]
        else:
            cn_s[...] = jnp.zeros_like(cn_s)
            m_s[...] = jnp.zeros_like(m_s)

    ones_b = jnp.ones((L, DH_B), BF16)
    sel_rows = _iota((LANES, LANES), 0)
    lower = _tri(L, True)
    upper = _tri(L, False)
    lower_b = lower.astype(BF16)
    upper_b = upper.astype(BF16)
    col = _iota((1, LANES), 1)
    units = []
    for d, (q_ref, k_ref, v_ref, g_ref, h_ref) in enumerate(((qf, kf, vf, gf, hf), (qb, kb, vb, gb, hb))):
        g = g_ref[0] + bias[...]
        g = jnp.where(col < 2 * H_B, g, _log_sigmoid(g))
        gt = g.T
        if d == 0:
            bc_all = _dot_sel_left(lower_b, g)
            br_all = _dot_sel_right(gt, upper_b)
            mask = lower
        else:
            bc_all = _dot_sel_left(upper_b, g)
            br_all = _dot_sel_right(gt, lower_b)
            mask = upper
        for h in range(H_B):
            u = d * H_B + h
            sl = slice(h * DH_B, (h + 1) * DH_B)
            units.append(dict(
                u=u, d=d, sl=sl, h_ref=h_ref, mask=mask, g=g, bc_all=bc_all,
                li_row=gt[u:u + 1, :], b_row=br_all[2 * H_B + u:2 * H_B + u + 1, :],
                q=q_ref[0, :, sl].astype(BF16), k=k_ref[0, :, sl].astype(F32),
                v1=jnp.concatenate([v_ref[0, :, sl].astype(BF16), ones_b], axis=1)))
    for x in units:
        x["li_b"] = _dot_sel_right(x["g"], (sel_rows == x["u"]).astype(BF16))
        x["b_b"] = _dot_sel_right(x["bc_all"], (sel_rows == 2 * H_B + x["u"]).astype(BF16))
        x["qk"] = _dot_nt(x["q"], x["k"].astype(BF16))
        x["cn"] = cn_s[x["u"]]
        x["qc"] = _dot(x["q"], x["cn"].astype(BF16))
    for x in units:
        x["m_prev"] = m_s[x["u"]]
        a_b = x["b_b"] + x["m_prev"]
        dm = jnp.where(x["mask"], x["b_b"] - x["b_row"] + x["li_row"], NEG_BIG)
        x["m_t"] = jnp.maximum(a_b, jnp.max(dm, axis=-1, keepdims=True))
        x["w_inter"] = jnp.exp(a_b - x["m_t"])
        x["s"] = (x["qk"] * jnp.exp(dm - x["m_t"])).astype(BF16)
    for x in units:
        x["sv"] = _dot(x["s"], x["v1"])
        b_end = x["b_b"][L - 1:L, :] if x["d"] == 0 else x["b_b"][0:1, :]
        g_b = b_end - x["b_b"] + x["li_b"]
        x["m_new"] = jnp.maximum(b_end + x["m_prev"], jnp.max(g_b, axis=0, keepdims=True))
        x["w_old"] = jnp.exp(b_end + x["m_prev"] - x["m_new"])
        x["ks"] = (jnp.exp(g_b - x["m_new"]) * x["k"]).astype(BF16)
    for x in units:
        num = x["w_inter"] * x["qc"][:, 0:DH_B] + x["sv"][:, 0:DH_B]
        den = x["w_inter"] * x["qc"][:, DH_B:] + x["sv"][:, DH_B:]
        x["h_ref"][0, :, x["sl"]] = (num / jnp.maximum(jnp.abs(den), jnp.exp(-x["m_t"]))).astype(BF16)
        w2 = jnp.concatenate([x["w_old"], x["w_old"]], axis=1)
        cn_s[x["u"]] = w2 * x["cn"] + _dot_tn(x["ks"], x["v1"])
        m_s[x["u"]] = x["m_new"]

    if out_state:
        @pl.when(step == pl.num_programs(1) - 1)
        def _():
            for u in range(nu):
                c_out[0, u] = cn_s[u, :, 0:DH_B]
                n_out[0, u] = cn_s[u, :, DH_B:2 * DH_B].T[0:1, :]
            m_out[0] = m_s[...]


def _mlstm(qk, v, gates, bias, init, out_state):
    b, t, _ = v.shape
    L = SCAN_CHUNK
    nc = t // L
    nu = 2 * H_B
    fwd = lambda j: (lambda bi, c: (bi, c, j))
    bwd = lambda j: (lambda bi, c: (bi, nc - 1 - c, j))
    blk = lambda w, im: pl.BlockSpec((1, L, w), im)
    in_specs = [blk(W_B, fwd(0)), blk(W_B, fwd(1)), blk(W_B, fwd(0)), blk(LANES, fwd(0)),
                blk(W_B, bwd(0)), blk(W_B, bwd(1)), blk(W_B, bwd(0)), blk(LANES, bwd(0)),
                pl.BlockSpec((1, LANES), lambda bi, c: (0, 0))]
    args = [qk, qk, v, gates, qk, qk, v, gates, bias]
    st_specs = [pl.BlockSpec((1, nu, DH_B, DH_B), lambda bi, c: (bi, 0, 0, 0)),
                pl.BlockSpec((1, nu, 1, DH_B), lambda bi, c: (bi, 0, 0, 0)),
                pl.BlockSpec((1, nu, 1, LANES), lambda bi, c: (bi, 0, 0, 0))]
    st_shapes = [jax.ShapeDtypeStruct((b, nu, DH_B, DH_B), F32),
                 jax.ShapeDtypeStruct((b, nu, 1, DH_B), F32),
                 jax.ShapeDtypeStruct((b, nu, 1, LANES), F32)]
    has_init = init is not None
    if has_init:
        in_specs += st_specs
        args += list(init)
    out_specs = [blk(W_B, fwd(0)), blk(W_B, bwd(0))]
    out_shape = [jax.ShapeDtypeStruct((b, t, W_B), BF16)] * 2
    if out_state:
        out_specs += st_specs
        out_shape += st_shapes
    return pl.pallas_call(
        functools.partial(_mlstm_kernel, has_init=has_init, out_state=out_state),
        grid=(b, nc),
        in_specs=in_specs,
        out_specs=out_specs,
        out_shape=out_shape,
        scratch_shapes=[pltpu.VMEM((nu, DH_B, 2 * DH_B), F32), pltpu.VMEM((nu, 1, LANES), F32)],
        compiler_params=_cp("parallel", "arbitrary"),
        name="mlstm_scan",
    )(*args)


def _head_rms(x, g, heads, width):
    parts = []
    for h in range(heads):
        sl = slice(h * width, (h + 1) * width)
        parts.append(_rms_rows(x[:, sl], g[:, sl]))
    return parts


OUT_PROJ_PARTS = 4


def _row_parts(tm):
    step = tm // OUT_PROJ_PARTS
    return [slice(r, r + step) for r in range(0, tm, step)]


def _project_and_route(cats, parts, x_ref, gate_ref, w_ref, g_ref, sh_ref, sc_ref, wr_ref, o_ref, h_ref, aff_ref):
    w = w_ref[...]
    xs = [x_ref[rs, :] + gate_ref[0] * _dot(cat, w) for cat, rs in zip(cats, parts)]
    hs = []
    for x, rs in zip(xs, parts):
        o_ref[rs, :] = x
        h = _rms_rows(x, g_ref[...]) * (1.0 + sc_ref[0]) + sh_ref[0]
        h_ref[rs, :] = h.astype(BF16)
        hs.append(h)
    wr = wr_ref[...]
    logits = [_dot_f32x3(h, wr) for h in hs]
    valid = _iota((1, LANES), 1) < N_EXPERTS
    for lg, rs in zip(logits, parts):
        lg = jnp.where(valid, lg, NEG_BIG)
        e = jnp.exp(lg - jnp.max(lg, axis=-1, keepdims=True))
        e = jnp.where(valid, e, 0.0)
        aff_ref[rs, :] = (e / jnp.sum(e, axis=-1, keepdims=True))[:, :N_EXPERTS]


def _route_specs(route, tm, t, per_batch):
    g2, sh2, sc2, wr = route
    full = lambda a: pl.BlockSpec(a.shape, lambda i: (0,) * a.ndim)
    row = lambda c: pl.BlockSpec((tm, c), lambda i: (i, 0))
    in_specs = [full(g2), _mod_spec(per_batch, tm, t), _mod_spec(per_batch, tm, t), full(wr)]
    out_specs = [row(D_MODEL), row(D_MODEL), row(N_EXPERTS)]
    return in_specs, out_specs


def _route_shapes(n):
    return [jax.ShapeDtypeStruct((n, D_MODEL), F32), jax.ShapeDtypeStruct((n, D_MODEL), BF16),
            jax.ShapeDtypeStruct((n, N_EXPERTS), F32)]


def _out_ab_kernel(at_ref, hf_ref, hb_ref, bo_ref, x_ref, gate_ref, ga_ref, gm_ref, w_ref,
                   g2_ref, sh2_ref, sc2_ref, wr_ref, o_ref, h_ref, aff_ref, *, out_scale):
    parts = _row_parts(x_ref.shape[0])
    cats = []
    for rs in parts:
        oa = [p * out_scale for p in _head_rms(at_ref[rs, :].astype(F32), ga_ref[...], H_A, DV_A)]
        hm = _head_rms(hf_ref[rs, :].astype(F32) + hb_ref[rs, :].astype(F32), gm_ref[...], H_B, DH_B)
        bo = bo_ref[rs, :].astype(F32)
        ob = [_sigmoid(bo[:, h * DH_B:(h + 1) * DH_B]) * hm[h] for h in range(H_B)]
        cats.append(jnp.concatenate(oa + ob, axis=1).astype(BF16))
    _project_and_route(cats, parts, x_ref, gate_ref, w_ref, g2_ref, sh2_ref, sc2_ref, wr_ref, o_ref, h_ref, aff_ref)


def _out_ab(attn, hf, hb, bo, x2, gate, g_attn, g_mlstm, w_out, route, t, per_batch, out_scale):
    n = x2.shape[0]
    tm = _row_tile(t)
    row = lambda c: pl.BlockSpec((tm, c), lambda i: (i, 0))
    full = lambda a: pl.BlockSpec(a.shape, lambda i: (0,) * a.ndim)
    r_in, r_out = _route_specs(route, tm, t, per_batch)
    return pl.pallas_call(
        functools.partial(_out_ab_kernel, out_scale=out_scale),
        grid=(n // tm,),
        in_specs=[row(W_A), row(W_B), row(W_B), row(W_B), row(D_MODEL), _mod_spec(per_batch, tm, t),
                  full(g_attn), full(g_mlstm), full(w_out)] + r_in,
        out_specs=r_out,
        out_shape=_route_shapes(n),
        compiler_params=_cp("parallel"),
        name="out_proj_even",
    )(attn, hf, hb, bo, x2, gate, g_attn, g_mlstm, w_out, *route)


def _in_c_kernel(x_ref, g_ref, sh_ref, sc_ref, w_ref, bf_ref, gam_ref, q_ref, v_ref, og_ref, lf_ref, kk_ref, *, layer):
    gam = gam_ref[...]
    e = jnp.exp(gam - jnp.max(gam, axis=0, keepdims=True))
    p = e / jnp.sum(e, axis=0, keepdims=True)
    lbs = p[0:1]
    lbs0 = lbs
    for i in range(1, layer + 1):
        lbs = lbs + p[i:i + 1]
    lb = jnp.maximum(lbs - lbs0, 0.0)
    L = SCAN_CHUNK
    w = w_ref[...]
    chunks = [slice(c0, c0 + L) for c0 in range(0, x_ref.shape[0], L)]
    ys = []
    for rs in chunks:
        h = _rms_rows(x_ref[rs, :], g_ref[...]) * (1.0 + sc_ref[0]) + sh_ref[0]
        ys.append(_dot(h.astype(BF16), w))
    for rs, y in zip(chunks, ys):
        q_ref[rs, :] = y[:, 0:W_C].astype(BF16)
        v_ref[rs, :] = y[:, W_C:2 * W_C].astype(BF16)
        og_ref[rs, :] = y[:, 2 * W_C:3 * W_C].astype(BF16)
        for d in range(2):
            f = y[:, (3 + d) * W_C:(4 + d) * W_C] + bf_ref[d:d + 1]
            u = jnp.exp(-jnp.abs(f))
            neg = f < 0.0
            lf2 = (jnp.log(jnp.where(neg, lb + u, 1.0 + lb * u)) - jnp.log(1.0 + u)) * LOG2E
            kk_ref[rs, d * W_C:(d + 1) * W_C] = ((1.0 - lb) * jnp.where(neg, 1.0, u) / (1.0 + u)).astype(BF16)
            lf_ref[rs, d * W_C:(d + 1) * W_C] = _dot_sel_left2(_tri(L, d == 0).astype(BF16), lf2)


def _in_c(x2, g, shift, scale, w_in, b_f, gamma_lb, layer, t, per_batch):
    n = x2.shape[0]
    tm = _row_tile(t)
    row = lambda c: pl.BlockSpec((tm, c), lambda i: (i, 0))
    full = lambda a: pl.BlockSpec(a.shape, lambda i: (0,) * a.ndim)
    widths = (W_C, W_C, W_C, 2 * W_C, 2 * W_C)
    return pl.pallas_call(
        functools.partial(_in_c_kernel, layer=layer),
        grid=(n // tm,),
        in_specs=[row(D_MODEL), full(g), _mod_spec(per_batch, tm, t), _mod_spec(per_batch, tm, t),
                  full(w_in), full(b_f), full(gamma_lb)],
        out_specs=[row(c) for c in widths],
        out_shape=[jax.ShapeDtypeStruct((n, c), dt) for c, dt in zip(widths, (BF16, BF16, BF16, F32, BF16))],
        compiler_params=_cp("parallel"),
        name="in_proj_odd",
    )(x2, g, shift, scale, w_in, b_f, gamma_lb)


HGRN_LEVELS = (8, 16, 32, 64)
HGRN_DIAG = 8
HGRN_HEADS_PER_STEP = 8


def _hgrn_unit(q, kk, v, b, st_ref, b_s, rev, store_o):
    L = SCAN_CHUNK
    b_s[...] = b
    row = _iota((L, 1), 0)
    col = _iota((1, L), 1)
    k16 = kk.astype(BF16)
    v16 = v.astype(BF16)
    q = q.astype(F32)
    kk = kk.astype(F32)
    st = st_ref[...]
    o = _dot_nt((q * jnp.exp2(b)).astype(BF16), st.astype(BF16))
    yield
    c = HGRN_DIAG
    sub = _iota((c, 1), 0)
    ys = []
    for blk in range(L // c):
        r0 = blk * c
        qb = q[r0:r0 + c]
        bb = b[r0:r0 + c]
        for s in range(c):
            keep = (sub <= s) if rev else (sub >= s)
            ys.append(qb * jnp.exp2(jnp.where(keep, bb - b_s[r0 + s:r0 + s + 1, :], NEG_BIG)))
    rs = _dot_nt(jnp.concatenate(ys, axis=0).astype(BF16), k16)
    yield
    parts = []
    for m in HGRN_LEVELS:
        par = 2 * m
        pieces = []
        for p0 in range(0, L, par):
            ref_row = p0 + m if rev else p0 + m - 1
            pieces.append(jnp.broadcast_to(b_s[ref_row:ref_row + 1, :], (par, LANES)))
        bref = jnp.concatenate(pieces, axis=0)
        in_first = (row % par) < m
        reader = in_first if rev else jnp.logical_not(in_first)
        z = (jnp.where(reader, q, kk) * jnp.exp2(-jnp.abs(b - bref))).astype(BF16)
        keep = jnp.logical_and((row // par) == (col // par),
                               ((col % par) >= m) if rev else ((col % par) < m))
        parts.append((jnp.logical_and(keep, reader), _dot_nt(z, z)))
        yield
    lane = _iota((c, LANES), 1)
    rows = []
    for blk in range(L // c):
        acc = jnp.zeros((c, LANES), F32)
        for s in range(c):
            i = blk * c + s
            acc = jnp.where(lane == i, rs[i * c:(i + 1) * c], acc)
        rows.append(acc)
    att = jnp.concatenate(rows, axis=0)
    for keep, a in parts:
        att = att + jnp.where(keep, a, 0.0)
    o = o + _dot(att.astype(BF16), v16)
    b_end = b_s[0:1, :] if rev else b_s[L - 1:L, :]
    kd = (kk * jnp.exp2(b_end - b)).astype(BF16)
    st_new = st * jnp.exp2(b_end) + _dot_tn(v16, kd)
    yield
    store_o(o)
    st_ref[...] = st_new
    yield


def _hgrn_kernel(*refs, has_init, out_state, hp):
    qf, vf, lff, kkf, qb, vb, lfb, kkb = refs[:8]
    pos = 8
    if has_init:
        s0 = refs[pos]
        pos += 1
    of, ob = refs[pos:pos + 2]
    pos += 2
    if out_state:
        s_out = refs[pos]
        pos += 1
    st_s, b_s = refs[pos:pos + 2]
    step = pl.program_id(2)

    @pl.when(step == 0)
    def _():
        for d in range(2):
            for h in range(hp):
                st_s[d * hp + h] = s0[0, d, h].T if has_init else jnp.zeros((DH_C, DH_C), F32)

    def storer(o_ref, sl):
        def store(o):
            o_ref[0, :, sl] = o.astype(BF16)
        return store

    units = []
    for d, (q_ref, v_ref, lf_ref, kk_ref, o_ref) in enumerate(((qf, vf, lff, kkf, of), (qb, vb, lfb, kkb, ob))):
        for h in range(hp):
            u = d * hp + h
            sl = slice(h * DH_C, (h + 1) * DH_C)
            units.append(_hgrn_unit(q_ref[0, :, sl], kk_ref[0, :, sl], v_ref[0, :, sl], lf_ref[0, :, sl],
                                    st_s.at[u], b_s.at[u], d == 1, storer(o_ref, sl)))
    live = units
    while live:
        live = [g for g in live if next(g, "done") != "done"]

    if out_state:
        @pl.when(step == pl.num_programs(2) - 1)
        def _():
            for d in range(2):
                for h in range(hp):
                    s_out[0, d, h] = st_s[d * hp + h].T


def _hgrn(q, v, lf, kk, init, out_state):
    b, t, _ = q.shape
    L = SCAN_CHUNK
    nc = t // L
    hp = HGRN_HEADS_PER_STEP
    ng = H_C // hp
    blk = lambda im: pl.BlockSpec((1, L, hp * DH_C), im)
    fwd = lambda off: (lambda bi, h, c: (bi, c, h + off))
    bwd = lambda off: (lambda bi, h, c: (bi, nc - 1 - c, h + off))
    in_specs = [blk(fwd(0)), blk(fwd(0)), blk(fwd(0)), blk(fwd(0)),
                blk(bwd(0)), blk(bwd(0)), blk(bwd(ng)), blk(bwd(ng))]
    args = [q, v, lf, kk, q, v, lf, kk]
    st_spec = pl.BlockSpec((1, 2, hp, DH_C, DH_C), lambda bi, h, c: (bi, 0, h, 0, 0))
    has_init = init is not None
    if has_init:
        in_specs.append(st_spec)
        args.append(init)
    out_specs = [blk(fwd(0)), blk(bwd(0))]
    out_shape = [jax.ShapeDtypeStruct((b, t, W_C), BF16)] * 2
    if out_state:
        out_specs.append(st_spec)
        out_shape.append(jax.ShapeDtypeStruct((b, 2, H_C, DH_C, DH_C), F32))
    return pl.pallas_call(
        functools.partial(_hgrn_kernel, has_init=has_init, out_state=out_state, hp=hp),
        grid=(b, ng, nc),
        in_specs=in_specs,
        out_specs=out_specs,
        out_shape=out_shape,
        scratch_shapes=[pltpu.VMEM((2 * hp, DH_C, DH_C), F32), pltpu.VMEM((2 * hp, L, DH_C), F32)],
        compiler_params=_cp("parallel", "parallel", "arbitrary"),
        name="hgrn_scan",
    )(*args)


def _out_c_kernel(of_ref, ob_ref, og_ref, x_ref, gate_ref, gh_ref, w_ref,
                  g2_ref, sh2_ref, sc2_ref, wr_ref, o_ref, h_ref, aff_ref):
    parts = _row_parts(x_ref.shape[0])
    cats = []
    for rs in parts:
        hm = _head_rms(of_ref[rs, :].astype(F32) + ob_ref[rs, :].astype(F32), gh_ref[...], H_C, DH_C)
        og = og_ref[rs, :].astype(F32)
        heads = [hm[h] * _silu(og[:, h * DH_C:(h + 1) * DH_C]) for h in range(H_C)]
        cats.append(jnp.concatenate(heads, axis=1).astype(BF16))
    _project_and_route(cats, parts, x_ref, gate_ref, w_ref, g2_ref, sh2_ref, sc2_ref, wr_ref, o_ref, h_ref, aff_ref)


def _out_c(of, ob, og, x2, gate, g_hgrn, w_out, route, t, per_batch):
    n = x2.shape[0]
    tm = _row_tile(t)
    row = lambda c: pl.BlockSpec((tm, c), lambda i: (i, 0))
    full = lambda a: pl.BlockSpec(a.shape, lambda i: (0,) * a.ndim)
    r_in, r_out = _route_specs(route, tm, t, per_batch)
    return pl.pallas_call(
        _out_c_kernel,
        grid=(n // tm,),
        in_specs=[row(W_C), row(W_C), row(W_C), row(D_MODEL), _mod_spec(per_batch, tm, t),
                  full(g_hgrn), full(w_out)] + r_in,
        out_specs=r_out,
        out_shape=_route_shapes(n),
        compiler_params=_cp("parallel"),
        name="out_proj_odd",
    )(of, ob, og, x2, gate, g_hgrn, w_out, *route)


def _topk_kernel(aff_ref, pos_ref, s0_ref, bits_s, sel_s, *, cap, tb):
    t = aff_ref.shape[1]
    nb = t // tb
    bits_s[...] = pltpu.bitcast(aff_ref[0], I32)

    def count_ge(cand):
        def body(j, acc):
            r = pl.multiple_of(j * tb, tb)
            blk = bits_s[pl.ds(r, tb), :]
            return acc + jnp.sum((blk >= cand).astype(F32), axis=0, keepdims=True)
        return lax.fori_loop(0, nb, body, jnp.zeros((1, LANES), F32))

    def bit_body(i, thr):
        cand = thr | lax.shift_left(jnp.int32(1), 30 - i)
        return jnp.where(count_ge(cand) >= cap, cand, thr)

    thr = lax.fori_loop(0, 31, bit_body, jnp.zeros((1, LANES), I32))

    def count_gt(j, acc):
        r = pl.multiple_of(j * tb, tb)
        return acc + jnp.sum((bits_s[pl.ds(r, tb), :] > thr).astype(F32), axis=0, keepdims=True)

    need = cap - lax.fori_loop(0, nb, count_gt, jnp.zeros((1, LANES), F32))
    strict = (_iota((tb, tb), 0) > _iota((tb, tb), 1)).astype(BF16)

    def select(j, carry):
        r = pl.multiple_of(j * tb, tb)
        blk = bits_s[pl.ds(r, tb), :]
        eq = blk == thr
        before = carry + _dot(strict, eq.astype(BF16))
        sel = jnp.logical_or(blk > thr, jnp.logical_and(eq, before < need))
        sel_s[pl.ds(r, tb), :] = sel.astype(F32)
        return carry + jnp.sum(eq.astype(F32), axis=0, keepdims=True)

    lax.fori_loop(0, nb, select, jnp.zeros((1, LANES), F32))

    def place(j, carry):
        r = pl.multiple_of(j * tb, tb)
        sel = sel_s[pl.ds(r, tb), :]
        slot = carry + _dot(strict, sel.astype(BF16))
        pos_ref[0, pl.ds(r, tb), :] = jnp.where(sel > 0.5, slot, -1.0).astype(I32)
        s0_ref[0, pl.ds(j, 1), :] = carry.astype(I32)
        return carry + jnp.sum(sel, axis=0, keepdims=True)

    lax.fori_loop(0, nb, place, jnp.zeros((1, LANES), F32))


def _topk(aff, cap, tb):
    b, t, _ = aff.shape
    nb = t // tb
    return pl.pallas_call(
        functools.partial(_topk_kernel, cap=cap, tb=tb),
        grid=(b,),
        in_specs=[pl.BlockSpec((1, t, LANES), lambda bi: (bi, 0, 0))],
        out_specs=[pl.BlockSpec((1, t, LANES), lambda bi: (bi, 0, 0)),
                   pl.BlockSpec((1, nb, LANES), lambda bi: (bi, 0, 0))],
        out_shape=[jax.ShapeDtypeStruct((b, t, LANES), I32), jax.ShapeDtypeStruct((b, nb, LANES), I32)],
        scratch_shapes=[pltpu.VMEM((t, LANES), I32), pltpu.VMEM((t, LANES), F32)],
        compiler_params=_cp("parallel"),
        name="moe_topk",
    )(aff)


MOE_SMALL_COUNT = 48
MOE_STACK_ROWS = 1024


def _windows(cap, tb):
    narrow = MOE_SMALL_COUNT + BF16_ROWS
    if cap <= narrow:
        return (cap,)
    return (narrow, min(tb + BF16_ROWS, cap))


def _window_starts(s0_ref, bi, j, cap, w):
    if w == cap:
        return [0] * N_EXPERTS
    out = []
    for e in range(N_EXPERTS):
        st = jnp.minimum((s0_ref[bi, j, e] // BF16_ROWS) * BF16_ROWS, cap - w)
        out.append(pl.multiple_of(st, BF16_ROWS))
    return out


def _per_window(windows, narrow_ref, bi, j, run):
    if len(windows) == 1:
        run(windows[0])
    else:
        narrow = narrow_ref[bi, j] > 0
        pl.when(narrow)(lambda: run(windows[0]))
        pl.when(jnp.logical_not(narrow))(lambda: run(windows[1]))


def _gather_kernel(s0_ref, narrow_ref, h_ref, post_ref, o_ref, *, cap, tb):
    bi = pl.program_id(0)
    j = pl.program_id(1)

    @pl.when(j == 0)
    def _():
        o_ref[...] = jnp.zeros_like(o_ref)

    def run(w):
        starts = _window_starts(s0_ref, bi, j, cap, w)
        post = post_ref[0]
        r = _iota((w, tb), 0)
        group = max(1, min(N_EXPERTS, MOE_STACK_ROWS // w))
        for e0 in range(0, N_EXPERTS, group):
            es = range(e0, min(e0 + group, N_EXPERTS))
            onehot = jnp.concatenate(
                [jnp.where((post[e:e + 1, :] - starts[e]) == r, 1.0, 0.0).astype(BF16) for e in es],
                axis=0)
            rows = _dot(onehot, h_ref[0]).astype(BF16)
            for k, e in enumerate(es):
                cur = o_ref[0, e, pl.ds(starts[e], w), :]
                o_ref[0, e, pl.ds(starts[e], w), :] = cur + rows[k * w:(k + 1) * w]

    _per_window(_windows(cap, tb), narrow_ref, bi, j, run)


def _gather(h2, post, s0, narrow, cap, tb):
    b, t, d = h2.shape
    nb = t // tb
    grid_spec = pltpu.PrefetchScalarGridSpec(
        num_scalar_prefetch=2,
        grid=(b, nb),
        in_specs=[pl.BlockSpec((1, tb, d), lambda bi, j, s, n: (bi, j, 0)),
                  pl.BlockSpec((1, N_EXPERTS, tb), lambda bi, j, s, n: (bi, 0, j))],
        out_specs=pl.BlockSpec((1, N_EXPERTS, cap, d), lambda bi, j, s, n: (bi, 0, 0, 0)),
    )
    return pl.pallas_call(
        functools.partial(_gather_kernel, cap=cap, tb=tb),
        grid_spec=grid_spec,
        out_shape=jax.ShapeDtypeStruct((b, N_EXPERTS, cap, d), BF16),
        compiler_params=_cp("parallel", "arbitrary"),
        name="moe_gather",
    )(s0, narrow, h2, post)


def _expert_kernel(xc_ref, xs_ref, wg_ref, wu_ref, wd_ref, yc_ref, ys_ref, wg_s, wu_s, wd_s):
    i = pl.program_id(1)

    @pl.when(i == 0)
    def _():
        wg_s[...] = wg_ref[0].astype(BF16)
        wu_s[...] = wu_ref[0].astype(BF16)
        wd_s[...] = wd_ref[0].astype(BF16)

    def mlp(x_ref, y_ref):
        shp = x_ref.shape
        rows = shp[0] * shp[2]
        x = x_ref[...].reshape(rows, shp[3])
        step = rows // max(1, rows // 128)
        xs = [x[r:r + step] for r in range(0, rows, step)]
        hs = [(_silu(_dot(xp, wg_s[...])) * _dot(xp, wu_s[...])).astype(BF16) for xp in xs]
        ys = [_dot(hm, wd_s[...]).astype(BF16) for hm in hs]
        y_ref[...] = jnp.concatenate(ys, axis=0).reshape(shp)

    pl.when(i == 0)(lambda: mlp(xc_ref, yc_ref))
    pl.when(i > 0)(lambda: mlp(xs_ref, ys_ref))


def _experts(xin_c, xin_s, w_gate, w_up, w_down, layer):
    bc, e, cap_c, d = xin_c.shape
    bs, _, cap_s, _ = xin_s.shape
    f = w_gate.shape[-1]
    cspec = pl.BlockSpec((bc, 1, cap_c, d), lambda ei, i: (0, ei, 0, 0))
    sspec = pl.BlockSpec((1, 1, cap_s, d), lambda ei, i: (jnp.maximum(i - 1, 0), ei, 0, 0))
    ws = lambda a: pl.BlockSpec((None, 1) + a.shape[2:], lambda ei, i: (layer, ei, 0, 0))
    return pl.pallas_call(
        _expert_kernel,
        grid=(e, bs + 1),
        in_specs=[cspec, sspec, ws(w_gate), ws(w_up), ws(w_down)],
        out_specs=[cspec, sspec],
        out_shape=[jax.ShapeDtypeStruct(xin_c.shape, BF16), jax.ShapeDtypeStruct(xin_s.shape, BF16)],
        scratch_shapes=[pltpu.VMEM((d, f), BF16), pltpu.VMEM((d, f), BF16), pltpu.VMEM((f, d), BF16)],
        compiler_params=_cp("parallel", "arbitrary"),
        name="moe_experts",
    )(xin_c, xin_s, w_gate, w_up, w_down)


def _combine_kernel(*refs, cap, tb, final):
    windows = _windows(cap, tb)
    s0_ref, narrow_ref, y_ref, pos_ref, aff_ref, x_ref, gate_ref = refs[:7]
    pos_n = 7
    if final:
        gf_ref = refs[pos_n]
        pos_n += 1
    spread_ref, row_id_ref = refs[pos_n:pos_n + 2]
    o_ref = refs[-1]
    bi = pl.program_id(0)
    j = pl.program_id(1)
    offset = 1024
    lane = _iota((1, N_EXPERTS), 1)

    def run(w):
        starts = _window_starts(s0_ref, bi, j, cap, w)
        if w == windows[0]:
            stv = jnp.zeros((1, N_EXPERTS), I32)
            for e in range(N_EXPERTS):
                stv = jnp.where(lane == e, starts[e], stv)
            rel = pos_ref[0] - stv + offset
            spread = spread_ref[...]
            hi = lax.shift_right_logical(rel, 5).astype(F32).astype(BF16)
            lo = jnp.bitwise_and(rel, 31).astype(F32).astype(BF16)
            relx = 32.0 * _dot(hi, spread) + _dot(lo, spread) - offset
            gx = _dot(aff_ref[0].astype(BF16), spread)
            g = jnp.where(relx == row_id_ref[...], gx, 0.0)
            ys = jnp.concatenate([y_ref[0, e, pl.ds(starts[e], w), :] for e in range(N_EXPERTS)], axis=0)
            acc = _dot(g.astype(BF16), ys)
        else:
            pos = pos_ref[0]
            aff = aff_ref[0].astype(BF16).astype(F32)
            col = _iota((tb, w), 1)
            acc = jnp.zeros((tb, D_MODEL), F32)
            for e in range(N_EXPERTS):
                g = jnp.where((pos[:, e:e + 1] - starts[e]) == col, aff[:, e:e + 1], 0.0)
                acc = acc + _dot(g.astype(BF16), y_ref[0, e, pl.ds(starts[e], w), :])
        out = x_ref[0] + gate_ref[0] * acc
        if final:
            out = _rms_rows(out, gf_ref[...])
        o_ref[0] = out

    _per_window(windows, narrow_ref, bi, j, run)


def _combine(y, pos, aff, x3, gate, s0, narrow, cap, tb, per_batch, g_final):
    b, t, d = x3.shape
    nb = t // tb
    final = g_final is not None
    tok = lambda c: pl.BlockSpec((1, tb, c), lambda bi, j, s, n: (bi, j, 0))
    const = lambda a: pl.BlockSpec(a.shape, lambda bi, j, s, n: (0, 0))
    in_specs = [pl.BlockSpec((1, N_EXPERTS, cap, d), lambda bi, j, s, n: (bi, 0, 0, 0)),
                tok(N_EXPERTS), tok(N_EXPERTS), tok(d),
                pl.BlockSpec((1, 1, d), (lambda bi, j, s, n: (bi, 0, 0)) if per_batch
                             else (lambda bi, j, s, n: (0, 0, 0)))]
    args = [y, pos, aff, x3, gate]
    if final:
        in_specs.append(const(g_final))
        args.append(g_final)
    w = _windows(cap, tb)[0]
    spread = jnp.repeat(jnp.eye(N_EXPERTS, dtype=BF16), w, axis=1)
    row_id = jnp.tile(jnp.arange(w, dtype=F32), N_EXPERTS).reshape(1, -1)
    in_specs += [const(spread), const(row_id)]
    args += [spread, row_id]
    grid_spec = pltpu.PrefetchScalarGridSpec(
        num_scalar_prefetch=2, grid=(b, nb), in_specs=in_specs, out_specs=tok(d))
    return pl.pallas_call(
        functools.partial(_combine_kernel, cap=cap, tb=tb, final=final),
        grid_spec=grid_spec,
        out_shape=jax.ShapeDtypeStruct((b, t, d), F32),
        compiler_params=_cp("parallel", "arbitrary"),
        name="moe_combine",
    )(s0, narrow, *args)


def _moe(x2, h2, aff, b, t, gate, per_batch, g_final):
    cap = CAPACITY_FACTOR * t // N_EXPERTS
    tb = min(MOE_TOKEN_BLOCK, t)
    aff3 = aff.reshape(b, t, N_EXPERTS)
    per = LANES // N_EXPERTS
    groups = -(-b // per)
    nb = t // tb

    def pack(a):
        a = jnp.pad(a, ((0, groups * per - b), (0, 0), (0, 0)))
        return a.reshape(groups, per, a.shape[1], N_EXPERTS).transpose(0, 2, 1, 3).reshape(groups, a.shape[1], LANES)

    def unpack(a):
        return a.reshape(groups, a.shape[1], per, N_EXPERTS).transpose(0, 2, 1, 3).reshape(
            groups * per, a.shape[1], N_EXPERTS)[:b]

    pos, s0 = _topk(pack(aff3), cap, tb)
    pos, s0 = unpack(pos), unpack(s0)
    counts = jnp.diff(s0, axis=1, append=jnp.full((b, 1, N_EXPERTS), cap, I32))
    narrow = jnp.all(counts <= MOE_SMALL_COUNT, axis=-1).astype(I32)
    xin = _gather(h2.reshape(b, t, D_MODEL), pos.transpose(0, 2, 1), s0, narrow, cap, tb)
    y = yield xin
    out = _combine(y, pos, aff3, x2.reshape(b, t, D_MODEL), gate, s0, narrow, cap, tb, per_batch, g_final)
    return out.reshape(b * t, D_MODEL)


def _rope_tables(t):
    rows = t // GRID_W
    row = jnp.repeat(jnp.arange(rows, dtype=F32), GRID_W)
    col = jnp.tile(jnp.arange(GRID_W, dtype=F32), rows)
    n_freq = DQK_A // 4
    inv_freq = jnp.power(ROPE_BASE, -jnp.arange(n_freq, dtype=F32) / n_freq)
    ang = jnp.concatenate([row[:, None] * inv_freq, col[:, None] * inv_freq], axis=-1)
    cos, sin = jnp.cos(ang), jnp.sin(ang)
    zero = jnp.zeros_like(sin)
    reps = LANES // DQK_A
    cos_t = jnp.tile(jnp.concatenate([cos, cos], axis=-1), (1, reps))
    sin_a = jnp.tile(jnp.concatenate([-sin, zero], axis=-1), (1, reps))
    sin_b = jnp.tile(jnp.concatenate([zero, sin], axis=-1), (1, reps))
    return cos_t, sin_a, sin_b


def _stream(x, mods, per_batch, rope, cache, p):
    b, t, d = x.shape
    x2 = x.reshape(b * t, d)
    depth = p["g_norm"].shape[0]
    even_out, odd_out = [], []
    for l in range(depth):
        j = l // 2
        sh1, sc1, gt1, sh2, sc2, gt2 = mods[l]
        g1 = p["g_norm"][l, 0].reshape(1, d)
        g2 = p["g_norm"][l, 1].reshape(1, d)
        route = (g2, sh2, sc2, p["w_router_pad"][l])
        if l % 2 == 0:
            aq, ak, av, bqk, bv, bo, gates = _in_ab(x2, g1, sh1, sc1, p["w_in_ab_main"][j], p["w_in_ab_gate"][j],
                                                     rope, t, per_batch, F32 if cache is None else BF16)
            r3 = lambda a: a.reshape(b, t, a.shape[-1])
            if cache is not None:
                kc = cache[0][:, j].reshape(b, -1, W_A)
                vc = cache[1][:, j].reshape(b, -1, W_A)
                nu = 2 * H_B
                init = (cache[2][:, j].reshape(b, nu, DH_B, DH_B),
                        cache[3][:, j].reshape(b, nu, 1, DH_B),
                        jnp.broadcast_to(cache[4][:, j].reshape(b, nu, 1, 1), (b, nu, 1, LANES)))
            else:
                kc = vc = init = None
            lam_init = 0.8 - 0.6 * math.exp(-0.3 * l)
            attn = _attention_pipe(r3(aq), r3(ak), r3(av), kc, vc, p["lam_ab"][j], lam_init)
            qk = _conv_silu(r3(bqk), p["conv_ab"][j])
            bias = jnp.pad(jnp.concatenate([p["b_ig_ab"][j].reshape(-1), p["b_fg_ab"][j].reshape(-1)]),
                           (0, LANES - 4 * H_B)).reshape(1, LANES)
            res = _mlstm(qk, r3(bv), r3(gates), bias, init, cache is None)
            hf, hb = res[0], res[1]
            if cache is None:
                even_out.append((ak, av, res[2], res[3], res[4]))
            x2, h2, aff = _out_ab(attn.reshape(b * t, W_A), hf.reshape(b * t, W_B), hb.reshape(b * t, W_B), bo, x2,
                                  gt1, p["g_attn_ab"][j].reshape(1, W_A), p["g_mlstm_ab"][j].reshape(1, W_B),
                                  p["w_out_ab_bf16"][j], route, t, per_batch, 1.0 - lam_init)
        else:
            q, v, og, lf, kk = _in_c(x2, g1, sh1, sc1, p["w_in_c_bf16"][j], p["b_f_c"][j], p["gamma_lb"], l, t,
                                     per_batch)
            r3 = lambda a: a.reshape(b, t, a.shape[-1])
            init = None if cache is None else cache[5][:, j]
            res = _hgrn(r3(q), r3(v), r3(lf), r3(kk), init, cache is None)
            if cache is None:
                odd_out.append(res[2])
            x2, h2, aff = _out_c(res[0].reshape(b * t, W_C), res[1].reshape(b * t, W_C), og, x2, gt1,
                                 p["g_hgrn_c"][j].reshape(1, W_C), p["w_out_c_bf16"][j], route, t, per_batch)
        g_final = p["g_final"].reshape(1, d) if l == depth - 1 else None
        x2 = yield from _moe(x2, h2, aff, b, t, gt2, per_batch, g_final)
    return x2.reshape(b, t, d), even_out, odd_out


def kernel(x_prompt, x_sample, cache_dattn_k, cache_dattn_v, state_mlstm_C, state_mlstm_n, state_mlstm_m,
           state_hgrn_S, c, c_ctx, w_ada, b_ada, g_norm, g_final, w_in_ab, b_ig_ab, b_fg_ab, lam_ab, conv_ab,
           g_attn_ab, g_mlstm_ab, w_out_ab, w_in_c, b_f_c, gamma_lb, g_hgrn_c, w_out_c,
           w_router, w_gate_e, w_up_e, w_down_e):
    d = D_MODEL
    bs = c.shape[0]
    bp, tp, _ = x_prompt.shape
    depth = w_ada.shape[0]
    n_main = 3 * W_A + 4 * W_B
    p = dict(
        g_norm=g_norm, g_final=g_final, b_ig_ab=b_ig_ab, b_fg_ab=b_fg_ab, lam_ab=lam_ab, conv_ab=conv_ab,
        g_attn_ab=g_attn_ab, g_mlstm_ab=g_mlstm_ab, b_f_c=b_f_c, gamma_lb=gamma_lb, g_hgrn_c=g_hgrn_c,
        w_gate_e=w_gate_e, w_up_e=w_up_e, w_down_e=w_down_e,
        w_in_ab_main=w_in_ab[:, :, :n_main].astype(BF16),
        w_in_ab_gate=jnp.pad(w_in_ab[:, :, n_main:], ((0, 0), (0, 0), (0, LANES - 4 * H_B))).astype(BF16),
        w_out_ab_bf16=w_out_ab.astype(BF16),
        w_in_c_bf16=w_in_c.astype(BF16),
        w_out_c_bf16=w_out_c.astype(BF16),
        w_router_pad=jnp.pad(w_router, ((0, 0), (0, 0), (0, LANES - N_EXPERTS))),
    )
    rows = -(-(bs + 1) // 8) * 8
    cond = jnp.concatenate([c, c_ctx.reshape(1, d), jnp.zeros((rows - bs - 1, d), F32)], axis=0)
    mods = _ada(cond, w_ada, b_ada)
    mods_s = [[mods[l, :bs, i * d:(i + 1) * d].reshape(bs, 1, d) for i in range(6)] for l in range(depth)]
    mods_c = [[mods[l, bs:bs + 1, i * d:(i + 1) * d].reshape(1, 1, d) for i in range(6)] for l in range(depth)]

    cache = (cache_dattn_k, cache_dattn_v, state_mlstm_C, state_mlstm_n, state_mlstm_m, state_hgrn_S)
    ctx = _stream(x_prompt, mods_c, False, None, None, p)
    smp = _stream(x_sample, mods_s, True, _rope_tables(x_sample.shape[1]), cache, p)
    xin_c, xin_s = next(ctx), next(smp)
    for l in range(depth):
        y_c, y_s = _experts(xin_c, xin_s, w_gate_e, w_up_e, w_down_e, l)
        try:
            xin_c = ctx.send(y_c)
        except StopIteration as done:
            y_prompt, ev, od = done.value
        try:
            xin_s = smp.send(y_s)
        except StopIteration as done:
            y_sample = done.value[0]
    new_k = jnp.stack([e[0].reshape(bp, tp, H_A, 2 * DQK_A) for e in ev], axis=1)
    new_v = jnp.stack([e[1].reshape(bp, tp, H_A, DV_A) for e in ev], axis=1)
    new_c = jnp.stack([e[2].reshape(bp, 2, H_B, DH_B, DH_B) for e in ev], axis=1)
    new_n = jnp.stack([e[3].reshape(bp, 2, H_B, DH_B) for e in ev], axis=1)
    new_m = jnp.stack([e[4][..., 0].reshape(bp, 2, H_B) for e in ev], axis=1)
    new_s = jnp.stack(od, axis=1)
    return (y_prompt, y_sample, new_k, new_v, new_c, new_n, new_m, new_s)
```

```python
import functools
import math

import jax
import jax.numpy as jnp
from jax import lax
from jax.experimental import pallas as pl
from jax.experimental.pallas import tpu as pltpu

F32 = jnp.float32
BF16 = jnp.bfloat16
I32 = jnp.int32

D_MODEL = 1024
H_A = 4
DV_A = 128
DQK_A = 64
W_A = H_A * DV_A
H_B = 4
DH_B = 128
W_B = H_B * DH_B
H_C = 8
DH_C = 128
W_C = H_C * DH_C
N_EXPERTS = 16
CAPACITY_FACTOR = 2
GRID_W = 64
ROPE_BASE = 10000.0
RMS_EPS = 1e-6
NEG_BIG = -1e30
LOG2E = 1.4426950408889634
LANES = 128
BF16_ROWS = 16
SCAN_CHUNK = 128
MLSTM_BATCHES_PER_STEP = 2
MOE_TOKEN_BLOCK = 256
VMEM_LIMIT = 56 * 1024 * 1024


def _cp(*sem):
    return pltpu.CompilerParams(dimension_semantics=sem, vmem_limit_bytes=VMEM_LIMIT)


def _dot(a, b):
    return jnp.dot(a, b, preferred_element_type=F32)


def _dot_nt(a, b):
    return lax.dot_general(a, b, (((1,), (1,)), ((), ())), preferred_element_type=F32)


def _dot_tn(a, b):
    return lax.dot_general(a, b, (((0,), (0,)), ((), ())), preferred_element_type=F32)


def _split3(x):
    hi = x.astype(BF16)
    r = x - hi.astype(F32)
    mid = r.astype(BF16)
    lo = (r - mid.astype(F32)).astype(BF16)
    return hi, mid, lo


def _dot_sel_left(sel, x):
    hi, mid, lo = _split3(x)
    return _dot(sel, hi) + _dot(sel, mid) + _dot(sel, lo)


def _dot_sel_right(x, sel):
    hi, mid, lo = _split3(x)
    return _dot(hi, sel) + _dot(mid, sel) + _dot(lo, sel)


def _dot_f32(a, b):
    ah, am, al = _split3(a)
    bh, bm, bl = _split3(b)
    return (_dot(ah, bh) + (_dot(ah, bm) + _dot(am, bh))
            + (_dot(ah, bl) + _dot(am, bm) + _dot(al, bh)))


def _dot_f32x3(a, b):
    ah = a.astype(BF16)
    al = (a - ah.astype(F32)).astype(BF16)
    bh = b.astype(BF16)
    bl = (b - bh.astype(F32)).astype(BF16)
    return _dot(ah, bh) + (_dot(ah, bl) + _dot(al, bh))


def _dot_sel_left2(sel, x):
    hi = x.astype(BF16)
    lo = (x - hi.astype(F32)).astype(BF16)
    return _dot(sel, hi) + _dot(sel, lo)


def _sigmoid(x):
    return 1.0 / (1.0 + jnp.exp(-x))


def _silu(x):
    return x * _sigmoid(x)


def _log_sigmoid(x):
    return jnp.minimum(x, 0.0) - jnp.log(1.0 + jnp.exp(-jnp.abs(x)))


def _rms_rows(x, g):
    return x * lax.rsqrt(jnp.mean(x * x, axis=-1, keepdims=True) + RMS_EPS) * g


def _iota(shape, dim):
    return lax.broadcasted_iota(I32, shape, dim)


def _ada_kernel(c_ref, w_ref, b_ref, o_ref):
    c = c_ref[...]
    o_ref[0] = _dot_f32(_silu(c), w_ref[0]) + b_ref[0]


def _ada(cond_rows, w_ada, b_ada):
    depth, d, n6 = w_ada.shape
    rows = cond_rows.shape[0]
    tn = 1536
    return pl.pallas_call(
        _ada_kernel,
        grid=(depth, n6 // tn),
        in_specs=[pl.BlockSpec((rows, d), lambda l, j: (0, 0)),
                  pl.BlockSpec((1, d, tn), lambda l, j: (l, 0, j)),
                  pl.BlockSpec((1, 1, tn), lambda l, j: (l, 0, j))],
        out_specs=pl.BlockSpec((1, rows, tn), lambda l, j: (l, 0, j)),
        out_shape=jax.ShapeDtypeStruct((depth, rows, n6), F32),
        compiler_params=_cp("parallel", "parallel"),
        name="adaln",
    )(cond_rows, w_ada, b_ada.reshape(depth, 1, n6))


def _mod_spec(per_batch, tm, t):
    if per_batch:
        return pl.BlockSpec((1, 1, D_MODEL), lambda i: (i * tm // t, 0, 0))
    return pl.BlockSpec((1, 1, D_MODEL), lambda i: (0, 0, 0))


def _row_tile(t):
    return min(512, t)


def _in_ab_kernel(*refs, use_rope):
    if use_rope:
        (x_ref, g_ref, sh_ref, sc_ref, w_ref, wg_ref, cos_ref, sa_ref, sb_ref,
         aq_ref, ak_ref, av_ref, bqk_ref, bv_ref, bo_ref, gt_ref) = refs
    else:
        (x_ref, g_ref, sh_ref, sc_ref, w_ref, wg_ref,
         aq_ref, ak_ref, av_ref, bqk_ref, bv_ref, bo_ref, gt_ref) = refs
    h = _rms_rows(x_ref[...], g_ref[...]) * (1.0 + sc_ref[0]) + sh_ref[0]
    hb = h.astype(BF16)
    y = _dot(hb, w_ref[...])
    gt_ref[...] = _dot(hb, wg_ref[...])
    aq = y[:, 0:W_A] * (DQK_A ** -0.5 * LOG2E)
    ak = y[:, W_A:2 * W_A]
    if use_rope:
        cos = jnp.concatenate([cos_ref[...]] * H_A, axis=1)
        sa = jnp.concatenate([sa_ref[...]] * H_A, axis=1)
        sb = jnp.concatenate([sb_ref[...]] * H_A, axis=1)
        half = DQK_A // 2

        def rope(v):
            return (v * cos + pltpu.roll(v, W_A - half, axis=1) * sa
                    + pltpu.roll(v, half, axis=1) * sb)

        aq = rope(aq)
        ak = rope(ak)
    aq_ref[...] = aq.astype(aq_ref.dtype)
    ak_ref[...] = ak.astype(ak_ref.dtype)
    av_ref[...] = y[:, 2 * W_A:3 * W_A].astype(av_ref.dtype)
    bqk_ref[...] = y[:, 3 * W_A:3 * W_A + 2 * W_B].astype(BF16)
    bv_ref[...] = y[:, 3 * W_A + 2 * W_B:3 * W_A + 3 * W_B].astype(BF16)
    bo_ref[...] = y[:, 3 * W_A + 3 * W_B:3 * W_A + 4 * W_B].astype(BF16)


def _in_ab(x2, g, shift, scale, w_main, w_gate, rope, t, per_batch, qkv_dtype):
    n = x2.shape[0]
    tm = _row_tile(t)
    use_rope = rope is not None
    row = lambda c: pl.BlockSpec((tm, c), lambda i: (i, 0))
    full = lambda a: pl.BlockSpec(a.shape, lambda i: (0,) * a.ndim)
    in_specs = [row(D_MODEL), full(g), _mod_spec(per_batch, tm, t), _mod_spec(per_batch, tm, t),
                full(w_main), full(w_gate)]
    args = [x2, g, shift, scale, w_main, w_gate]
    if use_rope:
        nt = t // tm
        tab = pl.BlockSpec((tm, LANES), lambda i: (i % nt, 0))
        in_specs += [tab, tab, tab]
        args += list(rope)
    widths = (W_A, W_A, W_A, 2 * W_B, W_B, W_B, LANES)
    return pl.pallas_call(
        functools.partial(_in_ab_kernel, use_rope=use_rope),
        grid=(n // tm,),
        in_specs=in_specs,
        out_specs=[row(c) for c in widths],
        out_shape=[jax.ShapeDtypeStruct((n, c), dt) for c, dt in
                   zip(widths, (qkv_dtype, qkv_dtype, qkv_dtype, BF16, BF16, BF16, F32))],
        compiler_params=_cp("parallel"),
        name="in_proj_even",
    )(*args)


def _attn_pipe_kernel(*refs, has_cache, tk, lam_init, nq):
    if has_cache:
        q_ref, k_ref, v_ref, kc_ref, vc_ref, lam_ref, o_ref, s_a, s_b, mp_s, vt_s, sc_s, vct_s = refs
    else:
        q_ref, k_ref, v_ref, lam_ref, o_ref, s_a, s_b, mp_s, vt_s = refs
    s_s = (s_a, s_b)
    i = pl.program_id(2)
    q = q_ref[0]
    tq = q.shape[0]
    first = _iota((1, LANES), 1) < DQK_A
    qs = (jnp.where(first, q, 0).astype(BF16), jnp.where(first, 0, q).astype(BF16))
    nkb = k_ref.shape[1] // tk

    @pl.when(i == 0)
    def _():
        for j in range(nkb):
            vt_s[j, 0:DV_A, :] = v_ref[0, j * tk:(j + 1) * tk, :].astype(F32).T.astype(BF16)
            vt_s[j, DV_A:, :] = jnp.ones((BF16_ROWS, tk), BF16)
        if has_cache:
            vct_s[0:DV_A, :] = vc_ref[0].astype(F32).T.astype(BF16)
            vct_s[DV_A:, :] = jnp.ones((BF16_ROWS, vct_s.shape[1]), BF16)

    def groups(x):
        return [x[:, g * LANES:(g + 1) * LANES] for g in range(x.shape[1] // LANES)]

    def body(score, finish, cur):
        prev = 1 - cur
        if finish:
            m = [jnp.max(mp_s[prev, a], axis=-1, keepdims=True) for a in range(2)]
        mpart = [jnp.full((tq, LANES), NEG_BIG, F32)] * 2
        acc = [jnp.zeros((DV_A + BF16_ROWS, tq), F32)] * 2

        def pass1(kb, store):
            for a in range(2):
                s = _dot_nt(qs[a], kb)
                store(a, s)
                mpart[a] = functools.reduce(jnp.maximum, groups(s), mpart[a])

        def pass2(a, s, vt):
            acc[a] = acc[a] + _dot_nt(vt, jnp.exp2(s - m[a]).astype(BF16))

        if has_cache:
            def store_c(a, s):
                sc_s[cur, a] = s
            if score:
                pass1(kc_ref[0].astype(BF16), store_c)
            if finish:
                for a in range(2):
                    pass2(a, sc_s[prev, a], vct_s[...])
        for j in range(nkb):
            def store_j(a, s, j=j):
                s_s[cur][a, j, :, 0:tk] = s
            if score:
                pass1(k_ref[0, j * tk:(j + 1) * tk, :].astype(BF16), store_j)
            if finish:
                for a in range(2):
                    pass2(a, s_s[prev][a, j, :, 0:tk], vt_s[j])
        if score:
            for a in range(2):
                mp_s[cur, a] = mpart[a]
        if finish:
            lp = lam_ref[...]
            lam = (jnp.exp(jnp.sum(lp[0:1] * lp[1:2], axis=-1, keepdims=True))
                   - jnp.exp(jnp.sum(lp[2:3] * lp[3:4], axis=-1, keepdims=True)) + lam_init)
            o1 = acc[0][0:DV_A] / acc[0][DV_A:DV_A + 1]
            o2 = acc[1][0:DV_A] / acc[1][DV_A:DV_A + 1]
            o_ref[0] = (o1 - lam * o2).T.astype(BF16)

    pl.when(i == 0)(lambda: body(True, False, 0))
    inner = jnp.logical_and(i > 0, i < nq)
    pl.when(jnp.logical_and(inner, i % 2 == 0))(lambda: body(True, True, 0))
    pl.when(jnp.logical_and(inner, i % 2 == 1))(lambda: body(True, True, 1))
    pl.when(i == nq)(lambda: body(False, True, nq % 2))


def _attention_pipe(aq, ak, av, kc, vc, lam_p, lam_init):
    b, t, _ = aq.shape
    tq = min(256, t)
    tk = min(512, t)
    nq = t // tq
    has_cache = kc is not None
    qspec = pl.BlockSpec((1, tq, DV_A), lambda bi, h, i: (bi, jnp.minimum(i, nq - 1), h))
    ospec = pl.BlockSpec((1, tq, DV_A), lambda bi, h, i: (bi, jnp.maximum(i - 1, 0), h))
    ks = pl.BlockSpec((1, t, DV_A), lambda bi, h, i: (bi, 0, h))
    in_specs = [qspec, ks, ks]
    args = [aq, ak, av]
    if has_cache:
        cs = pl.BlockSpec((1, kc.shape[1], DV_A), lambda bi, h, i: (bi, 0, h))
        in_specs += [cs, cs]
        args += [kc, vc]
    in_specs.append(pl.BlockSpec(lam_p.shape, lambda bi, h, i: (0, 0)))
    args.append(lam_p)
    scratch = [pltpu.VMEM((2, t // tk, tq, tk), F32), pltpu.VMEM((2, t // tk, tq, tk + LANES), F32),
               pltpu.VMEM((2, 2, tq, LANES), F32),
               pltpu.VMEM((t // tk, DV_A + BF16_ROWS, tk), BF16)]
    if has_cache:
        scratch += [pltpu.VMEM((2, 2, tq, kc.shape[1]), F32),
                    pltpu.VMEM((DV_A + BF16_ROWS, kc.shape[1]), BF16)]
    return pl.pallas_call(
        functools.partial(_attn_pipe_kernel, has_cache=has_cache, tk=tk, lam_init=lam_init, nq=nq),
        grid=(b, H_A, nq + 1),
        in_specs=in_specs,
        out_specs=ospec,
        out_shape=jax.ShapeDtypeStruct((b, t, W_A), BF16),
        scratch_shapes=scratch,
        compiler_params=_cp("parallel", "parallel", "arbitrary"),
        name="diff_attention",
    )(*args)


def _conv_kernel(x_ref, w_ref, o_ref):
    x = x_ref[0].astype(F32)
    t = x.shape[0]
    w = w_ref[...]
    r = _iota((t, 1), 0)
    prev = jnp.where(r == 0, 0.0, pltpu.roll(x, 1, axis=0))
    nxt = jnp.where(r == t - 1, 0.0, pltpu.roll(x, t - 1, axis=0))
    scale = jnp.where(pl.program_id(1) * x.shape[1] >= W_B, DH_B ** -0.5, 1.0)
    o_ref[0] = (_silu(prev * w[0:1] + x * w[1:2] + nxt * w[2:3]) * scale).astype(BF16)


def _conv_silu(bqk, conv_w):
    b, t, c = bqk.shape
    tc = 256
    spec = pl.BlockSpec((1, t, tc), lambda bi, j: (bi, 0, j))
    return pl.pallas_call(
        _conv_kernel,
        grid=(b, c // tc),
        in_specs=[spec, pl.BlockSpec((conv_w.shape[0], tc), lambda bi, j: (0, j))],
        out_specs=spec,
        out_shape=jax.ShapeDtypeStruct(bqk.shape, BF16),
        compiler_params=_cp("parallel", "parallel"),
        name="mlstm_conv",
    )(bqk, conv_w)


def _tri(n, lower):
    r = _iota((n, n), 0)
    c = _iota((n, n), 1)
    return r >= c if lower else r <= c


def _mlstm_kernel(*refs, has_init, out_state):
    qf, kf, vf, gf, qb, kb, vb, gb, bias = refs[:9]
    pos = 9
    if has_init:
        c0, n0, m0 = refs[pos:pos + 3]
        pos += 3
    hf, hb = refs[pos:pos + 2]
    pos += 2
    if out_state:
        c_out, n_out, m_out = refs[pos:pos + 3]
        pos += 3
    cn_s, m_s = refs[pos:pos + 2]
    step = pl.program_id(1)
    L = SCAN_CHUNK
    nu = 2 * H_B
    nbt = qf.shape[0]

    @pl.when(step == 0)
    def _():
        if has_init:
            for bb in range(nbt):
                for u in range(nu):
                    cn_s[bb * nu + u, :, 0:DH_B] = c0[bb, u]
                    cn_s[bb * nu + u, :, DH_B:2 * DH_B] = jnp.broadcast_to(n0[bb, u], (DH_B, DH_B)).T
                m_s[bb * nu:(bb + 1) * nu] = m0[bb]
        else:
            cn_s[...] = jnp.zeros_like(cn_s)
            m_s[...] = jnp.zeros_like(m_s)

    ones_b = jnp.ones((L, DH_B), BF16)
    sel_rows = _iota((LANES, LANES), 0)
    lower = _tri(L, True)
    upper = _tri(L, False)
    lower_b = lower.astype(BF16)
    upper_b = upper.astype(BF16)
    col = _iota((1, LANES), 1)
    units = []
    for bb, d, (q_ref, k_ref, v_ref, g_ref, h_ref) in [
            (bb, d, refs_d) for bb in range(nbt)
            for d, refs_d in enumerate(((qf, kf, vf, gf, hf), (qb, kb, vb, gb, hb)))]:
        g = g_ref[bb] + bias[...]
        g = jnp.where(col < 2 * H_B, g, _log_sigmoid(g))
        gt = g.T
        if d == 0:
            bc_all = _dot_sel_left(lower_b, g)
            br_all = _dot_sel_right(gt, upper_b)
            mask = lower
        else:
            bc_all = _dot_sel_left(upper_b, g)
            br_all = _dot_sel_right(gt, lower_b)
            mask = upper
        for h in range(H_B):
            u = d * H_B + h
            sl = slice(h * DH_B, (h + 1) * DH_B)
            units.append(dict(
                u=u, su=bb * nu + u, bb=bb, d=d, sl=sl, h_ref=h_ref, mask=mask, g=g, bc_all=bc_all,
                li_row=gt[u:u + 1, :], b_row=br_all[2 * H_B + u:2 * H_B + u + 1, :],
                q=q_ref[bb, :, sl].astype(BF16), k=k_ref[bb, :, sl].astype(F32),
                v1=jnp.concatenate([v_ref[bb, :, sl].astype(BF16), ones_b], axis=1)))
    for x in units:
        x["li_b"] = _dot_sel_right(x["g"], (sel_rows == x["u"]).astype(BF16))
        x["b_b"] = _dot_sel_right(x["bc_all"], (sel_rows == 2 * H_B + x["u"]).astype(BF16))
        x["qk"] = _dot_nt(x["q"], x["k"].astype(BF16))
        x["cn"] = cn_s[x["su"]]
        x["qc"] = _dot(x["q"], x["cn"].astype(BF16))
    for x in units:
        x["m_prev"] = m_s[x["su"]]
        a_b = x["b_b"] + x["m_prev"]
        dm = jnp.where(x["mask"], x["b_b"] - x["b_row"] + x["li_row"], NEG_BIG)
        x["m_t"] = jnp.maximum(a_b, jnp.max(dm, axis=-1, keepdims=True))
        x["w_inter"] = jnp.exp(a_b - x["m_t"])
        x["s"] = (x["qk"] * jnp.exp(dm - x["m_t"])).astype(BF16)
    for x in units:
        x["sv"] = _dot(x["s"], x["v1"])
        b_end = x["b_b"][L - 1:L, :] if x["d"] == 0 else x["b_b"][0:1, :]
        g_b = b_end - x["b_b"] + x["li_b"]
        x["m_new"] = jnp.maximum(b_end + x["m_prev"], jnp.max(g_b, axis=0, keepdims=True))
        x["w_old"] = jnp.exp(b_end + x["m_prev"] - x["m_new"])
        x["ks"] = (jnp.exp(g_b - x["m_new"]) * x["k"]).astype(BF16)
    for x in units:
        num = x["w_inter"] * x["qc"][:, 0:DH_B] + x["sv"][:, 0:DH_B]
        den = x["w_inter"] * x["qc"][:, DH_B:] + x["sv"][:, DH_B:]
        x["h_ref"][x["bb"], :, x["sl"]] = (num / jnp.maximum(jnp.abs(den), jnp.exp(-x["m_t"]))).astype(BF16)
        w2 = jnp.concatenate([x["w_old"], x["w_old"]], axis=1)
        cn_s[x["su"]] = w2 * x["cn"] + _dot_tn(x["ks"], x["v1"])
        m_s[x["su"]] = x["m_new"]

    if out_state:
        @pl.when(step == pl.num_programs(1) - 1)
        def _():
            for bb in range(nbt):
                for u in range(nu):
                    c_out[bb, u] = cn_s[bb * nu + u, :, 0:DH_B]
                    n_out[bb, u] = cn_s[bb * nu + u, :, DH_B:2 * DH_B].T[0:1, :]
                m_out[bb] = m_s[bb * nu:(bb + 1) * nu]


def _mlstm(qk, v, gates, bias, init, out_state):
    b, t, _ = v.shape
    L = SCAN_CHUNK
    nc = t // L
    nu = 2 * H_B
    nbt = MLSTM_BATCHES_PER_STEP if b % MLSTM_BATCHES_PER_STEP == 0 else 1
    fwd = lambda j: (lambda bi, c: (bi, c, j))
    bwd = lambda j: (lambda bi, c: (bi, nc - 1 - c, j))
    blk = lambda w, im: pl.BlockSpec((nbt, L, w), im)
    in_specs = [blk(W_B, fwd(0)), blk(W_B, fwd(1)), blk(W_B, fwd(0)), blk(LANES, fwd(0)),
                blk(W_B, bwd(0)), blk(W_B, bwd(1)), blk(W_B, bwd(0)), blk(LANES, bwd(0)),
                pl.BlockSpec((1, LANES), lambda bi, c: (0, 0))]
    args = [qk, qk, v, gates, qk, qk, v, gates, bias]
    st_specs = [pl.BlockSpec((nbt, nu, DH_B, DH_B), lambda bi, c: (bi, 0, 0, 0)),
                pl.BlockSpec((nbt, nu, 1, DH_B), lambda bi, c: (bi, 0, 0, 0)),
                pl.BlockSpec((nbt, nu, 1, LANES), lambda bi, c: (bi, 0, 0, 0))]
    st_shapes = [jax.ShapeDtypeStruct((b, nu, DH_B, DH_B), F32),
                 jax.ShapeDtypeStruct((b, nu, 1, DH_B), F32),
                 jax.ShapeDtypeStruct((b, nu, 1, LANES), F32)]
    has_init = init is not None
    if has_init:
        in_specs += st_specs
        args += list(init)
    out_specs = [blk(W_B, fwd(0)), blk(W_B, bwd(0))]
    out_shape = [jax.ShapeDtypeStruct((b, t, W_B), BF16)] * 2
    if out_state:
        out_specs += st_specs
        out_shape += st_shapes
    return pl.pallas_call(
        functools.partial(_mlstm_kernel, has_init=has_init, out_state=out_state),
        grid=(b // nbt, nc),
        in_specs=in_specs,
        out_specs=out_specs,
        out_shape=out_shape,
        scratch_shapes=[pltpu.VMEM((nbt * nu, DH_B, 2 * DH_B), F32), pltpu.VMEM((nbt * nu, 1, LANES), F32)],
        compiler_params=_cp("parallel", "arbitrary"),
        name="mlstm_scan",
    )(*args)


def _head_rms(x, g, heads, width):
    parts = []
    for h in range(heads):
        sl = slice(h * width, (h + 1) * width)
        parts.append(_rms_rows(x[:, sl], g[:, sl]))
    return parts


OUT_PROJ_PARTS = 4


def _row_parts(tm):
    step = tm // OUT_PROJ_PARTS
    return [slice(r, r + step) for r in range(0, tm, step)]


def _project_and_route(cats, parts, x_ref, gate_ref, w_ref, g_ref, sh_ref, sc_ref, wr_ref, o_ref, h_ref, aff_ref):
    w = w_ref[...]
    xs = [x_ref[rs, :] + gate_ref[0] * _dot(cat, w) for cat, rs in zip(cats, parts)]
    hs = []
    for x, rs in zip(xs, parts):
        o_ref[rs, :] = x
        h = _rms_rows(x, g_ref[...]) * (1.0 + sc_ref[0]) + sh_ref[0]
        h_ref[rs, :] = h.astype(BF16)
        hs.append(h)
    wr = wr_ref[...]
    logits = [_dot_f32x3(h, wr) for h in hs]
    valid = _iota((1, LANES), 1) < N_EXPERTS
    for lg, rs in zip(logits, parts):
        lg = jnp.where(valid, lg, NEG_BIG)
        e = jnp.exp(lg - jnp.max(lg, axis=-1, keepdims=True))
        e = jnp.where(valid, e, 0.0)
        aff_ref[rs, :] = (e / jnp.sum(e, axis=-1, keepdims=True))[:, :N_EXPERTS]


def _route_specs(route, tm, t, per_batch):
    g2, sh2, sc2, wr = route
    full = lambda a: pl.BlockSpec(a.shape, lambda i: (0,) * a.ndim)
    row = lambda c: pl.BlockSpec((tm, c), lambda i: (i, 0))
    in_specs = [full(g2), _mod_spec(per_batch, tm, t), _mod_spec(per_batch, tm, t), full(wr)]
    out_specs = [row(D_MODEL), row(D_MODEL), row(N_EXPERTS)]
    return in_specs, out_specs


def _route_shapes(n):
    return [jax.ShapeDtypeStruct((n, D_MODEL), F32), jax.ShapeDtypeStruct((n, D_MODEL), BF16),
            jax.ShapeDtypeStruct((n, N_EXPERTS), F32)]


def _out_ab_kernel(at_ref, hf_ref, hb_ref, bo_ref, x_ref, gate_ref, ga_ref, gm_ref, w_ref,
                   g2_ref, sh2_ref, sc2_ref, wr_ref, o_ref, h_ref, aff_ref, *, out_scale):
    parts = _row_parts(x_ref.shape[0])
    cats = []
    for rs in parts:
        oa = [p * out_scale for p in _head_rms(at_ref[rs, :].astype(F32), ga_ref[...], H_A, DV_A)]
        hm = _head_rms(hf_ref[rs, :].astype(F32) + hb_ref[rs, :].astype(F32), gm_ref[...], H_B, DH_B)
        bo = bo_ref[rs, :].astype(F32)
        ob = [_sigmoid(bo[:, h * DH_B:(h + 1) * DH_B]) * hm[h] for h in range(H_B)]
        cats.append(jnp.concatenate(oa + ob, axis=1).astype(BF16))
    _project_and_route(cats, parts, x_ref, gate_ref, w_ref, g2_ref, sh2_ref, sc2_ref, wr_ref, o_ref, h_ref, aff_ref)


def _out_ab(attn, hf, hb, bo, x2, gate, g_attn, g_mlstm, w_out, route, t, per_batch, out_scale):
    n = x2.shape[0]
    tm = _row_tile(t)
    row = lambda c: pl.BlockSpec((tm, c), lambda i: (i, 0))
    full = lambda a: pl.BlockSpec(a.shape, lambda i: (0,) * a.ndim)
    r_in, r_out = _route_specs(route, tm, t, per_batch)
    return pl.pallas_call(
        functools.partial(_out_ab_kernel, out_scale=out_scale),
        grid=(n // tm,),
        in_specs=[row(W_A), row(W_B), row(W_B), row(W_B), row(D_MODEL), _mod_spec(per_batch, tm, t),
                  full(g_attn), full(g_mlstm), full(w_out)] + r_in,
        out_specs=r_out,
        out_shape=_route_shapes(n),
        compiler_params=_cp("parallel"),
        name="out_proj_even",
    )(attn, hf, hb, bo, x2, gate, g_attn, g_mlstm, w_out, *route)


def _in_c_kernel(x_ref, g_ref, sh_ref, sc_ref, w_ref, bf_ref, gam_ref, q_ref, v_ref, og_ref, lf_ref, kk_ref, *, layer):
    gam = gam_ref[...]
    e = jnp.exp(gam - jnp.max(gam, axis=0, keepdims=True))
    p = e / jnp.sum(e, axis=0, keepdims=True)
    lbs = p[0:1]
    lbs0 = lbs
    for i in range(1, layer + 1):
        lbs = lbs + p[i:i + 1]
    lb = jnp.maximum(lbs - lbs0, 0.0)
    L = SCAN_CHUNK
    w = w_ref[...]
    chunks = [slice(c0, c0 + L) for c0 in range(0, x_ref.shape[0], L)]
    ys = []
    for rs in chunks:
        h = _rms_rows(x_ref[rs, :], g_ref[...]) * (1.0 + sc_ref[0]) + sh_ref[0]
        ys.append(_dot(h.astype(BF16), w))
    for rs, y in zip(chunks, ys):
        q_ref[rs, :] = y[:, 0:W_C].astype(BF16)
        v_ref[rs, :] = y[:, W_C:2 * W_C].astype(BF16)
        og_ref[rs, :] = y[:, 2 * W_C:3 * W_C].astype(BF16)
        for d in range(2):
            f = y[:, (3 + d) * W_C:(4 + d) * W_C] + bf_ref[d:d + 1]
            u = jnp.exp(-jnp.abs(f))
            neg = f < 0.0
            lf2 = (jnp.log(jnp.where(neg, lb + u, 1.0 + lb * u)) - jnp.log(1.0 + u)) * LOG2E
            kk_ref[rs, d * W_C:(d + 1) * W_C] = ((1.0 - lb) * jnp.where(neg, 1.0, u) / (1.0 + u)).astype(BF16)
            lf_ref[rs, d * W_C:(d + 1) * W_C] = _dot_sel_left2(_tri(L, d == 0).astype(BF16), lf2)


def _in_c(x2, g, shift, scale, w_in, b_f, gamma_lb, layer, t, per_batch):
    n = x2.shape[0]
    tm = _row_tile(t)
    row = lambda c: pl.BlockSpec((tm, c), lambda i: (i, 0))
    full = lambda a: pl.BlockSpec(a.shape, lambda i: (0,) * a.ndim)
    widths = (W_C, W_C, W_C, 2 * W_C, 2 * W_C)
    return pl.pallas_call(
        functools.partial(_in_c_kernel, layer=layer),
        grid=(n // tm,),
        in_specs=[row(D_MODEL), full(g), _mod_spec(per_batch, tm, t), _mod_spec(per_batch, tm, t),
                  full(w_in), full(b_f), full(gamma_lb)],
        out_specs=[row(c) for c in widths],
        out_shape=[jax.ShapeDtypeStruct((n, c), dt) for c, dt in zip(widths, (BF16, BF16, BF16, F32, BF16))],
        compiler_params=_cp("parallel"),
        name="in_proj_odd",
    )(x2, g, shift, scale, w_in, b_f, gamma_lb)


HGRN_LEVELS = (8, 16, 32, 64)
HGRN_DIAG = 8
HGRN_HEADS_PER_STEP = 8


def _hgrn_unit(q, kk, v, b, st_ref, b_s, rev, store_o):
    L = SCAN_CHUNK
    b_s[...] = b
    row = _iota((L, 1), 0)
    col = _iota((1, L), 1)
    k16 = kk.astype(BF16)
    v16 = v.astype(BF16)
    q = q.astype(F32)
    kk = kk.astype(F32)
    st = st_ref[...]
    o = _dot_nt((q * jnp.exp2(b)).astype(BF16), st.astype(BF16))
    yield
    c = HGRN_DIAG
    sub = _iota((c, 1), 0)
    ys = []
    for blk in range(L // c):
        r0 = blk * c
        qb = q[r0:r0 + c]
        bb = b[r0:r0 + c]
        for s in range(c):
            keep = (sub <= s) if rev else (sub >= s)
            ys.append(qb * jnp.exp2(jnp.where(keep, bb - b_s[r0 + s:r0 + s + 1, :], NEG_BIG)))
    rs = _dot_nt(jnp.concatenate(ys, axis=0).astype(BF16), k16)
    yield
    parts = []
    for m in HGRN_LEVELS:
        par = 2 * m
        pieces = []
        for p0 in range(0, L, par):
            ref_row = p0 + m if rev else p0 + m - 1
            pieces.append(jnp.broadcast_to(b_s[ref_row:ref_row + 1, :], (par, LANES)))
        bref = jnp.concatenate(pieces, axis=0)
        in_first = (row % par) < m
        reader = in_first if rev else jnp.logical_not(in_first)
        z = (jnp.where(reader, q, kk) * jnp.exp2(-jnp.abs(b - bref))).astype(BF16)
        keep = jnp.logical_and((row // par) == (col // par),
                               ((col % par) >= m) if rev else ((col % par) < m))
        parts.append((jnp.logical_and(keep, reader), _dot_nt(z, z)))
        yield
    lane = _iota((c, LANES), 1)
    rows = []
    for blk in range(L // c):
        acc = jnp.zeros((c, LANES), F32)
        for s in range(c):
            i = blk * c + s
            acc = jnp.where(lane == i, rs[i * c:(i + 1) * c], acc)
        rows.append(acc)
    att = jnp.concatenate(rows, axis=0)
    for keep, a in parts:
        att = att + jnp.where(keep, a, 0.0)
    o = o + _dot(att.astype(BF16), v16)
    b_end = b_s[0:1, :] if rev else b_s[L - 1:L, :]
    kd = (kk * jnp.exp2(b_end - b)).astype(BF16)
    st_new = st * jnp.exp2(b_end) + _dot_tn(v16, kd)
    yield
    store_o(o)
    st_ref[...] = st_new
    yield


def _hgrn_kernel(*refs, has_init, out_state, hp):
    qf, vf, lff, kkf, qb, vb, lfb, kkb = refs[:8]
    pos = 8
    if has_init:
        s0 = refs[pos]
        pos += 1
    of, ob = refs[pos:pos + 2]
    pos += 2
    if out_state:
        s_out = refs[pos]
        pos += 1
    st_s, b_s = refs[pos:pos + 2]
    step = pl.program_id(2)

    @pl.when(step == 0)
    def _():
        for d in range(2):
            for h in range(hp):
                st_s[d * hp + h] = s0[0, d, h].T if has_init else jnp.zeros((DH_C, DH_C), F32)

    def storer(o_ref, sl):
        def store(o):
            o_ref[0, :, sl] = o.astype(BF16)
        return store

    units = []
    for d, (q_ref, v_ref, lf_ref, kk_ref, o_ref) in enumerate(((qf, vf, lff, kkf, of), (qb, vb, lfb, kkb, ob))):
        for h in range(hp):
            u = d * hp + h
            sl = slice(h * DH_C, (h + 1) * DH_C)
            units.append(_hgrn_unit(q_ref[0, :, sl], kk_ref[0, :, sl], v_ref[0, :, sl], lf_ref[0, :, sl],
                                    st_s.at[u], b_s.at[u], d == 1, storer(o_ref, sl)))
    live = units
    while live:
        live = [g for g in live if next(g, "done") != "done"]

    if out_state:
        @pl.when(step == pl.num_programs(2) - 1)
        def _():
            for d in range(2):
                for h in range(hp):
                    s_out[0, d, h] = st_s[d * hp + h].T


def _hgrn(q, v, lf, kk, init, out_state):
    b, t, _ = q.shape
    L = SCAN_CHUNK
    nc = t // L
    hp = HGRN_HEADS_PER_STEP
    ng = H_C // hp
    blk = lambda im: pl.BlockSpec((1, L, hp * DH_C), im)
    fwd = lambda off: (lambda bi, h, c: (bi, c, h + off))
    bwd = lambda off: (lambda bi, h, c: (bi, nc - 1 - c, h + off))
    in_specs = [blk(fwd(0)), blk(fwd(0)), blk(fwd(0)), blk(fwd(0)),
                blk(bwd(0)), blk(bwd(0)), blk(bwd(ng)), blk(bwd(ng))]
    args = [q, v, lf, kk, q, v, lf, kk]
    st_spec = pl.BlockSpec((1, 2, hp, DH_C, DH_C), lambda bi, h, c: (bi, 0, h, 0, 0))
    has_init = init is not None
    if has_init:
        in_specs.append(st_spec)
        args.append(init)
    out_specs = [blk(fwd(0)), blk(bwd(0))]
    out_shape = [jax.ShapeDtypeStruct((b, t, W_C), BF16)] * 2
    if out_state:
        out_specs.append(st_spec)
        out_shape.append(jax.ShapeDtypeStruct((b, 2, H_C, DH_C, DH_C), F32))
    return pl.pallas_call(
        functools.partial(_hgrn_kernel, has_init=has_init, out_state=out_state, hp=hp),
        grid=(b, ng, nc),
        in_specs=in_specs,
        out_specs=out_specs,
        out_shape=out_shape,
        scratch_shapes=[pltpu.VMEM((2 * hp, DH_C, DH_C), F32), pltpu.VMEM((2 * hp, L, DH_C), F32)],
        compiler_params=_cp("parallel", "parallel", "arbitrary"),
        name="hgrn_scan",
    )(*args)


def _out_c_kernel(of_ref, ob_ref, og_ref, x_ref, gate_ref, gh_ref, w_ref,
                  g2_ref, sh2_ref, sc2_ref, wr_ref, o_ref, h_ref, aff_ref):
    parts = _row_parts(x_ref.shape[0])
    cats = []
    for rs in parts:
        hm = _head_rms(of_ref[rs, :].astype(F32) + ob_ref[rs, :].astype(F32), gh_ref[...], H_C, DH_C)
        og = og_ref[rs, :].astype(F32)
        heads = [hm[h] * _silu(og[:, h * DH_C:(h + 1) * DH_C]) for h in range(H_C)]
        cats.append(jnp.concatenate(heads, axis=1).astype(BF16))
    _project_and_route(cats, parts, x_ref, gate_ref, w_ref, g2_ref, sh2_ref, sc2_ref, wr_ref, o_ref, h_ref, aff_ref)


def _out_c(of, ob, og, x2, gate, g_hgrn, w_out, route, t, per_batch):
    n = x2.shape[0]
    tm = _row_tile(t)
    row = lambda c: pl.BlockSpec((tm, c), lambda i: (i, 0))
    full = lambda a: pl.BlockSpec(a.shape, lambda i: (0,) * a.ndim)
    r_in, r_out = _route_specs(route, tm, t, per_batch)
    return pl.pallas_call(
        _out_c_kernel,
        grid=(n // tm,),
        in_specs=[row(W_C), row(W_C), row(W_C), row(D_MODEL), _mod_spec(per_batch, tm, t),
                  full(g_hgrn), full(w_out)] + r_in,
        out_specs=r_out,
        out_shape=_route_shapes(n),
        compiler_params=_cp("parallel"),
        name="out_proj_odd",
    )(of, ob, og, x2, gate, g_hgrn, w_out, *route)


def _topk_kernel(aff_ref, pos_ref, s0_ref, bits_s, sel_s, *, cap, tb):
    t = aff_ref.shape[1]
    nb = t // tb
    bits_s[...] = pltpu.bitcast(aff_ref[0], I32)

    def count_ge(cand):
        def body(j, acc):
            r = pl.multiple_of(j * tb, tb)
            blk = bits_s[pl.ds(r, tb), :]
            return acc + jnp.sum((blk >= cand).astype(F32), axis=0, keepdims=True)
        return lax.fori_loop(0, nb, body, jnp.zeros((1, LANES), F32))

    def bit_body(i, thr):
        cand = thr | lax.shift_left(jnp.int32(1), 30 - i)
        return jnp.where(count_ge(cand) >= cap, cand, thr)

    thr = lax.fori_loop(0, 31, bit_body, jnp.zeros((1, LANES), I32))

    def count_gt(j, acc):
        r = pl.multiple_of(j * tb, tb)
        return acc + jnp.sum((bits_s[pl.ds(r, tb), :] > thr).astype(F32), axis=0, keepdims=True)

    need = cap - lax.fori_loop(0, nb, count_gt, jnp.zeros((1, LANES), F32))
    strict = (_iota((tb, tb), 0) > _iota((tb, tb), 1)).astype(BF16)

    def select(j, carry):
        r = pl.multiple_of(j * tb, tb)
        blk = bits_s[pl.ds(r, tb), :]
        eq = blk == thr
        before = carry + _dot(strict, eq.astype(BF16))
        sel = jnp.logical_or(blk > thr, jnp.logical_and(eq, before < need))
        sel_s[pl.ds(r, tb), :] = sel.astype(F32)
        return carry + jnp.sum(eq.astype(F32), axis=0, keepdims=True)

    lax.fori_loop(0, nb, select, jnp.zeros((1, LANES), F32))

    def place(j, carry):
        r = pl.multiple_of(j * tb, tb)
        sel = sel_s[pl.ds(r, tb), :]
        slot = carry + _dot(strict, sel.astype(BF16))
        pos_ref[0, pl.ds(r, tb), :] = jnp.where(sel > 0.5, slot, -1.0).astype(I32)
        s0_ref[0, pl.ds(j, 1), :] = carry.astype(I32)
        return carry + jnp.sum(sel, axis=0, keepdims=True)

    lax.fori_loop(0, nb, place, jnp.zeros((1, LANES), F32))


def _topk(aff, cap, tb):
    b, t, _ = aff.shape
    nb = t // tb
    return pl.pallas_call(
        functools.partial(_topk_kernel, cap=cap, tb=tb),
        grid=(b,),
        in_specs=[pl.BlockSpec((1, t, LANES), lambda bi: (bi, 0, 0))],
        out_specs=[pl.BlockSpec((1, t, LANES), lambda bi: (bi, 0, 0)),
                   pl.BlockSpec((1, nb, LANES), lambda bi: (bi, 0, 0))],
        out_shape=[jax.ShapeDtypeStruct((b, t, LANES), I32), jax.ShapeDtypeStruct((b, nb, LANES), I32)],
        scratch_shapes=[pltpu.VMEM((t, LANES), I32), pltpu.VMEM((t, LANES), F32)],
        compiler_params=_cp("parallel"),
        name="moe_topk",
    )(aff)


MOE_SMALL_COUNT = 48
MOE_STACK_ROWS = 1024


def _windows(cap, tb):
    narrow = MOE_SMALL_COUNT + BF16_ROWS
    if cap <= narrow:
        return (cap,)
    return (narrow, min(tb + BF16_ROWS, cap))


def _window_starts(s0_ref, bi, j, cap, w):
    if w == cap:
        return [0] * N_EXPERTS
    out = []
    for e in range(N_EXPERTS):
        st = jnp.minimum((s0_ref[bi, j, e] // BF16_ROWS) * BF16_ROWS, cap - w)
        out.append(pl.multiple_of(st, BF16_ROWS))
    return out


def _per_window(windows, narrow_ref, bi, j, run):
    if len(windows) == 1:
        run(windows[0])
    else:
        narrow = narrow_ref[bi, j] > 0
        pl.when(narrow)(lambda: run(windows[0]))
        pl.when(jnp.logical_not(narrow))(lambda: run(windows[1]))


def _gather_kernel(s0_ref, narrow_ref, h_ref, post_ref, o_ref, *, cap, tb):
    bi = pl.program_id(0)
    j = pl.program_id(1)

    @pl.when(j == 0)
    def _():
        o_ref[...] = jnp.zeros_like(o_ref)

    def run(w):
        starts = _window_starts(s0_ref, bi, j, cap, w)
        post = post_ref[0]
        r = _iota((w, tb), 0)
        group = max(1, min(N_EXPERTS, MOE_STACK_ROWS // w))
        for e0 in range(0, N_EXPERTS, group):
            es = range(e0, min(e0 + group, N_EXPERTS))
            onehot = jnp.concatenate(
                [jnp.where((post[e:e + 1, :] - starts[e]) == r, 1.0, 0.0).astype(BF16) for e in es],
                axis=0)
            rows = _dot(onehot, h_ref[0]).astype(BF16)
            for k, e in enumerate(es):
                cur = o_ref[0, e, pl.ds(starts[e], w), :]
                o_ref[0, e, pl.ds(starts[e], w), :] = cur + rows[k * w:(k + 1) * w]

    _per_window(_windows(cap, tb), narrow_ref, bi, j, run)


def _gather(h2, post, s0, narrow, cap, tb):
    b, t, d = h2.shape
    nb = t // tb
    grid_spec = pltpu.PrefetchScalarGridSpec(
        num_scalar_prefetch=2,
        grid=(b, nb),
        in_specs=[pl.BlockSpec((1, tb, d), lambda bi, j, s, n: (bi, j, 0)),
                  pl.BlockSpec((1, N_EXPERTS, tb), lambda bi, j, s, n: (bi, 0, j))],
        out_specs=pl.BlockSpec((1, N_EXPERTS, cap, d), lambda bi, j, s, n: (bi, 0, 0, 0)),
    )
    return pl.pallas_call(
        functools.partial(_gather_kernel, cap=cap, tb=tb),
        grid_spec=grid_spec,
        out_shape=jax.ShapeDtypeStruct((b, N_EXPERTS, cap, d), BF16),
        compiler_params=_cp("parallel", "arbitrary"),
        name="moe_gather",
    )(s0, narrow, h2, post)


def _expert_kernel(xc_ref, xs_ref, wg_ref, wu_ref, wd_ref, yc_ref, ys_ref, wg_s, wu_s, wd_s):
    i = pl.program_id(1)

    @pl.when(i == 0)
    def _():
        wg_s[...] = wg_ref[0].astype(BF16)
        wu_s[...] = wu_ref[0].astype(BF16)
        wd_s[...] = wd_ref[0].astype(BF16)

    def mlp(x_ref, y_ref):
        shp = x_ref.shape
        rows = shp[0] * shp[2]
        x = x_ref[...].reshape(rows, shp[3])
        step = rows // max(1, rows // 128)
        xs = [x[r:r + step] for r in range(0, rows, step)]
        hs = [(_silu(_dot(xp, wg_s[...])) * _dot(xp, wu_s[...])).astype(BF16) for xp in xs]
        ys = [_dot(hm, wd_s[...]).astype(BF16) for hm in hs]
        y_ref[...] = jnp.concatenate(ys, axis=0).reshape(shp)

    pl.when(i == 0)(lambda: mlp(xc_ref, yc_ref))
    pl.when(i > 0)(lambda: mlp(xs_ref, ys_ref))


def _experts(xin_c, xin_s, w_gate, w_up, w_down, layer):
    bc, e, cap_c, d = xin_c.shape
    bs, _, cap_s, _ = xin_s.shape
    f = w_gate.shape[-1]
    cspec = pl.BlockSpec((bc, 1, cap_c, d), lambda ei, i: (0, ei, 0, 0))
    sspec = pl.BlockSpec((1, 1, cap_s, d), lambda ei, i: (jnp.maximum(i - 1, 0), ei, 0, 0))
    ws = lambda a: pl.BlockSpec((None, 1) + a.shape[2:], lambda ei, i: (layer, ei, 0, 0))
    return pl.pallas_call(
        _expert_kernel,
        grid=(e, bs + 1),
        in_specs=[cspec, sspec, ws(w_gate), ws(w_up), ws(w_down)],
        out_specs=[cspec, sspec],
        out_shape=[jax.ShapeDtypeStruct(xin_c.shape, BF16), jax.ShapeDtypeStruct(xin_s.shape, BF16)],
        scratch_shapes=[pltpu.VMEM((d, f), BF16), pltpu.VMEM((d, f), BF16), pltpu.VMEM((f, d), BF16)],
        compiler_params=_cp("parallel", "arbitrary"),
        name="moe_experts",
    )(xin_c, xin_s, w_gate, w_up, w_down)


def _combine_kernel(*refs, cap, tb, final):
    windows = _windows(cap, tb)
    s0_ref, narrow_ref, y_ref, pos_ref, aff_ref, x_ref, gate_ref = refs[:7]
    pos_n = 7
    if final:
        gf_ref = refs[pos_n]
        pos_n += 1
    spread_ref, row_id_ref = refs[pos_n:pos_n + 2]
    o_ref = refs[-1]
    bi = pl.program_id(0)
    j = pl.program_id(1)
    offset = 1024
    lane = _iota((1, N_EXPERTS), 1)

    def run(w):
        starts = _window_starts(s0_ref, bi, j, cap, w)
        if w == windows[0]:
            stv = jnp.zeros((1, N_EXPERTS), I32)
            for e in range(N_EXPERTS):
                stv = jnp.where(lane == e, starts[e], stv)
            rel = pos_ref[0] - stv + offset
            spread = spread_ref[...]
            hi = lax.shift_right_logical(rel, 5).astype(F32).astype(BF16)
            lo = jnp.bitwise_and(rel, 31).astype(F32).astype(BF16)
            relx = 32.0 * _dot(hi, spread) + _dot(lo, spread) - offset
            gx = _dot(aff_ref[0].astype(BF16), spread)
            g = jnp.where(relx == row_id_ref[...], gx, 0.0)
            ys = jnp.concatenate([y_ref[0, e, pl.ds(starts[e], w), :] for e in range(N_EXPERTS)], axis=0)
            acc = _dot(g.astype(BF16), ys)
        else:
            pos = pos_ref[0]
            aff = aff_ref[0].astype(BF16).astype(F32)
            col = _iota((tb, w), 1)
            acc = jnp.zeros((tb, D_MODEL), F32)
            for e in range(N_EXPERTS):
                g = jnp.where((pos[:, e:e + 1] - starts[e]) == col, aff[:, e:e + 1], 0.0)
                acc = acc + _dot(g.astype(BF16), y_ref[0, e, pl.ds(starts[e], w), :])
        out = x_ref[0] + gate_ref[0] * acc
        if final:
            out = _rms_rows(out, gf_ref[...])
        o_ref[0] = out

    _per_window(windows, narrow_ref, bi, j, run)


def _combine(y, pos, aff, x3, gate, s0, narrow, cap, tb, per_batch, g_final):
    b, t, d = x3.shape
    nb = t // tb
    final = g_final is not None
    tok = lambda c: pl.BlockSpec((1, tb, c), lambda bi, j, s, n: (bi, j, 0))
    const = lambda a: pl.BlockSpec(a.shape, lambda bi, j, s, n: (0, 0))
    in_specs = [pl.BlockSpec((1, N_EXPERTS, cap, d), lambda bi, j, s, n: (bi, 0, 0, 0)),
                tok(N_EXPERTS), tok(N_EXPERTS), tok(d),
                pl.BlockSpec((1, 1, d), (lambda bi, j, s, n: (bi, 0, 0)) if per_batch
                             else (lambda bi, j, s, n: (0, 0, 0)))]
    args = [y, pos, aff, x3, gate]
    if final:
        in_specs.append(const(g_final))
        args.append(g_final)
    w = _windows(cap, tb)[0]
    spread = jnp.repeat(jnp.eye(N_EXPERTS, dtype=BF16), w, axis=1)
    row_id = jnp.tile(jnp.arange(w, dtype=F32), N_EXPERTS).reshape(1, -1)
    in_specs += [const(spread), const(row_id)]
    args += [spread, row_id]
    grid_spec = pltpu.PrefetchScalarGridSpec(
        num_scalar_prefetch=2, grid=(b, nb), in_specs=in_specs, out_specs=tok(d))
    return pl.pallas_call(
        functools.partial(_combine_kernel, cap=cap, tb=tb, final=final),
        grid_spec=grid_spec,
        out_shape=jax.ShapeDtypeStruct((b, t, d), F32),
        compiler_params=_cp("parallel", "arbitrary"),
        name="moe_combine",
    )(s0, narrow, *args)


def _moe(x2, h2, aff, b, t, gate, per_batch, g_final):
    cap = CAPACITY_FACTOR * t // N_EXPERTS
    tb = min(MOE_TOKEN_BLOCK, t)
    aff3 = aff.reshape(b, t, N_EXPERTS)
    per = LANES // N_EXPERTS
    groups = -(-b // per)
    nb = t // tb

    def pack(a):
        a = jnp.pad(a, ((0, groups * per - b), (0, 0), (0, 0)))
        return a.reshape(groups, per, a.shape[1], N_EXPERTS).transpose(0, 2, 1, 3).reshape(groups, a.shape[1], LANES)

    def unpack(a):
        return a.reshape(groups, a.shape[1], per, N_EXPERTS).transpose(0, 2, 1, 3).reshape(
            groups * per, a.shape[1], N_EXPERTS)[:b]

    pos, s0 = _topk(pack(aff3), cap, tb)
    pos, s0 = unpack(pos), unpack(s0)
    counts = jnp.diff(s0, axis=1, append=jnp.full((b, 1, N_EXPERTS), cap, I32))
    narrow = jnp.all(counts <= MOE_SMALL_COUNT, axis=-1).astype(I32)
    xin = _gather(h2.reshape(b, t, D_MODEL), pos.transpose(0, 2, 1), s0, narrow, cap, tb)
    y = yield xin
    out = _combine(y, pos, aff3, x2.reshape(b, t, D_MODEL), gate, s0, narrow, cap, tb, per_batch, g_final)
    return out.reshape(b * t, D_MODEL)


def _rope_tables(t):
    rows = t // GRID_W
    row = jnp.repeat(jnp.arange(rows, dtype=F32), GRID_W)
    col = jnp.tile(jnp.arange(GRID_W, dtype=F32), rows)
    n_freq = DQK_A // 4
    inv_freq = jnp.power(ROPE_BASE, -jnp.arange(n_freq, dtype=F32) / n_freq)
    ang = jnp.concatenate([row[:, None] * inv_freq, col[:, None] * inv_freq], axis=-1)
    cos, sin = jnp.cos(ang), jnp.sin(ang)
    zero = jnp.zeros_like(sin)
    reps = LANES // DQK_A
    cos_t = jnp.tile(jnp.concatenate([cos, cos], axis=-1), (1, reps))
    sin_a = jnp.tile(jnp.concatenate([-sin, zero], axis=-1), (1, reps))
    sin_b = jnp.tile(jnp.concatenate([zero, sin], axis=-1), (1, reps))
    return cos_t, sin_a, sin_b


def _stream(x, mods, per_batch, rope, cache, p):
    b, t, d = x.shape
    x2 = x.reshape(b * t, d)
    depth = p["g_norm"].shape[0]
    even_out, odd_out = [], []
    for l in range(depth):
        j = l // 2
        sh1, sc1, gt1, sh2, sc2, gt2 = mods[l]
        g1 = p["g_norm"][l, 0].reshape(1, d)
        g2 = p["g_norm"][l, 1].reshape(1, d)
        route = (g2, sh2, sc2, p["w_router_pad"][l])
        if l % 2 == 0:
            aq, ak, av, bqk, bv, bo, gates = _in_ab(x2, g1, sh1, sc1, p["w_in_ab_main"][j], p["w_in_ab_gate"][j],
                                                     rope, t, per_batch, F32 if cache is None else BF16)
            r3 = lambda a: a.reshape(b, t, a.shape[-1])
            if cache is not None:
                kc = cache[0][:, j].reshape(b, -1, W_A)
                vc = cache[1][:, j].reshape(b, -1, W_A)
                nu = 2 * H_B
                init = (cache[2][:, j].reshape(b, nu, DH_B, DH_B),
                        cache[3][:, j].reshape(b, nu, 1, DH_B),
                        jnp.broadcast_to(cache[4][:, j].reshape(b, nu, 1, 1), (b, nu, 1, LANES)))
            else:
                kc = vc = init = None
            lam_init = 0.8 - 0.6 * math.exp(-0.3 * l)
            attn = _attention_pipe(r3(aq), r3(ak), r3(av), kc, vc, p["lam_ab"][j], lam_init)
            qk = _conv_silu(r3(bqk), p["conv_ab"][j])
            bias = jnp.pad(jnp.concatenate([p["b_ig_ab"][j].reshape(-1), p["b_fg_ab"][j].reshape(-1)]),
                           (0, LANES - 4 * H_B)).reshape(1, LANES)
            res = _mlstm(qk, r3(bv), r3(gates), bias, init, cache is None)
            hf, hb = res[0], res[1]
            if cache is None:
                even_out.append((ak, av, res[2], res[3], res[4]))
            x2, h2, aff = _out_ab(attn.reshape(b * t, W_A), hf.reshape(b * t, W_B), hb.reshape(b * t, W_B), bo, x2,
                                  gt1, p["g_attn_ab"][j].reshape(1, W_A), p["g_mlstm_ab"][j].reshape(1, W_B),
                                  p["w_out_ab_bf16"][j], route, t, per_batch, 1.0 - lam_init)
        else:
            q, v, og, lf, kk = _in_c(x2, g1, sh1, sc1, p["w_in_c_bf16"][j], p["b_f_c"][j], p["gamma_lb"], l, t,
                                     per_batch)
            r3 = lambda a: a.reshape(b, t, a.shape[-1])
            init = None if cache is None else cache[5][:, j]
            res = _hgrn(r3(q), r3(v), r3(lf), r3(kk), init, cache is None)
            if cache is None:
                odd_out.append(res[2])
            x2, h2, aff = _out_c(res[0].reshape(b * t, W_C), res[1].reshape(b * t, W_C), og, x2, gt1,
                                 p["g_hgrn_c"][j].reshape(1, W_C), p["w_out_c_bf16"][j], route, t, per_batch)
        g_final = p["g_final"].reshape(1, d) if l == depth - 1 else None
        x2 = yield from _moe(x2, h2, aff, b, t, gt2, per_batch, g_final)
    return x2.reshape(b, t, d), even_out, odd_out


def kernel(x_prompt, x_sample, cache_dattn_k, cache_dattn_v, state_mlstm_C, state_mlstm_n, state_mlstm_m,
           state_hgrn_S, c, c_ctx, w_ada, b_ada, g_norm, g_final, w_in_ab, b_ig_ab, b_fg_ab, lam_ab, conv_ab,
           g_attn_ab, g_mlstm_ab, w_out_ab, w_in_c, b_f_c, gamma_lb, g_hgrn_c, w_out_c,
           w_router, w_gate_e, w_up_e, w_down_e):
    d = D_MODEL
    bs = c.shape[0]
    bp, tp, _ = x_prompt.shape
    depth = w_ada.shape[0]
    n_main = 3 * W_A + 4 * W_B
    p = dict(
        g_norm=g_norm, g_final=g_final, b_ig_ab=b_ig_ab, b_fg_ab=b_fg_ab, lam_ab=lam_ab, conv_ab=conv_ab,
        g_attn_ab=g_attn_ab, g_mlstm_ab=g_mlstm_ab, b_f_c=b_f_c, gamma_lb=gamma_lb, g_hgrn_c=g_hgrn_c,
        w_gate_e=w_gate_e, w_up_e=w_up_e, w_down_e=w_down_e,
        w_in_ab_main=w_in_ab[:, :, :n_main].astype(BF16),
        w_in_ab_gate=jnp.pad(w_in_ab[:, :, n_main:], ((0, 0), (0, 0), (0, LANES - 4 * H_B))).astype(BF16),
        w_out_ab_bf16=w_out_ab.astype(BF16),
        w_in_c_bf16=w_in_c.astype(BF16),
        w_out_c_bf16=w_out_c.astype(BF16),
        w_router_pad=jnp.pad(w_router, ((0, 0), (0, 0), (0, LANES - N_EXPERTS))),
    )
    rows = -(-(bs + 1) // 8) * 8
    cond = jnp.concatenate([c, c_ctx.reshape(1, d), jnp.zeros((rows - bs - 1, d), F32)], axis=0)
    mods = _ada(cond, w_ada, b_ada)
    mods_s = [[mods[l, :bs, i * d:(i + 1) * d].reshape(bs, 1, d) for i in range(6)] for l in range(depth)]
    mods_c = [[mods[l, bs:bs + 1, i * d:(i + 1) * d].reshape(1, 1, d) for i in range(6)] for l in range(depth)]

    cache = (cache_dattn_k, cache_dattn_v, state_mlstm_C, state_mlstm_n, state_mlstm_m, state_hgrn_S)
    ctx = _stream(x_prompt, mods_c, False, None, None, p)
    smp = _stream(x_sample, mods_s, True, _rope_tables(x_sample.shape[1]), cache, p)
    xin_c, xin_s = next(ctx), next(smp)
    for l in range(depth):
        y_c, y_s = _experts(xin_c, xin_s, w_gate_e, w_up_e, w_down_e, l)
        try:
            xin_c = ctx.send(y_c)
        except StopIteration as done:
            y_prompt, ev, od = done.value
        try:
            xin_s = smp.send(y_s)
        except StopIteration as done:
            y_sample = done.value[0]
    new_k = jnp.stack([e[0].reshape(bp, tp, H_A, 2 * DQK_A) for e in ev], axis=1)
    new_v = jnp.stack([e[1].reshape(bp, tp, H_A, DV_A) for e in ev], axis=1)
    new_c = jnp.stack([e[2].reshape(bp, 2, H_B, DH_B, DH_B) for e in ev], axis=1)
    new_n = jnp.stack([e[3].reshape(bp, 2, H_B, DH_B) for e in ev], axis=1)
    new_m = jnp.stack([e[4][..., 0].reshape(bp, 2, H_B) for e in ev], axis=1)
    new_s = jnp.stack(od, axis=1)
    return (y_prompt, y_sample, new_k, new_v, new_c, new_n, new_m, new_s)
```

```python
import functools
import math

import jax
import jax.numpy as jnp
from jax import lax
from jax.experimental import pallas as pl
from jax.experimental.pallas import tpu as pltpu

F32 = jnp.float32
BF16 = jnp.bfloat16
I32 = jnp.int32

D_MODEL = 1024
H_A = 4
DV_A = 128
DQK_A = 64
W_A = H_A * DV_A
H_B = 4
DH_B = 128
W_B = H_B * DH_B
H_C = 8
DH_C = 128
W_C = H_C * DH_C
N_EXPERTS = 16
CAPACITY_FACTOR = 2
GRID_W = 64
ROPE_BASE = 10000.0
RMS_EPS = 1e-6
NEG_BIG = -1e30
LOG2E = 1.4426950408889634
LANES = 128
BF16_ROWS = 16
SCAN_CHUNK = 128
MLSTM_BATCHES_PER_STEP = 2
MOE_TOKEN_BLOCK = 256
VMEM_LIMIT = 56 * 1024 * 1024


def _cp(*sem):
    return pltpu.CompilerParams(dimension_semantics=sem, vmem_limit_bytes=VMEM_LIMIT)


def _dot(a, b):
    return jnp.dot(a, b, preferred_element_type=F32)


def _dot_nt(a, b):
    return lax.dot_general(a, b, (((1,), (1,)), ((), ())), preferred_element_type=F32)


def _dot_tn(a, b):
    return lax.dot_general(a, b, (((0,), (0,)), ((), ())), preferred_element_type=F32)


def _split3(x):
    hi = x.astype(BF16)
    r = x - hi.astype(F32)
    mid = r.astype(BF16)
    lo = (r - mid.astype(F32)).astype(BF16)
    return hi, mid, lo


def _dot_sel_left(sel, x):
    hi, mid, lo = _split3(x)
    return _dot(sel, hi) + _dot(sel, mid) + _dot(sel, lo)


def _dot_sel_right(x, sel):
    hi, mid, lo = _split3(x)
    return _dot(hi, sel) + _dot(mid, sel) + _dot(lo, sel)


def _dot_f32(a, b):
    ah, am, al = _split3(a)
    bh, bm, bl = _split3(b)
    return (_dot(ah, bh) + (_dot(ah, bm) + _dot(am, bh))
            + (_dot(ah, bl) + _dot(am, bm) + _dot(al, bh)))


def _dot_f32x3(a, b):
    ah = a.astype(BF16)
    al = (a - ah.astype(F32)).astype(BF16)
    bh = b.astype(BF16)
    bl = (b - bh.astype(F32)).astype(BF16)
    return _dot(ah, bh) + (_dot(ah, bl) + _dot(al, bh))


def _dot_sel_left2(sel, x):
    hi = x.astype(BF16)
    lo = (x - hi.astype(F32)).astype(BF16)
    return _dot(sel, hi) + _dot(sel, lo)


def _sigmoid(x):
    return 1.0 / (1.0 + jnp.exp(-x))


def _silu(x):
    return x * _sigmoid(x)


def _log_sigmoid(x):
    return jnp.minimum(x, 0.0) - jnp.log(1.0 + jnp.exp(-jnp.abs(x)))


def _rms_rows(x, g):
    return x * lax.rsqrt(jnp.mean(x * x, axis=-1, keepdims=True) + RMS_EPS) * g


def _iota(shape, dim):
    return lax.broadcasted_iota(I32, shape, dim)


def _ada_kernel(c_ref, w_ref, b_ref, o_ref):
    c = c_ref[...]
    o_ref[0] = _dot_f32(_silu(c), w_ref[0]) + b_ref[0]


def _ada(cond_rows, w_ada, b_ada):
    depth, d, n6 = w_ada.shape
    rows = cond_rows.shape[0]
    tn = 1536
    return pl.pallas_call(
        _ada_kernel,
        grid=(depth, n6 // tn),
        in_specs=[pl.BlockSpec((rows, d), lambda l, j: (0, 0)),
                  pl.BlockSpec((1, d, tn), lambda l, j: (l, 0, j)),
                  pl.BlockSpec((1, 1, tn), lambda l, j: (l, 0, j))],
        out_specs=pl.BlockSpec((1, rows, tn), lambda l, j: (l, 0, j)),
        out_shape=jax.ShapeDtypeStruct((depth, rows, n6), F32),
        compiler_params=_cp("parallel", "parallel"),
        name="adaln",
    )(cond_rows, w_ada, b_ada.reshape(depth, 1, n6))


def _mod_spec(per_batch, tm, t):
    if per_batch:
        return pl.BlockSpec((1, 1, D_MODEL), lambda i: (i * tm // t, 0, 0))
    return pl.BlockSpec((1, 1, D_MODEL), lambda i: (0, 0, 0))


def _row_tile(t):
    return min(512, t)


def _in_ab_kernel(*refs, use_rope):
    if use_rope:
        (x_ref, g_ref, sh_ref, sc_ref, w_ref, wg_ref, cos_ref, sa_ref, sb_ref,
         aq_ref, ak_ref, av_ref, bqk_ref, bv_ref, bo_ref, gt_ref) = refs
    else:
        (x_ref, g_ref, sh_ref, sc_ref, w_ref, wg_ref,
         aq_ref, ak_ref, av_ref, bqk_ref, bv_ref, bo_ref, gt_ref) = refs
    h = _rms_rows(x_ref[...], g_ref[...]) * (1.0 + sc_ref[0]) + sh_ref[0]
    hb = h.astype(BF16)
    y = _dot(hb, w_ref[...])
    gt_ref[...] = _dot(hb, wg_ref[...])
    aq = y[:, 0:W_A] * (DQK_A ** -0.5 * LOG2E)
    ak = y[:, W_A:2 * W_A]
    if use_rope:
        cos = jnp.concatenate([cos_ref[...]] * H_A, axis=1)
        sa = jnp.concatenate([sa_ref[...]] * H_A, axis=1)
        sb = jnp.concatenate([sb_ref[...]] * H_A, axis=1)
        half = DQK_A // 2

        def rope(v):
            return (v * cos + pltpu.roll(v, W_A - half, axis=1) * sa
                    + pltpu.roll(v, half, axis=1) * sb)

        aq = rope(aq)
        ak = rope(ak)
    aq_ref[...] = aq.astype(aq_ref.dtype)
    ak_ref[...] = ak.astype(ak_ref.dtype)
    av_ref[...] = y[:, 2 * W_A:3 * W_A].astype(av_ref.dtype)
    bqk_ref[...] = y[:, 3 * W_A:3 * W_A + 2 * W_B].astype(BF16)
    bv_ref[...] = y[:, 3 * W_A + 2 * W_B:3 * W_A + 3 * W_B].astype(BF16)
    bo_ref[...] = y[:, 3 * W_A + 3 * W_B:3 * W_A + 4 * W_B].astype(BF16)


def _in_ab(x2, g, shift, scale, w_main, w_gate, rope, t, per_batch, qkv_dtype):
    n = x2.shape[0]
    tm = _row_tile(t)
    use_rope = rope is not None
    row = lambda c: pl.BlockSpec((tm, c), lambda i: (i, 0))
    full = lambda a: pl.BlockSpec(a.shape, lambda i: (0,) * a.ndim)
    in_specs = [row(D_MODEL), full(g), _mod_spec(per_batch, tm, t), _mod_spec(per_batch, tm, t),
                full(w_main), full(w_gate)]
    args = [x2, g, shift, scale, w_main, w_gate]
    if use_rope:
        nt = t // tm
        tab = pl.BlockSpec((tm, LANES), lambda i: (i % nt, 0))
        in_specs += [tab, tab, tab]
        args += list(rope)
    widths = (W_A, W_A, W_A, 2 * W_B, W_B, W_B, LANES)
    return pl.pallas_call(
        functools.partial(_in_ab_kernel, use_rope=use_rope),
        grid=(n // tm,),
        in_specs=in_specs,
        out_specs=[row(c) for c in widths],
        out_shape=[jax.ShapeDtypeStruct((n, c), dt) for c, dt in
                   zip(widths, (qkv_dtype, qkv_dtype, qkv_dtype, BF16, BF16, BF16, F32))],
        compiler_params=_cp("parallel"),
        name="in_proj_even",
    )(*args)


def _attn_pipe_kernel(*refs, has_cache, tk, lam_init, nq):
    if has_cache:
        q_ref, k_ref, v_ref, kc_ref, vc_ref, lam_ref, o_ref, s_a, s_b, mp_s, vt_s, sc_s, vct_s = refs
    else:
        q_ref, k_ref, v_ref, lam_ref, o_ref, s_a, s_b, mp_s, vt_s = refs
    s_s = (s_a, s_b)
    i = pl.program_id(2)
    q = q_ref[0]
    tq = q.shape[0]
    first = _iota((1, LANES), 1) < DQK_A
    qs = (jnp.where(first, q, 0).astype(BF16), jnp.where(first, 0, q).astype(BF16))
    nkb = k_ref.shape[1] // tk

    @pl.when(i == 0)
    def _():
        for j in range(nkb):
            vt_s[j, 0:DV_A, :] = v_ref[0, j * tk:(j + 1) * tk, :].astype(F32).T.astype(BF16)
            vt_s[j, DV_A:, :] = jnp.ones((BF16_ROWS, tk), BF16)
        if has_cache:
            vct_s[0:DV_A, :] = vc_ref[0].astype(F32).T.astype(BF16)
            vct_s[DV_A:, :] = jnp.ones((BF16_ROWS, vct_s.shape[1]), BF16)

    def groups(x):
        return [x[:, g * LANES:(g + 1) * LANES] for g in range(x.shape[1] // LANES)]

    def body(score, finish, cur):
        prev = 1 - cur
        if finish:
            m = [jnp.max(mp_s[prev, a], axis=-1, keepdims=True) for a in range(2)]
        mpart = [jnp.full((tq, LANES), NEG_BIG, F32)] * 2
        acc = [jnp.zeros((DV_A + BF16_ROWS, tq), F32)] * 2

        def pass1(kb, store):
            for a in range(2):
                s = _dot_nt(qs[a], kb)
                store(a, s)
                mpart[a] = functools.reduce(jnp.maximum, groups(s), mpart[a])

        def pass2(a, s, vt):
            acc[a] = acc[a] + _dot_nt(vt, jnp.exp2(s - m[a]).astype(BF16))

        if has_cache:
            def store_c(a, s):
                sc_s[cur, a] = s
            if score:
                pass1(kc_ref[0].astype(BF16), store_c)
            if finish:
                for a in range(2):
                    pass2(a, sc_s[prev, a], vct_s[...])
        for j in range(nkb):
            def store_j(a, s, j=j):
                s_s[cur][a, j, :, 0:tk] = s
            if score:
                pass1(k_ref[0, j * tk:(j + 1) * tk, :].astype(BF16), store_j)
            if finish:
                for a in range(2):
                    pass2(a, s_s[prev][a, j, :, 0:tk], vt_s[j])
        if score:
            for a in range(2):
                mp_s[cur, a] = mpart[a]
        if finish:
            lp = lam_ref[...]
            lam = (jnp.exp(jnp.sum(lp[0:1] * lp[1:2], axis=-1, keepdims=True))
                   - jnp.exp(jnp.sum(lp[2:3] * lp[3:4], axis=-1, keepdims=True)) + lam_init)
            o1 = acc[0][0:DV_A] / acc[0][DV_A:DV_A + 1]
            o2 = acc[1][0:DV_A] / acc[1][DV_A:DV_A + 1]
            o_ref[0] = (o1 - lam * o2).T.astype(BF16)

    pl.when(i == 0)(lambda: body(True, False, 0))
    inner = jnp.logical_and(i > 0, i < nq)
    pl.when(jnp.logical_and(inner, i % 2 == 0))(lambda: body(True, True, 0))
    pl.when(jnp.logical_and(inner, i % 2 == 1))(lambda: body(True, True, 1))
    pl.when(i == nq)(lambda: body(False, True, nq % 2))


def _attention_pipe(aq, ak, av, kc, vc, lam_p, lam_init):
    b, t, _ = aq.shape
    tq = min(256, t)
    tk = min(512, t)
    nq = t // tq
    has_cache = kc is not None
    qspec = pl.BlockSpec((1, tq, DV_A), lambda bi, h, i: (bi, jnp.minimum(i, nq - 1), h))
    ospec = pl.BlockSpec((1, tq, DV_A), lambda bi, h, i: (bi, jnp.maximum(i - 1, 0), h))
    ks = pl.BlockSpec((1, t, DV_A), lambda bi, h, i: (bi, 0, h))
    in_specs = [qspec, ks, ks]
    args = [aq, ak, av]
    if has_cache:
        cs = pl.BlockSpec((1, kc.shape[1], DV_A), lambda bi, h, i: (bi, 0, h))
        in_specs += [cs, cs]
        args += [kc, vc]
    in_specs.append(pl.BlockSpec(lam_p.shape, lambda bi, h, i: (0, 0)))
    args.append(lam_p)
    scratch = [pltpu.VMEM((2, t // tk, tq, tk), F32), pltpu.VMEM((2, t // tk, tq, tk + LANES), F32),
               pltpu.VMEM((2, 2, tq, LANES), F32),
               pltpu.VMEM((t // tk, DV_A + BF16_ROWS, tk), BF16)]
    if has_cache:
        scratch += [pltpu.VMEM((2, 2, tq, kc.shape[1]), F32),
                    pltpu.VMEM((DV_A + BF16_ROWS, kc.shape[1]), BF16)]
    return pl.pallas_call(
        functools.partial(_attn_pipe_kernel, has_cache=has_cache, tk=tk, lam_init=lam_init, nq=nq),
        grid=(b, H_A, nq + 1),
        in_specs=in_specs,
        out_specs=ospec,
        out_shape=jax.ShapeDtypeStruct((b, t, W_A), BF16),
        scratch_shapes=scratch,
        compiler_params=_cp("parallel", "parallel", "arbitrary"),
        name="diff_attention",
    )(*args)


def _conv_kernel(x_ref, w_ref, o_ref):
    x = x_ref[0].astype(F32)
    t = x.shape[0]
    w = w_ref[...]
    r = _iota((t, 1), 0)
    prev = jnp.where(r == 0, 0.0, pltpu.roll(x, 1, axis=0))
    nxt = jnp.where(r == t - 1, 0.0, pltpu.roll(x, t - 1, axis=0))
    scale = jnp.where(pl.program_id(1) * x.shape[1] >= W_B, DH_B ** -0.5, 1.0)
    o_ref[0] = (_silu(prev * w[0:1] + x * w[1:2] + nxt * w[2:3]) * scale).astype(BF16)


def _conv_silu(bqk, conv_w):
    b, t, c = bqk.shape
    tc = 256
    spec = pl.BlockSpec((1, t, tc), lambda bi, j: (bi, 0, j))
    return pl.pallas_call(
        _conv_kernel,
        grid=(b, c // tc),
        in_specs=[spec, pl.BlockSpec((conv_w.shape[0], tc), lambda bi, j: (0, j))],
        out_specs=spec,
        out_shape=jax.ShapeDtypeStruct(bqk.shape, BF16),
        compiler_params=_cp("parallel", "parallel"),
        name="mlstm_conv",
    )(bqk, conv_w)


def _tri(n, lower):
    r = _iota((n, n), 0)
    c = _iota((n, n), 1)
    return r >= c if lower else r <= c


def _mlstm_kernel(*refs, has_init, out_state):
    qf, kf, vf, gf, qb, kb, vb, gb, bias = refs[:9]
    pos = 9
    if has_init:
        c0, n0, m0 = refs[pos:pos + 3]
        pos += 3
    hf, hb = refs[pos:pos + 2]
    pos += 2
    if out_state:
        c_out, n_out, m_out = refs[pos:pos + 3]
        pos += 3
    cn_s, m_s = refs[pos:pos + 2]
    step = pl.program_id(1)
    L = SCAN_CHUNK
    nu = 2 * H_B
    nbt = qf.shape[0]

    @pl.when(step == 0)
    def _():
        if has_init:
            for bb in range(nbt):
                for u in range(nu):
                    cn_s[bb * nu + u, :, 0:DH_B] = c0[bb, u]
                    cn_s[bb * nu + u, :, DH_B:2 * DH_B] = jnp.broadcast_to(n0[bb, u], (DH_B, DH_B)).T
                m_s[bb * nu:(bb + 1) * nu] = m0[bb]
        else:
            cn_s[...] = jnp.zeros_like(cn_s)
            m_s[...] = jnp.zeros_like(m_s)

    ones_b = jnp.ones((L, DH_B), BF16)
    sel_rows = _iota((LANES, LANES), 0)
    lower = _tri(L, True)
    upper = _tri(L, False)
    lower_b = lower.astype(BF16)
    upper_b = upper.astype(BF16)
    col = _iota((1, LANES), 1)
    units = []
    for bb, d, (q_ref, k_ref, v_ref, g_ref, h_ref) in [
            (bb, d, refs_d) for bb in range(nbt)
            for d, refs_d in enumerate(((qf, kf, vf, gf, hf), (qb, kb, vb, gb, hb)))]:
        g = g_ref[bb] + bias[...]
        g = jnp.where(col < 2 * H_B, g, _log_sigmoid(g))
        gt = g.T
        if d == 0:
            bc_all = _dot_sel_left(lower_b, g)
            br_all = _dot_sel_right(gt, upper_b)
            mask = lower
        else:
            bc_all = _dot_sel_left(upper_b, g)
            br_all = _dot_sel_right(gt, lower_b)
            mask = upper
        for h in range(H_B):
            u = d * H_B + h
            sl = slice(h * DH_B, (h + 1) * DH_B)
            units.append(dict(
                u=u, su=bb * nu + u, bb=bb, d=d, sl=sl, h_ref=h_ref, mask=mask, g=g, bc_all=bc_all,
                li_row=gt[u:u + 1, :], b_row=br_all[2 * H_B + u:2 * H_B + u + 1, :],
                q=q_ref[bb, :, sl].astype(BF16), k=k_ref[bb, :, sl].astype(F32),
                v1=jnp.concatenate([v_ref[bb, :, sl].astype(BF16), ones_b], axis=1)))
    for x in units:
        x["li_b"] = _dot_sel_right(x["g"], (sel_rows == x["u"]).astype(BF16))
        x["b_b"] = _dot_sel_right(x["bc_all"], (sel_rows == 2 * H_B + x["u"]).astype(BF16))
        x["qk"] = _dot_nt(x["q"], x["k"].astype(BF16))
        x["cn"] = cn_s[x["su"]]
        x["qc"] = _dot(x["q"], x["cn"].astype(BF16))
    for x in units:
        x["m_prev"] = m_s[x["su"]]
        a_b = x["b_b"] + x["m_prev"]
        dm = jnp.where(x["mask"], x["b_b"] - x["b_row"] + x["li_row"], NEG_BIG)
        x["m_t"] = jnp.maximum(a_b, jnp.max(dm, axis=-1, keepdims=True))
        x["w_inter"] = jnp.exp(a_b - x["m_t"])
        x["s"] = (x["qk"] * jnp.exp(dm - x["m_t"])).astype(BF16)
    for x in units:
        x["sv"] = _dot(x["s"], x["v1"])
        b_end = x["b_b"][L - 1:L, :] if x["d"] == 0 else x["b_b"][0:1, :]
        g_b = b_end - x["b_b"] + x["li_b"]
        x["m_new"] = jnp.maximum(b_end + x["m_prev"], jnp.max(g_b, axis=0, keepdims=True))
        x["w_old"] = jnp.exp(b_end + x["m_prev"] - x["m_new"])
        x["ks"] = (jnp.exp(g_b - x["m_new"]) * x["k"]).astype(BF16)
    for x in units:
        num = x["w_inter"] * x["qc"][:, 0:DH_B] + x["sv"][:, 0:DH_B]
        den = x["w_inter"] * x["qc"][:, DH_B:] + x["sv"][:, DH_B:]
        x["h_ref"][x["bb"], :, x["sl"]] = (num / jnp.maximum(jnp.abs(den), jnp.exp(-x["m_t"]))).astype(BF16)
        w2 = jnp.concatenate([x["w_old"], x["w_old"]], axis=1)
        cn_s[x["su"]] = w2 * x["cn"] + _dot_tn(x["ks"], x["v1"])
        m_s[x["su"]] = x["m_new"]

    if out_state:
        @pl.when(step == pl.num_programs(1) - 1)
        def _():
            for bb in range(nbt):
                for u in range(nu):
                    c_out[bb, u] = cn_s[bb * nu + u, :, 0:DH_B]
                    n_out[bb, u] = cn_s[bb * nu + u, :, DH_B:2 * DH_B].T[0:1, :]
                m_out[bb] = m_s[bb * nu:(bb + 1) * nu]


def _mlstm(qk, v, gates, bias, init, out_state):
    b, t, _ = v.shape
    L = SCAN_CHUNK
    nc = t // L
    nu = 2 * H_B
    nbt = MLSTM_BATCHES_PER_STEP if b % MLSTM_BATCHES_PER_STEP == 0 else 1
    fwd = lambda j: (lambda bi, c: (bi, c, j))
    bwd = lambda j: (lambda bi, c: (bi, nc - 1 - c, j))
    blk = lambda w, im: pl.BlockSpec((nbt, L, w), im)
    in_specs = [blk(W_B, fwd(0)), blk(W_B, fwd(1)), blk(W_B, fwd(0)), blk(LANES, fwd(0)),
                blk(W_B, bwd(0)), blk(W_B, bwd(1)), blk(W_B, bwd(0)), blk(LANES, bwd(0)),
                pl.BlockSpec((1, LANES), lambda bi, c: (0, 0))]
    args = [qk, qk, v, gates, qk, qk, v, gates, bias]
    st_specs = [pl.BlockSpec((nbt, nu, DH_B, DH_B), lambda bi, c: (bi, 0, 0, 0)),
                pl.BlockSpec((nbt, nu, 1, DH_B), lambda bi, c: (bi, 0, 0, 0)),
                pl.BlockSpec((nbt, nu, 1, LANES), lambda bi, c: (bi, 0, 0, 0))]
    st_shapes = [jax.ShapeDtypeStruct((b, nu, DH_B, DH_B), F32),
                 jax.ShapeDtypeStruct((b, nu, 1, DH_B), F32),
                 jax.ShapeDtypeStruct((b, nu, 1, LANES), F32)]
    has_init = init is not None
    if has_init:
        in_specs += st_specs
        args += list(init)
    out_specs = [blk(W_B, fwd(0)), blk(W_B, bwd(0))]
    out_shape = [jax.ShapeDtypeStruct((b, t, W_B), BF16)] * 2
    if out_state:
        out_specs += st_specs
        out_shape += st_shapes
    return pl.pallas_call(
        functools.partial(_mlstm_kernel, has_init=has_init, out_state=out_state),
        grid=(b // nbt, nc),
        in_specs=in_specs,
        out_specs=out_specs,
        out_shape=out_shape,
        scratch_shapes=[pltpu.VMEM((nbt * nu, DH_B, 2 * DH_B), F32), pltpu.VMEM((nbt * nu, 1, LANES), F32)],
        compiler_params=_cp("parallel", "arbitrary"),
        name="mlstm_scan",
    )(*args)


def _head_rms(x, g, heads, width):
    parts = []
    for h in range(heads):
        sl = slice(h * width, (h + 1) * width)
        parts.append(_rms_rows(x[:, sl], g[:, sl]))
    return parts


OUT_PROJ_PARTS = 4


def _row_parts(tm):
    step = tm // OUT_PROJ_PARTS
    return [slice(r, r + step) for r in range(0, tm, step)]


def _project_and_route(cats, parts, x_ref, gate_ref, w_ref, g_ref, sh_ref, sc_ref, wr_ref, o_ref, h_ref, aff_ref):
    w = w_ref[...]
    xs = [x_ref[rs, :] + gate_ref[0] * _dot(cat, w) for cat, rs in zip(cats, parts)]
    hs = []
    for x, rs in zip(xs, parts):
        o_ref[rs, :] = x
        h = _rms_rows(x, g_ref[...]) * (1.0 + sc_ref[0]) + sh_ref[0]
        h_ref[rs, :] = h.astype(BF16)
        hs.append(h)
    wr = wr_ref[...]
    logits = [_dot_f32x3(h, wr) for h in hs]
    valid = _iota((1, LANES), 1) < N_EXPERTS
    for lg, rs in zip(logits, parts):
        lg = jnp.where(valid, lg, NEG_BIG)
        e = jnp.exp(lg - jnp.max(lg, axis=-1, keepdims=True))
        e = jnp.where(valid, e, 0.0)
        aff_ref[rs, :] = e / jnp.sum(e, axis=-1, keepdims=True)


def _route_specs(route, tm, t, per_batch):
    g2, sh2, sc2, wr = route
    full = lambda a: pl.BlockSpec(a.shape, lambda i: (0,) * a.ndim)
    row = lambda c: pl.BlockSpec((tm, c), lambda i: (i, 0))
    in_specs = [full(g2), _mod_spec(per_batch, tm, t), _mod_spec(per_batch, tm, t), full(wr)]
    out_specs = [row(D_MODEL), row(D_MODEL), row(LANES)]
    return in_specs, out_specs


def _route_shapes(n):
    return [jax.ShapeDtypeStruct((n, D_MODEL), F32), jax.ShapeDtypeStruct((n, D_MODEL), BF16),
            jax.ShapeDtypeStruct((n, LANES), F32)]


def _out_ab_kernel(at_ref, hf_ref, hb_ref, bo_ref, x_ref, gate_ref, ga_ref, gm_ref, w_ref,
                   g2_ref, sh2_ref, sc2_ref, wr_ref, o_ref, h_ref, aff_ref, *, out_scale):
    parts = _row_parts(x_ref.shape[0])
    cats = []
    for rs in parts:
        oa = [p * out_scale for p in _head_rms(at_ref[rs, :].astype(F32), ga_ref[...], H_A, DV_A)]
        hm = _head_rms(hf_ref[rs, :].astype(F32) + hb_ref[rs, :].astype(F32), gm_ref[...], H_B, DH_B)
        bo = bo_ref[rs, :].astype(F32)
        ob = [_sigmoid(bo[:, h * DH_B:(h + 1) * DH_B]) * hm[h] for h in range(H_B)]
        cats.append(jnp.concatenate(oa + ob, axis=1).astype(BF16))
    _project_and_route(cats, parts, x_ref, gate_ref, w_ref, g2_ref, sh2_ref, sc2_ref, wr_ref, o_ref, h_ref, aff_ref)


def _out_ab(attn, hf, hb, bo, x2, gate, g_attn, g_mlstm, w_out, route, t, per_batch, out_scale):
    n = x2.shape[0]
    tm = _row_tile(t)
    row = lambda c: pl.BlockSpec((tm, c), lambda i: (i, 0))
    full = lambda a: pl.BlockSpec(a.shape, lambda i: (0,) * a.ndim)
    r_in, r_out = _route_specs(route, tm, t, per_batch)
    return pl.pallas_call(
        functools.partial(_out_ab_kernel, out_scale=out_scale),
        grid=(n // tm,),
        in_specs=[row(W_A), row(W_B), row(W_B), row(W_B), row(D_MODEL), _mod_spec(per_batch, tm, t),
                  full(g_attn), full(g_mlstm), full(w_out)] + r_in,
        out_specs=r_out,
        out_shape=_route_shapes(n),
        compiler_params=_cp("parallel"),
        name="out_proj_even",
    )(attn, hf, hb, bo, x2, gate, g_attn, g_mlstm, w_out, *route)


def _in_c_kernel(x_ref, g_ref, sh_ref, sc_ref, w_ref, bf_ref, gam_ref, q_ref, v_ref, og_ref, lf_ref, kk_ref, *, layer):
    gam = gam_ref[...]
    e = jnp.exp(gam - jnp.max(gam, axis=0, keepdims=True))
    p = e / jnp.sum(e, axis=0, keepdims=True)
    lbs = p[0:1]
    lbs0 = lbs
    for i in range(1, layer + 1):
        lbs = lbs + p[i:i + 1]
    lb = jnp.maximum(lbs - lbs0, 0.0)
    L = SCAN_CHUNK
    w = w_ref[...]
    chunks = [slice(c0, c0 + L) for c0 in range(0, x_ref.shape[0], L)]
    ys = []
    for rs in chunks:
        h = _rms_rows(x_ref[rs, :], g_ref[...]) * (1.0 + sc_ref[0]) + sh_ref[0]
        ys.append(_dot(h.astype(BF16), w))
    for rs, y in zip(chunks, ys):
        q_ref[rs, :] = y[:, 0:W_C].astype(BF16)
        v_ref[rs, :] = y[:, W_C:2 * W_C].astype(BF16)
        og_ref[rs, :] = y[:, 2 * W_C:3 * W_C].astype(BF16)
        for d in range(2):
            f = y[:, (3 + d) * W_C:(4 + d) * W_C] + bf_ref[d:d + 1]
            u = jnp.exp(-jnp.abs(f))
            neg = f < 0.0
            lf2 = (jnp.log(jnp.where(neg, lb + u, 1.0 + lb * u)) - jnp.log(1.0 + u)) * LOG2E
            kk_ref[rs, d * W_C:(d + 1) * W_C] = ((1.0 - lb) * jnp.where(neg, 1.0, u) / (1.0 + u)).astype(BF16)
            lf_ref[rs, d * W_C:(d + 1) * W_C] = _dot_sel_left2(_tri(L, d == 0).astype(BF16), lf2)


def _in_c(x2, g, shift, scale, w_in, b_f, gamma_lb, layer, t, per_batch):
    n = x2.shape[0]
    tm = _row_tile(t)
    row = lambda c: pl.BlockSpec((tm, c), lambda i: (i, 0))
    full = lambda a: pl.BlockSpec(a.shape, lambda i: (0,) * a.ndim)
    widths = (W_C, W_C, W_C, 2 * W_C, 2 * W_C)
    return pl.pallas_call(
        functools.partial(_in_c_kernel, layer=layer),
        grid=(n // tm,),
        in_specs=[row(D_MODEL), full(g), _mod_spec(per_batch, tm, t), _mod_spec(per_batch, tm, t),
                  full(w_in), full(b_f), full(gamma_lb)],
        out_specs=[row(c) for c in widths],
        out_shape=[jax.ShapeDtypeStruct((n, c), dt) for c, dt in zip(widths, (BF16, BF16, BF16, F32, BF16))],
        compiler_params=_cp("parallel"),
        name="in_proj_odd",
    )(x2, g, shift, scale, w_in, b_f, gamma_lb)


HGRN_LEVELS = (8, 16, 32, 64)
HGRN_DIAG = 8
HGRN_HEADS_PER_STEP = 8


def _hgrn_unit(q, kk, v, b, st_ref, b_s, rev, store_o):
    L = SCAN_CHUNK
    b_s[...] = b
    row = _iota((L, 1), 0)
    col = _iota((1, L), 1)
    k16 = kk.astype(BF16)
    v16 = v.astype(BF16)
    q = q.astype(F32)
    kk = kk.astype(F32)
    st = st_ref[...]
    o = _dot_nt((q * jnp.exp2(b)).astype(BF16), st.astype(BF16))
    yield
    c = HGRN_DIAG
    sub = _iota((c, 1), 0)
    ys = []
    for blk in range(L // c):
        r0 = blk * c
        qb = q[r0:r0 + c]
        bb = b[r0:r0 + c]
        for s in range(c):
            keep = (sub <= s) if rev else (sub >= s)
            ys.append(qb * jnp.exp2(jnp.where(keep, bb - b_s[r0 + s:r0 + s + 1, :], NEG_BIG)))
    rs = _dot_nt(jnp.concatenate(ys, axis=0).astype(BF16), k16)
    yield
    parts = []
    for m in HGRN_LEVELS:
        par = 2 * m
        pieces = []
        for p0 in range(0, L, par):
            ref_row = p0 + m if rev else p0 + m - 1
            pieces.append(jnp.broadcast_to(b_s[ref_row:ref_row + 1, :], (par, LANES)))
        bref = jnp.concatenate(pieces, axis=0)
        in_first = (row % par) < m
        reader = in_first if rev else jnp.logical_not(in_first)
        z = (jnp.where(reader, q, kk) * jnp.exp2(-jnp.abs(b - bref))).astype(BF16)
        keep = jnp.logical_and((row // par) == (col // par),
                               ((col % par) >= m) if rev else ((col % par) < m))
        parts.append((jnp.logical_and(keep, reader), _dot_nt(z, z)))
        yield
    lane = _iota((c, LANES), 1)
    rows = []
    for blk in range(L // c):
        acc = jnp.zeros((c, LANES), F32)
        for s in range(c):
            i = blk * c + s
            acc = jnp.where(lane == i, rs[i * c:(i + 1) * c], acc)
        rows.append(acc)
    att = jnp.concatenate(rows, axis=0)
    for keep, a in parts:
        att = att + jnp.where(keep, a, 0.0)
    o = o + _dot(att.astype(BF16), v16)
    b_end = b_s[0:1, :] if rev else b_s[L - 1:L, :]
    kd = (kk * jnp.exp2(b_end - b)).astype(BF16)
    st_new = st * jnp.exp2(b_end) + _dot_tn(v16, kd)
    yield
    store_o(o)
    st_ref[...] = st_new
    yield


def _hgrn_kernel(*refs, has_init, out_state, hp):
    qf, vf, lff, kkf, qb, vb, lfb, kkb = refs[:8]
    pos = 8
    if has_init:
        s0 = refs[pos]
        pos += 1
    of, ob = refs[pos:pos + 2]
    pos += 2
    if out_state:
        s_out = refs[pos]
        pos += 1
    st_s, b_s = refs[pos:pos + 2]
    step = pl.program_id(2)

    @pl.when(step == 0)
    def _():
        for d in range(2):
            for h in range(hp):
                st_s[d * hp + h] = s0[0, d, h].T if has_init else jnp.zeros((DH_C, DH_C), F32)

    def storer(o_ref, sl):
        def store(o):
            o_ref[0, :, sl] = o.astype(BF16)
        return store

    units = []
    for d, (q_ref, v_ref, lf_ref, kk_ref, o_ref) in enumerate(((qf, vf, lff, kkf, of), (qb, vb, lfb, kkb, ob))):
        for h in range(hp):
            u = d * hp + h
            sl = slice(h * DH_C, (h + 1) * DH_C)
            units.append(_hgrn_unit(q_ref[0, :, sl], kk_ref[0, :, sl], v_ref[0, :, sl], lf_ref[0, :, sl],
                                    st_s.at[u], b_s.at[u], d == 1, storer(o_ref, sl)))
    live = units
    while live:
        live = [g for g in live if next(g, "done") != "done"]

    if out_state:
        @pl.when(step == pl.num_programs(2) - 1)
        def _():
            for d in range(2):
                for h in range(hp):
                    s_out[0, d, h] = st_s[d * hp + h].T


def _hgrn(q, v, lf, kk, init, out_state):
    b, t, _ = q.shape
    L = SCAN_CHUNK
    nc = t // L
    hp = HGRN_HEADS_PER_STEP
    ng = H_C // hp
    blk = lambda im: pl.BlockSpec((1, L, hp * DH_C), im)
    fwd = lambda off: (lambda bi, h, c: (bi, c, h + off))
    bwd = lambda off: (lambda bi, h, c: (bi, nc - 1 - c, h + off))
    in_specs = [blk(fwd(0)), blk(fwd(0)), blk(fwd(0)), blk(fwd(0)),
                blk(bwd(0)), blk(bwd(0)), blk(bwd(ng)), blk(bwd(ng))]
    args = [q, v, lf, kk, q, v, lf, kk]
    st_spec = pl.BlockSpec((1, 2, hp, DH_C, DH_C), lambda bi, h, c: (bi, 0, h, 0, 0))
    has_init = init is not None
    if has_init:
        in_specs.append(st_spec)
        args.append(init)
    out_specs = [blk(fwd(0)), blk(bwd(0))]
    out_shape = [jax.ShapeDtypeStruct((b, t, W_C), BF16)] * 2
    if out_state:
        out_specs.append(st_spec)
        out_shape.append(jax.ShapeDtypeStruct((b, 2, H_C, DH_C, DH_C), F32))
    return pl.pallas_call(
        functools.partial(_hgrn_kernel, has_init=has_init, out_state=out_state, hp=hp),
        grid=(b, ng, nc),
        in_specs=in_specs,
        out_specs=out_specs,
        out_shape=out_shape,
        scratch_shapes=[pltpu.VMEM((2 * hp, DH_C, DH_C), F32), pltpu.VMEM((2 * hp, L, DH_C), F32)],
        compiler_params=_cp("parallel", "parallel", "arbitrary"),
        name="hgrn_scan",
    )(*args)


def _out_c_kernel(of_ref, ob_ref, og_ref, x_ref, gate_ref, gh_ref, w_ref,
                  g2_ref, sh2_ref, sc2_ref, wr_ref, o_ref, h_ref, aff_ref):
    parts = _row_parts(x_ref.shape[0])
    cats = []
    for rs in parts:
        hm = _head_rms(of_ref[rs, :].astype(F32) + ob_ref[rs, :].astype(F32), gh_ref[...], H_C, DH_C)
        og = og_ref[rs, :].astype(F32)
        heads = [hm[h] * _silu(og[:, h * DH_C:(h + 1) * DH_C]) for h in range(H_C)]
        cats.append(jnp.concatenate(heads, axis=1).astype(BF16))
    _project_and_route(cats, parts, x_ref, gate_ref, w_ref, g2_ref, sh2_ref, sc2_ref, wr_ref, o_ref, h_ref, aff_ref)


def _out_c(of, ob, og, x2, gate, g_hgrn, w_out, route, t, per_batch):
    n = x2.shape[0]
    tm = _row_tile(t)
    row = lambda c: pl.BlockSpec((tm, c), lambda i: (i, 0))
    full = lambda a: pl.BlockSpec(a.shape, lambda i: (0,) * a.ndim)
    r_in, r_out = _route_specs(route, tm, t, per_batch)
    return pl.pallas_call(
        _out_c_kernel,
        grid=(n // tm,),
        in_specs=[row(W_C), row(W_C), row(W_C), row(D_MODEL), _mod_spec(per_batch, tm, t),
                  full(g_hgrn), full(w_out)] + r_in,
        out_specs=r_out,
        out_shape=_route_shapes(n),
        compiler_params=_cp("parallel"),
        name="out_proj_odd",
    )(of, ob, og, x2, gate, g_hgrn, w_out, *route)


def _topk_kernel(aff_ref, pos_ref, s0_ref, bits_s, sel_s, *, cap, tb):
    t = aff_ref.shape[1]
    nb = t // tb
    bits_s[...] = pltpu.bitcast(aff_ref[0], I32)

    def count_ge(cand):
        def body(j, acc):
            r = pl.multiple_of(j * tb, tb)
            blk = bits_s[pl.ds(r, tb), :]
            return acc + jnp.sum((blk >= cand).astype(F32), axis=0, keepdims=True)
        return lax.fori_loop(0, nb, body, jnp.zeros((1, LANES), F32))

    def bit_body(i, thr):
        cand = thr | lax.shift_left(jnp.int32(1), 30 - i)
        return jnp.where(count_ge(cand) >= cap, cand, thr)

    thr = lax.fori_loop(0, 31, bit_body, jnp.zeros((1, LANES), I32))

    def count_gt(j, acc):
        r = pl.multiple_of(j * tb, tb)
        return acc + jnp.sum((bits_s[pl.ds(r, tb), :] > thr).astype(F32), axis=0, keepdims=True)

    need = cap - lax.fori_loop(0, nb, count_gt, jnp.zeros((1, LANES), F32))
    strict = (_iota((tb, tb), 0) > _iota((tb, tb), 1)).astype(BF16)

    def select(j, carry):
        r = pl.multiple_of(j * tb, tb)
        blk = bits_s[pl.ds(r, tb), :]
        eq = blk == thr
        before = carry + _dot(strict, eq.astype(BF16))
        sel = jnp.logical_or(blk > thr, jnp.logical_and(eq, before < need))
        sel_s[pl.ds(r, tb), :] = sel.astype(F32)
        return carry + jnp.sum(eq.astype(F32), axis=0, keepdims=True)

    lax.fori_loop(0, nb, select, jnp.zeros((1, LANES), F32))

    def place(j, carry):
        r = pl.multiple_of(j * tb, tb)
        sel = sel_s[pl.ds(r, tb), :]
        slot = carry + _dot(strict, sel.astype(BF16))
        pos_ref[0, pl.ds(r, tb), :] = jnp.where(sel > 0.5, slot, -1.0).astype(I32)
        s0_ref[0, pl.ds(j, 1), :] = carry.astype(I32)
        return carry + jnp.sum(sel, axis=0, keepdims=True)

    lax.fori_loop(0, nb, place, jnp.zeros((1, LANES), F32))


def _topk(aff, cap, tb):
    b, t, _ = aff.shape
    nb = t // tb
    return pl.pallas_call(
        functools.partial(_topk_kernel, cap=cap, tb=tb),
        grid=(b,),
        in_specs=[pl.BlockSpec((1, t, LANES), lambda bi: (bi, 0, 0))],
        out_specs=[pl.BlockSpec((1, t, LANES), lambda bi: (bi, 0, 0)),
                   pl.BlockSpec((1, nb, LANES), lambda bi: (bi, 0, 0))],
        out_shape=[jax.ShapeDtypeStruct((b, t, LANES), I32), jax.ShapeDtypeStruct((b, nb, LANES), I32)],
        scratch_shapes=[pltpu.VMEM((t, LANES), I32), pltpu.VMEM((t, LANES), F32)],
        compiler_params=_cp("parallel"),
        name="moe_topk",
    )(aff)


MOE_SMALL_COUNT = 48
MOE_STACK_ROWS = 1024


def _windows(cap, tb):
    narrow = MOE_SMALL_COUNT + BF16_ROWS
    if cap <= narrow:
        return (cap,)
    return (narrow, min(tb + BF16_ROWS, cap))


def _window_starts(s0_ref, bi, j, cap, w):
    if w == cap:
        return [0] * N_EXPERTS
    out = []
    for e in range(N_EXPERTS):
        st = jnp.minimum((s0_ref[bi, j, e] // BF16_ROWS) * BF16_ROWS, cap - w)
        out.append(pl.multiple_of(st, BF16_ROWS))
    return out


def _per_window(windows, narrow_ref, bi, j, run):
    if len(windows) == 1:
        run(windows[0])
    else:
        narrow = narrow_ref[bi, j] > 0
        pl.when(narrow)(lambda: run(windows[0]))
        pl.when(jnp.logical_not(narrow))(lambda: run(windows[1]))


def _gather_kernel(s0_ref, narrow_ref, h_ref, post_ref, o_ref, *, cap, tb):
    bi = pl.program_id(0)
    j = pl.program_id(1)

    @pl.when(j == 0)
    def _():
        o_ref[...] = jnp.zeros_like(o_ref)

    def run(w):
        starts = _window_starts(s0_ref, bi, j, cap, w)
        post = post_ref[0]
        r = _iota((w, tb), 0)
        group = max(1, min(N_EXPERTS, MOE_STACK_ROWS // w))
        for e0 in range(0, N_EXPERTS, group):
            es = range(e0, min(e0 + group, N_EXPERTS))
            onehot = jnp.concatenate(
                [jnp.where((post[e:e + 1, :] - starts[e]) == r, 1.0, 0.0).astype(BF16) for e in es],
                axis=0)
            rows = _dot(onehot, h_ref[0]).astype(BF16)
            for k, e in enumerate(es):
                cur = o_ref[0, e, pl.ds(starts[e], w), :]
                o_ref[0, e, pl.ds(starts[e], w), :] = cur + rows[k * w:(k + 1) * w]

    _per_window(_windows(cap, tb), narrow_ref, bi, j, run)


def _gather(h2, post, s0, narrow, cap, tb):
    b, t, d = h2.shape
    nb = t // tb
    grid_spec = pltpu.PrefetchScalarGridSpec(
        num_scalar_prefetch=2,
        grid=(b, nb),
        in_specs=[pl.BlockSpec((1, tb, d), lambda bi, j, s, n: (bi, j, 0)),
                  pl.BlockSpec((1, N_EXPERTS, tb), lambda bi, j, s, n: (bi, 0, j))],
        out_specs=pl.BlockSpec((1, N_EXPERTS, cap, d), lambda bi, j, s, n: (bi, 0, 0, 0)),
    )
    return pl.pallas_call(
        functools.partial(_gather_kernel, cap=cap, tb=tb),
        grid_spec=grid_spec,
        out_shape=jax.ShapeDtypeStruct((b, N_EXPERTS, cap, d), BF16),
        compiler_params=_cp("parallel", "arbitrary"),
        name="moe_gather",
    )(s0, narrow, h2, post)


def _expert_kernel(xc_ref, xs_ref, wg_ref, wu_ref, wd_ref, yc_ref, ys_ref, wg_s, wu_s, wd_s):
    i = pl.program_id(1)

    @pl.when(i == 0)
    def _():
        wg_s[...] = wg_ref[0].astype(BF16)
        wu_s[...] = wu_ref[0].astype(BF16)
        wd_s[...] = wd_ref[0].astype(BF16)

    def mlp(x_ref, y_ref):
        shp = x_ref.shape
        rows = shp[0] * shp[2]
        x = x_ref[...].reshape(rows, shp[3])
        step = rows // max(1, rows // 128)
        xs = [x[r:r + step] for r in range(0, rows, step)]
        hs = [(_silu(_dot(xp, wg_s[...])) * _dot(xp, wu_s[...])).astype(BF16) for xp in xs]
        ys = [_dot(hm, wd_s[...]).astype(BF16) for hm in hs]
        y_ref[...] = jnp.concatenate(ys, axis=0).reshape(shp)

    pl.when(i == 0)(lambda: mlp(xc_ref, yc_ref))
    pl.when(i > 0)(lambda: mlp(xs_ref, ys_ref))


def _experts(xin_c, xin_s, w_gate, w_up, w_down, layer):
    bc, e, cap_c, d = xin_c.shape
    bs, _, cap_s, _ = xin_s.shape
    f = w_gate.shape[-1]
    cspec = pl.BlockSpec((bc, 1, cap_c, d), lambda ei, i: (0, ei, 0, 0))
    sspec = pl.BlockSpec((1, 1, cap_s, d), lambda ei, i: (jnp.maximum(i - 1, 0), ei, 0, 0))
    ws = lambda a: pl.BlockSpec((None, 1) + a.shape[2:], lambda ei, i: (layer, ei, 0, 0))
    return pl.pallas_call(
        _expert_kernel,
        grid=(e, bs + 1),
        in_specs=[cspec, sspec, ws(w_gate), ws(w_up), ws(w_down)],
        out_specs=[cspec, sspec],
        out_shape=[jax.ShapeDtypeStruct(xin_c.shape, BF16), jax.ShapeDtypeStruct(xin_s.shape, BF16)],
        scratch_shapes=[pltpu.VMEM((d, f), BF16), pltpu.VMEM((d, f), BF16), pltpu.VMEM((f, d), BF16)],
        compiler_params=_cp("parallel", "arbitrary"),
        name="moe_experts",
    )(xin_c, xin_s, w_gate, w_up, w_down)


def _combine_kernel(*refs, cap, tb, final):
    windows = _windows(cap, tb)
    s0_ref, narrow_ref, y_ref, pos_ref, aff_ref, x_ref, gate_ref = refs[:7]
    pos_n = 7
    if final:
        gf_ref = refs[pos_n]
        pos_n += 1
    spread_ref, row_id_ref = refs[pos_n:pos_n + 2]
    o_ref = refs[-1]
    bi = pl.program_id(0)
    j = pl.program_id(1)
    offset = 1024
    lane = _iota((1, N_EXPERTS), 1)

    def run(w):
        starts = _window_starts(s0_ref, bi, j, cap, w)
        if w == windows[0]:
            stv = jnp.zeros((1, N_EXPERTS), I32)
            for e in range(N_EXPERTS):
                stv = jnp.where(lane == e, starts[e], stv)
            rel = pos_ref[0][:, :N_EXPERTS] - stv + offset
            spread = spread_ref[...]
            hi = lax.shift_right_logical(rel, 5).astype(F32).astype(BF16)
            lo = jnp.bitwise_and(rel, 31).astype(F32).astype(BF16)
            relx = 32.0 * _dot(hi, spread) + _dot(lo, spread) - offset
            gx = _dot(aff_ref[0][:, :N_EXPERTS].astype(BF16), spread)
            g = jnp.where(relx == row_id_ref[...], gx, 0.0)
            ys = jnp.concatenate([y_ref[0, e, pl.ds(starts[e], w), :] for e in range(N_EXPERTS)], axis=0)
            acc = _dot(g.astype(BF16), ys)
        else:
            pos = pos_ref[0]
            aff = aff_ref[0].astype(BF16).astype(F32)
            col = _iota((tb, w), 1)
            acc = jnp.zeros((tb, D_MODEL), F32)
            for e in range(N_EXPERTS):
                g = jnp.where((pos[:, e:e + 1] - starts[e]) == col, aff[:, e:e + 1], 0.0)
                acc = acc + _dot(g.astype(BF16), y_ref[0, e, pl.ds(starts[e], w), :])
        out = x_ref[0] + gate_ref[0] * acc
        if final:
            out = _rms_rows(out, gf_ref[...])
        o_ref[0] = out

    _per_window(windows, narrow_ref, bi, j, run)


def _combine(y, pos, aff, x3, gate, s0, narrow, cap, tb, per_batch, g_final):
    b, t, d = x3.shape
    nb = t // tb
    final = g_final is not None
    tok = lambda c: pl.BlockSpec((1, tb, c), lambda bi, j, s, n: (bi, j, 0))
    const = lambda a: pl.BlockSpec(a.shape, lambda bi, j, s, n: (0, 0))
    in_specs = [pl.BlockSpec((1, N_EXPERTS, cap, d), lambda bi, j, s, n: (bi, 0, 0, 0)),
                tok(LANES), tok(LANES), tok(d),
                pl.BlockSpec((1, 1, d), (lambda bi, j, s, n: (bi, 0, 0)) if per_batch
                             else (lambda bi, j, s, n: (0, 0, 0)))]
    args = [y, pos, aff, x3, gate]
    if final:
        in_specs.append(const(g_final))
        args.append(g_final)
    w = _windows(cap, tb)[0]
    spread = jnp.repeat(jnp.eye(N_EXPERTS, dtype=BF16), w, axis=1)
    row_id = jnp.tile(jnp.arange(w, dtype=F32), N_EXPERTS).reshape(1, -1)
    in_specs += [const(spread), const(row_id)]
    args += [spread, row_id]
    grid_spec = pltpu.PrefetchScalarGridSpec(
        num_scalar_prefetch=2, grid=(b, nb), in_specs=in_specs, out_specs=tok(d))
    return pl.pallas_call(
        functools.partial(_combine_kernel, cap=cap, tb=tb, final=final),
        grid_spec=grid_spec,
        out_shape=jax.ShapeDtypeStruct((b, t, d), F32),
        compiler_params=_cp("parallel", "arbitrary"),
        name="moe_combine",
    )(s0, narrow, *args)


def _moe(x2, h2, aff, b, t, gate, per_batch, g_final):
    cap = CAPACITY_FACTOR * t // N_EXPERTS
    tb = min(MOE_TOKEN_BLOCK, t)
    aff_wide = aff.reshape(b, t, LANES)
    aff3 = aff_wide[:, :, :N_EXPERTS]
    per = LANES // N_EXPERTS
    groups = -(-b // per)
    nb = t // tb

    def pack(a):
        a = jnp.pad(a, ((0, groups * per - b), (0, 0), (0, 0)))
        return a.reshape(groups, per, a.shape[1], N_EXPERTS).transpose(0, 2, 1, 3).reshape(groups, a.shape[1], LANES)

    def unpack(a):
        return a.reshape(groups, a.shape[1], per, N_EXPERTS).transpose(0, 2, 1, 3).reshape(
            groups * per, a.shape[1], N_EXPERTS)[:b]

    pos, s0 = _topk(pack(aff3), cap, tb)
    pos, s0 = unpack(pos), unpack(s0)
    counts = jnp.diff(s0, axis=1, append=jnp.full((b, 1, N_EXPERTS), cap, I32))
    narrow = jnp.all(counts <= MOE_SMALL_COUNT, axis=-1).astype(I32)
    xin = _gather(h2.reshape(b, t, D_MODEL), pos.transpose(0, 2, 1), s0, narrow, cap, tb)
    y = yield xin
    pos_wide = jnp.pad(pos, ((0, 0), (0, 0), (0, LANES - N_EXPERTS)), constant_values=-1)
    out = _combine(y, pos_wide, aff_wide, x2.reshape(b, t, D_MODEL), gate, s0, narrow, cap, tb, per_batch, g_final)
    return out.reshape(b * t, D_MODEL)


def _rope_tables(t):
    rows = t // GRID_W
    row = jnp.repeat(jnp.arange(rows, dtype=F32), GRID_W)
    col = jnp.tile(jnp.arange(GRID_W, dtype=F32), rows)
    n_freq = DQK_A // 4
    inv_freq = jnp.power(ROPE_BASE, -jnp.arange(n_freq, dtype=F32) / n_freq)
    ang = jnp.concatenate([row[:, None] * inv_freq, col[:, None] * inv_freq], axis=-1)
    cos, sin = jnp.cos(ang), jnp.sin(ang)
    zero = jnp.zeros_like(sin)
    reps = LANES // DQK_A
    cos_t = jnp.tile(jnp.concatenate([cos, cos], axis=-1), (1, reps))
    sin_a = jnp.tile(jnp.concatenate([-sin, zero], axis=-1), (1, reps))
    sin_b = jnp.tile(jnp.concatenate([zero, sin], axis=-1), (1, reps))
    return cos_t, sin_a, sin_b


def _stream(x, mods, per_batch, rope, cache, p):
    b, t, d = x.shape
    x2 = x.reshape(b * t, d)
    depth = p["g_norm"].shape[0]
    even_out, odd_out = [], []
    for l in range(depth):
        j = l // 2
        sh1, sc1, gt1, sh2, sc2, gt2 = mods[l]
        g1 = p["g_norm"][l, 0].reshape(1, d)
        g2 = p["g_norm"][l, 1].reshape(1, d)
        route = (g2, sh2, sc2, p["w_router_pad"][l])
        if l % 2 == 0:
            aq, ak, av, bqk, bv, bo, gates = _in_ab(x2, g1, sh1, sc1, p["w_in_ab_main"][j], p["w_in_ab_gate"][j],
                                                     rope, t, per_batch, F32 if cache is None else BF16)
            r3 = lambda a: a.reshape(b, t, a.shape[-1])
            if cache is not None:
                kc = cache[0][:, j].reshape(b, -1, W_A)
                vc = cache[1][:, j].reshape(b, -1, W_A)
                nu = 2 * H_B
                init = (cache[2][:, j].reshape(b, nu, DH_B, DH_B),
                        cache[3][:, j].reshape(b, nu, 1, DH_B),
                        jnp.broadcast_to(cache[4][:, j].reshape(b, nu, 1, 1), (b, nu, 1, LANES)))
            else:
                kc = vc = init = None
            lam_init = 0.8 - 0.6 * math.exp(-0.3 * l)
            attn = _attention_pipe(r3(aq), r3(ak), r3(av), kc, vc, p["lam_ab"][j], lam_init)
            qk = _conv_silu(r3(bqk), p["conv_ab"][j])
            bias = jnp.pad(jnp.concatenate([p["b_ig_ab"][j].reshape(-1), p["b_fg_ab"][j].reshape(-1)]),
                           (0, LANES - 4 * H_B)).reshape(1, LANES)
            res = _mlstm(qk, r3(bv), r3(gates), bias, init, cache is None)
            hf, hb = res[0], res[1]
            if cache is None:
                even_out.append((ak, av, res[2], res[3], res[4]))
            x2, h2, aff = _out_ab(attn.reshape(b * t, W_A), hf.reshape(b * t, W_B), hb.reshape(b * t, W_B), bo, x2,
                                  gt1, p["g_attn_ab"][j].reshape(1, W_A), p["g_mlstm_ab"][j].reshape(1, W_B),
                                  p["w_out_ab_bf16"][j], route, t, per_batch, 1.0 - lam_init)
        else:
            q, v, og, lf, kk = _in_c(x2, g1, sh1, sc1, p["w_in_c_bf16"][j], p["b_f_c"][j], p["gamma_lb"], l, t,
                                     per_batch)
            r3 = lambda a: a.reshape(b, t, a.shape[-1])
            init = None if cache is None else cache[5][:, j]
            res = _hgrn(r3(q), r3(v), r3(lf), r3(kk), init, cache is None)
            if cache is None:
                odd_out.append(res[2])
            x2, h2, aff = _out_c(res[0].reshape(b * t, W_C), res[1].reshape(b * t, W_C), og, x2, gt1,
                                 p["g_hgrn_c"][j].reshape(1, W_C), p["w_out_c_bf16"][j], route, t, per_batch)
        g_final = p["g_final"].reshape(1, d) if l == depth - 1 else None
        x2 = yield from _moe(x2, h2, aff, b, t, gt2, per_batch, g_final)
    return x2.reshape(b, t, d), even_out, odd_out


def kernel(x_prompt, x_sample, cache_dattn_k, cache_dattn_v, state_mlstm_C, state_mlstm_n, state_mlstm_m,
           state_hgrn_S, c, c_ctx, w_ada, b_ada, g_norm, g_final, w_in_ab, b_ig_ab, b_fg_ab, lam_ab, conv_ab,
           g_attn_ab, g_mlstm_ab, w_out_ab, w_in_c, b_f_c, gamma_lb, g_hgrn_c, w_out_c,
           w_router, w_gate_e, w_up_e, w_down_e):
    d = D_MODEL
    bs = c.shape[0]
    bp, tp, _ = x_prompt.shape
    depth = w_ada.shape[0]
    n_main = 3 * W_A + 4 * W_B
    p = dict(
        g_norm=g_norm, g_final=g_final, b_ig_ab=b_ig_ab, b_fg_ab=b_fg_ab, lam_ab=lam_ab, conv_ab=conv_ab,
        g_attn_ab=g_attn_ab, g_mlstm_ab=g_mlstm_ab, b_f_c=b_f_c, gamma_lb=gamma_lb, g_hgrn_c=g_hgrn_c,
        w_gate_e=w_gate_e, w_up_e=w_up_e, w_down_e=w_down_e,
        w_in_ab_main=w_in_ab[:, :, :n_main].astype(BF16),
        w_in_ab_gate=jnp.pad(w_in_ab[:, :, n_main:], ((0, 0), (0, 0), (0, LANES - 4 * H_B))).astype(BF16),
        w_out_ab_bf16=w_out_ab.astype(BF16),
        w_in_c_bf16=w_in_c.astype(BF16),
        w_out_c_bf16=w_out_c.astype(BF16),
        w_router_pad=jnp.pad(w_router, ((0, 0), (0, 0), (0, LANES - N_EXPERTS))),
    )
    rows = -(-(bs + 1) // 8) * 8
    cond = jnp.concatenate([c, c_ctx.reshape(1, d), jnp.zeros((rows - bs - 1, d), F32)], axis=0)
    mods = _ada(cond, w_ada, b_ada)
    mods_s = [[mods[l, :bs, i * d:(i + 1) * d].reshape(bs, 1, d) for i in range(6)] for l in range(depth)]
    mods_c = [[mods[l, bs:bs + 1, i * d:(i + 1) * d].reshape(1, 1, d) for i in range(6)] for l in range(depth)]

    cache = (cache_dattn_k, cache_dattn_v, state_mlstm_C, state_mlstm_n, state_mlstm_m, state_hgrn_S)
    ctx = _stream(x_prompt, mods_c, False, None, None, p)
    smp = _stream(x_sample, mods_s, True, _rope_tables(x_sample.shape[1]), cache, p)
    xin_c, xin_s = next(ctx), next(smp)
    for l in range(depth):
        y_c, y_s = _experts(xin_c, xin_s, w_gate_e, w_up_e, w_down_e, l)
        try:
            xin_c = ctx.send(y_c)
        except StopIteration as done:
            y_prompt, ev, od = done.value
        try:
            xin_s = smp.send(y_s)
        except StopIteration as done:
            y_sample = done.value[0]
    new_k = jnp.stack([e[0].reshape(bp, tp, H_A, 2 * DQK_A) for e in ev], axis=1)
    new_v = jnp.stack([e[1].reshape(bp, tp, H_A, DV_A) for e in ev], axis=1)
    new_c = jnp.stack([e[2].reshape(bp, 2, H_B, DH_B, DH_B) for e in ev], axis=1)
    new_n = jnp.stack([e[3].reshape(bp, 2, H_B, DH_B) for e in ev], axis=1)
    new_m = jnp.stack([e[4][..., 0].reshape(bp, 2, H_B) for e in ev], axis=1)
    new_s = jnp.stack(od, axis=1)
    return (y_prompt, y_sample, new_k, new_v, new_c, new_n, new_m, new_s)
```

```python
import functools
import math

import jax
import jax.numpy as jnp
from jax import lax
from jax.experimental import pallas as pl
from jax.experimental.pallas import tpu as pltpu

F32 = jnp.float32
BF16 = jnp.bfloat16
I32 = jnp.int32

D_MODEL = 1024
H_A = 4
DV_A = 128
DQK_A = 64
W_A = H_A * DV_A
H_B = 4
DH_B = 128
W_B = H_B * DH_B
H_C = 8
DH_C = 128
W_C = H_C * DH_C
N_EXPERTS = 16
CAPACITY_FACTOR = 2
GRID_W = 64
ROPE_BASE = 10000.0
RMS_EPS = 1e-6
NEG_BIG = -1e30
LOG2E = 1.4426950408889634
LANES = 128
BF16_ROWS = 16
SCAN_CHUNK = 128
MLSTM_BATCHES_PER_STEP = 2
MOE_TOKEN_BLOCK = 256
VMEM_LIMIT = 56 * 1024 * 1024


def _cp(*sem):
    return pltpu.CompilerParams(dimension_semantics=sem, vmem_limit_bytes=VMEM_LIMIT)


def _dot(a, b):
    return jnp.dot(a, b, preferred_element_type=F32)


def _dot_nt(a, b):
    return lax.dot_general(a, b, (((1,), (1,)), ((), ())), preferred_element_type=F32)


def _dot_tn(a, b):
    return lax.dot_general(a, b, (((0,), (0,)), ((), ())), preferred_element_type=F32)


def _split3(x):
    hi = x.astype(BF16)
    r = x - hi.astype(F32)
    mid = r.astype(BF16)
    lo = (r - mid.astype(F32)).astype(BF16)
    return hi, mid, lo


def _dot_sel_left(sel, x):
    hi, mid, lo = _split3(x)
    return _dot(sel, hi) + _dot(sel, mid) + _dot(sel, lo)


def _dot_sel_right(x, sel):
    hi, mid, lo = _split3(x)
    return _dot(hi, sel) + _dot(mid, sel) + _dot(lo, sel)


def _dot_f32(a, b):
    ah, am, al = _split3(a)
    bh, bm, bl = _split3(b)
    return (_dot(ah, bh) + (_dot(ah, bm) + _dot(am, bh))
            + (_dot(ah, bl) + _dot(am, bm) + _dot(al, bh)))


def _dot_f32x3(a, b):
    ah = a.astype(BF16)
    al = (a - ah.astype(F32)).astype(BF16)
    bh = b.astype(BF16)
    bl = (b - bh.astype(F32)).astype(BF16)
    return _dot(ah, bh) + (_dot(ah, bl) + _dot(al, bh))


def _dot_sel_left2(sel, x):
    hi = x.astype(BF16)
    lo = (x - hi.astype(F32)).astype(BF16)
    return _dot(sel, hi) + _dot(sel, lo)


def _sigmoid(x):
    return 1.0 / (1.0 + jnp.exp(-x))


def _silu(x):
    return x * _sigmoid(x)


def _log_sigmoid(x):
    return jnp.minimum(x, 0.0) - jnp.log(1.0 + jnp.exp(-jnp.abs(x)))


def _rms_rows(x, g):
    return x * lax.rsqrt(jnp.mean(x * x, axis=-1, keepdims=True) + RMS_EPS) * g


def _iota(shape, dim):
    return lax.broadcasted_iota(I32, shape, dim)


def _ada_kernel(c_ref, w_ref, b_ref, o_ref):
    c = c_ref[...]
    o_ref[0] = _dot_f32(_silu(c), w_ref[0]) + b_ref[0]


def _ada(cond_rows, w_ada, b_ada):
    depth, d, n6 = w_ada.shape
    rows = cond_rows.shape[0]
    tn = 1536
    return pl.pallas_call(
        _ada_kernel,
        grid=(depth, n6 // tn),
        in_specs=[pl.BlockSpec((rows, d), lambda l, j: (0, 0)),
                  pl.BlockSpec((1, d, tn), lambda l, j: (l, 0, j)),
                  pl.BlockSpec((1, 1, tn), lambda l, j: (l, 0, j))],
        out_specs=pl.BlockSpec((1, rows, tn), lambda l, j: (l, 0, j)),
        out_shape=jax.ShapeDtypeStruct((depth, rows, n6), F32),
        compiler_params=_cp("parallel", "parallel"),
        name="adaln",
    )(cond_rows, w_ada, b_ada.reshape(depth, 1, n6))


def _mod_spec(per_batch, tm, t):
    if per_batch:
        return pl.BlockSpec((1, 1, D_MODEL), lambda i: (i * tm // t, 0, 0))
    return pl.BlockSpec((1, 1, D_MODEL), lambda i: (0, 0, 0))


def _row_tile(t):
    return min(512, t)


def _in_ab_kernel(*refs, use_rope):
    if use_rope:
        (x_ref, g_ref, sh_ref, sc_ref, w_ref, wg_ref, cos_ref, sa_ref, sb_ref,
         aq_ref, ak_ref, av_ref, bqk_ref, bv_ref, bo_ref, gt_ref) = refs
    else:
        (x_ref, g_ref, sh_ref, sc_ref, w_ref, wg_ref,
         aq_ref, ak_ref, av_ref, bqk_ref, bv_ref, bo_ref, gt_ref) = refs
    h = _rms_rows(x_ref[...], g_ref[...]) * (1.0 + sc_ref[0]) + sh_ref[0]
    hb = h.astype(BF16)
    y = _dot(hb, w_ref[...])
    gt_ref[...] = _dot(hb, wg_ref[...])
    aq = y[:, 0:W_A] * (DQK_A ** -0.5 * LOG2E)
    ak = y[:, W_A:2 * W_A]
    if use_rope:
        cos = jnp.concatenate([cos_ref[...]] * H_A, axis=1)
        sa = jnp.concatenate([sa_ref[...]] * H_A, axis=1)
        sb = jnp.concatenate([sb_ref[...]] * H_A, axis=1)
        half = DQK_A // 2

        def rope(v):
            return (v * cos + pltpu.roll(v, W_A - half, axis=1) * sa
                    + pltpu.roll(v, half, axis=1) * sb)

        aq = rope(aq)
        ak = rope(ak)
    aq_ref[...] = aq.astype(aq_ref.dtype)
    ak_ref[...] = ak.astype(ak_ref.dtype)
    av_ref[...] = y[:, 2 * W_A:3 * W_A].astype(av_ref.dtype)
    bqk_ref[...] = y[:, 3 * W_A:3 * W_A + 2 * W_B].astype(BF16)
    bv_ref[...] = y[:, 3 * W_A + 2 * W_B:3 * W_A + 3 * W_B].astype(BF16)
    bo_ref[...] = y[:, 3 * W_A + 3 * W_B:3 * W_A + 4 * W_B].astype(BF16)


def _in_ab(x2, g, shift, scale, w_main, w_gate, rope, t, per_batch, qkv_dtype):
    n = x2.shape[0]
    tm = _row_tile(t)
    use_rope = rope is not None
    row = lambda c: pl.BlockSpec((tm, c), lambda i: (i, 0))
    full = lambda a: pl.BlockSpec(a.shape, lambda i: (0,) * a.ndim)
    in_specs = [row(D_MODEL), full(g), _mod_spec(per_batch, tm, t), _mod_spec(per_batch, tm, t),
                full(w_main), full(w_gate)]
    args = [x2, g, shift, scale, w_main, w_gate]
    if use_rope:
        nt = t // tm
        tab = pl.BlockSpec((tm, LANES), lambda i: (i % nt, 0))
        in_specs += [tab, tab, tab]
        args += list(rope)
    widths = (W_A, W_A, W_A, 2 * W_B, W_B, W_B, LANES)
    return pl.pallas_call(
        functools.partial(_in_ab_kernel, use_rope=use_rope),
        grid=(n // tm,),
        in_specs=in_specs,
        out_specs=[row(c) for c in widths],
        out_shape=[jax.ShapeDtypeStruct((n, c), dt) for c, dt in
                   zip(widths, (qkv_dtype, qkv_dtype, qkv_dtype, BF16, BF16, BF16, F32))],
        compiler_params=_cp("parallel"),
        name="in_proj_even",
    )(*args)


def _attn_pipe_kernel(*refs, has_cache, tk, lam_init, nq):
    if has_cache:
        q_ref, k_ref, v_ref, kc_ref, vc_ref, lam_ref, o_ref, s_a, s_b, mp_s, vt_s, sc_s, vct_s = refs
    else:
        q_ref, k_ref, v_ref, lam_ref, o_ref, s_a, s_b, mp_s, vt_s = refs
    s_s = (s_a, s_b)
    i = pl.program_id(2)
    q = q_ref[0]
    tq = q.shape[0]
    first = _iota((1, LANES), 1) < DQK_A
    qs = (jnp.where(first, q, 0).astype(BF16), jnp.where(first, 0, q).astype(BF16))
    nkb = k_ref.shape[1] // tk

    @pl.when(i == 0)
    def _():
        for j in range(nkb):
            vt_s[j, 0:DV_A, :] = v_ref[0, j * tk:(j + 1) * tk, :].astype(F32).T.astype(BF16)
            vt_s[j, DV_A:, :] = jnp.ones((BF16_ROWS, tk), BF16)
        if has_cache:
            vct_s[0:DV_A, :] = vc_ref[0].astype(F32).T.astype(BF16)
            vct_s[DV_A:, :] = jnp.ones((BF16_ROWS, vct_s.shape[1]), BF16)

    def groups(x):
        return [x[:, g * LANES:(g + 1) * LANES] for g in range(x.shape[1] // LANES)]

    def body(score, finish, cur):
        prev = 1 - cur
        if finish:
            m = [jnp.max(mp_s[prev, a], axis=-1, keepdims=True) for a in range(2)]
        mpart = [jnp.full((tq, LANES), NEG_BIG, F32)] * 2
        acc = [jnp.zeros((DV_A + BF16_ROWS, tq), F32)] * 2

        def pass1(kb, store):
            for a in range(2):
                s = _dot_nt(qs[a], kb)
                store(a, s)
                mpart[a] = functools.reduce(jnp.maximum, groups(s), mpart[a])

        def pass2(a, s, vt):
            acc[a] = acc[a] + _dot_nt(vt, jnp.exp2(s - m[a]).astype(BF16))

        if has_cache:
            def store_c(a, s):
                sc_s[cur, a] = s
            if score:
                pass1(kc_ref[0].astype(BF16), store_c)
            if finish:
                for a in range(2):
                    pass2(a, sc_s[prev, a], vct_s[...])
        for j in range(nkb):
            def store_j(a, s, j=j):
                s_s[cur][a, j] = s
            if score:
                pass1(k_ref[0, j * tk:(j + 1) * tk, :].astype(BF16), store_j)
            if finish:
                for a in range(2):
                    pass2(a, s_s[prev][a, j], vt_s[j])
        if score:
            for a in range(2):
                mp_s[cur, a] = mpart[a]
        if finish:
            lp = lam_ref[...]
            lam = (jnp.exp(jnp.sum(lp[0:1] * lp[1:2], axis=-1, keepdims=True))
                   - jnp.exp(jnp.sum(lp[2:3] * lp[3:4], axis=-1, keepdims=True)) + lam_init)
            o1 = acc[0][0:DV_A] / acc[0][DV_A:DV_A + 1]
            o2 = acc[1][0:DV_A] / acc[1][DV_A:DV_A + 1]
            o_ref[0] = (o1 - lam * o2).T.astype(BF16)

    pl.when(i == 0)(lambda: body(True, False, 0))
    inner = jnp.logical_and(i > 0, i < nq)
    pl.when(jnp.logical_and(inner, i % 2 == 0))(lambda: body(True, True, 0))
    pl.when(jnp.logical_and(inner, i % 2 == 1))(lambda: body(True, True, 1))
    pl.when(i == nq)(lambda: body(False, True, nq % 2))


def _attention_pipe(aq, ak, av, kc, vc, lam_p, lam_init):
    b, t, _ = aq.shape
    tq = min(256, t)
    tk = min(512, t)
    nq = t // tq
    has_cache = kc is not None
    qspec = pl.BlockSpec((1, tq, DV_A), lambda bi, h, i: (bi, jnp.minimum(i, nq - 1), h))
    ospec = pl.BlockSpec((1, tq, DV_A), lambda bi, h, i: (bi, jnp.maximum(i - 1, 0), h))
    ks = pl.BlockSpec((1, t, DV_A), lambda bi, h, i: (bi, 0, h))
    in_specs = [qspec, ks, ks]
    args = [aq, ak, av]
    if has_cache:
        cs = pl.BlockSpec((1, kc.shape[1], DV_A), lambda bi, h, i: (bi, 0, h))
        in_specs += [cs, cs]
        args += [kc, vc]
    in_specs.append(pl.BlockSpec(lam_p.shape, lambda bi, h, i: (0, 0)))
    args.append(lam_p)
    scratch = [pltpu.VMEM((2, t // tk, tq, tk), F32), pltpu.VMEM((2, t // tk, tq, tk), F32),
               pltpu.VMEM((2, 2, tq, LANES), F32),
               pltpu.VMEM((t // tk, DV_A + BF16_ROWS, tk), BF16)]
    if has_cache:
        scratch += [pltpu.VMEM((2, 2, tq, kc.shape[1]), F32),
                    pltpu.VMEM((DV_A + BF16_ROWS, kc.shape[1]), BF16)]
    return pl.pallas_call(
        functools.partial(_attn_pipe_kernel, has_cache=has_cache, tk=tk, lam_init=lam_init, nq=nq),
        grid=(b, H_A, nq + 1),
        in_specs=in_specs,
        out_specs=ospec,
        out_shape=jax.ShapeDtypeStruct((b, t, W_A), BF16),
        scratch_shapes=scratch,
        compiler_params=_cp("parallel", "parallel", "arbitrary"),
        name="diff_attention",
    )(*args)


def _conv_kernel(x_ref, w_ref, o_ref):
    x = x_ref[0].astype(F32)
    t = x.shape[0]
    w = w_ref[...]
    r = _iota((t, 1), 0)
    prev = jnp.where(r == 0, 0.0, pltpu.roll(x, 1, axis=0))
    nxt = jnp.where(r == t - 1, 0.0, pltpu.roll(x, t - 1, axis=0))
    scale = jnp.where(pl.program_id(1) * x.shape[1] >= W_B, DH_B ** -0.5, 1.0)
    o_ref[0] = (_silu(prev * w[0:1] + x * w[1:2] + nxt * w[2:3]) * scale).astype(BF16)


def _conv_silu(bqk, conv_w):
    b, t, c = bqk.shape
    tc = 256
    spec = pl.BlockSpec((1, t, tc), lambda bi, j: (bi, 0, j))
    return pl.pallas_call(
        _conv_kernel,
        grid=(b, c // tc),
        in_specs=[spec, pl.BlockSpec((conv_w.shape[0], tc), lambda bi, j: (0, j))],
        out_specs=spec,
        out_shape=jax.ShapeDtypeStruct(bqk.shape, BF16),
        compiler_params=_cp("parallel", "parallel"),
        name="mlstm_conv",
    )(bqk, conv_w)


def _tri(n, lower):
    r = _iota((n, n), 0)
    c = _iota((n, n), 1)
    return r >= c if lower else r <= c


def _mlstm_kernel(*refs, has_init, out_state):
    qf, kf, vf, gf, qb, kb, vb, gb, bias = refs[:9]
    pos = 9
    if has_init:
        c0, n0, m0 = refs[pos:pos + 3]
        pos += 3
    hf, hb = refs[pos:pos + 2]
    pos += 2
    if out_state:
        c_out, n_out, m_out = refs[pos:pos + 3]
        pos += 3
    cn_s, m_s = refs[pos:pos + 2]
    step = pl.program_id(1)
    L = SCAN_CHUNK
    nu = 2 * H_B
    nbt = qf.shape[0]

    @pl.when(step == 0)
    def _():
        if has_init:
            for bb in range(nbt):
                for u in range(nu):
                    cn_s[bb * nu + u, :, 0:DH_B] = c0[bb, u]
                    cn_s[bb * nu + u, :, DH_B:2 * DH_B] = jnp.broadcast_to(n0[bb, u], (DH_B, DH_B)).T
                m_s[bb * nu:(bb + 1) * nu] = m0[bb]
        else:
            cn_s[...] = jnp.zeros_like(cn_s)
            m_s[...] = jnp.zeros_like(m_s)

    ones_b = jnp.ones((L, DH_B), BF16)
    sel_rows = _iota((LANES, LANES), 0)
    lower = _tri(L, True)
    upper = _tri(L, False)
    lower_b = lower.astype(BF16)
    upper_b = upper.astype(BF16)
    col = _iota((1, LANES), 1)
    units = []
    for bb, d, (q_ref, k_ref, v_ref, g_ref, h_ref) in [
            (bb, d, refs_d) for bb in range(nbt)
            for d, refs_d in enumerate(((qf, kf, vf, gf, hf), (qb, kb, vb, gb, hb)))]:
        g = g_ref[bb] + bias[...]
        g = jnp.where(col < 2 * H_B, g, _log_sigmoid(g))
        gt = g.T
        if d == 0:
            bc_all = _dot_sel_left(lower_b, g)
            br_all = _dot_sel_right(gt, upper_b)
            mask = lower
        else:
            bc_all = _dot_sel_left(upper_b, g)
            br_all = _dot_sel_right(gt, lower_b)
            mask = upper
        for h in range(H_B):
            u = d * H_B + h
            sl = slice(h * DH_B, (h + 1) * DH_B)
            units.append(dict(
                u=u, su=bb * nu + u, bb=bb, d=d, sl=sl, h_ref=h_ref, mask=mask, g=g, bc_all=bc_all,
                li_row=gt[u:u + 1, :], b_row=br_all[2 * H_B + u:2 * H_B + u + 1, :],
                q=q_ref[bb, :, sl].astype(BF16), k=k_ref[bb, :, sl].astype(F32),
                v1=jnp.concatenate([v_ref[bb, :, sl].astype(BF16), ones_b], axis=1)))
    for x in units:
        x["li_b"] = _dot_sel_right(x["g"], (sel_rows == x["u"]).astype(BF16))
        x["b_b"] = _dot_sel_right(x["bc_all"], (sel_rows == 2 * H_B + x["u"]).astype(BF16))
        x["qk"] = _dot_nt(x["q"], x["k"].astype(BF16))
        x["cn"] = cn_s[x["su"]]
        x["qc"] = _dot(x["q"], x["cn"].astype(BF16))
    for x in units:
        x["m_prev"] = m_s[x["su"]]
        a_b = x["b_b"] + x["m_prev"]
        dm = jnp.where(x["mask"], x["b_b"] - x["b_row"] + x["li_row"], NEG_BIG)
        x["m_t"] = jnp.maximum(a_b, jnp.max(dm, axis=-1, keepdims=True))
        x["w_inter"] = jnp.exp(a_b - x["m_t"])
        x["s"] = (x["qk"] * jnp.exp(dm - x["m_t"])).astype(BF16)
    for x in units:
        x["sv"] = _dot(x["s"], x["v1"])
        b_end = x["b_b"][L - 1:L, :] if x["d"] == 0 else x["b_b"][0:1, :]
        g_b = b_end - x["b_b"] + x["li_b"]
        x["m_new"] = jnp.maximum(b_end + x["m_prev"], jnp.max(g_b, axis=0, keepdims=True))
        x["w_old"] = jnp.exp(b_end + x["m_prev"] - x["m_new"])
        x["ks"] = (jnp.exp(g_b - x["m_new"]) * x["k"]).astype(BF16)
    for x in units:
        num = x["w_inter"] * x["qc"][:, 0:DH_B] + x["sv"][:, 0:DH_B]
        den = x["w_inter"] * x["qc"][:, DH_B:] + x["sv"][:, DH_B:]
        x["h_ref"][x["bb"], :, x["sl"]] = (num / jnp.maximum(jnp.abs(den), jnp.exp(-x["m_t"]))).astype(BF16)
        w2 = jnp.concatenate([x["w_old"], x["w_old"]], axis=1)
        cn_s[x["su"]] = w2 * x["cn"] + _dot_tn(x["ks"], x["v1"])
        m_s[x["su"]] = x["m_new"]

    if out_state:
        @pl.when(step == pl.num_programs(1) - 1)
        def _():
            for bb in range(nbt):
                for u in range(nu):
                    c_out[bb, u] = cn_s[bb * nu + u, :, 0:DH_B]
                    n_out[bb, u] = cn_s[bb * nu + u, :, DH_B:2 * DH_B].T[0:1, :]
                m_out[bb] = m_s[bb * nu:(bb + 1) * nu]


def _mlstm(qk, v, gates, bias, init, out_state):
    b, t, _ = v.shape
    L = SCAN_CHUNK
    nc = t // L
    nu = 2 * H_B
    nbt = MLSTM_BATCHES_PER_STEP if b % MLSTM_BATCHES_PER_STEP == 0 else 1
    fwd = lambda j: (lambda bi, c: (bi, c, j))
    bwd = lambda j: (lambda bi, c: (bi, nc - 1 - c, j))
    blk = lambda w, im: pl.BlockSpec((nbt, L, w), im)
    in_specs = [blk(W_B, fwd(0)), blk(W_B, fwd(1)), blk(W_B, fwd(0)), blk(LANES, fwd(0)),
                blk(W_B, bwd(0)), blk(W_B, bwd(1)), blk(W_B, bwd(0)), blk(LANES, bwd(0)),
                pl.BlockSpec((1, LANES), lambda bi, c: (0, 0))]
    args = [qk, qk, v, gates, qk, qk, v, gates, bias]
    st_specs = [pl.BlockSpec((nbt, nu, DH_B, DH_B), lambda bi, c: (bi, 0, 0, 0)),
                pl.BlockSpec((nbt, nu, 1, DH_B), lambda bi, c: (bi, 0, 0, 0)),
                pl.BlockSpec((nbt, nu, 1, LANES), lambda bi, c: (bi, 0, 0, 0))]
    st_shapes = [jax.ShapeDtypeStruct((b, nu, DH_B, DH_B), F32),
                 jax.ShapeDtypeStruct((b, nu, 1, DH_B), F32),
                 jax.ShapeDtypeStruct((b, nu, 1, LANES), F32)]
    has_init = init is not None
    if has_init:
        in_specs += st_specs
        args += list(init)
    out_specs = [blk(W_B, fwd(0)), blk(W_B, bwd(0))]
    out_shape = [jax.ShapeDtypeStruct((b, t, W_B), BF16)] * 2
    if out_state:
        out_specs += st_specs
        out_shape += st_shapes
    return pl.pallas_call(
        functools.partial(_mlstm_kernel, has_init=has_init, out_state=out_state),
        grid=(b // nbt, nc),
        in_specs=in_specs,
        out_specs=out_specs,
        out_shape=out_shape,
        scratch_shapes=[pltpu.VMEM((nbt * nu, DH_B, 2 * DH_B), F32), pltpu.VMEM((nbt * nu, 1, LANES), F32)],
        compiler_params=_cp("parallel", "arbitrary"),
        name="mlstm_scan",
    )(*args)


def _head_rms(x, g, heads, width):
    parts = []
    for h in range(heads):
        sl = slice(h * width, (h + 1) * width)
        parts.append(_rms_rows(x[:, sl], g[:, sl]))
    return parts


OUT_PROJ_PARTS = 4


def _row_parts(tm):
    step = tm // OUT_PROJ_PARTS
    return [slice(r, r + step) for r in range(0, tm, step)]


def _project_and_route(cats, parts, x_ref, gate_ref, w_ref, g_ref, sh_ref, sc_ref, wr_ref, o_ref, h_ref, aff_ref):
    w = w_ref[...]
    xs = [x_ref[rs, :] + gate_ref[0] * _dot(cat, w) for cat, rs in zip(cats, parts)]
    hs = []
    for x, rs in zip(xs, parts):
        o_ref[rs, :] = x
        h = _rms_rows(x, g_ref[...]) * (1.0 + sc_ref[0]) + sh_ref[0]
        h_ref[rs, :] = h.astype(BF16)
        hs.append(h)
    wr = wr_ref[...]
    logits = [_dot_f32x3(h, wr) for h in hs]
    valid = _iota((1, LANES), 1) < N_EXPERTS
    for lg, rs in zip(logits, parts):
        lg = jnp.where(valid, lg, NEG_BIG)
        e = jnp.exp(lg - jnp.max(lg, axis=-1, keepdims=True))
        e = jnp.where(valid, e, 0.0)
        aff_ref[rs, :] = (e / jnp.sum(e, axis=-1, keepdims=True))[:, :N_EXPERTS]


def _route_specs(route, tm, t, per_batch):
    g2, sh2, sc2, wr = route
    full = lambda a: pl.BlockSpec(a.shape, lambda i: (0,) * a.ndim)
    row = lambda c: pl.BlockSpec((tm, c), lambda i: (i, 0))
    in_specs = [full(g2), _mod_spec(per_batch, tm, t), _mod_spec(per_batch, tm, t), full(wr)]
    out_specs = [row(D_MODEL), row(D_MODEL), row(N_EXPERTS)]
    return in_specs, out_specs


def _route_shapes(n):
    return [jax.ShapeDtypeStruct((n, D_MODEL), F32), jax.ShapeDtypeStruct((n, D_MODEL), BF16),
            jax.ShapeDtypeStruct((n, N_EXPERTS), F32)]


def _out_ab_kernel(at_ref, hf_ref, hb_ref, bo_ref, x_ref, gate_ref, ga_ref, gm_ref, w_ref,
                   g2_ref, sh2_ref, sc2_ref, wr_ref, o_ref, h_ref, aff_ref, *, out_scale):
    parts = _row_parts(x_ref.shape[0])
    cats = []
    for rs in parts:
        oa = [p * out_scale for p in _head_rms(at_ref[rs, :].astype(F32), ga_ref[...], H_A, DV_A)]
        hm = _head_rms(hf_ref[rs, :].astype(F32) + hb_ref[rs, :].astype(F32), gm_ref[...], H_B, DH_B)
        bo = bo_ref[rs, :].astype(F32)
        ob = [_sigmoid(bo[:, h * DH_B:(h + 1) * DH_B]) * hm[h] for h in range(H_B)]
        cats.append(jnp.concatenate(oa + ob, axis=1).astype(BF16))
    _project_and_route(cats, parts, x_ref, gate_ref, w_ref, g2_ref, sh2_ref, sc2_ref, wr_ref, o_ref, h_ref, aff_ref)


def _out_ab(attn, hf, hb, bo, x2, gate, g_attn, g_mlstm, w_out, route, t, per_batch, out_scale):
    n = x2.shape[0]
    tm = _row_tile(t)
    row = lambda c: pl.BlockSpec((tm, c), lambda i: (i, 0))
    full = lambda a: pl.BlockSpec(a.shape, lambda i: (0,) * a.ndim)
    r_in, r_out = _route_specs(route, tm, t, per_batch)
    return pl.pallas_call(
        functools.partial(_out_ab_kernel, out_scale=out_scale),
        grid=(n // tm,),
        in_specs=[row(W_A), row(W_B), row(W_B), row(W_B), row(D_MODEL), _mod_spec(per_batch, tm, t),
                  full(g_attn), full(g_mlstm), full(w_out)] + r_in,
        out_specs=r_out,
        out_shape=_route_shapes(n),
        compiler_params=_cp("parallel"),
        name="out_proj_even",
    )(attn, hf, hb, bo, x2, gate, g_attn, g_mlstm, w_out, *route)


def _in_c_kernel(x_ref, g_ref, sh_ref, sc_ref, w_ref, bf_ref, gam_ref, q_ref, v_ref, og_ref, lf_ref, kk_ref, *, layer):
    gam = gam_ref[...]
    e = jnp.exp(gam - jnp.max(gam, axis=0, keepdims=True))
    p = e / jnp.sum(e, axis=0, keepdims=True)
    lbs = p[0:1]
    lbs0 = lbs
    for i in range(1, layer + 1):
        lbs = lbs + p[i:i + 1]
    lb = jnp.maximum(lbs - lbs0, 0.0)
    L = SCAN_CHUNK
    w = w_ref[...]
    chunks = [slice(c0, c0 + L) for c0 in range(0, x_ref.shape[0], L)]
    ys = []
    for rs in chunks:
        h = _rms_rows(x_ref[rs, :], g_ref[...]) * (1.0 + sc_ref[0]) + sh_ref[0]
        ys.append(_dot(h.astype(BF16), w))
    for rs, y in zip(chunks, ys):
        q_ref[rs, :] = y[:, 0:W_C].astype(BF16)
        v_ref[rs, :] = y[:, W_C:2 * W_C].astype(BF16)
        og_ref[rs, :] = y[:, 2 * W_C:3 * W_C].astype(BF16)
        for d in range(2):
            f = y[:, (3 + d) * W_C:(4 + d) * W_C] + bf_ref[d:d + 1]
            u = jnp.exp(-jnp.abs(f))
            neg = f < 0.0
            lf2 = (jnp.log(jnp.where(neg, lb + u, 1.0 + lb * u)) - jnp.log(1.0 + u)) * LOG2E
            kk_ref[rs, d * W_C:(d + 1) * W_C] = ((1.0 - lb) * jnp.where(neg, 1.0, u) / (1.0 + u)).astype(BF16)
            lf_ref[rs, d * W_C:(d + 1) * W_C] = _dot_sel_left2(_tri(L, d == 0).astype(BF16), lf2)


def _in_c(x2, g, shift, scale, w_in, b_f, gamma_lb, layer, t, per_batch):
    n = x2.shape[0]
    tm = _row_tile(t)
    row = lambda c: pl.BlockSpec((tm, c), lambda i: (i, 0))
    full = lambda a: pl.BlockSpec(a.shape, lambda i: (0,) * a.ndim)
    widths = (W_C, W_C, W_C, 2 * W_C, 2 * W_C)
    return pl.pallas_call(
        functools.partial(_in_c_kernel, layer=layer),
        grid=(n // tm,),
        in_specs=[row(D_MODEL), full(g), _mod_spec(per_batch, tm, t), _mod_spec(per_batch, tm, t),
                  full(w_in), full(b_f), full(gamma_lb)],
        out_specs=[row(c) for c in widths],
        out_shape=[jax.ShapeDtypeStruct((n, c), dt) for c, dt in zip(widths, (BF16, BF16, BF16, F32, BF16))],
        compiler_params=_cp("parallel"),
        name="in_proj_odd",
    )(x2, g, shift, scale, w_in, b_f, gamma_lb)


HGRN_LEVELS = (8, 16, 32, 64)
HGRN_DIAG = 8
HGRN_HEADS_PER_STEP = 8


def _hgrn_unit(q, kk, v, b, st_ref, b_s, rev, store_o):
    L = SCAN_CHUNK
    b_s[...] = b
    row = _iota((L, 1), 0)
    col = _iota((1, L), 1)
    k16 = kk.astype(BF16)
    v16 = v.astype(BF16)
    q = q.astype(F32)
    kk = kk.astype(F32)
    st = st_ref[...]
    o = _dot_nt((q * jnp.exp2(b)).astype(BF16), st.astype(BF16))
    yield
    c = HGRN_DIAG
    sub = _iota((c, 1), 0)
    ys = []
    for blk in range(L // c):
        r0 = blk * c
        qb = q[r0:r0 + c]
        bb = b[r0:r0 + c]
        for s in range(c):
            keep = (sub <= s) if rev else (sub >= s)
            ys.append(qb * jnp.exp2(jnp.where(keep, bb - b_s[r0 + s:r0 + s + 1, :], NEG_BIG)))
    rs = _dot_nt(jnp.concatenate(ys, axis=0).astype(BF16), k16)
    yield
    parts = []
    for m in HGRN_LEVELS:
        par = 2 * m
        pieces = []
        for p0 in range(0, L, par):
            ref_row = p0 + m if rev else p0 + m - 1
            pieces.append(jnp.broadcast_to(b_s[ref_row:ref_row + 1, :], (par, LANES)))
        bref = jnp.concatenate(pieces, axis=0)
        in_first = (row % par) < m
        reader = in_first if rev else jnp.logical_not(in_first)
        z = (jnp.where(reader, q, kk) * jnp.exp2(-jnp.abs(b - bref))).astype(BF16)
        keep = jnp.logical_and((row // par) == (col // par),
                               ((col % par) >= m) if rev else ((col % par) < m))
        parts.append((jnp.logical_and(keep, reader), _dot_nt(z, z)))
        yield
    lane = _iota((c, LANES), 1)
    rows = []
    for blk in range(L // c):
        acc = jnp.zeros((c, LANES), F32)
        for s in range(c):
            i = blk * c + s
            acc = jnp.where(lane == i, rs[i * c:(i + 1) * c], acc)
        rows.append(acc)
    att = jnp.concatenate(rows, axis=0)
    for keep, a in parts:
        att = att + jnp.where(keep, a, 0.0)
    o = o + _dot(att.astype(BF16), v16)
    b_end = b_s[0:1, :] if rev else b_s[L - 1:L, :]
    kd = (kk * jnp.exp2(b_end - b)).astype(BF16)
    st_new = st * jnp.exp2(b_end) + _dot_tn(v16, kd)
    yield
    store_o(o)
    st_ref[...] = st_new
    yield


def _hgrn_kernel(*refs, has_init, out_state, hp):
    qf, vf, lff, kkf, qb, vb, lfb, kkb = refs[:8]
    pos = 8
    if has_init:
        s0 = refs[pos]
        pos += 1
    of, ob = refs[pos:pos + 2]
    pos += 2
    if out_state:
        s_out = refs[pos]
        pos += 1
    st_s, b_s = refs[pos:pos + 2]
    step = pl.program_id(2)

    @pl.when(step == 0)
    def _():
        for d in range(2):
            for h in range(hp):
                st_s[d * hp + h] = s0[0, d, h].T if has_init else jnp.zeros((DH_C, DH_C), F32)

    def storer(o_ref, sl):
        def store(o):
            o_ref[0, :, sl] = o.astype(BF16)
        return store

    units = []
    for d, (q_ref, v_ref, lf_ref, kk_ref, o_ref) in enumerate(((qf, vf, lff, kkf, of), (qb, vb, lfb, kkb, ob))):
        for h in range(hp):
            u = d * hp + h
            sl = slice(h * DH_C, (h + 1) * DH_C)
            units.append(_hgrn_unit(q_ref[0, :, sl], kk_ref[0, :, sl], v_ref[0, :, sl], lf_ref[0, :, sl],
                                    st_s.at[u], b_s.at[u], d == 1, storer(o_ref, sl)))
    live = units
    while live:
        live = [g for g in live if next(g, "done") != "done"]

    if out_state:
        @pl.when(step == pl.num_programs(2) - 1)
        def _():
            for d in range(2):
                for h in range(hp):
                    s_out[0, d, h] = st_s[d * hp + h].T


def _hgrn(q, v, lf, kk, init, out_state):
    b, t, _ = q.shape
    L = SCAN_CHUNK
    nc = t // L
    hp = HGRN_HEADS_PER_STEP
    ng = H_C // hp
    blk = lambda im: pl.BlockSpec((1, L, hp * DH_C), im)
    fwd = lambda off: (lambda bi, h, c: (bi, c, h + off))
    bwd = lambda off: (lambda bi, h, c: (bi, nc - 1 - c, h + off))
    in_specs = [blk(fwd(0)), blk(fwd(0)), blk(fwd(0)), blk(fwd(0)),
                blk(bwd(0)), blk(bwd(0)), blk(bwd(ng)), blk(bwd(ng))]
    args = [q, v, lf, kk, q, v, lf, kk]
    st_spec = pl.BlockSpec((1, 2, hp, DH_C, DH_C), lambda bi, h, c: (bi, 0, h, 0, 0))
    has_init = init is not None
    if has_init:
        in_specs.append(st_spec)
        args.append(init)
    out_specs = [blk(fwd(0)), blk(bwd(0))]
    out_shape = [jax.ShapeDtypeStruct((b, t, W_C), BF16)] * 2
    if out_state:
        out_specs.append(st_spec)
        out_shape.append(jax.ShapeDtypeStruct((b, 2, H_C, DH_C, DH_C), F32))
    return pl.pallas_call(
        functools.partial(_hgrn_kernel, has_init=has_init, out_state=out_state, hp=hp),
        grid=(b, ng, nc),
        in_specs=in_specs,
        out_specs=out_specs,
        out_shape=out_shape,
        scratch_shapes=[pltpu.VMEM((2 * hp, DH_C, DH_C), F32), pltpu.VMEM((2 * hp, L, DH_C), F32)],
        compiler_params=_cp("parallel", "parallel", "arbitrary"),
        name="hgrn_scan",
    )(*args)


def _out_c_kernel(of_ref, ob_ref, og_ref, x_ref, gate_ref, gh_ref, w_ref,
                  g2_ref, sh2_ref, sc2_ref, wr_ref, o_ref, h_ref, aff_ref):
    parts = _row_parts(x_ref.shape[0])
    cats = []
    for rs in parts:
        hm = _head_rms(of_ref[rs, :].astype(F32) + ob_ref[rs, :].astype(F32), gh_ref[...], H_C, DH_C)
        og = og_ref[rs, :].astype(F32)
        heads = [hm[h] * _silu(og[:, h * DH_C:(h + 1) * DH_C]) for h in range(H_C)]
        cats.append(jnp.concatenate(heads, axis=1).astype(BF16))
    _project_and_route(cats, parts, x_ref, gate_ref, w_ref, g2_ref, sh2_ref, sc2_ref, wr_ref, o_ref, h_ref, aff_ref)


def _out_c(of, ob, og, x2, gate, g_hgrn, w_out, route, t, per_batch):
    n = x2.shape[0]
    tm = _row_tile(t)
    row = lambda c: pl.BlockSpec((tm, c), lambda i: (i, 0))
    full = lambda a: pl.BlockSpec(a.shape, lambda i: (0,) * a.ndim)
    r_in, r_out = _route_specs(route, tm, t, per_batch)
    return pl.pallas_call(
        _out_c_kernel,
        grid=(n // tm,),
        in_specs=[row(W_C), row(W_C), row(W_C), row(D_MODEL), _mod_spec(per_batch, tm, t),
                  full(g_hgrn), full(w_out)] + r_in,
        out_specs=r_out,
        out_shape=_route_shapes(n),
        compiler_params=_cp("parallel"),
        name="out_proj_odd",
    )(of, ob, og, x2, gate, g_hgrn, w_out, *route)


def _topk_kernel(aff_ref, pos_ref, s0_ref, bits_s, sel_s, *, cap, tb):
    t = aff_ref.shape[1]
    nb = t // tb
    bits_s[...] = pltpu.bitcast(aff_ref[0], I32)

    def count_ge(cand):
        def body(j, acc):
            r = pl.multiple_of(j * tb, tb)
            blk = bits_s[pl.ds(r, tb), :]
            return acc + jnp.sum((blk >= cand).astype(F32), axis=0, keepdims=True)
        return lax.fori_loop(0, nb, body, jnp.zeros((1, LANES), F32))

    def bit_body(i, thr):
        cand = thr | lax.shift_left(jnp.int32(1), 30 - i)
        return jnp.where(count_ge(cand) >= cap, cand, thr)

    thr = lax.fori_loop(0, 31, bit_body, jnp.zeros((1, LANES), I32))

    def count_gt(j, acc):
        r = pl.multiple_of(j * tb, tb)
        return acc + jnp.sum((bits_s[pl.ds(r, tb), :] > thr).astype(F32), axis=0, keepdims=True)

    need = cap - lax.fori_loop(0, nb, count_gt, jnp.zeros((1, LANES), F32))
    strict = (_iota((tb, tb), 0) > _iota((tb, tb), 1)).astype(BF16)

    def select(j, carry):
        r = pl.multiple_of(j * tb, tb)
        blk = bits_s[pl.ds(r, tb), :]
        eq = blk == thr
        before = carry + _dot(strict, eq.astype(BF16))
        sel = jnp.logical_or(blk > thr, jnp.logical_and(eq, before < need))
        sel_s[pl.ds(r, tb), :] = sel.astype(F32)
        return carry + jnp.sum(eq.astype(F32), axis=0, keepdims=True)

    lax.fori_loop(0, nb, select, jnp.zeros((1, LANES), F32))

    def place(j, carry):
        r = pl.multiple_of(j * tb, tb)
        sel = sel_s[pl.ds(r, tb), :]
        slot = carry + _dot(strict, sel.astype(BF16))
        pos_ref[0, pl.ds(r, tb), :] = jnp.where(sel > 0.5, slot, -1.0).astype(I32)
        s0_ref[0, pl.ds(j, 1), :] = carry.astype(I32)
        return carry + jnp.sum(sel, axis=0, keepdims=True)

    lax.fori_loop(0, nb, place, jnp.zeros((1, LANES), F32))


def _topk(aff, cap, tb):
    b, t, _ = aff.shape
    nb = t // tb
    return pl.pallas_call(
        functools.partial(_topk_kernel, cap=cap, tb=tb),
        grid=(b,),
        in_specs=[pl.BlockSpec((1, t, LANES), lambda bi: (bi, 0, 0))],
        out_specs=[pl.BlockSpec((1, t, LANES), lambda bi: (bi, 0, 0)),
                   pl.BlockSpec((1, nb, LANES), lambda bi: (bi, 0, 0))],
        out_shape=[jax.ShapeDtypeStruct((b, t, LANES), I32), jax.ShapeDtypeStruct((b, nb, LANES), I32)],
        scratch_shapes=[pltpu.VMEM((t, LANES), I32), pltpu.VMEM((t, LANES), F32)],
        compiler_params=_cp("parallel"),
        name="moe_topk",
    )(aff)


MOE_SMALL_COUNT = 48
MOE_STACK_ROWS = 1024


def _windows(cap, tb):
    narrow = MOE_SMALL_COUNT + BF16_ROWS
    if cap <= narrow:
        return (cap,)
    return (narrow, min(tb + BF16_ROWS, cap))


def _window_starts(s0_ref, bi, j, cap, w):
    if w == cap:
        return [0] * N_EXPERTS
    out = []
    for e in range(N_EXPERTS):
        st = jnp.minimum((s0_ref[bi, j, e] // BF16_ROWS) * BF16_ROWS, cap - w)
        out.append(pl.multiple_of(st, BF16_ROWS))
    return out


def _per_window(windows, narrow_ref, bi, j, run):
    if len(windows) == 1:
        run(windows[0])
    else:
        narrow = narrow_ref[bi, j] > 0
        pl.when(narrow)(lambda: run(windows[0]))
        pl.when(jnp.logical_not(narrow))(lambda: run(windows[1]))


def _gather_kernel(s0_ref, narrow_ref, h_ref, post_ref, o_ref, *, cap, tb):
    bi = pl.program_id(0)
    j = pl.program_id(1)

    @pl.when(j == 0)
    def _():
        o_ref[...] = jnp.zeros_like(o_ref)

    def run(w):
        starts = _window_starts(s0_ref, bi, j, cap, w)
        post = post_ref[0]
        r = _iota((w, tb), 0)
        group = max(1, min(N_EXPERTS, MOE_STACK_ROWS // w))
        for e0 in range(0, N_EXPERTS, group):
            es = range(e0, min(e0 + group, N_EXPERTS))
            onehot = jnp.concatenate(
                [jnp.where((post[e:e + 1, :] - starts[e]) == r, 1.0, 0.0).astype(BF16) for e in es],
                axis=0)
            rows = _dot(onehot, h_ref[0]).astype(BF16)
            for k, e in enumerate(es):
                cur = o_ref[0, e, pl.ds(starts[e], w), :]
                o_ref[0, e, pl.ds(starts[e], w), :] = cur + rows[k * w:(k + 1) * w]

    _per_window(_windows(cap, tb), narrow_ref, bi, j, run)


def _gather(h2, post, s0, narrow, cap, tb):
    b, t, d = h2.shape
    nb = t // tb
    grid_spec = pltpu.PrefetchScalarGridSpec(
        num_scalar_prefetch=2,
        grid=(b, nb),
        in_specs=[pl.BlockSpec((1, tb, d), lambda bi, j, s, n: (bi, j, 0)),
                  pl.BlockSpec((1, N_EXPERTS, tb), lambda bi, j, s, n: (bi, 0, j))],
        out_specs=pl.BlockSpec((1, N_EXPERTS, cap, d), lambda bi, j, s, n: (bi, 0, 0, 0)),
    )
    return pl.pallas_call(
        functools.partial(_gather_kernel, cap=cap, tb=tb),
        grid_spec=grid_spec,
        out_shape=jax.ShapeDtypeStruct((b, N_EXPERTS, cap, d), BF16),
        compiler_params=_cp("parallel", "arbitrary"),
        name="moe_gather",
    )(s0, narrow, h2, post)


def _expert_kernel(xc_ref, xs_ref, wg_ref, wu_ref, wd_ref, yc_ref, ys_ref, wg_s, wu_s, wd_s):
    i = pl.program_id(1)

    @pl.when(i == 0)
    def _():
        wg_s[...] = wg_ref[0].astype(BF16)
        wu_s[...] = wu_ref[0].astype(BF16)
        wd_s[...] = wd_ref[0].astype(BF16)

    def mlp(x_ref, y_ref):
        shp = x_ref.shape
        rows = shp[0] * shp[2]
        x = x_ref[...].reshape(rows, shp[3])
        step = rows // max(1, rows // 128)
        xs = [x[r:r + step] for r in range(0, rows, step)]
        hs = [(_silu(_dot(xp, wg_s[...])) * _dot(xp, wu_s[...])).astype(BF16) for xp in xs]
        ys = [_dot(hm, wd_s[...]).astype(BF16) for hm in hs]
        y_ref[...] = jnp.concatenate(ys, axis=0).reshape(shp)

    pl.when(i == 0)(lambda: mlp(xc_ref, yc_ref))
    pl.when(i > 0)(lambda: mlp(xs_ref, ys_ref))


def _experts(xin_c, xin_s, w_gate, w_up, w_down, layer):
    bc, e, cap_c, d = xin_c.shape
    bs, _, cap_s, _ = xin_s.shape
    f = w_gate.shape[-1]
    cspec = pl.BlockSpec((bc, 1, cap_c, d), lambda ei, i: (0, ei, 0, 0))
    sspec = pl.BlockSpec((1, 1, cap_s, d), lambda ei, i: (jnp.maximum(i - 1, 0), ei, 0, 0))
    ws = lambda a: pl.BlockSpec((None, 1) + a.shape[2:], lambda ei, i: (layer, ei, 0, 0))
    return pl.pallas_call(
        _expert_kernel,
        grid=(e, bs + 1),
        in_specs=[cspec, sspec, ws(w_gate), ws(w_up), ws(w_down)],
        out_specs=[cspec, sspec],
        out_shape=[jax.ShapeDtypeStruct(xin_c.shape, BF16), jax.ShapeDtypeStruct(xin_s.shape, BF16)],
        scratch_shapes=[pltpu.VMEM((d, f), BF16), pltpu.VMEM((d, f), BF16), pltpu.VMEM((f, d), BF16)],
        compiler_params=_cp("parallel", "arbitrary"),
        name="moe_experts",
    )(xin_c, xin_s, w_gate, w_up, w_down)


def _combine_kernel(*refs, cap, tb, final):
    windows = _windows(cap, tb)
    s0_ref, narrow_ref, y_ref, pos_ref, aff_ref, x_ref, gate_ref = refs[:7]
    pos_n = 7
    if final:
        gf_ref = refs[pos_n]
        pos_n += 1
    spread_ref, row_id_ref = refs[pos_n:pos_n + 2]
    o_ref = refs[-1]
    bi = pl.program_id(0)
    j = pl.program_id(1)
    offset = 1024
    lane = _iota((1, N_EXPERTS), 1)

    def run(w):
        starts = _window_starts(s0_ref, bi, j, cap, w)
        if w == windows[0]:
            stv = jnp.zeros((1, N_EXPERTS), I32)
            for e in range(N_EXPERTS):
                stv = jnp.where(lane == e, starts[e], stv)
            rel = pos_ref[0] - stv + offset
            spread = spread_ref[...]
            hi = lax.shift_right_logical(rel, 5).astype(F32).astype(BF16)
            lo = jnp.bitwise_and(rel, 31).astype(F32).astype(BF16)
            relx = 32.0 * _dot(hi, spread) + _dot(lo, spread) - offset
            gx = _dot(aff_ref[0].astype(BF16), spread)
            g = jnp.where(relx == row_id_ref[...], gx, 0.0)
            ys = jnp.concatenate([y_ref[0, e, pl.ds(starts[e], w), :] for e in range(N_EXPERTS)], axis=0)
            acc = _dot(g.astype(BF16), ys)
        else:
            pos = pos_ref[0]
            aff = aff_ref[0].astype(BF16).astype(F32)
            col = _iota((tb, w), 1)
            acc = jnp.zeros((tb, D_MODEL), F32)
            for e in range(N_EXPERTS):
                g = jnp.where((pos[:, e:e + 1] - starts[e]) == col, aff[:, e:e + 1], 0.0)
                acc = acc + _dot(g.astype(BF16), y_ref[0, e, pl.ds(starts[e], w), :])
        out = x_ref[0] + gate_ref[0] * acc
        if final:
            out = _rms_rows(out, gf_ref[...])
        o_ref[0] = out

    _per_window(windows, narrow_ref, bi, j, run)


def _combine(y, pos, aff, x3, gate, s0, narrow, cap, tb, per_batch, g_final):
    b, t, d = x3.shape
    nb = t // tb
    final = g_final is not None
    tok = lambda c: pl.BlockSpec((1, tb, c), lambda bi, j, s, n: (bi, j, 0))
    const = lambda a: pl.BlockSpec(a.shape, lambda bi, j, s, n: (0, 0))
    in_specs = [pl.BlockSpec((1, N_EXPERTS, cap, d), lambda bi, j, s, n: (bi, 0, 0, 0)),
                tok(N_EXPERTS), tok(N_EXPERTS), tok(d),
                pl.BlockSpec((1, 1, d), (lambda bi, j, s, n: (bi, 0, 0)) if per_batch
                             else (lambda bi, j, s, n: (0, 0, 0)))]
    args = [y, pos, aff, x3, gate]
    if final:
        in_specs.append(const(g_final))
        args.append(g_final)
    w = _windows(cap, tb)[0]
    spread = jnp.repeat(jnp.eye(N_EXPERTS, dtype=BF16), w, axis=1)
    row_id = jnp.tile(jnp.arange(w, dtype=F32), N_EXPERTS).reshape(1, -1)
    in_specs += [const(spread), const(row_id)]
    args += [spread, row_id]
    grid_spec = pltpu.PrefetchScalarGridSpec(
        num_scalar_prefetch=2, grid=(b, nb), in_specs=in_specs, out_specs=tok(d))
    return pl.pallas_call(
        functools.partial(_combine_kernel, cap=cap, tb=tb, final=final),
        grid_spec=grid_spec,
        out_shape=jax.ShapeDtypeStruct((b, t, d), F32),
        compiler_params=_cp("parallel", "arbitrary"),
        name="moe_combine",
    )(s0, narrow, *args)


def _moe(x2, h2, aff, b, t, gate, per_batch, g_final):
    cap = CAPACITY_FACTOR * t // N_EXPERTS
    tb = min(MOE_TOKEN_BLOCK, t)
    aff3 = aff.reshape(b, t, N_EXPERTS)
    per = LANES // N_EXPERTS
    groups = -(-b // per)
    nb = t // tb

    def pack(a):
        a = jnp.pad(a, ((0, groups * per - b), (0, 0), (0, 0)))
        return a.reshape(groups, per, a.shape[1], N_EXPERTS).transpose(0, 2, 1, 3).reshape(groups, a.shape[1], LANES)

    def unpack(a):
        return a.reshape(groups, a.shape[1], per, N_EXPERTS).transpose(0, 2, 1, 3).reshape(
            groups * per, a.shape[1], N_EXPERTS)[:b]

    pos, s0 = _topk(pack(aff3), cap, tb)
    pos, s0 = unpack(pos), unpack(s0)
    counts = jnp.diff(s0, axis=1, append=jnp.full((b, 1, N_EXPERTS), cap, I32))
    narrow = jnp.all(counts <= MOE_SMALL_COUNT, axis=-1).astype(I32)
    xin = _gather(h2.reshape(b, t, D_MODEL), pos.transpose(0, 2, 1), s0, narrow, cap, tb)
    y = yield xin
    out = _combine(y, pos, aff3, x2.reshape(b, t, D_MODEL), gate, s0, narrow, cap, tb, per_batch, g_final)
    return out.reshape(b * t, D_MODEL)


def _rope_tables(t):
    rows = t // GRID_W
    row = jnp.repeat(jnp.arange(rows, dtype=F32), GRID_W)
    col = jnp.tile(jnp.arange(GRID_W, dtype=F32), rows)
    n_freq = DQK_A // 4
    inv_freq = jnp.power(ROPE_BASE, -jnp.arange(n_freq, dtype=F32) / n_freq)
    ang = jnp.concatenate([row[:, None] * inv_freq, col[:, None] * inv_freq], axis=-1)
    cos, sin = jnp.cos(ang), jnp.sin(ang)
    zero = jnp.zeros_like(sin)
    reps = LANES // DQK_A
    cos_t = jnp.tile(jnp.concatenate([cos, cos], axis=-1), (1, reps))
    sin_a = jnp.tile(jnp.concatenate([-sin, zero], axis=-1), (1, reps))
    sin_b = jnp.tile(jnp.concatenate([zero, sin], axis=-1), (1, reps))
    return cos_t, sin_a, sin_b


def _stream(x, mods, per_batch, rope, cache, p):
    b, t, d = x.shape
    x2 = x.reshape(b * t, d)
    depth = p["g_norm"].shape[0]
    even_out, odd_out = [], []
    for l in range(depth):
        j = l // 2
        sh1, sc1, gt1, sh2, sc2, gt2 = mods[l]
        g1 = p["g_norm"][l, 0].reshape(1, d)
        g2 = p["g_norm"][l, 1].reshape(1, d)
        route = (g2, sh2, sc2, p["w_router_pad"][l])
        if l % 2 == 0:
            aq, ak, av, bqk, bv, bo, gates = _in_ab(x2, g1, sh1, sc1, p["w_in_ab_main"][j], p["w_in_ab_gate"][j],
                                                     rope, t, per_batch, F32 if cache is None else BF16)
            r3 = lambda a: a.reshape(b, t, a.shape[-1])
            if cache is not None:
                kc = cache[0][:, j].reshape(b, -1, W_A)
                vc = cache[1][:, j].reshape(b, -1, W_A)
                nu = 2 * H_B
                init = (cache[2][:, j].reshape(b, nu, DH_B, DH_B),
                        cache[3][:, j].reshape(b, nu, 1, DH_B),
                        jnp.broadcast_to(cache[4][:, j].reshape(b, nu, 1, 1), (b, nu, 1, LANES)))
            else:
                kc = vc = init = None
            lam_init = 0.8 - 0.6 * math.exp(-0.3 * l)
            attn = _attention_pipe(r3(aq), r3(ak), r3(av), kc, vc, p["lam_ab"][j], lam_init)
            qk = _conv_silu(r3(bqk), p["conv_ab"][j])
            bias = jnp.pad(jnp.concatenate([p["b_ig_ab"][j].reshape(-1), p["b_fg_ab"][j].reshape(-1)]),
                           (0, LANES - 4 * H_B)).reshape(1, LANES)
            res = _mlstm(qk, r3(bv), r3(gates), bias, init, cache is None)
            hf, hb = res[0], res[1]
            if cache is None:
                even_out.append((ak, av, res[2], res[3], res[4]))
            x2, h2, aff = _out_ab(attn.reshape(b * t, W_A), hf.reshape(b * t, W_B), hb.reshape(b * t, W_B), bo, x2,
                                  gt1, p["g_attn_ab"][j].reshape(1, W_A), p["g_mlstm_ab"][j].reshape(1, W_B),
                                  p["w_out_ab_bf16"][j], route, t, per_batch, 1.0 - lam_init)
        else:
            q, v, og, lf, kk = _in_c(x2, g1, sh1, sc1, p["w_in_c_bf16"][j], p["b_f_c"][j], p["gamma_lb"], l, t,
                                     per_batch)
            r3 = lambda a: a.reshape(b, t, a.shape[-1])
            init = None if cache is None else cache[5][:, j]
            res = _hgrn(r3(q), r3(v), r3(lf), r3(kk), init, cache is None)
            if cache is None:
                odd_out.append(res[2])
            x2, h2, aff = _out_c(res[0].reshape(b * t, W_C), res[1].reshape(b * t, W_C), og, x2, gt1,
                                 p["g_hgrn_c"][j].reshape(1, W_C), p["w_out_c_bf16"][j], route, t, per_batch)
        g_final = p["g_final"].reshape(1, d) if l == depth - 1 else None
        x2 = yield from _moe(x2, h2, aff, b, t, gt2, per_batch, g_final)
    return x2.reshape(b, t, d), even_out, odd_out


def kernel(x_prompt, x_sample, cache_dattn_k, cache_dattn_v, state_mlstm_C, state_mlstm_n, state_mlstm_m,
           state_hgrn_S, c, c_ctx, w_ada, b_ada, g_norm, g_final, w_in_ab, b_ig_ab, b_fg_ab, lam_ab, conv_ab,
           g_attn_ab, g_mlstm_ab, w_out_ab, w_in_c, b_f_c, gamma_lb, g_hgrn_c, w_out_c,
           w_router, w_gate_e, w_up_e, w_down_e):
    d = D_MODEL
    bs = c.shape[0]
    bp, tp, _ = x_prompt.shape
    depth = w_ada.shape[0]
    n_main = 3 * W_A + 4 * W_B
    p = dict(
        g_norm=g_norm, g_final=g_final, b_ig_ab=b_ig_ab, b_fg_ab=b_fg_ab, lam_ab=lam_ab, conv_ab=conv_ab,
        g_attn_ab=g_attn_ab, g_mlstm_ab=g_mlstm_ab, b_f_c=b_f_c, gamma_lb=gamma_lb, g_hgrn_c=g_hgrn_c,
        w_gate_e=w_gate_e, w_up_e=w_up_e, w_down_e=w_down_e,
        w_in_ab_main=w_in_ab[:, :, :n_main].astype(BF16),
        w_in_ab_gate=jnp.pad(w_in_ab[:, :, n_main:], ((0, 0), (0, 0), (0, LANES - 4 * H_B))).astype(BF16),
        w_out_ab_bf16=w_out_ab.astype(BF16),
        w_in_c_bf16=w_in_c.astype(BF16),
        w_out_c_bf16=w_out_c.astype(BF16),
        w_router_pad=jnp.pad(w_router, ((0, 0), (0, 0), (0, LANES - N_EXPERTS))),
    )
    rows = -(-(bs + 1) // 8) * 8
    cond = jnp.concatenate([c, c_ctx.reshape(1, d), jnp.zeros((rows - bs - 1, d), F32)], axis=0)
    mods = _ada(cond, w_ada, b_ada)
    mods_s = [[mods[l, :bs, i * d:(i + 1) * d].reshape(bs, 1, d) for i in range(6)] for l in range(depth)]
    mods_c = [[mods[l, bs:bs + 1, i * d:(i + 1) * d].reshape(1, 1, d) for i in range(6)] for l in range(depth)]

    cache = (cache_dattn_k, cache_dattn_v, state_mlstm_C, state_mlstm_n, state_mlstm_m, state_hgrn_S)
    ctx = _stream(x_prompt, mods_c, False, None, None, p)
    smp = _stream(x_sample, mods_s, True, _rope_tables(x_sample.shape[1]), cache, p)
    xin_c, xin_s = next(ctx), next(smp)
    for l in range(depth):
        y_c, y_s = _experts(xin_c, xin_s, w_gate_e, w_up_e, w_down_e, l)
        try:
            xin_c = ctx.send(y_c)
        except StopIteration as done:
            y_prompt, ev, od = done.value
        try:
            xin_s = smp.send(y_s)
        except StopIteration as done:
            y_sample = done.value[0]
    new_k = jnp.stack([e[0].reshape(bp, tp, H_A, 2 * DQK_A) for e in ev], axis=1)
    new_v = jnp.stack([e[1].reshape(bp, tp, H_A, DV_A) for e in ev], axis=1)
    new_c = jnp.stack([e[2].reshape(bp, 2, H_B, DH_B, DH_B) for e in ev], axis=1)
    new_n = jnp.stack([e[3].reshape(bp, 2, H_B, DH_B) for e in ev], axis=1)
    new_m = jnp.stack([e[4][..., 0].reshape(bp, 2, H_B) for e in ev], axis=1)
    new_s = jnp.stack(od, axis=1)
    return (y_prompt, y_sample, new_k, new_v, new_c, new_n, new_m, new_s)
```

```python
import functools
import math

import jax
import jax.numpy as jnp
from jax import lax
from jax.experimental import pallas as pl
from jax.experimental.pallas import tpu as pltpu

F32 = jnp.float32
BF16 = jnp.bfloat16
I32 = jnp.int32

D_MODEL = 1024
H_A = 4
DV_A = 128
DQK_A = 64
W_A = H_A * DV_A
H_B = 4
DH_B = 128
W_B = H_B * DH_B
H_C = 8
DH_C = 128
W_C = H_C * DH_C
N_EXPERTS = 16
CAPACITY_FACTOR = 2
GRID_W = 64
ROPE_BASE = 10000.0
RMS_EPS = 1e-6
NEG_BIG = -1e30
LOG2E = 1.4426950408889634
LANES = 128
BF16_ROWS = 16
SCAN_CHUNK = 128
MLSTM_BATCHES_PER_STEP = 2
MOE_TOKEN_BLOCK = 256
VMEM_LIMIT = 56 * 1024 * 1024


def _cp(*sem):
    return pltpu.CompilerParams(dimension_semantics=sem, vmem_limit_bytes=VMEM_LIMIT)


def _dot(a, b):
    return jnp.dot(a, b, preferred_element_type=F32)


def _dot_nt(a, b):
    return lax.dot_general(a, b, (((1,), (1,)), ((), ())), preferred_element_type=F32)


def _dot_tn(a, b):
    return lax.dot_general(a, b, (((0,), (0,)), ((), ())), preferred_element_type=F32)


def _split3(x):
    hi = x.astype(BF16)
    r = x - hi.astype(F32)
    mid = r.astype(BF16)
    lo = (r - mid.astype(F32)).astype(BF16)
    return hi, mid, lo


def _dot_sel_left(sel, x):
    hi, mid, lo = _split3(x)
    return _dot(sel, hi) + _dot(sel, mid) + _dot(sel, lo)


def _dot_sel_right(x, sel):
    hi, mid, lo = _split3(x)
    return _dot(hi, sel) + _dot(mid, sel) + _dot(lo, sel)


def _dot_f32(a, b):
    ah, am, al = _split3(a)
    bh, bm, bl = _split3(b)
    return (_dot(ah, bh) + (_dot(ah, bm) + _dot(am, bh))
            + (_dot(ah, bl) + _dot(am, bm) + _dot(al, bh)))


def _dot_f32x3(a, b):
    ah = a.astype(BF16)
    al = (a - ah.astype(F32)).astype(BF16)
    bh = b.astype(BF16)
    bl = (b - bh.astype(F32)).astype(BF16)
    return _dot(ah, bh) + (_dot(ah, bl) + _dot(al, bh))


def _dot_sel_left2(sel, x):
    hi = x.astype(BF16)
    lo = (x - hi.astype(F32)).astype(BF16)
    return _dot(sel, hi) + _dot(sel, lo)


def _sigmoid(x):
    return 1.0 / (1.0 + jnp.exp(-x))


def _silu(x):
    return x * _sigmoid(x)


def _log_sigmoid(x):
    return jnp.minimum(x, 0.0) - jnp.log(1.0 + jnp.exp(-jnp.abs(x)))


def _rms_rows(x, g):
    return x * lax.rsqrt(jnp.mean(x * x, axis=-1, keepdims=True) + RMS_EPS) * g


def _iota(shape, dim):
    return lax.broadcasted_iota(I32, shape, dim)


def _ada_kernel(c_ref, w_ref, b_ref, o_ref):
    c = c_ref[...]
    o_ref[0] = _dot_f32(_silu(c), w_ref[0]) + b_ref[0]


def _ada(cond_rows, w_ada, b_ada):
    depth, d, n6 = w_ada.shape
    rows = cond_rows.shape[0]
    tn = 1536
    return pl.pallas_call(
        _ada_kernel,
        grid=(depth, n6 // tn),
        in_specs=[pl.BlockSpec((rows, d), lambda l, j: (0, 0)),
                  pl.BlockSpec((1, d, tn), lambda l, j: (l, 0, j)),
                  pl.BlockSpec((1, 1, tn), lambda l, j: (l, 0, j))],
        out_specs=pl.BlockSpec((1, rows, tn), lambda l, j: (l, 0, j)),
        out_shape=jax.ShapeDtypeStruct((depth, rows, n6), F32),
        compiler_params=_cp("parallel", "parallel"),
        name="adaln",
    )(cond_rows, w_ada, b_ada.reshape(depth, 1, n6))


def _mod_spec(per_batch, tm, t):
    if per_batch:
        return pl.BlockSpec((1, 1, D_MODEL), lambda i: (i * tm // t, 0, 0))
    return pl.BlockSpec((1, 1, D_MODEL), lambda i: (0, 0, 0))


def _row_tile(t):
    return min(512, t)


def _in_ab_kernel(*refs, use_rope):
    if use_rope:
        (x_ref, g_ref, sh_ref, sc_ref, w_ref, wg_ref, cos_ref, sa_ref, sb_ref,
         aq_ref, ak_ref, av_ref, bqk_ref, bv_ref, bo_ref, gt_ref) = refs
    else:
        (x_ref, g_ref, sh_ref, sc_ref, w_ref, wg_ref,
         aq_ref, ak_ref, av_ref, bqk_ref, bv_ref, bo_ref, gt_ref) = refs
    h = _rms_rows(x_ref[...], g_ref[...]) * (1.0 + sc_ref[0]) + sh_ref[0]
    hb = h.astype(BF16)
    y = _dot(hb, w_ref[...])
    gt_ref[...] = _dot(hb, wg_ref[...])
    aq = y[:, 0:W_A] * (DQK_A ** -0.5 * LOG2E)
    ak = y[:, W_A:2 * W_A]
    if use_rope:
        cos = jnp.concatenate([cos_ref[...]] * H_A, axis=1)
        sa = jnp.concatenate([sa_ref[...]] * H_A, axis=1)
        sb = jnp.concatenate([sb_ref[...]] * H_A, axis=1)
        half = DQK_A // 2

        def rope(v):
            return (v * cos + pltpu.roll(v, W_A - half, axis=1) * sa
                    + pltpu.roll(v, half, axis=1) * sb)

        aq = rope(aq)
        ak = rope(ak)
    aq_ref[...] = aq.astype(aq_ref.dtype)
    ak_ref[...] = ak.astype(ak_ref.dtype)
    av_ref[...] = y[:, 2 * W_A:3 * W_A].astype(av_ref.dtype)
    bqk_ref[...] = y[:, 3 * W_A:3 * W_A + 2 * W_B].astype(BF16)
    bv_ref[...] = y[:, 3 * W_A + 2 * W_B:3 * W_A + 3 * W_B].astype(BF16)
    bo_ref[...] = y[:, 3 * W_A + 3 * W_B:3 * W_A + 4 * W_B].astype(BF16)


def _in_ab(x2, g, shift, scale, w_main, w_gate, rope, t, per_batch, qkv_dtype):
    n = x2.shape[0]
    tm = _row_tile(t)
    use_rope = rope is not None
    row = lambda c: pl.BlockSpec((tm, c), lambda i: (i, 0))
    full = lambda a: pl.BlockSpec(a.shape, lambda i: (0,) * a.ndim)
    in_specs = [row(D_MODEL), full(g), _mod_spec(per_batch, tm, t), _mod_spec(per_batch, tm, t),
                full(w_main), full(w_gate)]
    args = [x2, g, shift, scale, w_main, w_gate]
    if use_rope:
        nt = t // tm
        tab = pl.BlockSpec((tm, LANES), lambda i: (i % nt, 0))
        in_specs += [tab, tab, tab]
        args += list(rope)
    widths = (W_A, W_A, W_A, 2 * W_B, W_B, W_B, LANES)
    return pl.pallas_call(
        functools.partial(_in_ab_kernel, use_rope=use_rope),
        grid=(n // tm,),
        in_specs=in_specs,
        out_specs=[row(c) for c in widths],
        out_shape=[jax.ShapeDtypeStruct((n, c), dt) for c, dt in
                   zip(widths, (qkv_dtype, qkv_dtype, qkv_dtype, BF16, BF16, BF16, F32))],
        compiler_params=_cp("parallel"),
        name="in_proj_even",
    )(*args)


def _attn_pipe_kernel(*refs, has_cache, tk, lam_init, nq):
    if has_cache:
        q_ref, k_ref, v_ref, kc_ref, vc_ref, lam_ref, o_ref, s_a, s_b, mp_s, vt_s, sc_s, vct_s = refs
    else:
        q_ref, k_ref, v_ref, lam_ref, o_ref, s_a, s_b, mp_s, vt_s = refs
    s_s = (s_a, s_b)
    i = pl.program_id(2)
    q = q_ref[0]
    tq = q.shape[0]
    first = _iota((1, LANES), 1) < DQK_A
    qs = (jnp.where(first, q, 0).astype(BF16), jnp.where(first, 0, q).astype(BF16))
    nkb = k_ref.shape[1] // tk

    @pl.when(i == 0)
    def _():
        for j in range(nkb):
            vt_s[j, 0:DV_A, :] = v_ref[0, j * tk:(j + 1) * tk, :].astype(F32).T.astype(BF16)
            vt_s[j, DV_A:, :] = jnp.ones((BF16_ROWS, tk), BF16)
        if has_cache:
            vct_s[0:DV_A, :] = vc_ref[0].astype(F32).T.astype(BF16)
            vct_s[DV_A:, :] = jnp.ones((BF16_ROWS, vct_s.shape[1]), BF16)

    def groups(x):
        return [x[:, g * LANES:(g + 1) * LANES] for g in range(x.shape[1] // LANES)]

    def body(score, finish, cur):
        prev = 1 - cur
        if finish:
            m = [jnp.max(mp_s[prev, a], axis=-1, keepdims=True) for a in range(2)]
        mpart = [jnp.full((tq, LANES), NEG_BIG, F32)] * 2
        acc = [jnp.zeros((DV_A + BF16_ROWS, tq), F32)] * 2

        def pass1(kb, store):
            for a in range(2):
                s = _dot_nt(qs[a], kb)
                store(a, s)
                mpart[a] = functools.reduce(jnp.maximum, groups(s), mpart[a])

        def pass2(a, s, vt):
            acc[a] = acc[a] + _dot_nt(vt, jnp.exp2(s - m[a]).astype(BF16))

        if has_cache:
            def store_c(a, s):
                sc_s[cur, a] = s
            if score:
                pass1(kc_ref[0].astype(BF16), store_c)
            if finish:
                for a in range(2):
                    pass2(a, sc_s[prev, a], vct_s[...])
        for j in range(nkb):
            def store_j(a, s, j=j):
                s_s[cur][a, j] = s
            if score:
                pass1(k_ref[0, j * tk:(j + 1) * tk, :].astype(BF16), store_j)
            if finish:
                for a in range(2):
                    pass2(a, s_s[prev][a, j], vt_s[j])
        if score:
            for a in range(2):
                mp_s[cur, a] = mpart[a]
        if finish:
            lp = lam_ref[...]
            lam = (jnp.exp(jnp.sum(lp[0:1] * lp[1:2], axis=-1, keepdims=True))
                   - jnp.exp(jnp.sum(lp[2:3] * lp[3:4], axis=-1, keepdims=True)) + lam_init)
            o1 = acc[0][0:DV_A] / acc[0][DV_A:DV_A + 1]
            o2 = acc[1][0:DV_A] / acc[1][DV_A:DV_A + 1]
            o_ref[0] = (o1 - lam * o2).T.astype(BF16)

    pl.when(i == 0)(lambda: body(True, False, 0))
    inner = jnp.logical_and(i > 0, i < nq)
    pl.when(jnp.logical_and(inner, i % 2 == 0))(lambda: body(True, True, 0))
    pl.when(jnp.logical_and(inner, i % 2 == 1))(lambda: body(True, True, 1))
    pl.when(i == nq)(lambda: body(False, True, nq % 2))


def _attention_pipe(aq, ak, av, kc, vc, lam_p, lam_init):
    b, t, _ = aq.shape
    tq = min(256, t)
    tk = min(256, t)
    nq = t // tq
    has_cache = kc is not None
    qspec = pl.BlockSpec((1, tq, DV_A), lambda bi, h, i: (bi, jnp.minimum(i, nq - 1), h))
    ospec = pl.BlockSpec((1, tq, DV_A), lambda bi, h, i: (bi, jnp.maximum(i - 1, 0), h))
    ks = pl.BlockSpec((1, t, DV_A), lambda bi, h, i: (bi, 0, h))
    in_specs = [qspec, ks, ks]
    args = [aq, ak, av]
    if has_cache:
        cs = pl.BlockSpec((1, kc.shape[1], DV_A), lambda bi, h, i: (bi, 0, h))
        in_specs += [cs, cs]
        args += [kc, vc]
    in_specs.append(pl.BlockSpec(lam_p.shape, lambda bi, h, i: (0, 0)))
    args.append(lam_p)
    scratch = [pltpu.VMEM((2, t // tk, tq, tk), F32), pltpu.VMEM((2, t // tk, tq, tk), F32),
               pltpu.VMEM((2, 2, tq, LANES), F32),
               pltpu.VMEM((t // tk, DV_A + BF16_ROWS, tk), BF16)]
    if has_cache:
        scratch += [pltpu.VMEM((2, 2, tq, kc.shape[1]), F32),
                    pltpu.VMEM((DV_A + BF16_ROWS, kc.shape[1]), BF16)]
    return pl.pallas_call(
        functools.partial(_attn_pipe_kernel, has_cache=has_cache, tk=tk, lam_init=lam_init, nq=nq),
        grid=(b, H_A, nq + 1),
        in_specs=in_specs,
        out_specs=ospec,
        out_shape=jax.ShapeDtypeStruct((b, t, W_A), BF16),
        scratch_shapes=scratch,
        compiler_params=_cp("parallel", "parallel", "arbitrary"),
        name="diff_attention",
    )(*args)


def _conv_kernel(x_ref, w_ref, o_ref):
    x = x_ref[0].astype(F32)
    t = x.shape[0]
    w = w_ref[...]
    r = _iota((t, 1), 0)
    prev = jnp.where(r == 0, 0.0, pltpu.roll(x, 1, axis=0))
    nxt = jnp.where(r == t - 1, 0.0, pltpu.roll(x, t - 1, axis=0))
    scale = jnp.where(pl.program_id(1) * x.shape[1] >= W_B, DH_B ** -0.5, 1.0)
    o_ref[0] = (_silu(prev * w[0:1] + x * w[1:2] + nxt * w[2:3]) * scale).astype(BF16)


def _conv_silu(bqk, conv_w):
    b, t, c = bqk.shape
    tc = 256
    spec = pl.BlockSpec((1, t, tc), lambda bi, j: (bi, 0, j))
    return pl.pallas_call(
        _conv_kernel,
        grid=(b, c // tc),
        in_specs=[spec, pl.BlockSpec((conv_w.shape[0], tc), lambda bi, j: (0, j))],
        out_specs=spec,
        out_shape=jax.ShapeDtypeStruct(bqk.shape, BF16),
        compiler_params=_cp("parallel", "parallel"),
        name="mlstm_conv",
    )(bqk, conv_w)


def _tri(n, lower):
    r = _iota((n, n), 0)
    c = _iota((n, n), 1)
    return r >= c if lower else r <= c


def _mlstm_kernel(*refs, has_init, out_state):
    qf, kf, vf, gf, qb, kb, vb, gb, bias = refs[:9]
    pos = 9
    if has_init:
        c0, n0, m0 = refs[pos:pos + 3]
        pos += 3
    hf, hb = refs[pos:pos + 2]
    pos += 2
    if out_state:
        c_out, n_out, m_out = refs[pos:pos + 3]
        pos += 3
    cn_s, m_s = refs[pos:pos + 2]
    step = pl.program_id(1)
    L = SCAN_CHUNK
    nu = 2 * H_B
    nbt = qf.shape[0]

    @pl.when(step == 0)
    def _():
        if has_init:
            for bb in range(nbt):
                for u in range(nu):
                    cn_s[bb * nu + u, :, 0:DH_B] = c0[bb, u]
                    cn_s[bb * nu + u, :, DH_B:2 * DH_B] = jnp.broadcast_to(n0[bb, u], (DH_B, DH_B)).T
                m_s[bb * nu:(bb + 1) * nu] = m0[bb]
        else:
            cn_s[...] = jnp.zeros_like(cn_s)
            m_s[...] = jnp.zeros_like(m_s)

    ones_b = jnp.ones((L, DH_B), BF16)
    sel_rows = _iota((LANES, LANES), 0)
    lower = _tri(L, True)
    upper = _tri(L, False)
    lower_b = lower.astype(BF16)
    upper_b = upper.astype(BF16)
    col = _iota((1, LANES), 1)
    units = []
    for bb, d, (q_ref, k_ref, v_ref, g_ref, h_ref) in [
            (bb, d, refs_d) for bb in range(nbt)
            for d, refs_d in enumerate(((qf, kf, vf, gf, hf), (qb, kb, vb, gb, hb)))]:
        g = g_ref[bb] + bias[...]
        g = jnp.where(col < 2 * H_B, g, _log_sigmoid(g))
        gt = g.T
        if d == 0:
            bc_all = _dot_sel_left(lower_b, g)
            br_all = _dot_sel_right(gt, upper_b)
            mask = lower
        else:
            bc_all = _dot_sel_left(upper_b, g)
            br_all = _dot_sel_right(gt, lower_b)
            mask = upper
        for h in range(H_B):
            u = d * H_B + h
            sl = slice(h * DH_B, (h + 1) * DH_B)
            units.append(dict(
                u=u, su=bb * nu + u, bb=bb, d=d, sl=sl, h_ref=h_ref, mask=mask, g=g, bc_all=bc_all,
                li_row=gt[u:u + 1, :], b_row=br_all[2 * H_B + u:2 * H_B + u + 1, :],
                q=q_ref[bb, :, sl].astype(BF16), k=k_ref[bb, :, sl].astype(F32),
                v1=jnp.concatenate([v_ref[bb, :, sl].astype(BF16), ones_b], axis=1)))
    for x in units:
        x["li_b"] = _dot_sel_right(x["g"], (sel_rows == x["u"]).astype(BF16))
        x["b_b"] = _dot_sel_right(x["bc_all"], (sel_rows == 2 * H_B + x["u"]).astype(BF16))
        x["qk"] = _dot_nt(x["q"], x["k"].astype(BF16))
        x["cn"] = cn_s[x["su"]]
        x["qc"] = _dot(x["q"], x["cn"].astype(BF16))
    for x in units:
        x["m_prev"] = m_s[x["su"]]
        a_b = x["b_b"] + x["m_prev"]
        dm = jnp.where(x["mask"], x["b_b"] - x["b_row"] + x["li_row"], NEG_BIG)
        x["m_t"] = jnp.maximum(a_b, jnp.max(dm, axis=-1, keepdims=True))
        x["w_inter"] = jnp.exp(a_b - x["m_t"])
        x["s"] = (x["qk"] * jnp.exp(dm - x["m_t"])).astype(BF16)
    for x in units:
        x["sv"] = _dot(x["s"], x["v1"])
        b_end = x["b_b"][L - 1:L, :] if x["d"] == 0 else x["b_b"][0:1, :]
        g_b = b_end - x["b_b"] + x["li_b"]
        x["m_new"] = jnp.maximum(b_end + x["m_prev"], jnp.max(g_b, axis=0, keepdims=True))
        x["w_old"] = jnp.exp(b_end + x["m_prev"] - x["m_new"])
        x["ks"] = (jnp.exp(g_b - x["m_new"]) * x["k"]).astype(BF16)
    for x in units:
        num = x["w_inter"] * x["qc"][:, 0:DH_B] + x["sv"][:, 0:DH_B]
        den = x["w_inter"] * x["qc"][:, DH_B:] + x["sv"][:, DH_B:]
        x["h_ref"][x["bb"], :, x["sl"]] = (num / jnp.maximum(jnp.abs(den), jnp.exp(-x["m_t"]))).astype(BF16)
        w2 = jnp.concatenate([x["w_old"], x["w_old"]], axis=1)
        cn_s[x["su"]] = w2 * x["cn"] + _dot_tn(x["ks"], x["v1"])
        m_s[x["su"]] = x["m_new"]

    if out_state:
        @pl.when(step == pl.num_programs(1) - 1)
        def _():
            for bb in range(nbt):
                for u in range(nu):
                    c_out[bb, u] = cn_s[bb * nu + u, :, 0:DH_B]
                    n_out[bb, u] = cn_s[bb * nu + u, :, DH_B:2 * DH_B].T[0:1, :]
                m_out[bb] = m_s[bb * nu:(bb + 1) * nu]


def _mlstm(qk, v, gates, bias, init, out_state):
    b, t, _ = v.shape
    L = SCAN_CHUNK
    nc = t // L
    nu = 2 * H_B
    nbt = MLSTM_BATCHES_PER_STEP if b % MLSTM_BATCHES_PER_STEP == 0 else 1
    fwd = lambda j: (lambda bi, c: (bi, c, j))
    bwd = lambda j: (lambda bi, c: (bi, nc - 1 - c, j))
    blk = lambda w, im: pl.BlockSpec((nbt, L, w), im)
    in_specs = [blk(W_B, fwd(0)), blk(W_B, fwd(1)), blk(W_B, fwd(0)), blk(LANES, fwd(0)),
                blk(W_B, bwd(0)), blk(W_B, bwd(1)), blk(W_B, bwd(0)), blk(LANES, bwd(0)),
                pl.BlockSpec((1, LANES), lambda bi, c: (0, 0))]
    args = [qk, qk, v, gates, qk, qk, v, gates, bias]
    st_specs = [pl.BlockSpec((nbt, nu, DH_B, DH_B), lambda bi, c: (bi, 0, 0, 0)),
                pl.BlockSpec((nbt, nu, 1, DH_B), lambda bi, c: (bi, 0, 0, 0)),
                pl.BlockSpec((nbt, nu, 1, LANES), lambda bi, c: (bi, 0, 0, 0))]
    st_shapes = [jax.ShapeDtypeStruct((b, nu, DH_B, DH_B), F32),
                 jax.ShapeDtypeStruct((b, nu, 1, DH_B), F32),
                 jax.ShapeDtypeStruct((b, nu, 1, LANES), F32)]
    has_init = init is not None
    if has_init:
        in_specs += st_specs
        args += list(init)
    out_specs = [blk(W_B, fwd(0)), blk(W_B, bwd(0))]
    out_shape = [jax.ShapeDtypeStruct((b, t, W_B), BF16)] * 2
    if out_state:
        out_specs += st_specs
        out_shape += st_shapes
    return pl.pallas_call(
        functools.partial(_mlstm_kernel, has_init=has_init, out_state=out_state),
        grid=(b // nbt, nc),
        in_specs=in_specs,
        out_specs=out_specs,
        out_shape=out_shape,
        scratch_shapes=[pltpu.VMEM((nbt * nu, DH_B, 2 * DH_B), F32), pltpu.VMEM((nbt * nu, 1, LANES), F32)],
        compiler_params=_cp("parallel", "arbitrary"),
        name="mlstm_scan",
    )(*args)


def _head_rms(x, g, heads, width):
    parts = []
    for h in range(heads):
        sl = slice(h * width, (h + 1) * width)
        parts.append(_rms_rows(x[:, sl], g[:, sl]))
    return parts


OUT_PROJ_PARTS = 4


def _row_parts(tm):
    step = tm // OUT_PROJ_PARTS
    return [slice(r, r + step) for r in range(0, tm, step)]


def _project_and_route(cats, parts, x_ref, gate_ref, w_ref, g_ref, sh_ref, sc_ref, wr_ref, o_ref, h_ref, aff_ref):
    w = w_ref[...]
    xs = [x_ref[rs, :] + gate_ref[0] * _dot(cat, w) for cat, rs in zip(cats, parts)]
    hs = []
    for x, rs in zip(xs, parts):
        o_ref[rs, :] = x
        h = _rms_rows(x, g_ref[...]) * (1.0 + sc_ref[0]) + sh_ref[0]
        h_ref[rs, :] = h.astype(BF16)
        hs.append(h)
    wr = wr_ref[...]
    logits = [_dot_f32x3(h, wr) for h in hs]
    valid = _iota((1, LANES), 1) < N_EXPERTS
    for lg, rs in zip(logits, parts):
        lg = jnp.where(valid, lg, NEG_BIG)
        e = jnp.exp(lg - jnp.max(lg, axis=-1, keepdims=True))
        e = jnp.where(valid, e, 0.0)
        aff_ref[rs, :] = (e / jnp.sum(e, axis=-1, keepdims=True))[:, :N_EXPERTS]


def _route_specs(route, tm, t, per_batch):
    g2, sh2, sc2, wr = route
    full = lambda a: pl.BlockSpec(a.shape, lambda i: (0,) * a.ndim)
    row = lambda c: pl.BlockSpec((tm, c), lambda i: (i, 0))
    in_specs = [full(g2), _mod_spec(per_batch, tm, t), _mod_spec(per_batch, tm, t), full(wr)]
    out_specs = [row(D_MODEL), row(D_MODEL), row(N_EXPERTS)]
    return in_specs, out_specs


def _route_shapes(n):
    return [jax.ShapeDtypeStruct((n, D_MODEL), F32), jax.ShapeDtypeStruct((n, D_MODEL), BF16),
            jax.ShapeDtypeStruct((n, N_EXPERTS), F32)]


def _out_ab_kernel(at_ref, hf_ref, hb_ref, bo_ref, x_ref, gate_ref, ga_ref, gm_ref, w_ref,
                   g2_ref, sh2_ref, sc2_ref, wr_ref, o_ref, h_ref, aff_ref, *, out_scale):
    parts = _row_parts(x_ref.shape[0])
    cats = []
    for rs in parts:
        oa = [p * out_scale for p in _head_rms(at_ref[rs, :].astype(F32), ga_ref[...], H_A, DV_A)]
        hm = _head_rms(hf_ref[rs, :].astype(F32) + hb_ref[rs, :].astype(F32), gm_ref[...], H_B, DH_B)
        bo = bo_ref[rs, :].astype(F32)
        ob = [_sigmoid(bo[:, h * DH_B:(h + 1) * DH_B]) * hm[h] for h in range(H_B)]
        cats.append(jnp.concatenate(oa + ob, axis=1).astype(BF16))
    _project_and_route(cats, parts, x_ref, gate_ref, w_ref, g2_ref, sh2_ref, sc2_ref, wr_ref, o_ref, h_ref, aff_ref)


def _out_ab(attn, hf, hb, bo, x2, gate, g_attn, g_mlstm, w_out, route, t, per_batch, out_scale):
    n = x2.shape[0]
    tm = _row_tile(t)
    row = lambda c: pl.BlockSpec((tm, c), lambda i: (i, 0))
    full = lambda a: pl.BlockSpec(a.shape, lambda i: (0,) * a.ndim)
    r_in, r_out = _route_specs(route, tm, t, per_batch)
    return pl.pallas_call(
        functools.partial(_out_ab_kernel, out_scale=out_scale),
        grid=(n // tm,),
        in_specs=[row(W_A), row(W_B), row(W_B), row(W_B), row(D_MODEL), _mod_spec(per_batch, tm, t),
                  full(g_attn), full(g_mlstm), full(w_out)] + r_in,
        out_specs=r_out,
        out_shape=_route_shapes(n),
        compiler_params=_cp("parallel"),
        name="out_proj_even",
    )(attn, hf, hb, bo, x2, gate, g_attn, g_mlstm, w_out, *route)


def _in_c_kernel(x_ref, g_ref, sh_ref, sc_ref, w_ref, bf_ref, gam_ref, q_ref, v_ref, og_ref, lf_ref, kk_ref, *, layer):
    gam = gam_ref[...]
    e = jnp.exp(gam - jnp.max(gam, axis=0, keepdims=True))
    p = e / jnp.sum(e, axis=0, keepdims=True)
    lbs = p[0:1]
    lbs0 = lbs
    for i in range(1, layer + 1):
        lbs = lbs + p[i:i + 1]
    lb = jnp.maximum(lbs - lbs0, 0.0)
    L = SCAN_CHUNK
    w = w_ref[...]
    chunks = [slice(c0, c0 + L) for c0 in range(0, x_ref.shape[0], L)]
    ys = []
    for rs in chunks:
        h = _rms_rows(x_ref[rs, :], g_ref[...]) * (1.0 + sc_ref[0]) + sh_ref[0]
        ys.append(_dot(h.astype(BF16), w))
    for rs, y in zip(chunks, ys):
        q_ref[rs, :] = y[:, 0:W_C].astype(BF16)
        v_ref[rs, :] = y[:, W_C:2 * W_C].astype(BF16)
        og_ref[rs, :] = y[:, 2 * W_C:3 * W_C].astype(BF16)
        for d in range(2):
            f = y[:, (3 + d) * W_C:(4 + d) * W_C] + bf_ref[d:d + 1]
            u = jnp.exp(-jnp.abs(f))
            neg = f < 0.0
            lf2 = (jnp.log(jnp.where(neg, lb + u, 1.0 + lb * u)) - jnp.log(1.0 + u)) * LOG2E
            kk_ref[rs, d * W_C:(d + 1) * W_C] = ((1.0 - lb) * jnp.where(neg, 1.0, u) / (1.0 + u)).astype(BF16)
            lf_ref[rs, d * W_C:(d + 1) * W_C] = _dot_sel_left2(_tri(L, d == 0).astype(BF16), lf2)


def _in_c(x2, g, shift, scale, w_in, b_f, gamma_lb, layer, t, per_batch):
    n = x2.shape[0]
    tm = _row_tile(t)
    row = lambda c: pl.BlockSpec((tm, c), lambda i: (i, 0))
    full = lambda a: pl.BlockSpec(a.shape, lambda i: (0,) * a.ndim)
    widths = (W_C, W_C, W_C, 2 * W_C, 2 * W_C)
    return pl.pallas_call(
        functools.partial(_in_c_kernel, layer=layer),
        grid=(n // tm,),
        in_specs=[row(D_MODEL), full(g), _mod_spec(per_batch, tm, t), _mod_spec(per_batch, tm, t),
                  full(w_in), full(b_f), full(gamma_lb)],
        out_specs=[row(c) for c in widths],
        out_shape=[jax.ShapeDtypeStruct((n, c), dt) for c, dt in zip(widths, (BF16, BF16, BF16, F32, BF16))],
        compiler_params=_cp("parallel"),
        name="in_proj_odd",
    )(x2, g, shift, scale, w_in, b_f, gamma_lb)


HGRN_LEVELS = (8, 16, 32, 64)
HGRN_DIAG = 8
HGRN_HEADS_PER_STEP = 8


def _hgrn_unit(q, kk, v, b, st_ref, b_s, rev, store_o):
    L = SCAN_CHUNK
    b_s[...] = b
    row = _iota((L, 1), 0)
    col = _iota((1, L), 1)
    k16 = kk.astype(BF16)
    v16 = v.astype(BF16)
    q = q.astype(F32)
    kk = kk.astype(F32)
    st = st_ref[...]
    o = _dot_nt((q * jnp.exp2(b)).astype(BF16), st.astype(BF16))
    yield
    c = HGRN_DIAG
    sub = _iota((c, 1), 0)
    ys = []
    for blk in range(L // c):
        r0 = blk * c
        qb = q[r0:r0 + c]
        bb = b[r0:r0 + c]
        for s in range(c):
            keep = (sub <= s) if rev else (sub >= s)
            ys.append(qb * jnp.exp2(jnp.where(keep, bb - b_s[r0 + s:r0 + s + 1, :], NEG_BIG)))
    rs = _dot_nt(jnp.concatenate(ys, axis=0).astype(BF16), k16)
    yield
    parts = []
    for m in HGRN_LEVELS:
        par = 2 * m
        pieces = []
        for p0 in range(0, L, par):
            ref_row = p0 + m if rev else p0 + m - 1
            pieces.append(jnp.broadcast_to(b_s[ref_row:ref_row + 1, :], (par, LANES)))
        bref = jnp.concatenate(pieces, axis=0)
        in_first = (row % par) < m
        reader = in_first if rev else jnp.logical_not(in_first)
        z = (jnp.where(reader, q, kk) * jnp.exp2(-jnp.abs(b - bref))).astype(BF16)
        keep = jnp.logical_and((row // par) == (col // par),
                               ((col % par) >= m) if rev else ((col % par) < m))
        parts.append((jnp.logical_and(keep, reader), _dot_nt(z, z)))
        yield
    lane = _iota((c, LANES), 1)
    rows = []
    for blk in range(L // c):
        acc = jnp.zeros((c, LANES), F32)
        for s in range(c):
            i = blk * c + s
            acc = jnp.where(lane == i, rs[i * c:(i + 1) * c], acc)
        rows.append(acc)
    att = jnp.concatenate(rows, axis=0)
    for keep, a in parts:
        att = att + jnp.where(keep, a, 0.0)
    o = o + _dot(att.astype(BF16), v16)
    b_end = b_s[0:1, :] if rev else b_s[L - 1:L, :]
    kd = (kk * jnp.exp2(b_end - b)).astype(BF16)
    st_new = st * jnp.exp2(b_end) + _dot_tn(v16, kd)
    yield
    store_o(o)
    st_ref[...] = st_new
    yield


def _hgrn_kernel(*refs, has_init, out_state, hp):
    qf, vf, lff, kkf, qb, vb, lfb, kkb = refs[:8]
    pos = 8
    if has_init:
        s0 = refs[pos]
        pos += 1
    of, ob = refs[pos:pos + 2]
    pos += 2
    if out_state:
        s_out = refs[pos]
        pos += 1
    st_s, b_s = refs[pos:pos + 2]
    step = pl.program_id(2)

    @pl.when(step == 0)
    def _():
        for d in range(2):
            for h in range(hp):
                st_s[d * hp + h] = s0[0, d, h].T if has_init else jnp.zeros((DH_C, DH_C), F32)

    def storer(o_ref, sl):
        def store(o):
            o_ref[0, :, sl] = o.astype(BF16)
        return store

    units = []
    for d, (q_ref, v_ref, lf_ref, kk_ref, o_ref) in enumerate(((qf, vf, lff, kkf, of), (qb, vb, lfb, kkb, ob))):
        for h in range(hp):
            u = d * hp + h
            sl = slice(h * DH_C, (h + 1) * DH_C)
            units.append(_hgrn_unit(q_ref[0, :, sl], kk_ref[0, :, sl], v_ref[0, :, sl], lf_ref[0, :, sl],
                                    st_s.at[u], b_s.at[u], d == 1, storer(o_ref, sl)))
    live = units
    while live:
        live = [g for g in live if next(g, "done") != "done"]

    if out_state:
        @pl.when(step == pl.num_programs(2) - 1)
        def _():
            for d in range(2):
                for h in range(hp):
                    s_out[0, d, h] = st_s[d * hp + h].T


def _hgrn(q, v, lf, kk, init, out_state):
    b, t, _ = q.shape
    L = SCAN_CHUNK
    nc = t // L
    hp = HGRN_HEADS_PER_STEP
    ng = H_C // hp
    blk = lambda im: pl.BlockSpec((1, L, hp * DH_C), im)
    fwd = lambda off: (lambda bi, h, c: (bi, c, h + off))
    bwd = lambda off: (lambda bi, h, c: (bi, nc - 1 - c, h + off))
    in_specs = [blk(fwd(0)), blk(fwd(0)), blk(fwd(0)), blk(fwd(0)),
                blk(bwd(0)), blk(bwd(0)), blk(bwd(ng)), blk(bwd(ng))]
    args = [q, v, lf, kk, q, v, lf, kk]
    st_spec = pl.BlockSpec((1, 2, hp, DH_C, DH_C), lambda bi, h, c: (bi, 0, h, 0, 0))
    has_init = init is not None
    if has_init:
        in_specs.append(st_spec)
        args.append(init)
    out_specs = [blk(fwd(0)), blk(bwd(0))]
    out_shape = [jax.ShapeDtypeStruct((b, t, W_C), BF16)] * 2
    if out_state:
        out_specs.append(st_spec)
        out_shape.append(jax.ShapeDtypeStruct((b, 2, H_C, DH_C, DH_C), F32))
    return pl.pallas_call(
        functools.partial(_hgrn_kernel, has_init=has_init, out_state=out_state, hp=hp),
        grid=(b, ng, nc),
        in_specs=in_specs,
        out_specs=out_specs,
        out_shape=out_shape,
        scratch_shapes=[pltpu.VMEM((2 * hp, DH_C, DH_C), F32), pltpu.VMEM((2 * hp, L, DH_C), F32)],
        compiler_params=_cp("parallel", "parallel", "arbitrary"),
        name="hgrn_scan",
    )(*args)


def _out_c_kernel(of_ref, ob_ref, og_ref, x_ref, gate_ref, gh_ref, w_ref,
                  g2_ref, sh2_ref, sc2_ref, wr_ref, o_ref, h_ref, aff_ref):
    parts = _row_parts(x_ref.shape[0])
    cats = []
    for rs in parts:
        hm = _head_rms(of_ref[rs, :].astype(F32) + ob_ref[rs, :].astype(F32), gh_ref[...], H_C, DH_C)
        og = og_ref[rs, :].astype(F32)
        heads = [hm[h] * _silu(og[:, h * DH_C:(h + 1) * DH_C]) for h in range(H_C)]
        cats.append(jnp.concatenate(heads, axis=1).astype(BF16))
    _project_and_route(cats, parts, x_ref, gate_ref, w_ref, g2_ref, sh2_ref, sc2_ref, wr_ref, o_ref, h_ref, aff_ref)


def _out_c(of, ob, og, x2, gate, g_hgrn, w_out, route, t, per_batch):
    n = x2.shape[0]
    tm = _row_tile(t)
    row = lambda c: pl.BlockSpec((tm, c), lambda i: (i, 0))
    full = lambda a: pl.BlockSpec(a.shape, lambda i: (0,) * a.ndim)
    r_in, r_out = _route_specs(route, tm, t, per_batch)
    return pl.pallas_call(
        _out_c_kernel,
        grid=(n // tm,),
        in_specs=[row(W_C), row(W_C), row(W_C), row(D_MODEL), _mod_spec(per_batch, tm, t),
                  full(g_hgrn), full(w_out)] + r_in,
        out_specs=r_out,
        out_shape=_route_shapes(n),
        compiler_params=_cp("parallel"),
        name="out_proj_odd",
    )(of, ob, og, x2, gate, g_hgrn, w_out, *route)


def _topk_kernel(aff_ref, pos_ref, s0_ref, bits_s, sel_s, *, cap, tb):
    t = aff_ref.shape[1]
    nb = t // tb
    bits_s[...] = pltpu.bitcast(aff_ref[0], I32)

    def count_ge(cand):
        def body(j, acc):
            r = pl.multiple_of(j * tb, tb)
            blk = bits_s[pl.ds(r, tb), :]
            return acc + jnp.sum((blk >= cand).astype(F32), axis=0, keepdims=True)
        return lax.fori_loop(0, nb, body, jnp.zeros((1, LANES), F32))

    def bit_body(i, thr):
        cand = thr | lax.shift_left(jnp.int32(1), 30 - i)
        return jnp.where(count_ge(cand) >= cap, cand, thr)

    thr = lax.fori_loop(0, 31, bit_body, jnp.zeros((1, LANES), I32))

    def count_gt(j, acc):
        r = pl.multiple_of(j * tb, tb)
        return acc + jnp.sum((bits_s[pl.ds(r, tb), :] > thr).astype(F32), axis=0, keepdims=True)

    need = cap - lax.fori_loop(0, nb, count_gt, jnp.zeros((1, LANES), F32))
    strict = (_iota((tb, tb), 0) > _iota((tb, tb), 1)).astype(BF16)

    def select(j, carry):
        r = pl.multiple_of(j * tb, tb)
        blk = bits_s[pl.ds(r, tb), :]
        eq = blk == thr
        before = carry + _dot(strict, eq.astype(BF16))
        sel = jnp.logical_or(blk > thr, jnp.logical_and(eq, before < need))
        sel_s[pl.ds(r, tb), :] = sel.astype(F32)
        return carry + jnp.sum(eq.astype(F32), axis=0, keepdims=True)

    lax.fori_loop(0, nb, select, jnp.zeros((1, LANES), F32))

    def place(j, carry):
        r = pl.multiple_of(j * tb, tb)
        sel = sel_s[pl.ds(r, tb), :]
        slot = carry + _dot(strict, sel.astype(BF16))
        pos_ref[0, pl.ds(r, tb), :] = jnp.where(sel > 0.5, slot, -1.0).astype(I32)
        s0_ref[0, pl.ds(j, 1), :] = carry.astype(I32)
        return carry + jnp.sum(sel, axis=0, keepdims=True)

    lax.fori_loop(0, nb, place, jnp.zeros((1, LANES), F32))


def _topk(aff, cap, tb):
    b, t, _ = aff.shape
    nb = t // tb
    return pl.pallas_call(
        functools.partial(_topk_kernel, cap=cap, tb=tb),
        grid=(b,),
        in_specs=[pl.BlockSpec((1, t, LANES), lambda bi: (bi, 0, 0))],
        out_specs=[pl.BlockSpec((1, t, LANES), lambda bi: (bi, 0, 0)),
                   pl.BlockSpec((1, nb, LANES), lambda bi: (bi, 0, 0))],
        out_shape=[jax.ShapeDtypeStruct((b, t, LANES), I32), jax.ShapeDtypeStruct((b, nb, LANES), I32)],
        scratch_shapes=[pltpu.VMEM((t, LANES), I32), pltpu.VMEM((t, LANES), F32)],
        compiler_params=_cp("parallel"),
        name="moe_topk",
    )(aff)


MOE_SMALL_COUNT = 48
MOE_STACK_ROWS = 1024


def _windows(cap, tb):
    narrow = MOE_SMALL_COUNT + BF16_ROWS
    if cap <= narrow:
        return (cap,)
    return (narrow, min(tb + BF16_ROWS, cap))


def _window_starts(s0_ref, bi, j, cap, w):
    if w == cap:
        return [0] * N_EXPERTS
    out = []
    for e in range(N_EXPERTS):
        st = jnp.minimum((s0_ref[bi, j, e] // BF16_ROWS) * BF16_ROWS, cap - w)
        out.append(pl.multiple_of(st, BF16_ROWS))
    return out


def _per_window(windows, narrow_ref, bi, j, run):
    if len(windows) == 1:
        run(windows[0])
    else:
        narrow = narrow_ref[bi, j] > 0
        pl.when(narrow)(lambda: run(windows[0]))
        pl.when(jnp.logical_not(narrow))(lambda: run(windows[1]))


def _gather_kernel(s0_ref, narrow_ref, h_ref, post_ref, o_ref, *, cap, tb):
    bi = pl.program_id(0)
    j = pl.program_id(1)

    @pl.when(j == 0)
    def _():
        o_ref[...] = jnp.zeros_like(o_ref)

    def run(w):
        starts = _window_starts(s0_ref, bi, j, cap, w)
        post = post_ref[0]
        r = _iota((w, tb), 0)
        group = max(1, min(N_EXPERTS, MOE_STACK_ROWS // w))
        for e0 in range(0, N_EXPERTS, group):
            es = range(e0, min(e0 + group, N_EXPERTS))
            onehot = jnp.concatenate(
                [jnp.where((post[e:e + 1, :] - starts[e]) == r, 1.0, 0.0).astype(BF16) for e in es],
                axis=0)
            rows = _dot(onehot, h_ref[0]).astype(BF16)
            for k, e in enumerate(es):
                cur = o_ref[0, e, pl.ds(starts[e], w), :]
                o_ref[0, e, pl.ds(starts[e], w), :] = cur + rows[k * w:(k + 1) * w]

    _per_window(_windows(cap, tb), narrow_ref, bi, j, run)


def _gather(h2, post, s0, narrow, cap, tb):
    b, t, d = h2.shape
    nb = t // tb
    grid_spec = pltpu.PrefetchScalarGridSpec(
        num_scalar_prefetch=2,
        grid=(b, nb),
        in_specs=[pl.BlockSpec((1, tb, d), lambda bi, j, s, n: (bi, j, 0)),
                  pl.BlockSpec((1, N_EXPERTS, tb), lambda bi, j, s, n: (bi, 0, j))],
        out_specs=pl.BlockSpec((1, N_EXPERTS, cap, d), lambda bi, j, s, n: (bi, 0, 0, 0)),
    )
    return pl.pallas_call(
        functools.partial(_gather_kernel, cap=cap, tb=tb),
        grid_spec=grid_spec,
        out_shape=jax.ShapeDtypeStruct((b, N_EXPERTS, cap, d), BF16),
        compiler_params=_cp("parallel", "arbitrary"),
        name="moe_gather",
    )(s0, narrow, h2, post)


def _expert_kernel(xc_ref, xs_ref, wg_ref, wu_ref, wd_ref, yc_ref, ys_ref, wg_s, wu_s, wd_s):
    i = pl.program_id(1)

    @pl.when(i == 0)
    def _():
        wg_s[...] = wg_ref[0].astype(BF16)
        wu_s[...] = wu_ref[0].astype(BF16)
        wd_s[...] = wd_ref[0].astype(BF16)

    def mlp(x_ref, y_ref):
        shp = x_ref.shape
        rows = shp[0] * shp[2]
        x = x_ref[...].reshape(rows, shp[3])
        step = rows // max(1, rows // 128)
        xs = [x[r:r + step] for r in range(0, rows, step)]
        hs = [(_silu(_dot(xp, wg_s[...])) * _dot(xp, wu_s[...])).astype(BF16) for xp in xs]
        ys = [_dot(hm, wd_s[...]).astype(BF16) for hm in hs]
        y_ref[...] = jnp.concatenate(ys, axis=0).reshape(shp)

    pl.when(i == 0)(lambda: mlp(xc_ref, yc_ref))
    pl.when(i > 0)(lambda: mlp(xs_ref, ys_ref))


def _experts(xin_c, xin_s, w_gate, w_up, w_down, layer):
    bc, e, cap_c, d = xin_c.shape
    bs, _, cap_s, _ = xin_s.shape
    f = w_gate.shape[-1]
    cspec = pl.BlockSpec((bc, 1, cap_c, d), lambda ei, i: (0, ei, 0, 0))
    sspec = pl.BlockSpec((1, 1, cap_s, d), lambda ei, i: (jnp.maximum(i - 1, 0), ei, 0, 0))
    ws = lambda a: pl.BlockSpec((None, 1) + a.shape[2:], lambda ei, i: (layer, ei, 0, 0))
    return pl.pallas_call(
        _expert_kernel,
        grid=(e, bs + 1),
        in_specs=[cspec, sspec, ws(w_gate), ws(w_up), ws(w_down)],
        out_specs=[cspec, sspec],
        out_shape=[jax.ShapeDtypeStruct(xin_c.shape, BF16), jax.ShapeDtypeStruct(xin_s.shape, BF16)],
        scratch_shapes=[pltpu.VMEM((d, f), BF16), pltpu.VMEM((d, f), BF16), pltpu.VMEM((f, d), BF16)],
        compiler_params=_cp("parallel", "arbitrary"),
        name="moe_experts",
    )(xin_c, xin_s, w_gate, w_up, w_down)


def _combine_kernel(*refs, cap, tb, final):
    windows = _windows(cap, tb)
    s0_ref, narrow_ref, y_ref, pos_ref, aff_ref, x_ref, gate_ref = refs[:7]
    pos_n = 7
    if final:
        gf_ref = refs[pos_n]
        pos_n += 1
    spread_ref, row_id_ref = refs[pos_n:pos_n + 2]
    o_ref = refs[-1]
    bi = pl.program_id(0)
    j = pl.program_id(1)
    offset = 1024
    lane = _iota((1, N_EXPERTS), 1)

    def run(w):
        starts = _window_starts(s0_ref, bi, j, cap, w)
        if w == windows[0]:
            stv = jnp.zeros((1, N_EXPERTS), I32)
            for e in range(N_EXPERTS):
                stv = jnp.where(lane == e, starts[e], stv)
            rel = pos_ref[0] - stv + offset
            spread = spread_ref[...]
            hi = lax.shift_right_logical(rel, 5).astype(F32).astype(BF16)
            lo = jnp.bitwise_and(rel, 31).astype(F32).astype(BF16)
            relx = 32.0 * _dot(hi, spread) + _dot(lo, spread) - offset
            gx = _dot(aff_ref[0].astype(BF16), spread)
            g = jnp.where(relx == row_id_ref[...], gx, 0.0)
            ys = jnp.concatenate([y_ref[0, e, pl.ds(starts[e], w), :] for e in range(N_EXPERTS)], axis=0)
            acc = _dot(g.astype(BF16), ys)
        else:
            pos = pos_ref[0]
            aff = aff_ref[0].astype(BF16).astype(F32)
            col = _iota((tb, w), 1)
            acc = jnp.zeros((tb, D_MODEL), F32)
            for e in range(N_EXPERTS):
                g = jnp.where((pos[:, e:e + 1] - starts[e]) == col, aff[:, e:e + 1], 0.0)
                acc = acc + _dot(g.astype(BF16), y_ref[0, e, pl.ds(starts[e], w), :])
        out = x_ref[0] + gate_ref[0] * acc
        if final:
            out = _rms_rows(out, gf_ref[...])
        o_ref[0] = out

    _per_window(windows, narrow_ref, bi, j, run)


def _combine(y, pos, aff, x3, gate, s0, narrow, cap, tb, per_batch, g_final):
    b, t, d = x3.shape
    nb = t // tb
    final = g_final is not None
    tok = lambda c: pl.BlockSpec((1, tb, c), lambda bi, j, s, n: (bi, j, 0))
    const = lambda a: pl.BlockSpec(a.shape, lambda bi, j, s, n: (0, 0))
    in_specs = [pl.BlockSpec((1, N_EXPERTS, cap, d), lambda bi, j, s, n: (bi, 0, 0, 0)),
                tok(N_EXPERTS), tok(N_EXPERTS), tok(d),
                pl.BlockSpec((1, 1, d), (lambda bi, j, s, n: (bi, 0, 0)) if per_batch
                             else (lambda bi, j, s, n: (0, 0, 0)))]
    args = [y, pos, aff, x3, gate]
    if final:
        in_specs.append(const(g_final))
        args.append(g_final)
    w = _windows(cap, tb)[0]
    spread = jnp.repeat(jnp.eye(N_EXPERTS, dtype=BF16), w, axis=1)
    row_id = jnp.tile(jnp.arange(w, dtype=F32), N_EXPERTS).reshape(1, -1)
    in_specs += [const(spread), const(row_id)]
    args += [spread, row_id]
    grid_spec = pltpu.PrefetchScalarGridSpec(
        num_scalar_prefetch=2, grid=(b, nb), in_specs=in_specs, out_specs=tok(d))
    return pl.pallas_call(
        functools.partial(_combine_kernel, cap=cap, tb=tb, final=final),
        grid_spec=grid_spec,
        out_shape=jax.ShapeDtypeStruct((b, t, d), F32),
        compiler_params=_cp("parallel", "arbitrary"),
        name="moe_combine",
    )(s0, narrow, *args)


def _moe(x2, h2, aff, b, t, gate, per_batch, g_final):
    cap = CAPACITY_FACTOR * t // N_EXPERTS
    tb = min(MOE_TOKEN_BLOCK, t)
    aff3 = aff.reshape(b, t, N_EXPERTS)
    per = LANES // N_EXPERTS
    groups = -(-b // per)
    nb = t // tb

    def pack(a):
        a = jnp.pad(a, ((0, groups * per - b), (0, 0), (0, 0)))
        return a.reshape(groups, per, a.shape[1], N_EXPERTS).transpose(0, 2, 1, 3).reshape(groups, a.shape[1], LANES)

    def unpack(a):
        return a.reshape(groups, a.shape[1], per, N_EXPERTS).transpose(0, 2, 1, 3).reshape(
            groups * per, a.shape[1], N_EXPERTS)[:b]

    pos, s0 = _topk(pack(aff3), cap, tb)
    pos, s0 = unpack(pos), unpack(s0)
    counts = jnp.diff(s0, axis=1, append=jnp.full((b, 1, N_EXPERTS), cap, I32))
    narrow = jnp.all(counts <= MOE_SMALL_COUNT, axis=-1).astype(I32)
    xin = _gather(h2.reshape(b, t, D_MODEL), pos.transpose(0, 2, 1), s0, narrow, cap, tb)
    y = yield xin
    out = _combine(y, pos, aff3, x2.reshape(b, t, D_MODEL), gate, s0, narrow, cap, tb, per_batch, g_final)
    return out.reshape(b * t, D_MODEL)


def _rope_tables(t):
    rows = t // GRID_W
    row = jnp.repeat(jnp.arange(rows, dtype=F32), GRID_W)
    col = jnp.tile(jnp.arange(GRID_W, dtype=F32), rows)
    n_freq = DQK_A // 4
    inv_freq = jnp.power(ROPE_BASE, -jnp.arange(n_freq, dtype=F32) / n_freq)
    ang = jnp.concatenate([row[:, None] * inv_freq, col[:, None] * inv_freq], axis=-1)
    cos, sin = jnp.cos(ang), jnp.sin(ang)
    zero = jnp.zeros_like(sin)
    reps = LANES // DQK_A
    cos_t = jnp.tile(jnp.concatenate([cos, cos], axis=-1), (1, reps))
    sin_a = jnp.tile(jnp.concatenate([-sin, zero], axis=-1), (1, reps))
    sin_b = jnp.tile(jnp.concatenate([zero, sin], axis=-1), (1, reps))
    return cos_t, sin_a, sin_b


def _stream(x, mods, per_batch, rope, cache, p):
    b, t, d = x.shape
    x2 = x.reshape(b * t, d)
    depth = p["g_norm"].shape[0]
    even_out, odd_out = [], []
    for l in range(depth):
        j = l // 2
        sh1, sc1, gt1, sh2, sc2, gt2 = mods[l]
        g1 = p["g_norm"][l, 0].reshape(1, d)
        g2 = p["g_norm"][l, 1].reshape(1, d)
        route = (g2, sh2, sc2, p["w_router_pad"][l])
        if l % 2 == 0:
            aq, ak, av, bqk, bv, bo, gates = _in_ab(x2, g1, sh1, sc1, p["w_in_ab_main"][j], p["w_in_ab_gate"][j],
                                                     rope, t, per_batch, F32 if cache is None else BF16)
            r3 = lambda a: a.reshape(b, t, a.shape[-1])
            if cache is not None:
                kc = cache[0][:, j].reshape(b, -1, W_A)
                vc = cache[1][:, j].reshape(b, -1, W_A)
                nu = 2 * H_B
                init = (cache[2][:, j].reshape(b, nu, DH_B, DH_B),
                        cache[3][:, j].reshape(b, nu, 1, DH_B),
                        jnp.broadcast_to(cache[4][:, j].reshape(b, nu, 1, 1), (b, nu, 1, LANES)))
            else:
                kc = vc = init = None
            lam_init = 0.8 - 0.6 * math.exp(-0.3 * l)
            attn = _attention_pipe(r3(aq), r3(ak), r3(av), kc, vc, p["lam_ab"][j], lam_init)
            qk = _conv_silu(r3(bqk), p["conv_ab"][j])
            bias = jnp.pad(jnp.concatenate([p["b_ig_ab"][j].reshape(-1), p["b_fg_ab"][j].reshape(-1)]),
                           (0, LANES - 4 * H_B)).reshape(1, LANES)
            res = _mlstm(qk, r3(bv), r3(gates), bias, init, cache is None)
            hf, hb = res[0], res[1]
            if cache is None:
                even_out.append((ak, av, res[2], res[3], res[4]))
            x2, h2, aff = _out_ab(attn.reshape(b * t, W_A), hf.reshape(b * t, W_B), hb.reshape(b * t, W_B), bo, x2,
                                  gt1, p["g_attn_ab"][j].reshape(1, W_A), p["g_mlstm_ab"][j].reshape(1, W_B),
                                  p["w_out_ab_bf16"][j], route, t, per_batch, 1.0 - lam_init)
        else:
            q, v, og, lf, kk = _in_c(x2, g1, sh1, sc1, p["w_in_c_bf16"][j], p["b_f_c"][j], p["gamma_lb"], l, t,
                                     per_batch)
            r3 = lambda a: a.reshape(b, t, a.shape[-1])
            init = None if cache is None else cache[5][:, j]
            res = _hgrn(r3(q), r3(v), r3(lf), r3(kk), init, cache is None)
            if cache is None:
                odd_out.append(res[2])
            x2, h2, aff = _out_c(res[0].reshape(b * t, W_C), res[1].reshape(b * t, W_C), og, x2, gt1,
                                 p["g_hgrn_c"][j].reshape(1, W_C), p["w_out_c_bf16"][j], route, t, per_batch)
        g_final = p["g_final"].reshape(1, d) if l == depth - 1 else None
        x2 = yield from _moe(x2, h2, aff, b, t, gt2, per_batch, g_final)
    return x2.reshape(b, t, d), even_out, odd_out


def kernel(x_prompt, x_sample, cache_dattn_k, cache_dattn_v, state_mlstm_C, state_mlstm_n, state_mlstm_m,
           state_hgrn_S, c, c_ctx, w_ada, b_ada, g_norm, g_final, w_in_ab, b_ig_ab, b_fg_ab, lam_ab, conv_ab,
           g_attn_ab, g_mlstm_ab, w_out_ab, w_in_c, b_f_c, gamma_lb, g_hgrn_c, w_out_c,
           w_router, w_gate_e, w_up_e, w_down_e):
    d = D_MODEL
    bs = c.shape[0]
    bp, tp, _ = x_prompt.shape
    depth = w_ada.shape[0]
    n_main = 3 * W_A + 4 * W_B
    p = dict(
        g_norm=g_norm, g_final=g_final, b_ig_ab=b_ig_ab, b_fg_ab=b_fg_ab, lam_ab=lam_ab, conv_ab=conv_ab,
        g_attn_ab=g_attn_ab, g_mlstm_ab=g_mlstm_ab, b_f_c=b_f_c, gamma_lb=gamma_lb, g_hgrn_c=g_hgrn_c,
        w_gate_e=w_gate_e, w_up_e=w_up_e, w_down_e=w_down_e,
        w_in_ab_main=w_in_ab[:, :, :n_main].astype(BF16),
        w_in_ab_gate=jnp.pad(w_in_ab[:, :, n_main:], ((0, 0), (0, 0), (0, LANES - 4 * H_B))).astype(BF16),
        w_out_ab_bf16=w_out_ab.astype(BF16),
        w_in_c_bf16=w_in_c.astype(BF16),
        w_out_c_bf16=w_out_c.astype(BF16),
        w_router_pad=jnp.pad(w_router, ((0, 0), (0, 0), (0, LANES - N_EXPERTS))),
    )
    rows = -(-(bs + 1) // 8) * 8
    cond = jnp.concatenate([c, c_ctx.reshape(1, d), jnp.zeros((rows - bs - 1, d), F32)], axis=0)
    mods = _ada(cond, w_ada, b_ada)
    mods_s = [[mods[l, :bs, i * d:(i + 1) * d].reshape(bs, 1, d) for i in range(6)] for l in range(depth)]
    mods_c = [[mods[l, bs:bs + 1, i * d:(i + 1) * d].reshape(1, 1, d) for i in range(6)] for l in range(depth)]

    cache = (cache_dattn_k, cache_dattn_v, state_mlstm_C, state_mlstm_n, state_mlstm_m, state_hgrn_S)
    ctx = _stream(x_prompt, mods_c, False, None, None, p)
    smp = _stream(x_sample, mods_s, True, _rope_tables(x_sample.shape[1]), cache, p)
    xin_c, xin_s = next(ctx), next(smp)
    for l in range(depth):
        y_c, y_s = _experts(xin_c, xin_s, w_gate_e, w_up_e, w_down_e, l)
        try:
            xin_c = ctx.send(y_c)
        except StopIteration as done:
            y_prompt, ev, od = done.value
        try:
            xin_s = smp.send(y_s)
        except StopIteration as done:
            y_sample = done.value[0]
    new_k = jnp.stack([e[0].reshape(bp, tp, H_A, 2 * DQK_A) for e in ev], axis=1)
    new_v = jnp.stack([e[1].reshape(bp, tp, H_A, DV_A) for e in ev], axis=1)
    new_c = jnp.stack([e[2].reshape(bp, 2, H_B, DH_B, DH_B) for e in ev], axis=1)
    new_n = jnp.stack([e[3].reshape(bp, 2, H_B, DH_B) for e in ev], axis=1)
    new_m = jnp.stack([e[4][..., 0].reshape(bp, 2, H_B) for e in ev], axis=1)
    new_s = jnp.stack(od, axis=1)
    return (y_prompt, y_sample, new_k, new_v, new_c, new_n, new_m, new_s)
```

```python
import functools
import math

import jax
import jax.numpy as jnp
from jax import lax
from jax.experimental import pallas as pl
from jax.experimental.pallas import tpu as pltpu

F32 = jnp.float32
BF16 = jnp.bfloat16
I32 = jnp.int32

D_MODEL = 1024
H_A = 4
DV_A = 128
DQK_A = 64
W_A = H_A * DV_A
H_B = 4
DH_B = 128
W_B = H_B * DH_B
H_C = 8
DH_C = 128
W_C = H_C * DH_C
N_EXPERTS = 16
CAPACITY_FACTOR = 2
GRID_W = 64
ROPE_BASE = 10000.0
RMS_EPS = 1e-6
NEG_BIG = -1e30
LOG2E = 1.4426950408889634
LANES = 128
BF16_ROWS = 16
SCAN_CHUNK = 128
MLSTM_BATCHES_PER_STEP = 2
MOE_TOKEN_BLOCK = 256
VMEM_LIMIT = 56 * 1024 * 1024


def _cp(*sem):
    return pltpu.CompilerParams(dimension_semantics=sem, vmem_limit_bytes=VMEM_LIMIT)


def _dot(a, b):
    return jnp.dot(a, b, preferred_element_type=F32)


def _dot_nt(a, b):
    return lax.dot_general(a, b, (((1,), (1,)), ((), ())), preferred_element_type=F32)


def _dot_tn(a, b):
    return lax.dot_general(a, b, (((0,), (0,)), ((), ())), preferred_element_type=F32)


def _split3(x):
    hi = x.astype(BF16)
    r = x - hi.astype(F32)
    mid = r.astype(BF16)
    lo = (r - mid.astype(F32)).astype(BF16)
    return hi, mid, lo


def _dot_sel_left(sel, x):
    hi, mid, lo = _split3(x)
    return _dot(sel, hi) + _dot(sel, mid) + _dot(sel, lo)


def _dot_sel_right(x, sel):
    hi, mid, lo = _split3(x)
    return _dot(hi, sel) + _dot(mid, sel) + _dot(lo, sel)


def _dot_f32(a, b):
    ah, am, al = _split3(a)
    bh, bm, bl = _split3(b)
    return (_dot(ah, bh) + (_dot(ah, bm) + _dot(am, bh))
            + (_dot(ah, bl) + _dot(am, bm) + _dot(al, bh)))


def _dot_f32x3(a, b):
    ah = a.astype(BF16)
    al = (a - ah.astype(F32)).astype(BF16)
    bh = b.astype(BF16)
    bl = (b - bh.astype(F32)).astype(BF16)
    return _dot(ah, bh) + (_dot(ah, bl) + _dot(al, bh))


def _dot_sel_left2(sel, x):
    hi = x.astype(BF16)
    lo = (x - hi.astype(F32)).astype(BF16)
    return _dot(sel, hi) + _dot(sel, lo)


def _sigmoid(x):
    return 1.0 / (1.0 + jnp.exp(-x))


def _silu(x):
    return x * _sigmoid(x)


def _log_sigmoid(x):
    return jnp.minimum(x, 0.0) - jnp.log(1.0 + jnp.exp(-jnp.abs(x)))


def _rms_rows(x, g):
    return x * lax.rsqrt(jnp.mean(x * x, axis=-1, keepdims=True) + RMS_EPS) * g


def _iota(shape, dim):
    return lax.broadcasted_iota(I32, shape, dim)


def _ada_kernel(c_ref, w_ref, b_ref, o_ref):
    c = c_ref[...]
    o_ref[0] = _dot_f32(_silu(c), w_ref[0]) + b_ref[0]


def _ada(cond_rows, w_ada, b_ada):
    depth, d, n6 = w_ada.shape
    rows = cond_rows.shape[0]
    tn = 1536
    return pl.pallas_call(
        _ada_kernel,
        grid=(depth, n6 // tn),
        in_specs=[pl.BlockSpec((rows, d), lambda l, j: (0, 0)),
                  pl.BlockSpec((1, d, tn), lambda l, j: (l, 0, j)),
                  pl.BlockSpec((1, 1, tn), lambda l, j: (l, 0, j))],
        out_specs=pl.BlockSpec((1, rows, tn), lambda l, j: (l, 0, j)),
        out_shape=jax.ShapeDtypeStruct((depth, rows, n6), F32),
        compiler_params=_cp("parallel", "parallel"),
        name="adaln",
    )(cond_rows, w_ada, b_ada.reshape(depth, 1, n6))


def _mod_spec(per_batch, tm, t):
    if per_batch:
        return pl.BlockSpec((1, 1, D_MODEL), lambda i: (i * tm // t, 0, 0))
    return pl.BlockSpec((1, 1, D_MODEL), lambda i: (0, 0, 0))


def _row_tile(t):
    return min(512, t)


def _in_ab_kernel(*refs, use_rope):
    if use_rope:
        (x_ref, g_ref, sh_ref, sc_ref, w_ref, wg_ref, cos_ref, sa_ref, sb_ref,
         aq_ref, ak_ref, av_ref, bqk_ref, bv_ref, bo_ref, gt_ref) = refs
    else:
        (x_ref, g_ref, sh_ref, sc_ref, w_ref, wg_ref,
         aq_ref, ak_ref, av_ref, bqk_ref, bv_ref, bo_ref, gt_ref) = refs
    h = _rms_rows(x_ref[...], g_ref[...]) * (1.0 + sc_ref[0]) + sh_ref[0]
    hb = h.astype(BF16)
    y = _dot(hb, w_ref[...])
    gt_ref[...] = _dot(hb, wg_ref[...])
    aq = y[:, 0:W_A] * (DQK_A ** -0.5 * LOG2E)
    ak = y[:, W_A:2 * W_A]
    if use_rope:
        cos = jnp.concatenate([cos_ref[...]] * H_A, axis=1)
        sa = jnp.concatenate([sa_ref[...]] * H_A, axis=1)
        sb = jnp.concatenate([sb_ref[...]] * H_A, axis=1)
        half = DQK_A // 2

        def rope(v):
            return (v * cos + pltpu.roll(v, W_A - half, axis=1) * sa
                    + pltpu.roll(v, half, axis=1) * sb)

        aq = rope(aq)
        ak = rope(ak)
    aq_ref[...] = aq.astype(aq_ref.dtype)
    ak_ref[...] = ak.astype(ak_ref.dtype)
    av_ref[...] = y[:, 2 * W_A:3 * W_A].astype(av_ref.dtype)
    bqk_ref[...] = y[:, 3 * W_A:3 * W_A + 2 * W_B].astype(BF16)
    bv_ref[...] = y[:, 3 * W_A + 2 * W_B:3 * W_A + 3 * W_B].astype(BF16)
    bo_ref[...] = y[:, 3 * W_A + 3 * W_B:3 * W_A + 4 * W_B].astype(BF16)


def _in_ab(x2, g, shift, scale, w_main, w_gate, rope, t, per_batch, qkv_dtype):
    n = x2.shape[0]
    tm = _row_tile(t)
    use_rope = rope is not None
    row = lambda c: pl.BlockSpec((tm, c), lambda i: (i, 0))
    full = lambda a: pl.BlockSpec(a.shape, lambda i: (0,) * a.ndim)
    in_specs = [row(D_MODEL), full(g), _mod_spec(per_batch, tm, t), _mod_spec(per_batch, tm, t),
                full(w_main), full(w_gate)]
    args = [x2, g, shift, scale, w_main, w_gate]
    if use_rope:
        nt = t // tm
        tab = pl.BlockSpec((tm, LANES), lambda i: (i % nt, 0))
        in_specs += [tab, tab, tab]
        args += list(rope)
    widths = (W_A, W_A, W_A, 2 * W_B, W_B, W_B, LANES)
    return pl.pallas_call(
        functools.partial(_in_ab_kernel, use_rope=use_rope),
        grid=(n // tm,),
        in_specs=in_specs,
        out_specs=[row(c) for c in widths],
        out_shape=[jax.ShapeDtypeStruct((n, c), dt) for c, dt in
                   zip(widths, (qkv_dtype, qkv_dtype, qkv_dtype, BF16, BF16, BF16, F32))],
        compiler_params=_cp("parallel"),
        name="in_proj_even",
    )(*args)


def _attn_pipe_kernel(*refs, has_cache, tk, lam_init, nq):
    if has_cache:
        q_ref, k_ref, v_ref, kc_ref, vc_ref, lam_ref, o_ref, s_a, s_b, mp_s, vt_s, sc_s, vct_s = refs
    else:
        q_ref, k_ref, v_ref, lam_ref, o_ref, s_a, s_b, mp_s, vt_s = refs
    s_s = (s_a, s_b)
    i = pl.program_id(2)
    q = q_ref[0]
    tq = q.shape[0]
    first = _iota((1, LANES), 1) < DQK_A
    qs = (jnp.where(first, q, 0).astype(BF16), jnp.where(first, 0, q).astype(BF16))
    nkb = k_ref.shape[1] // tk

    @pl.when(i == 0)
    def _():
        for j in range(nkb):
            vt_s[j, 0:DV_A, :] = v_ref[0, j * tk:(j + 1) * tk, :].astype(F32).T.astype(BF16)
            vt_s[j, DV_A:, :] = jnp.ones((BF16_ROWS, tk), BF16)
        if has_cache:
            vct_s[0:DV_A, :] = vc_ref[0].astype(F32).T.astype(BF16)
            vct_s[DV_A:, :] = jnp.ones((BF16_ROWS, vct_s.shape[1]), BF16)

    def groups(x):
        return [x[:, g * LANES:(g + 1) * LANES] for g in range(x.shape[1] // LANES)]

    def body(score, finish, cur):
        prev = 1 - cur
        if finish:
            m = [jnp.max(mp_s[prev, a], axis=-1, keepdims=True) for a in range(2)]
        mpart = [jnp.full((tq, LANES), NEG_BIG, F32)] * 2
        acc = [jnp.zeros((DV_A + BF16_ROWS, tq), F32)] * 2

        def pass1(kb, store):
            for a in range(2):
                s = _dot_nt(qs[a], kb)
                store(a, s)
                mpart[a] = functools.reduce(jnp.maximum, groups(s), mpart[a])

        def pass2(a, s, vt):
            acc[a] = acc[a] + _dot_nt(vt, jnp.exp2(s - m[a]).astype(BF16))

        if has_cache:
            def store_c(a, s):
                sc_s[cur, a] = s
            if score:
                pass1(kc_ref[0].astype(BF16), store_c)
            if finish:
                for a in range(2):
                    pass2(a, sc_s[prev, a], vct_s[...])
        for j in range(nkb):
            def store_j(a, s, j=j):
                s_s[cur][a, j] = s
            if score:
                pass1(k_ref[0, j * tk:(j + 1) * tk, :].astype(BF16), store_j)
            if finish:
                for a in range(2):
                    pass2(a, s_s[prev][a, j], vt_s[j])
        if score:
            for a in range(2):
                mp_s[cur, a] = mpart[a]
        if finish:
            lp = lam_ref[...]
            lam = (jnp.exp(jnp.sum(lp[0:1] * lp[1:2], axis=-1, keepdims=True))
                   - jnp.exp(jnp.sum(lp[2:3] * lp[3:4], axis=-1, keepdims=True)) + lam_init)
            o1 = acc[0][0:DV_A] / acc[0][DV_A:DV_A + 1]
            o2 = acc[1][0:DV_A] / acc[1][DV_A:DV_A + 1]
            o_ref[0] = (o1 - lam * o2).T.astype(BF16)

    pl.when(i == 0)(lambda: body(True, False, 0))
    inner = jnp.logical_and(i > 0, i < nq)
    pl.when(jnp.logical_and(inner, i % 2 == 0))(lambda: body(True, True, 0))
    pl.when(jnp.logical_and(inner, i % 2 == 1))(lambda: body(True, True, 1))
    pl.when(i == nq)(lambda: body(False, True, nq % 2))


def _attention_pipe(aq, ak, av, kc, vc, lam_p, lam_init):
    b, t, _ = aq.shape
    tq = min(256, t)
    tk = min(256, t)
    nq = t // tq
    has_cache = kc is not None
    qspec = pl.BlockSpec((1, tq, DV_A), lambda bi, h, i: (bi, jnp.minimum(i, nq - 1), h))
    ospec = pl.BlockSpec((1, tq, DV_A), lambda bi, h, i: (bi, jnp.maximum(i - 1, 0), h))
    ks = pl.BlockSpec((1, t, DV_A), lambda bi, h, i: (bi, 0, h))
    in_specs = [qspec, ks, ks]
    args = [aq, ak, av]
    if has_cache:
        cs = pl.BlockSpec((1, kc.shape[1], DV_A), lambda bi, h, i: (bi, 0, h))
        in_specs += [cs, cs]
        args += [kc, vc]
    in_specs.append(pl.BlockSpec(lam_p.shape, lambda bi, h, i: (0, 0)))
    args.append(lam_p)
    scratch = [pltpu.VMEM((2, t // tk, tq, tk), F32), pltpu.VMEM((2, t // tk, tq, tk), F32),
               pltpu.VMEM((2, 2, tq, LANES), F32),
               pltpu.VMEM((t // tk, DV_A + BF16_ROWS, tk), BF16)]
    if has_cache:
        scratch += [pltpu.VMEM((2, 2, tq, kc.shape[1]), F32),
                    pltpu.VMEM((DV_A + BF16_ROWS, kc.shape[1]), BF16)]
    return pl.pallas_call(
        functools.partial(_attn_pipe_kernel, has_cache=has_cache, tk=tk, lam_init=lam_init, nq=nq),
        grid=(b, H_A, nq + 1),
        in_specs=in_specs,
        out_specs=ospec,
        out_shape=jax.ShapeDtypeStruct((b, t, W_A), BF16),
        scratch_shapes=scratch,
        compiler_params=_cp("parallel", "parallel", "arbitrary"),
        name="diff_attention",
    )(*args)


def _conv_kernel(x_ref, w_ref, o_ref):
    x = x_ref[0].astype(F32)
    t = x.shape[0]
    w = w_ref[...]
    r = _iota((t, 1), 0)
    prev = jnp.where(r == 0, 0.0, pltpu.roll(x, 1, axis=0))
    nxt = jnp.where(r == t - 1, 0.0, pltpu.roll(x, t - 1, axis=0))
    scale = jnp.where(pl.program_id(1) * x.shape[1] >= W_B, DH_B ** -0.5, 1.0)
    o_ref[0] = (_silu(prev * w[0:1] + x * w[1:2] + nxt * w[2:3]) * scale).astype(BF16)


def _conv_silu(bqk, conv_w):
    b, t, c = bqk.shape
    tc = 256
    spec = pl.BlockSpec((1, t, tc), lambda bi, j: (bi, 0, j))
    return pl.pallas_call(
        _conv_kernel,
        grid=(b, c // tc),
        in_specs=[spec, pl.BlockSpec((conv_w.shape[0], tc), lambda bi, j: (0, j))],
        out_specs=spec,
        out_shape=jax.ShapeDtypeStruct(bqk.shape, BF16),
        compiler_params=_cp("parallel", "parallel"),
        name="mlstm_conv",
    )(bqk, conv_w)


def _tri(n, lower):
    r = _iota((n, n), 0)
    c = _iota((n, n), 1)
    return r >= c if lower else r <= c


def _mlstm_kernel(*refs, has_init, out_state):
    qf, kf, vf, gf, qb, kb, vb, gb, bias = refs[:9]
    pos = 9
    if has_init:
        c0, n0, m0 = refs[pos:pos + 3]
        pos += 3
    hf, hb = refs[pos:pos + 2]
    pos += 2
    if out_state:
        c_out, n_out, m_out = refs[pos:pos + 3]
        pos += 3
    cn_s, m_s = refs[pos:pos + 2]
    step = pl.program_id(1)
    L = SCAN_CHUNK
    nu = 2 * H_B
    nbt = qf.shape[0]

    @pl.when(step == 0)
    def _():
        if has_init:
            for bb in range(nbt):
                for u in range(nu):
                    cn_s[bb * nu + u, :, 0:DH_B] = c0[bb, u]
                    cn_s[bb * nu + u, :, DH_B:2 * DH_B] = jnp.broadcast_to(n0[bb, u], (DH_B, DH_B)).T
                m_s[bb * nu:(bb + 1) * nu] = m0[bb]
        else:
            cn_s[...] = jnp.zeros_like(cn_s)
            m_s[...] = jnp.zeros_like(m_s)

    ones_b = jnp.ones((L, DH_B), BF16)
    sel_rows = _iota((LANES, LANES), 0)
    lower = _tri(L, True)
    upper = _tri(L, False)
    lower_b = lower.astype(BF16)
    upper_b = upper.astype(BF16)
    col = _iota((1, LANES), 1)
    units = []
    for bb, d, (q_ref, k_ref, v_ref, g_ref, h_ref) in [
            (bb, d, refs_d) for bb in range(nbt)
            for d, refs_d in enumerate(((qf, kf, vf, gf, hf), (qb, kb, vb, gb, hb)))]:
        g = g_ref[bb] + bias[...]
        g = jnp.where(col < 2 * H_B, g, _log_sigmoid(g))
        gt = g.T
        if d == 0:
            bc_all = _dot_sel_left(lower_b, g)
            br_all = _dot_sel_right(gt, upper_b)
            mask = lower
        else:
            bc_all = _dot_sel_left(upper_b, g)
            br_all = _dot_sel_right(gt, lower_b)
            mask = upper
        for h in range(H_B):
            u = d * H_B + h
            sl = slice(h * DH_B, (h + 1) * DH_B)
            units.append(dict(
                u=u, su=bb * nu + u, bb=bb, d=d, sl=sl, h_ref=h_ref, mask=mask, g=g, bc_all=bc_all,
                li_row=gt[u:u + 1, :], b_row=br_all[2 * H_B + u:2 * H_B + u + 1, :],
                q=q_ref[bb, :, sl].astype(BF16), k=k_ref[bb, :, sl].astype(F32),
                v1=jnp.concatenate([v_ref[bb, :, sl].astype(BF16), ones_b], axis=1)))
    for x in units:
        x["li_b"] = _dot_sel_right(x["g"], (sel_rows == x["u"]).astype(BF16))
        x["b_b"] = _dot_sel_right(x["bc_all"], (sel_rows == 2 * H_B + x["u"]).astype(BF16))
        x["qk"] = _dot_nt(x["q"], x["k"].astype(BF16))
        x["cn"] = cn_s[x["su"]]
        x["qc"] = _dot(x["q"], x["cn"].astype(BF16))
    for x in units:
        x["m_prev"] = m_s[x["su"]]
        a_b = x["b_b"] + x["m_prev"]
        dm = jnp.where(x["mask"], x["b_b"] - x["b_row"] + x["li_row"], NEG_BIG)
        x["m_t"] = jnp.maximum(a_b, jnp.max(dm, axis=-1, keepdims=True))
        x["w_inter"] = jnp.exp(a_b - x["m_t"])
        x["s"] = (x["qk"] * jnp.exp(dm - x["m_t"])).astype(BF16)
    for x in units:
        x["sv"] = _dot(x["s"], x["v1"])
        b_end = x["b_b"][L - 1:L, :] if x["d"] == 0 else x["b_b"][0:1, :]
        g_b = b_end - x["b_b"] + x["li_b"]
        x["m_new"] = jnp.maximum(b_end + x["m_prev"], jnp.max(g_b, axis=0, keepdims=True))
        x["w_old"] = jnp.exp(b_end + x["m_prev"] - x["m_new"])
        x["ks"] = (jnp.exp(g_b - x["m_new"]) * x["k"]).astype(BF16)
    for x in units:
        num = x["w_inter"] * x["qc"][:, 0:DH_B] + x["sv"][:, 0:DH_B]
        den = x["w_inter"] * x["qc"][:, DH_B:] + x["sv"][:, DH_B:]
        x["h_ref"][x["bb"], :, x["sl"]] = (num / jnp.maximum(jnp.abs(den), jnp.exp(-x["m_t"]))).astype(BF16)
        w2 = jnp.concatenate([x["w_old"], x["w_old"]], axis=1)
        cn_s[x["su"]] = w2 * x["cn"] + _dot_tn(x["ks"], x["v1"])
        m_s[x["su"]] = x["m_new"]

    if out_state:
        @pl.when(step == pl.num_programs(1) - 1)
        def _():
            for bb in range(nbt):
                for u in range(nu):
                    c_out[bb, u] = cn_s[bb * nu + u, :, 0:DH_B]
                    n_out[bb, u] = cn_s[bb * nu + u, :, DH_B:2 * DH_B].T[0:1, :]
                m_out[bb] = m_s[bb * nu:(bb + 1) * nu]


def _mlstm(qk, v, gates, bias, init, out_state):
    b, t, _ = v.shape
    L = SCAN_CHUNK
    nc = t // L
    nu = 2 * H_B
    nbt = MLSTM_BATCHES_PER_STEP if b % MLSTM_BATCHES_PER_STEP == 0 else 1
    fwd = lambda j: (lambda bi, c: (bi, c, j))
    bwd = lambda j: (lambda bi, c: (bi, nc - 1 - c, j))
    blk = lambda w, im: pl.BlockSpec((nbt, L, w), im)
    in_specs = [blk(W_B, fwd(0)), blk(W_B, fwd(1)), blk(W_B, fwd(0)), blk(LANES, fwd(0)),
                blk(W_B, bwd(0)), blk(W_B, bwd(1)), blk(W_B, bwd(0)), blk(LANES, bwd(0)),
                pl.BlockSpec((1, LANES), lambda bi, c: (0, 0))]
    args = [qk, qk, v, gates, qk, qk, v, gates, bias]
    st_specs = [pl.BlockSpec((nbt, nu, DH_B, DH_B), lambda bi, c: (bi, 0, 0, 0)),
                pl.BlockSpec((nbt, nu, 1, DH_B), lambda bi, c: (bi, 0, 0, 0)),
                pl.BlockSpec((nbt, nu, 1, LANES), lambda bi, c: (bi, 0, 0, 0))]
    st_shapes = [jax.ShapeDtypeStruct((b, nu, DH_B, DH_B), F32),
                 jax.ShapeDtypeStruct((b, nu, 1, DH_B), F32),
                 jax.ShapeDtypeStruct((b, nu, 1, LANES), F32)]
    has_init = init is not None
    if has_init:
        in_specs += st_specs
        args += list(init)
    out_specs = [blk(W_B, fwd(0)), blk(W_B, bwd(0))]
    out_shape = [jax.ShapeDtypeStruct((b, t, W_B), BF16)] * 2
    if out_state:
        out_specs += st_specs
        out_shape += st_shapes
    return pl.pallas_call(
        functools.partial(_mlstm_kernel, has_init=has_init, out_state=out_state),
        grid=(b // nbt, nc),
        in_specs=in_specs,
        out_specs=out_specs,
        out_shape=out_shape,
        scratch_shapes=[pltpu.VMEM((nbt * nu, DH_B, 2 * DH_B), F32), pltpu.VMEM((nbt * nu, 1, LANES), F32)],
        compiler_params=_cp("parallel", "arbitrary"),
        name="mlstm_scan",
    )(*args)


def _head_rms(x, g, heads, width):
    parts = []
    for h in range(heads):
        sl = slice(h * width, (h + 1) * width)
        parts.append(_rms_rows(x[:, sl], g[:, sl]))
    return parts


OUT_PROJ_PARTS = 4


def _row_parts(tm):
    step = tm // OUT_PROJ_PARTS
    return [slice(r, r + step) for r in range(0, tm, step)]


def _project_and_route(cats, parts, x_ref, gate_ref, w_ref, g_ref, sh_ref, sc_ref, wr_ref, o_ref, h_ref, aff_ref):
    w = w_ref[...]
    xs = [x_ref[rs, :] + gate_ref[0] * _dot(cat, w) for cat, rs in zip(cats, parts)]
    hs = []
    for x, rs in zip(xs, parts):
        o_ref[rs, :] = x
        h = _rms_rows(x, g_ref[...]) * (1.0 + sc_ref[0]) + sh_ref[0]
        h_ref[rs, :] = h.astype(BF16)
        hs.append(h)
    wr = wr_ref[...]
    logits = [_dot_f32x3(h, wr) for h in hs]
    valid = _iota((1, LANES), 1) < N_EXPERTS
    for lg, rs in zip(logits, parts):
        lg = jnp.where(valid, lg, NEG_BIG)
        e = jnp.exp(lg - jnp.max(lg, axis=-1, keepdims=True))
        e = jnp.where(valid, e, 0.0)
        aff_ref[rs, :] = (e / jnp.sum(e, axis=-1, keepdims=True))[:, :N_EXPERTS]


def _route_specs(route, tm, t, per_batch):
    g2, sh2, sc2, wr = route
    full = lambda a: pl.BlockSpec(a.shape, lambda i: (0,) * a.ndim)
    row = lambda c: pl.BlockSpec((tm, c), lambda i: (i, 0))
    in_specs = [full(g2), _mod_spec(per_batch, tm, t), _mod_spec(per_batch, tm, t), full(wr)]
    out_specs = [row(D_MODEL), row(D_MODEL), row(N_EXPERTS)]
    return in_specs, out_specs


def _route_shapes(n):
    return [jax.ShapeDtypeStruct((n, D_MODEL), F32), jax.ShapeDtypeStruct((n, D_MODEL), BF16),
            jax.ShapeDtypeStruct((n, N_EXPERTS), F32)]


def _out_ab_kernel(at_ref, hf_ref, hb_ref, bo_ref, x_ref, gate_ref, ga_ref, gm_ref, w_ref,
                   g2_ref, sh2_ref, sc2_ref, wr_ref, o_ref, h_ref, aff_ref, *, out_scale):
    parts = _row_parts(x_ref.shape[0])
    cats = []
    for rs in parts:
        oa = [p * out_scale for p in _head_rms(at_ref[rs, :].astype(F32), ga_ref[...], H_A, DV_A)]
        hm = _head_rms(hf_ref[rs, :].astype(F32) + hb_ref[rs, :].astype(F32), gm_ref[...], H_B, DH_B)
        bo = bo_ref[rs, :].astype(F32)
        ob = [_sigmoid(bo[:, h * DH_B:(h + 1) * DH_B]) * hm[h] for h in range(H_B)]
        cats.append(jnp.concatenate(oa + ob, axis=1).astype(BF16))
    _project_and_route(cats, parts, x_ref, gate_ref, w_ref, g2_ref, sh2_ref, sc2_ref, wr_ref, o_ref, h_ref, aff_ref)


def _out_ab(attn, hf, hb, bo, x2, gate, g_attn, g_mlstm, w_out, route, t, per_batch, out_scale):
    n = x2.shape[0]
    tm = _row_tile(t)
    row = lambda c: pl.BlockSpec((tm, c), lambda i: (i, 0))
    full = lambda a: pl.BlockSpec(a.shape, lambda i: (0,) * a.ndim)
    r_in, r_out = _route_specs(route, tm, t, per_batch)
    return pl.pallas_call(
        functools.partial(_out_ab_kernel, out_scale=out_scale),
        grid=(n // tm,),
        in_specs=[row(W_A), row(W_B), row(W_B), row(W_B), row(D_MODEL), _mod_spec(per_batch, tm, t),
                  full(g_attn), full(g_mlstm), full(w_out)] + r_in,
        out_specs=r_out,
        out_shape=_route_shapes(n),
        compiler_params=_cp("parallel"),
        name="out_proj_even",
    )(attn, hf, hb, bo, x2, gate, g_attn, g_mlstm, w_out, *route)


def _in_c_kernel(x_ref, g_ref, sh_ref, sc_ref, w_ref, bf_ref, gam_ref, q_ref, v_ref, og_ref, lf_ref, kk_ref, *, layer):
    gam = gam_ref[...]
    e = jnp.exp(gam - jnp.max(gam, axis=0, keepdims=True))
    p = e / jnp.sum(e, axis=0, keepdims=True)
    lbs = p[0:1]
    lbs0 = lbs
    for i in range(1, layer + 1):
        lbs = lbs + p[i:i + 1]
    lb = jnp.maximum(lbs - lbs0, 0.0)
    L = SCAN_CHUNK
    w = w_ref[...]
    chunks = [slice(c0, c0 + L) for c0 in range(0, x_ref.shape[0], L)]
    ys = []
    for rs in chunks:
        h = _rms_rows(x_ref[rs, :], g_ref[...]) * (1.0 + sc_ref[0]) + sh_ref[0]
        ys.append(_dot(h.astype(BF16), w))
    for rs, y in zip(chunks, ys):
        q_ref[rs, :] = y[:, 0:W_C].astype(BF16)
        v_ref[rs, :] = y[:, W_C:2 * W_C].astype(BF16)
        og_ref[rs, :] = y[:, 2 * W_C:3 * W_C].astype(BF16)
        for d in range(2):
            f = y[:, (3 + d) * W_C:(4 + d) * W_C] + bf_ref[d:d + 1]
            u = jnp.exp(-jnp.abs(f))
            neg = f < 0.0
            lf2 = (jnp.log(jnp.where(neg, lb + u, 1.0 + lb * u)) - jnp.log(1.0 + u)) * LOG2E
            kk_ref[rs, d * W_C:(d + 1) * W_C] = ((1.0 - lb) * jnp.where(neg, 1.0, u) / (1.0 + u)).astype(BF16)
            lf_ref[rs, d * W_C:(d + 1) * W_C] = _dot_sel_left2(_tri(L, d == 0).astype(BF16), lf2)


def _in_c(x2, g, shift, scale, w_in, b_f, gamma_lb, layer, t, per_batch):
    n = x2.shape[0]
    tm = _row_tile(t)
    row = lambda c: pl.BlockSpec((tm, c), lambda i: (i, 0))
    full = lambda a: pl.BlockSpec(a.shape, lambda i: (0,) * a.ndim)
    widths = (W_C, W_C, W_C, 2 * W_C, 2 * W_C)
    return pl.pallas_call(
        functools.partial(_in_c_kernel, layer=layer),
        grid=(n // tm,),
        in_specs=[row(D_MODEL), full(g), _mod_spec(per_batch, tm, t), _mod_spec(per_batch, tm, t),
                  full(w_in), full(b_f), full(gamma_lb)],
        out_specs=[row(c) for c in widths],
        out_shape=[jax.ShapeDtypeStruct((n, c), dt) for c, dt in zip(widths, (BF16, BF16, BF16, F32, BF16))],
        compiler_params=_cp("parallel"),
        name="in_proj_odd",
    )(x2, g, shift, scale, w_in, b_f, gamma_lb)


HGRN_LEVELS = (8, 16, 32, 64)
HGRN_DIAG = 8
HGRN_HEADS_PER_STEP = 8


def _hgrn_unit(q, kk, v, b, st_ref, b_s, rev, store_o):
    L = SCAN_CHUNK
    b_s[...] = b
    row = _iota((L, 1), 0)
    col = _iota((1, L), 1)
    k16 = kk.astype(BF16)
    v16 = v.astype(BF16)
    q = q.astype(F32)
    kk = kk.astype(F32)
    st = st_ref[...]
    o = _dot_nt((q * jnp.exp2(b)).astype(BF16), st.astype(BF16))
    yield
    c = HGRN_DIAG
    sub = _iota((c, 1), 0)
    ys = []
    for blk in range(L // c):
        r0 = blk * c
        qb = q[r0:r0 + c]
        bb = b[r0:r0 + c]
        for s in range(c):
            keep = (sub <= s) if rev else (sub >= s)
            ys.append(qb * jnp.exp2(jnp.where(keep, bb - b_s[r0 + s:r0 + s + 1, :], NEG_BIG)))
    rs = _dot_nt(jnp.concatenate(ys, axis=0).astype(BF16), k16)
    yield
    parts = []
    for m in HGRN_LEVELS:
        par = 2 * m
        pieces = []
        for p0 in range(0, L, par):
            ref_row = p0 + m if rev else p0 + m - 1
            pieces.append(jnp.broadcast_to(b_s[ref_row:ref_row + 1, :], (par, LANES)))
        bref = jnp.concatenate(pieces, axis=0)
        in_first = (row % par) < m
        reader = in_first if rev else jnp.logical_not(in_first)
        z = (jnp.where(reader, q, kk) * jnp.exp2(-jnp.abs(b - bref))).astype(BF16)
        keep = jnp.logical_and((row // par) == (col // par),
                               ((col % par) >= m) if rev else ((col % par) < m))
        parts.append((jnp.logical_and(keep, reader), _dot_nt(z, z)))
        yield
    lane = _iota((c, LANES), 1)
    rows = []
    for blk in range(L // c):
        acc = jnp.zeros((c, LANES), F32)
        for s in range(c):
            i = blk * c + s
            acc = jnp.where(lane == i, rs[i * c:(i + 1) * c], acc)
        rows.append(acc)
    att = jnp.concatenate(rows, axis=0)
    for keep, a in parts:
        att = att + jnp.where(keep, a, 0.0)
    o = o + _dot(att.astype(BF16), v16)
    b_end = b_s[0:1, :] if rev else b_s[L - 1:L, :]
    kd = (kk * jnp.exp2(b_end - b)).astype(BF16)
    st_new = st * jnp.exp2(b_end) + _dot_tn(v16, kd)
    yield
    store_o(o)
    st_ref[...] = st_new
    yield


def _hgrn_kernel(*refs, has_init, out_state, hp):
    qf, vf, lff, kkf, qb, vb, lfb, kkb = refs[:8]
    pos = 8
    if has_init:
        s0 = refs[pos]
        pos += 1
    of, ob = refs[pos:pos + 2]
    pos += 2
    if out_state:
        s_out = refs[pos]
        pos += 1
    st_s, b_s = refs[pos:pos + 2]
    step = pl.program_id(2)

    @pl.when(step == 0)
    def _():
        for d in range(2):
            for h in range(hp):
                st_s[d * hp + h] = s0[0, d, h].T if has_init else jnp.zeros((DH_C, DH_C), F32)

    def storer(o_ref, sl):
        def store(o):
            o_ref[0, :, sl] = o.astype(BF16)
        return store

    units = []
    for d, (q_ref, v_ref, lf_ref, kk_ref, o_ref) in enumerate(((qf, vf, lff, kkf, of), (qb, vb, lfb, kkb, ob))):
        for h in range(hp):
            u = d * hp + h
            sl = slice(h * DH_C, (h + 1) * DH_C)
            units.append(_hgrn_unit(q_ref[0, :, sl], kk_ref[0, :, sl], v_ref[0, :, sl], lf_ref[0, :, sl],
                                    st_s.at[u], b_s.at[u], d == 1, storer(o_ref, sl)))
    live = units
    while live:
        live = [g for g in live if next(g, "done") != "done"]

    if out_state:
        @pl.when(step == pl.num_programs(2) - 1)
        def _():
            for d in range(2):
                for h in range(hp):
                    s_out[0, d, h] = st_s[d * hp + h].T


def _hgrn(q, v, lf, kk, init, out_state):
    b, t, _ = q.shape
    L = SCAN_CHUNK
    nc = t // L
    hp = HGRN_HEADS_PER_STEP
    ng = H_C // hp
    blk = lambda im: pl.BlockSpec((1, L, hp * DH_C), im)
    fwd = lambda off: (lambda bi, h, c: (bi, c, h + off))
    bwd = lambda off: (lambda bi, h, c: (bi, nc - 1 - c, h + off))
    in_specs = [blk(fwd(0)), blk(fwd(0)), blk(fwd(0)), blk(fwd(0)),
                blk(bwd(0)), blk(bwd(0)), blk(bwd(ng)), blk(bwd(ng))]
    args = [q, v, lf, kk, q, v, lf, kk]
    st_spec = pl.BlockSpec((1, 2, hp, DH_C, DH_C), lambda bi, h, c: (bi, 0, h, 0, 0))
    has_init = init is not None
    if has_init:
        in_specs.append(st_spec)
        args.append(init)
    out_specs = [blk(fwd(0)), blk(bwd(0))]
    out_shape = [jax.ShapeDtypeStruct((b, t, W_C), BF16)] * 2
    if out_state:
        out_specs.append(st_spec)
        out_shape.append(jax.ShapeDtypeStruct((b, 2, H_C, DH_C, DH_C), F32))
    return pl.pallas_call(
        functools.partial(_hgrn_kernel, has_init=has_init, out_state=out_state, hp=hp),
        grid=(b, ng, nc),
        in_specs=in_specs,
        out_specs=out_specs,
        out_shape=out_shape,
        scratch_shapes=[pltpu.VMEM((2 * hp, DH_C, DH_C), F32), pltpu.VMEM((2 * hp, L, DH_C), F32)],
        compiler_params=_cp("parallel", "parallel", "arbitrary"),
        name="hgrn_scan",
    )(*args)


def _out_c_kernel(of_ref, ob_ref, og_ref, x_ref, gate_ref, gh_ref, w_ref,
                  g2_ref, sh2_ref, sc2_ref, wr_ref, o_ref, h_ref, aff_ref):
    parts = _row_parts(x_ref.shape[0])
    cats = []
    for rs in parts:
        hm = _head_rms(of_ref[rs, :].astype(F32) + ob_ref[rs, :].astype(F32), gh_ref[...], H_C, DH_C)
        og = og_ref[rs, :].astype(F32)
        heads = [hm[h] * _silu(og[:, h * DH_C:(h + 1) * DH_C]) for h in range(H_C)]
        cats.append(jnp.concatenate(heads, axis=1).astype(BF16))
    _project_and_route(cats, parts, x_ref, gate_ref, w_ref, g2_ref, sh2_ref, sc2_ref, wr_ref, o_ref, h_ref, aff_ref)


def _out_c(of, ob, og, x2, gate, g_hgrn, w_out, route, t, per_batch):
    n = x2.shape[0]
    tm = _row_tile(t)
    row = lambda c: pl.BlockSpec((tm, c), lambda i: (i, 0))
    full = lambda a: pl.BlockSpec(a.shape, lambda i: (0,) * a.ndim)
    r_in, r_out = _route_specs(route, tm, t, per_batch)
    return pl.pallas_call(
        _out_c_kernel,
        grid=(n // tm,),
        in_specs=[row(W_C), row(W_C), row(W_C), row(D_MODEL), _mod_spec(per_batch, tm, t),
                  full(g_hgrn), full(w_out)] + r_in,
        out_specs=r_out,
        out_shape=_route_shapes(n),
        compiler_params=_cp("parallel"),
        name="out_proj_odd",
    )(of, ob, og, x2, gate, g_hgrn, w_out, *route)


def _topk_kernel(aff_ref, pos_ref, s0_ref, bits_s, sel_s, *, cap, tb):
    t = aff_ref.shape[1]
    nb = t // tb
    bits_s[...] = pltpu.bitcast(aff_ref[0], I32)

    def count_ge(cand):
        def body(j, acc):
            r = pl.multiple_of(j * tb, tb)
            blk = bits_s[pl.ds(r, tb), :]
            return acc + jnp.sum((blk >= cand).astype(F32), axis=0, keepdims=True)
        return lax.fori_loop(0, nb, body, jnp.zeros((1, LANES), F32))

    def bit_body(i, thr):
        cand = thr | lax.shift_left(jnp.int32(1), 30 - i)
        return jnp.where(count_ge(cand) >= cap, cand, thr)

    thr = lax.fori_loop(0, 31, bit_body, jnp.zeros((1, LANES), I32))

    def count_gt(j, acc):
        r = pl.multiple_of(j * tb, tb)
        return acc + jnp.sum((bits_s[pl.ds(r, tb), :] > thr).astype(F32), axis=0, keepdims=True)

    need = cap - lax.fori_loop(0, nb, count_gt, jnp.zeros((1, LANES), F32))
    strict = (_iota((tb, tb), 0) > _iota((tb, tb), 1)).astype(BF16)

    def select(j, carry):
        r = pl.multiple_of(j * tb, tb)
        blk = bits_s[pl.ds(r, tb), :]
        eq = blk == thr
        before = carry + _dot(strict, eq.astype(BF16))
        sel = jnp.logical_or(blk > thr, jnp.logical_and(eq, before < need))
        sel_s[pl.ds(r, tb), :] = sel.astype(F32)
        return carry + jnp.sum(eq.astype(F32), axis=0, keepdims=True)

    lax.fori_loop(0, nb, select, jnp.zeros((1, LANES), F32))

    def place(j, carry):
        r = pl.multiple_of(j * tb, tb)
        sel = sel_s[pl.ds(r, tb), :]
        slot = carry + _dot(strict, sel.astype(BF16))
        pos_ref[0, pl.ds(r, tb), :] = jnp.where(sel > 0.5, slot, -1.0).astype(I32)
        s0_ref[0, pl.ds(j, 1), :] = carry.astype(I32)
        return carry + jnp.sum(sel, axis=0, keepdims=True)

    lax.fori_loop(0, nb, place, jnp.zeros((1, LANES), F32))


def _topk(aff, cap, tb):
    b, t, _ = aff.shape
    nb = t // tb
    return pl.pallas_call(
        functools.partial(_topk_kernel, cap=cap, tb=tb),
        grid=(b,),
        in_specs=[pl.BlockSpec((1, t, LANES), lambda bi: (bi, 0, 0))],
        out_specs=[pl.BlockSpec((1, t, LANES), lambda bi: (bi, 0, 0)),
                   pl.BlockSpec((1, nb, LANES), lambda bi: (bi, 0, 0))],
        out_shape=[jax.ShapeDtypeStruct((b, t, LANES), I32), jax.ShapeDtypeStruct((b, nb, LANES), I32)],
        scratch_shapes=[pltpu.VMEM((t, LANES), I32), pltpu.VMEM((t, LANES), F32)],
        compiler_params=_cp("parallel"),
        name="moe_topk",
    )(aff)


MOE_SMALL_COUNT = 48
MOE_STACK_ROWS = 1024


def _windows(cap, tb):
    narrow = MOE_SMALL_COUNT + BF16_ROWS
    if cap <= narrow:
        return (cap,)
    return (narrow, min(tb + BF16_ROWS, cap))


def _window_starts(s0_ref, bi, j, cap, w):
    if w == cap:
        return [0] * N_EXPERTS
    out = []
    for e in range(N_EXPERTS):
        st = jnp.minimum((s0_ref[bi, j, e] // BF16_ROWS) * BF16_ROWS, cap - w)
        out.append(pl.multiple_of(st, BF16_ROWS))
    return out


def _per_window(windows, narrow_ref, bi, j, run):
    if len(windows) == 1:
        run(windows[0])
    else:
        narrow = narrow_ref[bi, j] > 0
        pl.when(narrow)(lambda: run(windows[0]))
        pl.when(jnp.logical_not(narrow))(lambda: run(windows[1]))


def _gather_kernel(s0_ref, narrow_ref, h_ref, post_ref, o_ref, *, cap, tb):
    bi = pl.program_id(0)
    j = pl.program_id(1)

    @pl.when(j == 0)
    def _():
        o_ref[...] = jnp.zeros_like(o_ref)

    def run(w):
        starts = _window_starts(s0_ref, bi, j, cap, w)
        post = post_ref[0]
        r = _iota((w, tb), 0)
        group = max(1, min(N_EXPERTS, MOE_STACK_ROWS // w))
        for e0 in range(0, N_EXPERTS, group):
            es = range(e0, min(e0 + group, N_EXPERTS))
            onehot = jnp.concatenate(
                [jnp.where((post[e:e + 1, :] - starts[e]) == r, 1.0, 0.0).astype(BF16) for e in es],
                axis=0)
            rows = _dot(onehot, h_ref[0]).astype(BF16)
            for k, e in enumerate(es):
                cur = o_ref[0, e, pl.ds(starts[e], w), :]
                o_ref[0, e, pl.ds(starts[e], w), :] = cur + rows[k * w:(k + 1) * w]

    _per_window(_windows(cap, tb), narrow_ref, bi, j, run)


def _gather(h2, post, s0, narrow, cap, tb):
    b, t, d = h2.shape
    nb = t // tb
    grid_spec = pltpu.PrefetchScalarGridSpec(
        num_scalar_prefetch=2,
        grid=(b, nb),
        in_specs=[pl.BlockSpec((1, tb, d), lambda bi, j, s, n: (bi, j, 0)),
                  pl.BlockSpec((1, N_EXPERTS, tb), lambda bi, j, s, n: (bi, 0, j))],
        out_specs=pl.BlockSpec((1, N_EXPERTS, cap, d), lambda bi, j, s, n: (bi, 0, 0, 0)),
    )
    return pl.pallas_call(
        functools.partial(_gather_kernel, cap=cap, tb=tb),
        grid_spec=grid_spec,
        out_shape=jax.ShapeDtypeStruct((b, N_EXPERTS, cap, d), BF16),
        compiler_params=_cp("parallel", "arbitrary"),
        name="moe_gather",
    )(s0, narrow, h2, post)


def _expert_kernel(xc_ref, xs_ref, wg_ref, wu_ref, wd_ref, yc_ref, ys_ref, wg_s, wu_s, wd_s):
    i = pl.program_id(1)

    @pl.when(i == 0)
    def _():
        wg_s[...] = wg_ref[0].astype(BF16)
        wu_s[...] = wu_ref[0].astype(BF16)
        wd_s[...] = wd_ref[0].astype(BF16)

    def mlp(x_ref, y_ref):
        shp = x_ref.shape
        rows = shp[0] * shp[2]
        x = x_ref[...].reshape(rows, shp[3])
        step = rows // max(1, rows // 128)
        xs = [x[r:r + step] for r in range(0, rows, step)]
        hs = [(_silu(_dot(xp, wg_s[...])) * _dot(xp, wu_s[...])).astype(BF16) for xp in xs]
        ys = [_dot(hm, wd_s[...]).astype(BF16) for hm in hs]
        y_ref[...] = jnp.concatenate(ys, axis=0).reshape(shp)

    pl.when(i == 0)(lambda: mlp(xc_ref, yc_ref))
    pl.when(i > 0)(lambda: mlp(xs_ref, ys_ref))


def _experts(xin_c, xin_s, w_gate, w_up, w_down, layer):
    bc, e, cap_c, d = xin_c.shape
    bs, _, cap_s, _ = xin_s.shape
    f = w_gate.shape[-1]
    cspec = pl.BlockSpec((bc, 1, cap_c, d), lambda ei, i: (0, ei, 0, 0))
    sb = 2 if bs % 2 == 0 else 1
    sspec = pl.BlockSpec((sb, 1, cap_s, d), lambda ei, i: (jnp.maximum(i - 1, 0), ei, 0, 0))
    ws = lambda a: pl.BlockSpec((None, 1) + a.shape[2:], lambda ei, i: (layer, ei, 0, 0))
    return pl.pallas_call(
        _expert_kernel,
        grid=(e, bs // sb + 1),
        in_specs=[cspec, sspec, ws(w_gate), ws(w_up), ws(w_down)],
        out_specs=[cspec, sspec],
        out_shape=[jax.ShapeDtypeStruct(xin_c.shape, BF16), jax.ShapeDtypeStruct(xin_s.shape, BF16)],
        scratch_shapes=[pltpu.VMEM((d, f), BF16), pltpu.VMEM((d, f), BF16), pltpu.VMEM((f, d), BF16)],
        compiler_params=_cp("parallel", "arbitrary"),
        name="moe_experts",
    )(xin_c, xin_s, w_gate, w_up, w_down)


def _combine_kernel(*refs, cap, tb, final):
    windows = _windows(cap, tb)
    s0_ref, narrow_ref, y_ref, pos_ref, aff_ref, x_ref, gate_ref = refs[:7]
    pos_n = 7
    if final:
        gf_ref = refs[pos_n]
        pos_n += 1
    spread_ref, row_id_ref = refs[pos_n:pos_n + 2]
    o_ref = refs[-1]
    bi = pl.program_id(0)
    j = pl.program_id(1)
    offset = 1024
    lane = _iota((1, N_EXPERTS), 1)

    def run(w):
        starts = _window_starts(s0_ref, bi, j, cap, w)
        if w == windows[0]:
            stv = jnp.zeros((1, N_EXPERTS), I32)
            for e in range(N_EXPERTS):
                stv = jnp.where(lane == e, starts[e], stv)
            rel = pos_ref[0] - stv + offset
            spread = spread_ref[...]
            hi = lax.shift_right_logical(rel, 5).astype(F32).astype(BF16)
            lo = jnp.bitwise_and(rel, 31).astype(F32).astype(BF16)
            relx = 32.0 * _dot(hi, spread) + _dot(lo, spread) - offset
            gx = _dot(aff_ref[0].astype(BF16), spread)
            g = jnp.where(relx == row_id_ref[...], gx, 0.0)
            ys = jnp.concatenate([y_ref[0, e, pl.ds(starts[e], w), :] for e in range(N_EXPERTS)], axis=0)
            acc = _dot(g.astype(BF16), ys)
        else:
            pos = pos_ref[0]
            aff = aff_ref[0].astype(BF16).astype(F32)
            col = _iota((tb, w), 1)
            acc = jnp.zeros((tb, D_MODEL), F32)
            for e in range(N_EXPERTS):
                g = jnp.where((pos[:, e:e + 1] - starts[e]) == col, aff[:, e:e + 1], 0.0)
                acc = acc + _dot(g.astype(BF16), y_ref[0, e, pl.ds(starts[e], w), :])
        out = x_ref[0] + gate_ref[0] * acc
        if final:
            out = _rms_rows(out, gf_ref[...])
        o_ref[0] = out

    _per_window(windows, narrow_ref, bi, j, run)


def _combine(y, pos, aff, x3, gate, s0, narrow, cap, tb, per_batch, g_final):
    b, t, d = x3.shape
    nb = t // tb
    final = g_final is not None
    tok = lambda c: pl.BlockSpec((1, tb, c), lambda bi, j, s, n: (bi, j, 0))
    const = lambda a: pl.BlockSpec(a.shape, lambda bi, j, s, n: (0, 0))
    in_specs = [pl.BlockSpec((1, N_EXPERTS, cap, d), lambda bi, j, s, n: (bi, 0, 0, 0)),
                tok(N_EXPERTS), tok(N_EXPERTS), tok(d),
                pl.BlockSpec((1, 1, d), (lambda bi, j, s, n: (bi, 0, 0)) if per_batch
                             else (lambda bi, j, s, n: (0, 0, 0)))]
    args = [y, pos, aff, x3, gate]
    if final:
        in_specs.append(const(g_final))
        args.append(g_final)
    w = _windows(cap, tb)[0]
    spread = jnp.repeat(jnp.eye(N_EXPERTS, dtype=BF16), w, axis=1)
    row_id = jnp.tile(jnp.arange(w, dtype=F32), N_EXPERTS).reshape(1, -1)
    in_specs += [const(spread), const(row_id)]
    args += [spread, row_id]
    grid_spec = pltpu.PrefetchScalarGridSpec(
        num_scalar_prefetch=2, grid=(b, nb), in_specs=in_specs, out_specs=tok(d))
    return pl.pallas_call(
        functools.partial(_combine_kernel, cap=cap, tb=tb, final=final),
        grid_spec=grid_spec,
        out_shape=jax.ShapeDtypeStruct((b, t, d), F32),
        compiler_params=_cp("parallel", "arbitrary"),
        name="moe_combine",
    )(s0, narrow, *args)


def _moe(x2, h2, aff, b, t, gate, per_batch, g_final):
    cap = CAPACITY_FACTOR * t // N_EXPERTS
    tb = min(MOE_TOKEN_BLOCK, t)
    aff3 = aff.reshape(b, t, N_EXPERTS)
    per = LANES // N_EXPERTS
    groups = -(-b // per)
    nb = t // tb

    def pack(a):
        a = jnp.pad(a, ((0, groups * per - b), (0, 0), (0, 0)))
        return a.reshape(groups, per, a.shape[1], N_EXPERTS).transpose(0, 2, 1, 3).reshape(groups, a.shape[1], LANES)

    def unpack(a):
        return a.reshape(groups, a.shape[1], per, N_EXPERTS).transpose(0, 2, 1, 3).reshape(
            groups * per, a.shape[1], N_EXPERTS)[:b]

    pos, s0 = _topk(pack(aff3), cap, tb)
    pos, s0 = unpack(pos), unpack(s0)
    counts = jnp.diff(s0, axis=1, append=jnp.full((b, 1, N_EXPERTS), cap, I32))
    narrow = jnp.all(counts <= MOE_SMALL_COUNT, axis=-1).astype(I32)
    xin = _gather(h2.reshape(b, t, D_MODEL), pos.transpose(0, 2, 1), s0, narrow, cap, tb)
    y = yield xin
    out = _combine(y, pos, aff3, x2.reshape(b, t, D_MODEL), gate, s0, narrow, cap, tb, per_batch, g_final)
    return out.reshape(b * t, D_MODEL)


def _rope_tables(t):
    rows = t // GRID_W
    row = jnp.repeat(jnp.arange(rows, dtype=F32), GRID_W)
    col = jnp.tile(jnp.arange(GRID_W, dtype=F32), rows)
    n_freq = DQK_A // 4
    inv_freq = jnp.power(ROPE_BASE, -jnp.arange(n_freq, dtype=F32) / n_freq)
    ang = jnp.concatenate([row[:, None] * inv_freq, col[:, None] * inv_freq], axis=-1)
    cos, sin = jnp.cos(ang), jnp.sin(ang)
    zero = jnp.zeros_like(sin)
    reps = LANES // DQK_A
    cos_t = jnp.tile(jnp.concatenate([cos, cos], axis=-1), (1, reps))
    sin_a = jnp.tile(jnp.concatenate([-sin, zero], axis=-1), (1, reps))
    sin_b = jnp.tile(jnp.concatenate([zero, sin], axis=-1), (1, reps))
    return cos_t, sin_a, sin_b


def _stream(x, mods, per_batch, rope, cache, p):
    b, t, d = x.shape
    x2 = x.reshape(b * t, d)
    depth = p["g_norm"].shape[0]
    even_out, odd_out = [], []
    for l in range(depth):
        j = l // 2
        sh1, sc1, gt1, sh2, sc2, gt2 = mods[l]
        g1 = p["g_norm"][l, 0].reshape(1, d)
        g2 = p["g_norm"][l, 1].reshape(1, d)
        route = (g2, sh2, sc2, p["w_router_pad"][l])
        if l % 2 == 0:
            aq, ak, av, bqk, bv, bo, gates = _in_ab(x2, g1, sh1, sc1, p["w_in_ab_main"][j], p["w_in_ab_gate"][j],
                                                     rope, t, per_batch, F32 if cache is None else BF16)
            r3 = lambda a: a.reshape(b, t, a.shape[-1])
            if cache is not None:
                kc = cache[0][:, j].reshape(b, -1, W_A)
                vc = cache[1][:, j].reshape(b, -1, W_A)
                nu = 2 * H_B
                init = (cache[2][:, j].reshape(b, nu, DH_B, DH_B),
                        cache[3][:, j].reshape(b, nu, 1, DH_B),
                        jnp.broadcast_to(cache[4][:, j].reshape(b, nu, 1, 1), (b, nu, 1, LANES)))
            else:
                kc = vc = init = None
            lam_init = 0.8 - 0.6 * math.exp(-0.3 * l)
            attn = _attention_pipe(r3(aq), r3(ak), r3(av), kc, vc, p["lam_ab"][j], lam_init)
            qk = _conv_silu(r3(bqk), p["conv_ab"][j])
            bias = jnp.pad(jnp.concatenate([p["b_ig_ab"][j].reshape(-1), p["b_fg_ab"][j].reshape(-1)]),
                           (0, LANES - 4 * H_B)).reshape(1, LANES)
            res = _mlstm(qk, r3(bv), r3(gates), bias, init, cache is None)
            hf, hb = res[0], res[1]
            if cache is None:
                even_out.append((ak, av, res[2], res[3], res[4]))
            x2, h2, aff = _out_ab(attn.reshape(b * t, W_A), hf.reshape(b * t, W_B), hb.reshape(b * t, W_B), bo, x2,
                                  gt1, p["g_attn_ab"][j].reshape(1, W_A), p["g_mlstm_ab"][j].reshape(1, W_B),
                                  p["w_out_ab_bf16"][j], route, t, per_batch, 1.0 - lam_init)
        else:
            q, v, og, lf, kk = _in_c(x2, g1, sh1, sc1, p["w_in_c_bf16"][j], p["b_f_c"][j], p["gamma_lb"], l, t,
                                     per_batch)
            r3 = lambda a: a.reshape(b, t, a.shape[-1])
            init = None if cache is None else cache[5][:, j]
            res = _hgrn(r3(q), r3(v), r3(lf), r3(kk), init, cache is None)
            if cache is None:
                odd_out.append(res[2])
            x2, h2, aff = _out_c(res[0].reshape(b * t, W_C), res[1].reshape(b * t, W_C), og, x2, gt1,
                                 p["g_hgrn_c"][j].reshape(1, W_C), p["w_out_c_bf16"][j], route, t, per_batch)
        g_final = p["g_final"].reshape(1, d) if l == depth - 1 else None
        x2 = yield from _moe(x2, h2, aff, b, t, gt2, per_batch, g_final)
    return x2.reshape(b, t, d), even_out, odd_out


def kernel(x_prompt, x_sample, cache_dattn_k, cache_dattn_v, state_mlstm_C, state_mlstm_n, state_mlstm_m,
           state_hgrn_S, c, c_ctx, w_ada, b_ada, g_norm, g_final, w_in_ab, b_ig_ab, b_fg_ab, lam_ab, conv_ab,
           g_attn_ab, g_mlstm_ab, w_out_ab, w_in_c, b_f_c, gamma_lb, g_hgrn_c, w_out_c,
           w_router, w_gate_e, w_up_e, w_down_e):
    d = D_MODEL
    bs = c.shape[0]
    bp, tp, _ = x_prompt.shape
    depth = w_ada.shape[0]
    n_main = 3 * W_A + 4 * W_B
    p = dict(
        g_norm=g_norm, g_final=g_final, b_ig_ab=b_ig_ab, b_fg_ab=b_fg_ab, lam_ab=lam_ab, conv_ab=conv_ab,
        g_attn_ab=g_attn_ab, g_mlstm_ab=g_mlstm_ab, b_f_c=b_f_c, gamma_lb=gamma_lb, g_hgrn_c=g_hgrn_c,
        w_gate_e=w_gate_e, w_up_e=w_up_e, w_down_e=w_down_e,
        w_in_ab_main=w_in_ab[:, :, :n_main].astype(BF16),
        w_in_ab_gate=jnp.pad(w_in_ab[:, :, n_main:], ((0, 0), (0, 0), (0, LANES - 4 * H_B))).astype(BF16),
        w_out_ab_bf16=w_out_ab.astype(BF16),
        w_in_c_bf16=w_in_c.astype(BF16),
        w_out_c_bf16=w_out_c.astype(BF16),
        w_router_pad=jnp.pad(w_router, ((0, 0), (0, 0), (0, LANES - N_EXPERTS))),
    )
    rows = -(-(bs + 1) // 8) * 8
    cond = jnp.concatenate([c, c_ctx.reshape(1, d), jnp.zeros((rows - bs - 1, d), F32)], axis=0)
    mods = _ada(cond, w_ada, b_ada)
    mods_s = [[mods[l, :bs, i * d:(i + 1) * d].reshape(bs, 1, d) for i in range(6)] for l in range(depth)]
    mods_c = [[mods[l, bs:bs + 1, i * d:(i + 1) * d].reshape(1, 1, d) for i in range(6)] for l in range(depth)]

    cache = (cache_dattn_k, cache_dattn_v, state_mlstm_C, state_mlstm_n, state_mlstm_m, state_hgrn_S)
    ctx = _stream(x_prompt, mods_c, False, None, None, p)
    smp = _stream(x_sample, mods_s, True, _rope_tables(x_sample.shape[1]), cache, p)
    xin_c, xin_s = next(ctx), next(smp)
    for l in range(depth):
        y_c, y_s = _experts(xin_c, xin_s, w_gate_e, w_up_e, w_down_e, l)
        try:
            xin_c = ctx.send(y_c)
        except StopIteration as done:
            y_prompt, ev, od = done.value
        try:
            xin_s = smp.send(y_s)
        except StopIteration as done:
            y_sample = done.value[0]
    new_k = jnp.stack([e[0].reshape(bp, tp, H_A, 2 * DQK_A) for e in ev], axis=1)
    new_v = jnp.stack([e[1].reshape(bp, tp, H_A, DV_A) for e in ev], axis=1)
    new_c = jnp.stack([e[2].reshape(bp, 2, H_B, DH_B, DH_B) for e in ev], axis=1)
    new_n = jnp.stack([e[3].reshape(bp, 2, H_B, DH_B) for e in ev], axis=1)
    new_m = jnp.stack([e[4][..., 0].reshape(bp, 2, H_B) for e in ev], axis=1)
    new_s = jnp.stack(od, axis=1)
    return (y_prompt, y_sample, new_k, new_v, new_c, new_n, new_m, new_s)
```
